```python
import math
import jax, jax.numpy as jnp
from jax import lax
import numpy as np

D_MODEL = 1024
BATCH = 16
SEQ = 256
DEPTH = 1
DEC_BATCH = 8
DEC_SEQ = 2048
PAST_LEN = 256

GRID_W = 64
ROPE_BASE = 10000.0
EPS = 1e-6
NEG_INF = -1e30
Q_BLOCK = 128
A_HEADS = 4
A_HEAD_DIM = 64
A_V_DIM = 2 * A_HEAD_DIM
A_WIDTH = A_HEADS * A_V_DIM
B_HEADS = 8
B_KV_HEADS = 2
B_GROUP = B_HEADS // B_KV_HEADS
B_HEAD_DIM = 64
B_WIDTH = B_HEADS * B_HEAD_DIM
WINDOW = 128
QA_COLS = A_HEADS * 2 * A_HEAD_DIM
KA_COLS = A_HEADS * 2 * A_HEAD_DIM
VA_COLS = A_WIDTH
QB_COLS = B_WIDTH
KB_COLS = B_KV_HEADS * B_HEAD_DIM
VB_COLS = B_KV_HEADS * B_HEAD_DIM
GATE_COLS = 2 * D_MODEL
IN_COLS = QA_COLS + KA_COLS + VA_COLS + QB_COLS + KB_COLS + VB_COLS + GATE_COLS
N_GROUPS = 4
EXPERTS_PER_GROUP = 8
N_EXPERTS = N_GROUPS * EXPERTS_PER_GROUP
TOP_K = 2
D_EXPERT = 512
MOE_BLOCK = 128

kernel_name = "hybrid_diffattn_swa_hmoe_dit_step"


def rms_norm(x, g):
    xf = x.astype(jnp.float32)
    y = xf * lax.rsqrt(jnp.mean(jnp.square(xf), axis=-1, keepdims=True) + EPS)
    return (y * g.astype(jnp.float32)).astype(x.dtype)


def adaln(cvec, w_ada, b_ada):
    m = (jax.nn.silu(cvec) @ w_ada + b_ada)[..., None, :]
    return jnp.split(m, 6, axis=-1)


def axial_rope(n_tok, head_dim):
    n_rows = n_tok // GRID_W
    rows = jnp.repeat(jnp.arange(n_rows), GRID_W).astype(jnp.float32)
    cols = jnp.tile(jnp.arange(GRID_W), n_rows).astype(jnp.float32)
    quarter = head_dim // 4
    inv = ROPE_BASE ** (-jnp.arange(quarter, dtype=jnp.float32) / quarter)
    ang = jnp.concatenate([rows[:, None] * inv, cols[:, None] * inv], axis=-1)
    return jnp.cos(ang), jnp.sin(ang)


def apply_rope(x, cos, sin):
    half = x.shape[-1] // 2
    shape = (x.shape[1],) + (1,) * (x.ndim - 3) + (half,)
    c, s = cos.reshape(shape), sin.reshape(shape)
    xf = x.astype(jnp.float32)
    x1, x2 = xf[..., :half], xf[..., half:]
    return jnp.concatenate([x1 * c - x2 * s, x2 * c + x1 * s], axis=-1).astype(x.dtype)


def sweep_query_blocks(fn, q):
    b, s = q.shape[:2]
    nb = s // Q_BLOCK
    qb = jnp.moveaxis(q.reshape((b, nb, Q_BLOCK) + q.shape[2:]), 1, 0)
    out = lax.map(lambda a: fn(a[0], a[1]), (jnp.arange(nb), qb))
    return jnp.moveaxis(out, 0, 1).reshape((b, s) + out.shape[3:])


def diff_lambda(lq1, lk1, lq2, lk2, lambda_init):
    f = jnp.float32
    return (jnp.exp(jnp.sum(lq1.astype(f) * lk1.astype(f)))
            - jnp.exp(jnp.sum(lq2.astype(f) * lk2.astype(f))) + lambda_init)


def diff_attention(q, k, v, lam, g_head, lambda_init):
    scale = A_HEAD_DIM ** -0.5

    def blk(i, qb):
        s = jnp.einsum('bqhmd,bkhmd->bhmqk', qb, k).astype(jnp.float32) * scale
        p = jax.nn.softmax(s, axis=-1)
        a = p[:, :, 0] - lam * p[:, :, 1]
        return jnp.einsum('bhqk,bkhe->bqhe', a.astype(v.dtype), v)

    o = sweep_query_blocks(blk, q)
    o = rms_norm(o, g_head) * (1.0 - lambda_init)
    return o.reshape(o.shape[:2] + (A_WIDTH,))


def sink_attend(qb, k, v, sink, mask):
    s = jnp.einsum('bqngd,bknd->bngqk', qb, k).astype(jnp.float32) * (B_HEAD_DIM ** -0.5)
    if mask is not None:
        s = jnp.where(mask, s, NEG_INF)
    sk = jnp.broadcast_to(sink.astype(jnp.float32)[None, :, :, None, None], s.shape[:-1] + (1,))
    p = jax.nn.softmax(jnp.concatenate([s, sk], axis=-1), axis=-1)[..., :-1]
    return jnp.einsum('bngqk,bknd->bqngd', p.astype(v.dtype), v)


def swa_context(q, k, v, sink):
    o = sweep_query_blocks(lambda i, qb: sink_attend(qb, k, v, sink, None), q)
    return o.reshape(o.shape[:2] + (B_WIDTH,))


def swa_latent(q, k, v, k_ctx, v_ctx, sink):
    s_len = q.shape[1]
    pad = ((0, 0), (WINDOW, WINDOW), (0, 0), (0, 0))
    kp, vp = jnp.pad(k, pad), jnp.pad(v, pad)
    span = Q_BLOCK + 2 * WINDOW
    rel = jnp.arange(span) - WINDOW
    near = jnp.abs(rel[None, :] - jnp.arange(Q_BLOCK)[:, None]) <= WINDOW
    ctx_vis = jnp.ones((Q_BLOCK, k_ctx.shape[1]), dtype=bool)

    def blk(i, qb):
        start = i * Q_BLOCK
        kb = lax.dynamic_slice_in_dim(kp, start, span, axis=1)
        vb = lax.dynamic_slice_in_dim(vp, start, span, axis=1)
        kabs = start + rel
        band = near & ((kabs >= 0) & (kabs < s_len))[None, :]
        return sink_attend(qb, jnp.concatenate([k_ctx, kb], axis=1), jnp.concatenate([v_ctx, vb], axis=1),
                           sink, jnp.concatenate([ctx_vis, band], axis=1))

    o = sweep_query_blocks(blk, q)
    return o.reshape(o.shape[:2] + (B_WIDTH,))


def split_projection(h, w_in):
    b, s = h.shape[:2]
    z = h @ w_in
    sizes = (QA_COLS, KA_COLS, VA_COLS, QB_COLS, KB_COLS, VB_COLS, D_MODEL, D_MODEL)
    qa, ka, va, qb, kb, vb, ga, gb = jnp.split(z, np.cumsum(sizes)[:-1].tolist(), axis=-1)
    qa = qa.reshape(b, s, A_HEADS, 2, A_HEAD_DIM)
    ka = ka.reshape(b, s, A_HEADS, 2, A_HEAD_DIM)
    va = va.reshape(b, s, A_HEADS, A_V_DIM)
    qb = qb.reshape(b, s, B_KV_HEADS, B_GROUP, B_HEAD_DIM)
    kb = kb.reshape(b, s, B_KV_HEADS, B_HEAD_DIM)
    vb = vb.reshape(b, s, B_KV_HEADS, B_HEAD_DIM)
    return qa, ka, va, qb, kb, vb, ga, gb


def merge_branches(oa, ob, ga, gb, p):
    mix = jax.nn.sigmoid(ga) * (oa @ p['w_proj_a']) + jax.nn.sigmoid(gb) * (ob @ p['w_proj_b'])
    return mix @ p['w_out']


def routed_experts(t, e_id, w_pair, w1, w3, w2):
    n_tok = t.shape[0]
    n_asg = n_tok * TOP_K
    eid = e_id.reshape(n_asg)
    tok = jnp.repeat(jnp.arange(n_tok, dtype=jnp.int32), TOP_K)
    wts = w_pair.reshape(n_asg)
    order = jnp.argsort(eid)
    eid_s, tok_s, w_s = eid[order], tok[order], wts[order]
    counts = jnp.bincount(eid, length=N_EXPERTS)
    padded = (counts + MOE_BLOCK - 1) // MOE_BLOCK * MOE_BLOCK
    pad_end = jnp.cumsum(padded)
    pad_start = pad_end - padded
    start = jnp.cumsum(counts) - counts
    dest = pad_start[eid_s] + jnp.arange(n_asg) - start[eid_s]
    n_blocks = -(-n_asg // MOE_BLOCK) + N_EXPERTS
    buf_tok = jnp.zeros((n_blocks * MOE_BLOCK,), jnp.int32).at[dest].set(tok_s)
    buf_w = jnp.zeros((n_blocks * MOE_BLOCK,), t.dtype).at[dest].set(w_s.astype(t.dtype))
    blk_expert = jnp.minimum(jnp.searchsorted(pad_end, jnp.arange(n_blocks) * MOE_BLOCK, side='right'), N_EXPERTS - 1)

    def blk(args):
        tok_b, w_b, e = args
        xb = t[tok_b]
        hb = jax.nn.silu(xb @ w1[e]) * (xb @ w3[e])
        return (hb @ w2[e]) * w_b[:, None]

    y = lax.map(blk, (buf_tok.reshape(n_blocks, MOE_BLOCK), buf_w.reshape(n_blocks, MOE_BLOCK), blk_expert))
    return jnp.zeros_like(t).at[buf_tok].add(y.reshape(-1, t.shape[1]))


def hierarchical_moe(h, p):
    b, s, d = h.shape
    n_tok = b * s
    t = h.reshape(n_tok, d)
    ar = jnp.arange(n_tok)
    lg = (t @ p['w_router_group'] + p['b_router_group']).astype(jnp.float32)
    g_sel = jnp.argmax(lg, axis=-1)
    g_w = jax.nn.softmax(lg, axis=-1)[ar, g_sel][:, None]
    le = (t @ p['w_router_expert'] + p['b_router_expert']).astype(jnp.float32)
    le_sel = le.reshape(n_tok, N_GROUPS, EXPERTS_PER_GROUP)[ar, g_sel]
    top_v, top_i = lax.top_k(le_sel, TOP_K)
    w_pair = g_w * jax.nn.softmax(top_v, axis=-1)
    e_id = g_sel[:, None] * EXPERTS_PER_GROUP + top_i
    y = routed_experts(t, e_id, w_pair, p['w_e1'], p['w_e3'], p['w_e2'])
    return y.reshape(b, s, d)


def channel_sublayer(x, sh, sc, gt, p):
    h = rms_norm(x, p['g_pre_ffn']) * (1 + sc) + sh
    return x + gt * rms_norm(hierarchical_moe(h, p), p['g_post_ffn'])


def context_layer(x, mod, p, lambda_init):
    sh1, sc1, g1, sh2, sc2, g2 = mod
    h = rms_norm(x, p['g_pre_mix']) * (1 + sc1) + sh1
    qa, ka, va, qb, kb, vb, ga, gb = split_projection(h, p['w_in'])
    lam = diff_lambda(p['lam_q1'], p['lam_k1'], p['lam_q2'], p['lam_k2'], lambda_init)
    oa = diff_attention(qa, ka, va, lam, p['g_diff_head'], lambda_init)
    ob = swa_context(qb, kb, vb, p['sink'].reshape(B_KV_HEADS, B_GROUP))
    x = x + g1 * rms_norm(merge_branches(oa, ob, ga, gb, p), p['g_post_mix'])
    x = channel_sublayer(x, sh2, sc2, g2, p)
    return x, (ka, va, kb, vb)


def latent_layer(x, mod, ctx_kv, rope_a, rope_b, p, lambda_init):
    sh1, sc1, g1, sh2, sc2, g2 = mod
    ka_c, va_c, kb_c, vb_c = ctx_kv
    h = rms_norm(x, p['g_pre_mix']) * (1 + sc1) + sh1
    qa, ka, va, qb, kb, vb, ga, gb = split_projection(h, p['w_in'])
    qa, ka = apply_rope(qa, *rope_a), apply_rope(ka, *rope_a)
    qb, kb = apply_rope(qb, *rope_b), apply_rope(kb, *rope_b)
    lam = diff_lambda(p['lam_q1'], p['lam_k1'], p['lam_q2'], p['lam_k2'], lambda_init)
    oa = diff_attention(qa, jnp.concatenate([ka_c, ka], axis=1), jnp.concatenate([va_c, va], axis=1),
                        lam, p['g_diff_head'], lambda_init)
    ob = swa_latent(qb, kb, vb, kb_c, vb_c, p['sink'].reshape(B_KV_HEADS, B_GROUP))
    x = x + g1 * rms_norm(merge_branches(oa, ob, ga, gb, p), p['g_post_mix'])
    return channel_sublayer(x, sh2, sc2, g2, p)


def setup_inputs(seed: int = 0) -> dict:
    key = jax.random.key(seed)
    ks = iter(jax.random.split(key, 40))
    f = jnp.float32

    def nrm(shape, scale=1.0):
        return jax.random.normal(next(ks), shape, f) * scale

    def gain(shape):
        return 1.0 + nrm(shape, 0.05)

    D = D_MODEL
    return {
        'x_prompt': nrm((BATCH, SEQ, D)),
        'x_sample': nrm((DEC_BATCH, DEC_SEQ, D)),
        'c': nrm((DEC_BATCH, D)),
        'cache_diff_k': nrm((DEC_BATCH, DEPTH, PAST_LEN, A_HEADS, 2, A_HEAD_DIM)),
        'cache_diff_v': nrm((DEC_BATCH, DEPTH, PAST_LEN, A_HEADS, A_V_DIM)),
        'cache_swa_k': nrm((DEC_BATCH, DEPTH, PAST_LEN, B_KV_HEADS, B_HEAD_DIM)),
        'cache_swa_v': nrm((DEC_BATCH, DEPTH, PAST_LEN, B_KV_HEADS, B_HEAD_DIM)),
        'c_ctx': nrm((D,)),
        'w_ada': nrm((DEPTH, D, 6 * D), 0.5 * D ** -0.5),
        'b_ada': nrm((DEPTH, 6 * D), 0.01),
        'g_pre_mix': gain((DEPTH, D)),
        'g_post_mix': gain((DEPTH, D)),
        'g_pre_ffn': gain((DEPTH, D)),
        'g_post_ffn': gain((DEPTH, D)),
        'w_in': nrm((DEPTH, D, IN_COLS), D ** -0.5),
        'lam_q1': nrm((DEPTH, A_HEAD_DIM), 0.1),
        'lam_k1': nrm((DEPTH, A_HEAD_DIM), 0.1),
        'lam_q2': nrm((DEPTH, A_HEAD_DIM), 0.1),
        'lam_k2': nrm((DEPTH, A_HEAD_DIM), 0.1),
        'g_diff_head': gain((DEPTH, A_V_DIM)),
        'sink': nrm((DEPTH, B_HEADS), 0.5),
        'w_proj_a': nrm((DEPTH, A_WIDTH, D), A_WIDTH ** -0.5),
        'w_proj_b': nrm((DEPTH, B_WIDTH, D), B_WIDTH ** -0.5),
        'w_out': nrm((DEPTH, D, D), D ** -0.5),
        'w_router_group': nrm((DEPTH, D, N_GROUPS), D ** -0.5),
        'b_router_group': nrm((DEPTH, N_GROUPS), 0.01),
        'w_router_expert': nrm((DEPTH, D, N_EXPERTS), D ** -0.5),
        'b_router_expert': nrm((DEPTH, N_EXPERTS), 0.01),
        'w_e1': nrm((DEPTH, N_EXPERTS, D, D_EXPERT), D ** -0.5),
        'w_e3': nrm((DEPTH, N_EXPERTS, D, D_EXPERT), D ** -0.5),
        'w_e2': nrm((DEPTH, N_EXPERTS, D_EXPERT, D), D_EXPERT ** -0.5),
    }


def reference(x_prompt, x_sample, c, cache_diff_k, cache_diff_v, cache_swa_k, cache_swa_v, c_ctx,
              w_ada, b_ada, g_pre_mix, g_post_mix, g_pre_ffn, g_post_ffn, w_in,
              lam_q1, lam_k1, lam_q2, lam_k2, g_diff_head, sink, w_proj_a, w_proj_b, w_out,
              w_router_group, b_router_group, w_router_expert, b_router_expert, w_e1, w_e3, w_e2):
    n_lat = x_sample.shape[1]
    rope_a = axial_rope(n_lat, A_HEAD_DIM)
    rope_b = axial_rope(n_lat, B_HEAD_DIM)
    y_p, y_s = x_prompt, x_sample
    ka_l, va_l, kb_l, vb_l = [], [], [], []
    for l in range(DEPTH):
        p = {
            'g_pre_mix': g_pre_mix[l], 'g_post_mix': g_post_mix[l], 'g_pre_ffn': g_pre_ffn[l],
            'g_post_ffn': g_post_ffn[l], 'w_in': w_in[l], 'lam_q1': lam_q1[l], 'lam_k1': lam_k1[l],
            'lam_q2': lam_q2[l], 'lam_k2': lam_k2[l], 'g_diff_head': g_diff_head[l], 'sink': sink[l],
            'w_proj_a': w_proj_a[l], 'w_proj_b': w_proj_b[l], 'w_out': w_out[l],
            'w_router_group': w_router_group[l], 'b_router_group': b_router_group[l],
            'w_router_expert': w_router_expert[l], 'b_router_expert': b_router_expert[l],
            'w_e1': w_e1[l], 'w_e3': w_e3[l], 'w_e2': w_e2[l],
        }
        lambda_init = 0.8 - 0.6 * math.exp(-0.3 * l)
        mod_ctx = adaln(c_ctx, w_ada[l], b_ada[l])
        mod_lat = adaln(c, w_ada[l], b_ada[l])
        y_p, (ka, va, kb, vb) = context_layer(y_p, mod_ctx, p, lambda_init)
        ka_l.append(ka)
        va_l.append(va)
        kb_l.append(kb)
        vb_l.append(vb)
        ctx_kv = (cache_diff_k[:, l], cache_diff_v[:, l], cache_swa_k[:, l], cache_swa_v[:, l])
        y_s = latent_layer(y_s, mod_lat, ctx_kv, rope_a, rope_b, p, lambda_init)
    new_diff_k = jnp.stack(ka_l, axis=1)
    new_diff_v = jnp.stack(va_l, axis=1)
    new_swa_k = jnp.stack(kb_l, axis=1)
    new_swa_v = jnp.stack(vb_l, axis=1)
    return (y_p, y_s, new_diff_k, new_diff_v, new_swa_k, new_swa_v)
```

```python
import functools
import math

import jax
import jax.numpy as jnp
from jax import lax
from jax.experimental import pallas as pl
from jax.experimental.pallas import tpu as pltpu

F32 = jnp.float32
BF16 = jnp.bfloat16
HIGHEST = lax.Precision.HIGHEST

GRID_W = 64
ROPE_BASE = 10000.0
EPS = 1e-6
NEG_INF = -1e30
A_HEADS = 4
A_HEAD_DIM = 64
A_V_DIM = 2 * A_HEAD_DIM
B_HEADS = 8
B_KV_HEADS = 2
B_GROUP = B_HEADS // B_KV_HEADS
B_HEAD_DIM = 64
WINDOW = 128
N_GROUPS = 4
EXPERTS_PER_GROUP = 8
N_EXPERTS = N_GROUPS * EXPERTS_PER_GROUP
TOP_K = 2

LANES = 128
MOD_ROWS = 16
MOE_ROWS = 256
SWA_Q = 128

_QA = 0
_KA = _QA + A_HEADS * 2 * A_HEAD_DIM
_VA = _KA + A_HEADS * 2 * A_HEAD_DIM
_QB = _VA + A_HEADS * A_V_DIM
_KB = _QB + B_HEADS * B_HEAD_DIM
_VB = _KB + B_KV_HEADS * B_HEAD_DIM
_GA = _VB + B_KV_HEADS * B_HEAD_DIM


def _rms(x, g):
    return x * lax.rsqrt(jnp.mean(x * x, axis=-1, keepdims=True) + EPS) * g


def _mod_kernel(c_ref, w_ref, b_ref, o_ref):
    c = c_ref[...]
    a = c * jax.nn.sigmoid(c)
    o_ref[...] = jnp.dot(a, w_ref[...], precision=HIGHEST, preferred_element_type=F32) + b_ref[...]


def _modulation(c_all, w_ada, b_ada):
    d, n = w_ada.shape
    tn = 512
    return pl.pallas_call(
        _mod_kernel,
        grid=(n // tn,),
        in_specs=[
            pl.BlockSpec((MOD_ROWS, d), lambda j: (0, 0)),
            pl.BlockSpec((d, tn), lambda j: (0, j)),
            pl.BlockSpec((1, tn), lambda j: (0, j)),
        ],
        out_specs=pl.BlockSpec((MOD_ROWS, tn), lambda j: (0, j)),
        out_shape=jax.ShapeDtypeStruct((MOD_ROWS, n), F32),
        name="modulation",
    )(c_all, w_ada, b_ada)


def _rope128(z, cos, sin_signed, first_half):
    rot = jnp.where(first_half, pltpu.roll(z, 96, 1), pltpu.roll(z, 32, 1))
    return z * cos + rot * sin_signed


def _inproj_kernel(x_ref, sh_ref, sc_ref, g_ref, cos_ref, sin_ref, w_ref, *outs, is_ctx):
    x = x_ref[...]
    h = _rms(x, g_ref[...]) * (1.0 + sc_ref[0]) + sh_ref[0]
    hb = h.astype(BF16)
    lane = lax.broadcasted_iota(jnp.int32, (1, LANES), 1)
    first_half = (lane % 64) < 32
    low = lane < 64

    def seg(lo, hi):
        return jnp.dot(hb, w_ref[:, lo:hi], preferred_element_type=F32)

    def rope(z):
        if is_ctx:
            return z
        cos = cos_ref[...]
        sin = sin_ref[...]
        parts = [_rope128(z[:, j:j + LANES], cos, sin, first_half) for j in range(0, z.shape[1], LANES)]
        return parts[0] if len(parts) == 1 else jnp.concatenate(parts, axis=1)

    if is_ctx:
        qa_o, ka_o, va_o, qb_o, kb2_o, vb_o, sga_o, sgb_o, kaf_o, vaf_o, kbf_o, vbf_o = outs
    else:
        qa_o, ka_o, va_o, qb_o, kb2_o, vb_o, sga_o, sgb_o = outs

    scale = A_HEAD_DIM ** -0.5
    qa_o[...] = (rope(seg(_QA, _KA)) * scale).astype(BF16)
    ka = rope(seg(_KA, _VA))
    ka_o[...] = ka.astype(BF16)
    va = seg(_VA, _QB)
    va_o[...] = va.astype(BF16)
    qb_o[...] = (rope(seg(_QB, _KB)) * (B_HEAD_DIM ** -0.5)).astype(BF16)
    kb = rope(seg(_KB, _VB))
    kb_sw = pltpu.roll(kb, 64, 1)
    kb2_o[:, 0:LANES] = jnp.where(low, kb, kb_sw).astype(BF16)
    kb2_o[:, LANES:2 * LANES] = jnp.where(low, kb_sw, kb).astype(BF16)
    vb = seg(_VB, _GA)
    vb_o[...] = vb.astype(BF16)
    d = x.shape[1]
    sga_o[...] = jax.nn.sigmoid(seg(_GA, _GA + d)).astype(BF16)
    sgb_o[...] = jax.nn.sigmoid(seg(_GA + d, _GA + 2 * d)).astype(BF16)
    if is_ctx:
        kaf_o[...] = ka
        vaf_o[...] = va
        kbf_o[...] = kb
        vbf_o[...] = vb


def _inproj(x2, mod3, mod_row0, g_pre, cos_t, sin_t, w_in_b, *, seq, tm, is_ctx):
    t, d = x2.shape
    per = seq // tm
    n_in = w_in_b.shape[1]
    wa = A_HEADS * 2 * A_HEAD_DIM
    wkb = B_KV_HEADS * B_HEAD_DIM

    def row(i):
        return (i // per) if not is_ctx else 0

    tok = lambda w: pl.BlockSpec((tm, w), lambda i: (i, 0))
    out_shape = [
        jax.ShapeDtypeStruct((t, wa), BF16), jax.ShapeDtypeStruct((t, wa), BF16),
        jax.ShapeDtypeStruct((t, wa), BF16), jax.ShapeDtypeStruct((t, wa), BF16),
        jax.ShapeDtypeStruct((t, 2 * wkb), BF16), jax.ShapeDtypeStruct((t, wkb), BF16),
        jax.ShapeDtypeStruct((t, d), BF16), jax.ShapeDtypeStruct((t, d), BF16),
    ]
    out_specs = [tok(wa), tok(wa), tok(wa), tok(wa), tok(2 * wkb), tok(wkb), tok(d), tok(d)]
    if is_ctx:
        out_shape += [jax.ShapeDtypeStruct((t, wa), F32), jax.ShapeDtypeStruct((t, wa), F32),
                      jax.ShapeDtypeStruct((t, wkb), F32), jax.ShapeDtypeStruct((t, wkb), F32)]
        out_specs += [tok(wa), tok(wa), tok(wkb), tok(wkb)]
    return pl.pallas_call(
        functools.partial(_inproj_kernel, is_ctx=is_ctx),
        grid=(t // tm,),
        in_specs=[
            pl.BlockSpec((tm, d), lambda i: (i, 0)),
            pl.BlockSpec((1, 1, d), lambda i: (mod_row0 + row(i), 0, 0)),
            pl.BlockSpec((1, 1, d), lambda i: (mod_row0 + row(i), 0, 1)),
            pl.BlockSpec((1, d), lambda i: (0, 0)),
            pl.BlockSpec((tm, LANES), lambda i: (i % per, 0)),
            pl.BlockSpec((tm, LANES), lambda i: (i % per, 0)),
            pl.BlockSpec((d, n_in), lambda i: (0, 0)),
        ],
        out_specs=out_specs,
        out_shape=out_shape,
        compiler_params=pltpu.CompilerParams(dimension_semantics=("arbitrary",)),
        name="inproj_ctx" if is_ctx else "inproj_lat",
    )(x2, mod3, mod3, g_pre, cos_t, sin_t, w_in_b)


def _nt(a, b):
    return lax.dot_general(a, b, (((1,), (1,)), ((), ())), preferred_element_type=F32)


def _diff_kernel(lam_ref, g_ref, q_ref, *refs, n_seg, lambda_init):
    kv = refs[:2 * n_seg]
    o_ref = refs[2 * n_seg]
    lp = lam_ref[...]
    lam = (jnp.exp(jnp.sum(lp[0:1] * lp[1:2], axis=-1, keepdims=True))
           - jnp.exp(jnp.sum(lp[2:3] * lp[3:4], axis=-1, keepdims=True)) + lambda_init)
    q = q_ref[0]
    lane = lax.broadcasted_iota(jnp.int32, (1, LANES), 1)
    qm = (q * (lane < 64).astype(BF16), q * (lane >= 64).astype(BF16))
    ks = [kv[2 * j][0].astype(BF16) for j in range(n_seg)]
    vs = [kv[2 * j + 1][0].astype(BF16) for j in range(n_seg)]
    ps, cs = [], []
    for m in range(2):
        s = [_nt(qm[m], k) for k in ks]
        mx = functools.reduce(jnp.maximum, [jnp.max(x, axis=-1, keepdims=True) for x in s])
        p = [jnp.exp(x - mx) for x in s]
        l = functools.reduce(jnp.add, [jnp.sum(x, axis=-1, keepdims=True) for x in p])
        ps.append(p)
        cs.append(1.0 / l)
    c0 = cs[0]
    c1 = cs[1] * lam
    o = None
    for j in range(n_seg):
        a = (ps[0][j] * c0 - ps[1][j] * c1).astype(BF16)
        oj = jnp.dot(a, vs[j], preferred_element_type=F32)
        o = oj if o is None else o + oj
    o = _rms(o, g_ref[...]) * (1.0 - lambda_init)
    o_ref[0] = o.astype(BF16)


def _diff_attention(lam_p, g_head, q, segs, *, tq, lambda_init):
    b, s, w = q.shape
    in_specs = [
        pl.BlockSpec((4, A_HEAD_DIM), lambda bi, h, qi: (0, 0)),
        pl.BlockSpec((1, A_V_DIM), lambda bi, h, qi: (0, 0)),
        pl.BlockSpec((1, tq, LANES), lambda bi, h, qi: (bi, qi, h)),
    ]
    args = [lam_p, g_head, q]
    for k, v in segs:
        nk = k.shape[1]
        in_specs += [pl.BlockSpec((1, nk, LANES), lambda bi, h, qi: (bi, 0, h)),
                     pl.BlockSpec((1, nk, LANES), lambda bi, h, qi: (bi, 0, h))]
        args += [k, v]
    return pl.pallas_call(
        functools.partial(_diff_kernel, n_seg=len(segs), lambda_init=lambda_init),
        grid=(b, A_HEADS, s // tq),
        in_specs=in_specs,
        out_specs=pl.BlockSpec((1, tq, LANES), lambda bi, h, qi: (bi, qi, h)),
        out_shape=jax.ShapeDtypeStruct((b, s, w), BF16),
        compiler_params=pltpu.CompilerParams(
            dimension_semantics=("arbitrary", "arbitrary", "arbitrary")),
        name="diff_attn_%d" % len(segs),
    )(*args)


def _pair_lanes(x, n, half):
    return jnp.where(half == n, x, pltpu.roll(x, 64, 1))


def _swa_kernel(sink_ref, q_ref, *refs, tq, seg_kinds):
    n_seg = len(seg_kinds)
    k_refs = refs[:n_seg]
    v_refs = refs[n_seg:2 * n_seg]
    o_ref = refs[2 * n_seg]
    n = pl.program_id(1)
    i = pl.program_id(2)
    nb = pl.num_programs(2)
    lane = lax.broadcasted_iota(jnp.int32, (1, LANES), 1)
    half = lane // 64
    lane2 = lax.broadcasted_iota(jnp.int32, (1, 2 * LANES), 1)
    q = q_ref[0]
    qs = jnp.concatenate([q * ((lane2 // 64) == g).astype(BF16) for g in range(B_GROUP)], axis=0)

    ks, vs, masks = [], [], []
    qi = lax.broadcasted_iota(jnp.int32, (B_GROUP * tq, SWA_Q), 0) & (tq - 1)
    kj = lax.broadcasted_iota(jnp.int32, (B_GROUP * tq, SWA_Q), 1)
    far = 2 * SWA_Q
    for kind, k_ref, v_ref in zip(seg_kinds, k_refs, v_refs):
        k = k_ref[0]
        if kind == "cache":
            k2 = _pair_lanes(k.astype(F32), n, half).astype(BF16)
        else:
            k2 = k
        ks.append(jnp.concatenate([k2, k2], axis=1))
        vs.append(v_ref[0].astype(BF16))
        if kind == "left":
            masks.append(kj >= qi + jnp.where(i > 0, 0, far))
        elif kind == "right":
            masks.append(kj <= qi - jnp.where(i < nb - 1, 0, far))
        else:
            masks.append(None)

    sinkcol = jnp.concatenate(
        [jnp.full((tq, 1), sink_ref[n * B_GROUP + g], F32) for g in range(B_GROUP)], axis=0)
    s = []
    for k, msk in zip(ks, masks):
        sj = _nt(qs, k)
        if msk is not None:
            sj = jnp.where(msk, sj, NEG_INF)
        s.append(sj)
    mx = functools.reduce(jnp.maximum, [jnp.max(x, axis=-1, keepdims=True) for x in s] + [sinkcol])
    p = [jnp.exp(x - mx) for x in s]
    l = functools.reduce(jnp.add, [jnp.sum(x, axis=-1, keepdims=True) for x in p]) + jnp.exp(sinkcol - mx)
    o = None
    for pj, v in zip(p, vs):
        oj = jnp.dot(pj.astype(BF16), v, preferred_element_type=F32)
        o = oj if o is None else o + oj
    o = o * (1.0 / l)
    on = _pair_lanes(o, n, half)
    low = lane < 64
    for j in range(B_GROUP // 2):
        a = on[(2 * j) * tq:(2 * j + 1) * tq]
        bb = on[(2 * j + 1) * tq:(2 * j + 2) * tq]
        o_ref[0, :, j * LANES:(j + 1) * LANES] = jnp.where(low, a, bb).astype(BF16)


def _swa_attention(sink, q, segs, *, tq):
    b, s, w = q.shape
    gw = B_GROUP * B_HEAD_DIM
    in_specs = [
        pl.BlockSpec(memory_space=pltpu.SMEM),
        pl.BlockSpec((1, tq, gw), lambda bi, n, i: (bi, i, n)),
    ]
    k_specs, v_specs, k_args, v_args, kinds = [], [], [], [], []
    for kind, k, v, rows, idx in segs:
        if kind == "cache":
            k_specs.append(pl.BlockSpec((1, rows, LANES), lambda bi, n, i, idx=idx: (bi, idx(i), 0)))
        else:
            k_specs.append(pl.BlockSpec((1, rows, LANES), lambda bi, n, i, idx=idx: (bi, idx(i), n)))
        v_specs.append(pl.BlockSpec((1, rows, LANES), lambda bi, n, i, idx=idx: (bi, idx(i), 0)))
        k_args.append(k)
        v_args.append(v)
        kinds.append(kind)
    return pl.pallas_call(
        functools.partial(_swa_kernel, tq=tq, seg_kinds=tuple(kinds)),
        grid=(b, B_KV_HEADS, s // tq),
        in_specs=in_specs + k_specs + v_specs,
        out_specs=pl.BlockSpec((1, tq, gw), lambda bi, n, i: (bi, i, n)),
        out_shape=jax.ShapeDtypeStruct((b, s, w), BF16),
        compiler_params=pltpu.CompilerParams(
            dimension_semantics=("arbitrary", "arbitrary", "arbitrary")),
        name="swa_attn_%d" % len(segs),
    )(sink, q, *k_args, *v_args)


def _postmix_kernel(x_ref, oa_ref, ob_ref, sga_ref, sgb_ref, g1_ref, sh2_ref, sc2_ref, gpm_ref, gpf_ref,
                    wpa_ref, wpb_ref, wo_ref, wr_ref, br_ref, x1_o, h2_o, route_o):
    pa = jnp.dot(oa_ref[...], wpa_ref[...], preferred_element_type=F32)
    pb = jnp.dot(ob_ref[...], wpb_ref[...], preferred_element_type=F32)
    mix = sga_ref[...].astype(F32) * pa + sgb_ref[...].astype(F32) * pb
    m2 = jnp.dot(mix.astype(BF16), wo_ref[...], preferred_element_type=F32)
    x1 = x_ref[...] + g1_ref[0] * _rms(m2, gpm_ref[...])
    x1_o[...] = x1
    h2 = _rms(x1, gpf_ref[...]) * (1.0 + sc2_ref[0]) + sh2_ref[0]
    h2_o[...] = h2

    logits = jnp.dot(h2, wr_ref[...], precision=HIGHEST, preferred_element_type=F32) + br_ref[...]
    tm = logits.shape[0]
    lane_i = lax.broadcasted_iota(jnp.int32, (tm, LANES), 1)
    lane = lane_i.astype(F32)
    big = float(LANES)
    is_g = lane_i < N_GROUPS
    lg = jnp.where(is_g, logits, -jnp.inf)
    mg = jnp.max(lg, axis=-1, keepdims=True)
    g_sel = jnp.min(jnp.where(is_g & (logits == mg), lane, big), axis=-1, keepdims=True)
    g_w = 1.0 / jnp.sum(jnp.where(is_g, jnp.exp(lg - mg), 0.0), axis=-1, keepdims=True)
    lane_grp = lax.shift_right_arithmetic(lane_i - N_GROUPS, 3).astype(F32)
    in_grp = (lane_grp == g_sel) & (lane_i < N_GROUPS + N_EXPERTS)
    le = jnp.where(in_grp, logits, -jnp.inf)
    v0 = jnp.max(le, axis=-1, keepdims=True)
    i0 = jnp.min(jnp.where(in_grp & (logits == v0), lane, big), axis=-1, keepdims=True)
    rest = in_grp & (lane != i0)
    le1 = jnp.where(rest, logits, -jnp.inf)
    v1 = jnp.max(le1, axis=-1, keepdims=True)
    i1 = jnp.min(jnp.where(rest & (logits == v1), lane, big), axis=-1, keepdims=True)
    e = jnp.exp(v1 - v0)
    w0 = g_w / (1.0 + e)
    w1 = g_w * e / (1.0 + e)
    e0 = i0 - N_GROUPS
    e1 = i1 - N_GROUPS
    route_o[...] = jnp.where(lane_i == 0, e0, jnp.where(lane_i == 1, e1,
                             jnp.where(lane_i == 2, w0, jnp.where(lane_i == 3, w1, 0.0))))


def _postmix(x2, oa, ob, sga, sgb, mod3, mod_row0, gpm, gpf, wpa, wpb, wo, wr, br, *, seq, tm, is_ctx):
    t, d = x2.shape
    per = seq // tm

    def row(i):
        return (i // per) if not is_ctx else 0

    tok = lambda w: pl.BlockSpec((tm, w), lambda i: (i, 0))
    full = lambda a: pl.BlockSpec(a.shape, lambda i: (0,) * a.ndim)
    modspec = lambda c: pl.BlockSpec((1, 1, d), lambda i: (mod_row0 + row(i), 0, c))
    return pl.pallas_call(
        _postmix_kernel,
        grid=(t // tm,),
        in_specs=[tok(d), tok(oa.shape[1]), tok(ob.shape[1]), tok(d), tok(d),
                  modspec(2), modspec(3), modspec(4), full(gpm), full(gpf),
                  full(wpa), full(wpb), full(wo), full(wr), full(br)],
        out_specs=[tok(d), tok(d), tok(LANES)],
        out_shape=[jax.ShapeDtypeStruct((t, d), F32), jax.ShapeDtypeStruct((t, d), F32),
                   jax.ShapeDtypeStruct((t, LANES), F32)],
        compiler_params=pltpu.CompilerParams(dimension_semantics=("arbitrary",)),
        name="postmix_ctx" if is_ctx else "postmix_lat",
    )(x2, oa, ob, sga, sgb, mod3, mod3, mod3, gpm, gpf, wpa, wpb, wo, wr, br)


def _row_copy(src_hbm, dst, sem, src_row, dst_row):
    return pltpu.make_async_copy(src_hbm.at[pl.ds(src_row, 1)], dst.at[pl.ds(dst_row, 1)], sem)


def _moe_kernel(be_ref, src0_ref, srcn_ref, dst_ref, h_hbm, w1_ref, w3_ref, w2_ref, y_hbm,
                xbuf, obuf, w1b, w3b, w2b, gsem, ssem):
    i = pl.program_id(0)
    nb = pl.num_programs(0)
    slot = i % 2
    nslot = 1 - slot
    rows = xbuf.shape[1]

    def gather(idx_ref, buf_slot):
        def body(r, c):
            pltpu.make_async_copy(h_hbm.at[pl.ds(idx_ref[0, 0, r], 1)],
                                  xbuf.at[buf_slot, pl.ds(r, 1)], gsem.at[buf_slot]).start()
            return c
        lax.fori_loop(0, rows, body, 0, unroll=8)

    @pl.when(i == 0)
    def _():
        gather(src0_ref, 0)

    @pl.when(i + 1 < nb)
    def _():
        gather(srcn_ref, nslot)

    changed = jnp.logical_or(i == 0, be_ref[i] != be_ref[jnp.maximum(i - 1, 0)])

    @pl.when(changed)
    def _():
        w1b[...] = w1_ref[0].astype(BF16)
        w3b[...] = w3_ref[0].astype(BF16)
        w2b[...] = w2_ref[0].astype(BF16)

    pltpu.make_async_copy(h_hbm.at[pl.ds(0, rows)], xbuf.at[slot], gsem.at[slot]).wait()

    @pl.when(i >= 2)
    def _():
        pltpu.make_async_copy(obuf.at[slot], y_hbm.at[pl.ds(0, rows)], ssem.at[slot]).wait()

    x = xbuf[slot].astype(BF16)
    a = jnp.dot(x, w1b[...], preferred_element_type=F32)
    b = jnp.dot(x, w3b[...], preferred_element_type=F32)
    hmid = (a * jax.nn.sigmoid(a) * b).astype(BF16)
    obuf[slot] = jnp.dot(hmid, w2b[...], preferred_element_type=F32)

    def scatter(r, c):
        pltpu.make_async_copy(obuf.at[slot, pl.ds(r, 1)],
                              y_hbm.at[pl.ds(dst_ref[0, 0, r], 1)], ssem.at[slot]).start()
        return c
    lax.fori_loop(0, rows, scatter, 0, unroll=8)

    @pl.when(i == nb - 1)
    def _():
        pltpu.make_async_copy(obuf.at[slot], y_hbm.at[pl.ds(0, rows)], ssem.at[slot]).wait()

        @pl.when(nb >= 2)
        def _():
            pltpu.make_async_copy(obuf.at[nslot], y_hbm.at[pl.ds(0, rows)], ssem.at[nslot]).wait()


def _moe(blk_expert, src_tok, dst_row, h2, w1, w3, w2, n_out_rows):
    nblk = blk_expert.shape[0]
    t, d = h2.shape
    de = w1.shape[2]
    rows = MOE_ROWS
    grid_spec = pltpu.PrefetchScalarGridSpec(
        num_scalar_prefetch=1,
        grid=(nblk,),
        in_specs=[
            pl.BlockSpec((1, 1, rows), lambda i, be: (0, 0, 0), memory_space=pltpu.SMEM),
            pl.BlockSpec((1, 1, rows), lambda i, be: (jnp.minimum(i + 1, nblk - 1), 0, 0), memory_space=pltpu.SMEM),
            pl.BlockSpec((1, 1, rows), lambda i, be: (i, 0, 0), memory_space=pltpu.SMEM),
            pl.BlockSpec(memory_space=pl.ANY),
            pl.BlockSpec((1, d, de), lambda i, be: (be[i], 0, 0)),
            pl.BlockSpec((1, d, de), lambda i, be: (be[i], 0, 0)),
            pl.BlockSpec((1, de, d), lambda i, be: (be[i], 0, 0)),
        ],
        out_specs=pl.BlockSpec(memory_space=pl.ANY),
        scratch_shapes=[
            pltpu.VMEM((2, rows, d), F32),
            pltpu.VMEM((2, rows, d), F32),
            pltpu.VMEM((d, de), BF16),
            pltpu.VMEM((d, de), BF16),
            pltpu.VMEM((de, d), BF16),
            pltpu.SemaphoreType.DMA((2,)),
            pltpu.SemaphoreType.DMA((2,)),
        ],
    )
    return pl.pallas_call(
        _moe_kernel,
        grid_spec=grid_spec,
        out_shape=jax.ShapeDtypeStruct((n_out_rows, d), F32),
        compiler_params=pltpu.CompilerParams(dimension_semantics=("arbitrary",)),
        name="expert_mlp",
    )(blk_expert, src_tok, src_tok, dst_row, h2, w1, w3, w2)


def _combine_kernel(x1_ref, y_ref, r_ref, g2_ref, gpost_ref, o_ref):
    d = x1_ref.shape[1]
    r = r_ref[...]
    w0 = r[:, 2:3]
    w1 = r[:, 3:4]
    y = y_ref[:, 0:d] * w0 + y_ref[:, d:2 * d] * w1
    o_ref[...] = x1_ref[...] + g2_ref[0] * _rms(y, gpost_ref[...])


def _combine(x1, y2, route, mod3, mod_row0, gpost, *, seq, tm, blk_off, is_ctx):
    t, d = x1.shape
    per = seq // tm

    def row(i):
        return (i // per) if not is_ctx else 0

    return pl.pallas_call(
        _combine_kernel,
        grid=(t // tm,),
        in_specs=[
            pl.BlockSpec((tm, d), lambda i: (i, 0)),
            pl.BlockSpec((tm, 2 * d), lambda i: (blk_off + i, 0)),
            pl.BlockSpec((tm, LANES), lambda i: (i, 0)),
            pl.BlockSpec((1, 1, d), lambda i: (mod_row0 + row(i), 0, 5)),
            pl.BlockSpec((1, d), lambda i: (0, 0)),
        ],
        out_specs=pl.BlockSpec((tm, d), lambda i: (i, 0)),
        out_shape=jax.ShapeDtypeStruct((t, d), F32),
        compiler_params=pltpu.CompilerParams(dimension_semantics=("arbitrary",)),
        name="combine_ctx" if is_ctx else "combine_lat",
    )(x1, y2, route, mod3, gpost)


def _routing_tables(route, rows):
    t = route.shape[0]
    n_asg = t * TOP_K
    eid = route[:, 0:TOP_K].astype(jnp.int32).reshape(n_asg)
    packed = eid * 65536 + jnp.arange(n_asg, dtype=jnp.int32)
    sorted_a = jnp.sort(packed) & 0xFFFF
    counts = jnp.sum((eid[:, None] == jnp.arange(N_EXPERTS, dtype=jnp.int32)[None, :]).astype(jnp.int32), axis=0)
    nblk_e = (counts + rows - 1) // rows
    blk_end = jnp.cumsum(nblk_e)
    blk_start = blk_end - nblk_e
    start = jnp.cumsum(counts) - counts
    n_blocks = n_asg // rows + N_EXPERTS
    blk_expert = jnp.minimum(
        jnp.searchsorted(blk_end, jnp.arange(n_blocks, dtype=jnp.int32), side="right"), N_EXPERTS - 1).astype(jnp.int32)
    slot = jnp.arange(n_blocks * rows, dtype=jnp.int32)
    sb = slot // rows
    se = blk_expert[sb]
    off = slot - blk_start[se] * rows
    valid = off < counts[se]
    a_slot = jnp.where(valid, sorted_a[jnp.clip(start[se] + off, 0, n_asg - 1)], 0)
    src_tok = (a_slot // TOP_K).reshape(n_blocks, 1, rows)
    trash = n_asg + (sb % 2) * rows + (slot % rows)
    dst_row = jnp.where(valid, a_slot, trash).reshape(n_blocks, 1, rows)
    return blk_expert, src_tok, dst_row, n_asg + 2 * rows


def _rope_tables(n_tok):
    n_rows = n_tok // GRID_W
    rows = jnp.repeat(jnp.arange(n_rows), GRID_W).astype(F32)
    cols = jnp.tile(jnp.arange(GRID_W), n_rows).astype(F32)
    quarter = A_HEAD_DIM // 4
    inv = ROPE_BASE ** (-jnp.arange(quarter, dtype=F32) / quarter)
    ang = jnp.concatenate([rows[:, None] * inv, cols[:, None] * inv], axis=-1)
    cos, sin = jnp.cos(ang), jnp.sin(ang)
    cos_t = jnp.tile(jnp.concatenate([cos, cos], axis=-1), (1, LANES // A_HEAD_DIM))
    sin_t = jnp.tile(jnp.concatenate([-sin, sin], axis=-1), (1, LANES // A_HEAD_DIM))
    return cos_t, sin_t


def kernel(x_prompt, x_sample, c, cache_diff_k, cache_diff_v, cache_swa_k, cache_swa_v, c_ctx, w_ada, b_ada, g_pre_mix, g_post_mix, g_pre_ffn, g_post_ffn, w_in, lam_q1, lam_k1, lam_q2, lam_k2, g_diff_head, sink, w_proj_a, w_proj_b, w_out, w_router_group, b_router_group, w_router_expert, b_router_expert, w_e1, w_e3, w_e2):
    depth = w_in.shape[0]
    assert depth == 1
    l = 0
    bp, sp, d = x_prompt.shape
    bs, ss, _ = x_sample.shape
    lambda_init = 0.8 - 0.6 * math.exp(-0.3 * l)
    assert A_HEAD_DIM == B_HEAD_DIM and ss % GRID_W == 0 and bs + 1 <= MOD_ROWS

    c_all = jnp.concatenate([c_ctx[None, :], c, jnp.zeros((MOD_ROWS - 1 - bs, d), F32)], axis=0)
    mod = _modulation(c_all, w_ada[l], b_ada[l][None, :])
    mod3 = mod.reshape(MOD_ROWS, 1, 6 * d)

    w_in_b = w_in[l].astype(BF16)
    wpa = w_proj_a[l].astype(BF16)
    wpb = w_proj_b[l].astype(BF16)
    wo = w_out[l].astype(BF16)
    n_r = N_GROUPS + N_EXPERTS
    wr = jnp.concatenate([w_router_group[l], w_router_expert[l], jnp.zeros((d, LANES - n_r), F32)], axis=1)
    br = jnp.concatenate([b_router_group[l], b_router_expert[l], jnp.zeros((LANES - n_r,), F32)])[None, :]
    lam_p = jnp.stack([lam_q1[l], lam_k1[l], lam_q2[l], lam_k2[l]], axis=0)
    g_head = g_diff_head[l][None, :]
    sink_l = sink[l]
    cos_t, sin_t = _rope_tables(ss)

    xp2 = x_prompt.reshape(bp * sp, d)
    xs2 = x_sample.reshape(bs * ss, d)
    gpre = g_pre_mix[l][None, :]

    (qa_c, ka_c, va_c, qb_c, kb2_c, vb_c, sga_c, sgb_c, kaf, vaf, kbf, vbf) = _inproj(
        xp2, mod3, 0, gpre, cos_t, sin_t, w_in_b, seq=sp, tm=sp, is_ctx=True)
    r3 = lambda a, b_: a.reshape(b_, -1, a.shape[-1])
    oa_c = _diff_attention(lam_p, g_head, r3(qa_c, bp), [(r3(ka_c, bp), r3(va_c, bp))],
                           tq=sp, lambda_init=lambda_init)
    ob_c = _swa_attention(sink_l, r3(qb_c, bp),
                          [("rep", r3(kb2_c, bp), r3(vb_c, bp), sp, lambda i: 0)], tq=sp)

    (qa_s, ka_s, va_s, qb_s, kb2_s, vb_s, sga_s, sgb_s) = _inproj(
        xs2, mod3, 1, gpre, cos_t, sin_t, w_in_b, seq=ss, tm=512, is_ctx=False)
    past = cache_diff_k.shape[2]
    ck = cache_diff_k[:, l].reshape(bs, past, -1)
    cv = cache_diff_v[:, l].reshape(bs, past, -1)
    oa_s = _diff_attention(lam_p, g_head, r3(qa_s, bs),
                           [(ck, cv), (r3(ka_s, bs), r3(va_s, bs))], tq=256, lambda_init=lambda_init)
    sk = cache_swa_k[:, l].reshape(bs, past, -1)
    sv = cache_swa_v[:, l].reshape(bs, past, -1)
    nqb = ss // SWA_Q
    kb2_3, vb_3 = r3(kb2_s, bs), r3(vb_s, bs)
    ob_s = _swa_attention(sink_l, r3(qb_s, bs), [
        ("cache", sk, sv, past, lambda i: 0),
        ("left", kb2_3, vb_3, SWA_Q, lambda i: jnp.maximum(i - 1, 0)),
        ("center", kb2_3, vb_3, SWA_Q, lambda i: i),
        ("right", kb2_3, vb_3, SWA_Q, lambda i: jnp.minimum(i + 1, nqb - 1)),
    ], tq=SWA_Q)

    gpm = g_post_mix[l][None, :]
    gpf = g_pre_ffn[l][None, :]
    x1_c, h2_c, route_c = _postmix(xp2, oa_c.reshape(bp * sp, -1), ob_c.reshape(bp * sp, -1), sga_c, sgb_c,
                                   mod3, 0, gpm, gpf, wpa, wpb, wo, wr, br, seq=sp, tm=sp, is_ctx=True)
    x1_s, h2_s, route_s = _postmix(xs2, oa_s.reshape(bs * ss, -1), ob_s.reshape(bs * ss, -1), sga_s, sgb_s,
                                   mod3, 1, gpm, gpf, wpa, wpb, wo, wr, br, seq=ss, tm=512, is_ctx=False)

    h2 = jnp.concatenate([h2_c, h2_s], axis=0)
    route = jnp.concatenate([route_c, route_s], axis=0)
    blk_expert, src_tok, dst_row, n_out_rows = _routing_tables(route, MOE_ROWS)
    y2 = _moe(blk_expert, src_tok, dst_row, h2, w_e1[l], w_e3[l], w_e2[l], n_out_rows)
    y2 = y2.reshape(n_out_rows // TOP_K, TOP_K * d)

    gpost = g_post_ffn[l][None, :]
    tmc = 256
    y_p = _combine(x1_c, y2, route_c, mod3, 0, gpost, seq=sp, tm=tmc, blk_off=0, is_ctx=True)
    y_s = _combine(x1_s, y2, route_s, mod3, 1, gpost, seq=ss, tm=tmc, blk_off=(bp * sp) // tmc, is_ctx=False)

    ha = A_HEADS
    return (y_p.reshape(bp, sp, d), y_s.reshape(bs, ss, d),
            kaf.reshape(bp, 1, sp, ha, 2, A_HEAD_DIM), vaf.reshape(bp, 1, sp, ha, A_V_DIM),
            kbf.reshape(bp, 1, sp, B_KV_HEADS, B_HEAD_DIM), vbf.reshape(bp, 1, sp, B_KV_HEADS, B_HEAD_DIM))
```

```python
import functools
import math

import jax
import jax.numpy as jnp
from jax import lax
from jax.experimental import pallas as pl
from jax.experimental.pallas import tpu as pltpu

F32 = jnp.float32
BF16 = jnp.bfloat16
HIGHEST = lax.Precision.HIGHEST

GRID_W = 64
ROPE_BASE = 10000.0
EPS = 1e-6
NEG_INF = -1e30
A_HEADS = 4
A_HEAD_DIM = 64
A_V_DIM = 2 * A_HEAD_DIM
B_HEADS = 8
B_KV_HEADS = 2
B_GROUP = B_HEADS // B_KV_HEADS
B_HEAD_DIM = 64
WINDOW = 128
N_GROUPS = 4
EXPERTS_PER_GROUP = 8
N_EXPERTS = N_GROUPS * EXPERTS_PER_GROUP
TOP_K = 2

LANES = 128
MOD_ROWS = 16
MOE_ROWS = 256
SWA_Q = 128

_QA = 0
_KA = _QA + A_HEADS * 2 * A_HEAD_DIM
_VA = _KA + A_HEADS * 2 * A_HEAD_DIM
_QB = _VA + A_HEADS * A_V_DIM
_KB = _QB + B_HEADS * B_HEAD_DIM
_VB = _KB + B_KV_HEADS * B_HEAD_DIM
_GA = _VB + B_KV_HEADS * B_HEAD_DIM


def _rms(x, g):
    return x * lax.rsqrt(jnp.mean(x * x, axis=-1, keepdims=True) + EPS) * g


def _store_row_tiles(ref, base, val):
    sub = val.shape[1] // LANES
    for s in range(sub):
        ref[pl.ds(base + s, val.shape[0], stride=sub), :] = val[:, s * LANES:(s + 1) * LANES]


def _load_row_tiles(ref, base, n_rows, sub):
    return jnp.concatenate([ref[pl.ds(base + s, n_rows, stride=sub), :] for s in range(sub)], axis=1)


def _mod_kernel(c_ref, w_ref, b_ref, o_ref):
    c = c_ref[...]
    a = c * jax.nn.sigmoid(c)
    o_ref[...] = jnp.dot(a, w_ref[...], precision=HIGHEST, preferred_element_type=F32) + b_ref[...]


def _modulation(c_all, w_ada, b_ada):
    d, n = w_ada.shape
    tn = 512
    return pl.pallas_call(
        _mod_kernel,
        grid=(n // tn,),
        in_specs=[
            pl.BlockSpec((MOD_ROWS, d), lambda j: (0, 0)),
            pl.BlockSpec((d, tn), lambda j: (0, j)),
            pl.BlockSpec((1, tn), lambda j: (0, j)),
        ],
        out_specs=pl.BlockSpec((MOD_ROWS, tn), lambda j: (0, j)),
        out_shape=jax.ShapeDtypeStruct((MOD_ROWS, n), F32),
        name="modulation",
    )(c_all, w_ada, b_ada)


def _rope128(z, cos, sin_signed, first_half):
    rot = jnp.where(first_half, pltpu.roll(z, 96, 1), pltpu.roll(z, 32, 1))
    return z * cos + rot * sin_signed


def _inproj_kernel(x_ref, sh_ref, sc_ref, g_ref, cos_ref, sin_ref, w_ref, *outs, is_ctx):
    x = x_ref[...]
    h = _rms(x, g_ref[...]) * (1.0 + sc_ref[0]) + sh_ref[0]
    hb = h.astype(BF16)
    lane = lax.broadcasted_iota(jnp.int32, (1, LANES), 1)
    first_half = (lane % 64) < 32
    low = lane < 64

    def seg(lo, hi):
        return jnp.dot(hb, w_ref[:, lo:hi], preferred_element_type=F32)

    def rope(z):
        if is_ctx:
            return z
        cos = cos_ref[...]
        sin = sin_ref[...]
        parts = [_rope128(z[:, j:j + LANES], cos, sin, first_half) for j in range(0, z.shape[1], LANES)]
        return parts[0] if len(parts) == 1 else jnp.concatenate(parts, axis=1)

    if is_ctx:
        qa_o, ka_o, va_o, qb_o, kb2_o, vb_o, sga_o, sgb_o, kaf_o, vaf_o, kbf_o, vbf_o = outs
    else:
        qa_o, ka_o, va_o, qb_o, kb2_o, vb_o, sga_o, sgb_o = outs

    scale = A_HEAD_DIM ** -0.5
    qa_o[...] = (rope(seg(_QA, _KA)) * scale).astype(BF16)
    ka = rope(seg(_KA, _VA))
    ka_o[...] = ka.astype(BF16)
    va = seg(_VA, _QB)
    va_o[...] = va.astype(BF16)
    qb_o[...] = (rope(seg(_QB, _KB)) * (B_HEAD_DIM ** -0.5)).astype(BF16)
    kb = rope(seg(_KB, _VB))
    kb_sw = pltpu.roll(kb, 64, 1)
    kb2_o[:, 0:LANES] = jnp.where(low, kb, kb_sw).astype(BF16)
    kb2_o[:, LANES:2 * LANES] = jnp.where(low, kb_sw, kb).astype(BF16)
    vb = seg(_VB, _GA)
    vb_o[...] = vb.astype(BF16)
    d = x.shape[1]
    sga_o[...] = jax.nn.sigmoid(seg(_GA, _GA + d)).astype(BF16)
    sgb_o[...] = jax.nn.sigmoid(seg(_GA + d, _GA + 2 * d)).astype(BF16)
    if is_ctx:
        kaf_o[...] = ka
        vaf_o[...] = va
        kbf_o[...] = kb
        vbf_o[...] = vb


def _inproj(x2, mod3, mod_row0, g_pre, cos_t, sin_t, w_in_b, *, seq, tm, is_ctx):
    t, d = x2.shape
    per = seq // tm
    n_in = w_in_b.shape[1]
    wa = A_HEADS * 2 * A_HEAD_DIM
    wkb = B_KV_HEADS * B_HEAD_DIM

    def row(i):
        return (i // per) if not is_ctx else 0

    tok = lambda w: pl.BlockSpec((tm, w), lambda i: (i, 0))
    out_shape = [
        jax.ShapeDtypeStruct((t, wa), BF16), jax.ShapeDtypeStruct((t, wa), BF16),
        jax.ShapeDtypeStruct((t, wa), BF16), jax.ShapeDtypeStruct((t, wa), BF16),
        jax.ShapeDtypeStruct((t, 2 * wkb), BF16), jax.ShapeDtypeStruct((t, wkb), BF16),
        jax.ShapeDtypeStruct((t, d), BF16), jax.ShapeDtypeStruct((t, d), BF16),
    ]
    out_specs = [tok(wa), tok(wa), tok(wa), tok(wa), tok(2 * wkb), tok(wkb), tok(d), tok(d)]
    if is_ctx:
        out_shape += [jax.ShapeDtypeStruct((t, wa), F32), jax.ShapeDtypeStruct((t, wa), F32),
                      jax.ShapeDtypeStruct((t, wkb), F32), jax.ShapeDtypeStruct((t, wkb), F32)]
        out_specs += [tok(wa), tok(wa), tok(wkb), tok(wkb)]
    return pl.pallas_call(
        functools.partial(_inproj_kernel, is_ctx=is_ctx),
        grid=(t // tm,),
        in_specs=[
            pl.BlockSpec((tm, d), lambda i: (i, 0)),
            pl.BlockSpec((1, 1, d), lambda i: (mod_row0 + row(i), 0, 0)),
            pl.BlockSpec((1, 1, d), lambda i: (mod_row0 + row(i), 0, 1)),
            pl.BlockSpec((1, d), lambda i: (0, 0)),
            pl.BlockSpec((tm, LANES), lambda i: (i % per, 0)),
            pl.BlockSpec((tm, LANES), lambda i: (i % per, 0)),
            pl.BlockSpec((d, n_in), lambda i: (0, 0)),
        ],
        out_specs=out_specs,
        out_shape=out_shape,
        compiler_params=pltpu.CompilerParams(dimension_semantics=("arbitrary",)),
        name="inproj_ctx" if is_ctx else "inproj_lat",
    )(x2, mod3, mod3, g_pre, cos_t, sin_t, w_in_b)


def _nt(a, b):
    return lax.dot_general(a, b, (((1,), (1,)), ((), ())), preferred_element_type=F32)


def _diff_kernel(lam_ref, g_ref, q_ref, *refs, n_seg, lambda_init):
    kv = refs[:2 * n_seg]
    o_ref = refs[2 * n_seg]
    lp = lam_ref[...]
    lam = (jnp.exp(jnp.sum(lp[0:1] * lp[1:2], axis=-1, keepdims=True))
           - jnp.exp(jnp.sum(lp[2:3] * lp[3:4], axis=-1, keepdims=True)) + lambda_init)
    q = q_ref[0]
    tq = q.shape[0]
    lane = lax.broadcasted_iota(jnp.int32, (1, LANES), 1)
    q2 = jnp.concatenate([q * (lane < 64).astype(BF16), q * (lane >= 64).astype(BF16)], axis=0)
    s = [_nt(q2, kv[2 * j][0].astype(BF16)) for j in range(n_seg)]
    mx = functools.reduce(jnp.maximum, [jnp.max(x, axis=-1, keepdims=True) for x in s])
    acc = None
    for j in range(n_seg):
        v = kv[2 * j + 1][0].astype(BF16)
        v_ext = jnp.concatenate([v, jnp.ones_like(v)], axis=1)
        aj = jnp.dot(jnp.exp(s[j] - mx).astype(BF16), v_ext, preferred_element_type=F32)
        acc = aj if acc is None else acc + aj
    on = acc[:, 0:LANES] / acc[:, LANES:2 * LANES]
    o = on[0:tq] - lam * on[tq:2 * tq]
    o = _rms(o, g_ref[...]) * (1.0 - lambda_init)
    o_ref[0] = o.astype(BF16)


def _diff_attention(lam_p, g_head, q, segs, *, tq, lambda_init):
    b, s, w = q.shape
    in_specs = [
        pl.BlockSpec((4, A_HEAD_DIM), lambda bi, h, qi: (0, 0)),
        pl.BlockSpec((1, A_V_DIM), lambda bi, h, qi: (0, 0)),
        pl.BlockSpec((1, tq, LANES), lambda bi, h, qi: (bi, qi, h)),
    ]
    args = [lam_p, g_head, q]
    for k, v in segs:
        nk = k.shape[1]
        in_specs += [pl.BlockSpec((1, nk, LANES), lambda bi, h, qi: (bi, 0, h)),
                     pl.BlockSpec((1, nk, LANES), lambda bi, h, qi: (bi, 0, h))]
        args += [k, v]
    return pl.pallas_call(
        functools.partial(_diff_kernel, n_seg=len(segs), lambda_init=lambda_init),
        grid=(b, A_HEADS, s // tq),
        in_specs=in_specs,
        out_specs=pl.BlockSpec((1, tq, LANES), lambda bi, h, qi: (bi, qi, h)),
        out_shape=jax.ShapeDtypeStruct((b, s, w), BF16),
        compiler_params=pltpu.CompilerParams(
            dimension_semantics=("arbitrary", "arbitrary", "arbitrary")),
        name="diff_attn_%d" % len(segs),
    )(*args)


def _swa_kernel(sink_ref, q_ref, *refs, tq, seg_kinds):
    n_seg = len(seg_kinds)
    k_refs = refs[:n_seg]
    v_refs = refs[n_seg:2 * n_seg]
    o_ref = refs[2 * n_seg]
    i = pl.program_id(1)
    nb = pl.num_programs(1)
    lane = lax.broadcasted_iota(jnp.int32, (1, LANES), 1)
    low = lane < 64
    lane2 = lax.broadcasted_iota(jnp.int32, (1, 2 * LANES), 1)
    head_masks = [((lane2 // 64) == g).astype(BF16) for g in range(B_GROUP)]
    qi = lax.broadcasted_iota(jnp.int32, (B_GROUP * tq, SWA_Q), 0) & (tq - 1)
    kj = lax.broadcasted_iota(jnp.int32, (B_GROUP * tq, SWA_Q), 1)
    far = 2 * SWA_Q
    gw = B_GROUP * B_HEAD_DIM

    vs = [v_ref[0].astype(BF16) for v_ref in v_refs]
    v_all = vs[0] if n_seg == 1 else jnp.concatenate(vs, axis=0)
    v_ext = jnp.concatenate([v_all, jnp.ones_like(v_all)], axis=1)

    for n in range(B_KV_HEADS):
        q = q_ref[0, :, n * gw:(n + 1) * gw]
        qs = jnp.concatenate([q * hm for hm in head_masks], axis=0)
        ks = []
        for kind, k_ref in zip(seg_kinds, k_refs):
            if kind == "cache":
                k = k_ref[0]
                sw = pltpu.roll(k, 64, 1)
                k2 = (jnp.where(low, k, sw) if n == 0 else jnp.where(low, sw, k)).astype(BF16)
            else:
                k2 = k_ref[0, :, n * LANES:(n + 1) * LANES]
            ks.append(jnp.concatenate([k2, k2], axis=1))
        k_all = ks[0] if n_seg == 1 else jnp.concatenate(ks, axis=0)
        s = _nt(qs, k_all)
        chunks, col = [], 0
        for kind, k in zip(seg_kinds, ks):
            for c in range(0, k.shape[0], LANES):
                sc = s[:, col + c:col + c + LANES]
                if kind == "left":
                    sc = jnp.where(kj >= qi + jnp.where(i > 0, 0, far), sc, NEG_INF)
                elif kind == "right":
                    sc = jnp.where(kj <= qi - jnp.where(i < nb - 1, 0, far), sc, NEG_INF)
                chunks.append(sc)
            col += k.shape[0]
        sinkcol = jnp.concatenate(
            [jnp.full((tq, 1), sink_ref[n * B_GROUP + g], F32) for g in range(B_GROUP)], axis=0)
        mx = jnp.maximum(jnp.max(functools.reduce(jnp.maximum, chunks), axis=-1, keepdims=True), sinkcol)
        p = jnp.concatenate([jnp.exp(c - mx).astype(BF16) for c in chunks], axis=1)
        acc = jnp.dot(p, v_ext, preferred_element_type=F32)
        o = acc[:, 0:LANES] / (acc[:, LANES:2 * LANES] + jnp.exp(sinkcol - mx))
        osw = pltpu.roll(o, 64, 1)
        for j in range(B_GROUP // 2):
            ra = slice((2 * j) * tq, (2 * j + 1) * tq)
            rb = slice((2 * j + 1) * tq, (2 * j + 2) * tq)
            pair = jnp.where(low, o[ra], osw[rb]) if n == 0 else jnp.where(low, osw[ra], o[rb])
            o_ref[0, :, n * gw + j * LANES:n * gw + (j + 1) * LANES] = pair.astype(BF16)


def _swa_attention(sink, q, segs, *, tq):
    b, s, w = q.shape
    in_specs = [
        pl.BlockSpec(memory_space=pltpu.SMEM),
        pl.BlockSpec((1, tq, w), lambda bi, i: (bi, i, 0)),
    ]
    k_specs, v_specs, k_args, v_args, kinds = [], [], [], [], []
    for kind, k, v, rows, idx in segs:
        k_specs.append(pl.BlockSpec((1, rows, k.shape[2]), lambda bi, i, idx=idx: (bi, idx(i), 0)))
        v_specs.append(pl.BlockSpec((1, rows, v.shape[2]), lambda bi, i, idx=idx: (bi, idx(i), 0)))
        k_args.append(k)
        v_args.append(v)
        kinds.append(kind)
    return pl.pallas_call(
        functools.partial(_swa_kernel, tq=tq, seg_kinds=tuple(kinds)),
        grid=(b, s // tq),
        in_specs=in_specs + k_specs + v_specs,
        out_specs=pl.BlockSpec((1, tq, w), lambda bi, i: (bi, i, 0)),
        out_shape=jax.ShapeDtypeStruct((b, s, w), BF16),
        compiler_params=pltpu.CompilerParams(dimension_semantics=("arbitrary", "arbitrary")),
        name="swa_attn_%d" % len(segs),
    )(sink, q, *k_args, *v_args)


def _postmix_kernel(*refs, n_ctx_tiles):
    (xc, xl, oac, oal, obc, obl, sgac, sgal, sgbc, sgbl, g1_ref, sh2_ref, sc2_ref, gpm_ref, gpf_ref,
     wpa_ref, wpb_ref, wo_ref, wr_ref, br_ref, x1_o, h2_o, route_o) = refs
    is_ctx = pl.program_id(0) < n_ctx_tiles
    pick = lambda a, b: jnp.where(is_ctx, a[...], b[...])
    pa = jnp.dot(pick(oac, oal), wpa_ref[...], preferred_element_type=F32)
    pb = jnp.dot(pick(obc, obl), wpb_ref[...], preferred_element_type=F32)
    mix = pick(sgac, sgal).astype(F32) * pa + pick(sgbc, sgbl).astype(F32) * pb
    m2 = jnp.dot(mix.astype(BF16), wo_ref[...], preferred_element_type=F32)
    x1 = pick(xc, xl) + g1_ref[0] * _rms(m2, gpm_ref[...])
    x1_o[...] = x1
    h2 = _rms(x1, gpf_ref[...]) * (1.0 + sc2_ref[0]) + sh2_ref[0]
    _store_row_tiles(h2_o, 0, h2)

    logits = jnp.dot(h2, wr_ref[...], precision=HIGHEST, preferred_element_type=F32) + br_ref[...]
    tm = logits.shape[0]
    lane_i = lax.broadcasted_iota(jnp.int32, (tm, LANES), 1)
    lane = lane_i.astype(F32)
    big = float(LANES)
    is_g = lane_i < N_GROUPS
    lg = jnp.where(is_g, logits, -jnp.inf)
    mg = jnp.max(lg, axis=-1, keepdims=True)
    g_sel = jnp.min(jnp.where(is_g & (logits == mg), lane, big), axis=-1, keepdims=True)
    g_w = 1.0 / jnp.sum(jnp.where(is_g, jnp.exp(lg - mg), 0.0), axis=-1, keepdims=True)
    lane_grp = lax.shift_right_arithmetic(lane_i - N_GROUPS, 3).astype(F32)
    in_grp = (lane_grp == g_sel) & (lane_i < N_GROUPS + N_EXPERTS)
    le = jnp.where(in_grp, logits, -jnp.inf)
    v0 = jnp.max(le, axis=-1, keepdims=True)
    i0 = jnp.min(jnp.where(in_grp & (logits == v0), lane, big), axis=-1, keepdims=True)
    rest = in_grp & (lane != i0)
    le1 = jnp.where(rest, logits, -jnp.inf)
    v1 = jnp.max(le1, axis=-1, keepdims=True)
    i1 = jnp.min(jnp.where(rest & (logits == v1), lane, big), axis=-1, keepdims=True)
    e = jnp.exp(v1 - v0)
    w0 = g_w / (1.0 + e)
    w1 = g_w * e / (1.0 + e)
    e0 = i0 - N_GROUPS
    e1 = i1 - N_GROUPS
    route_o[...] = jnp.where(lane_i == 0, e0, jnp.where(lane_i == 1, e1,
                             jnp.where(lane_i == 2, w0, jnp.where(lane_i == 3, w1, 0.0))))


def _postmix(ctx_in, lat_in, mod3, gpm, gpf, wpa, wpb, wo, wr, br, *, lat_seq, tm):
    t_ctx, d = ctx_in[0].shape
    t_lat = lat_in[0].shape[0]
    assert t_ctx % tm == 0 and lat_seq % tm == 0
    nc = t_ctx // tm
    nl = t_lat // tm
    per = lat_seq // tm
    sub = d // LANES
    t_all = t_ctx + t_lat

    mod_row = lambda i: jnp.where(i < nc, 0, 1 + jnp.maximum(i - nc, 0) // per)
    full = lambda a: pl.BlockSpec(a.shape, lambda i: (0,) * a.ndim)
    modspec = lambda c: pl.BlockSpec((1, 1, d), lambda i: (mod_row(i), 0, c))
    in_specs, args = [], []
    for a_c, a_l in zip(ctx_in, lat_in):
        w = a_c.shape[1]
        in_specs += [pl.BlockSpec((tm, w), lambda i: (jnp.minimum(i, nc - 1), 0)),
                     pl.BlockSpec((tm, w), lambda i: (jnp.maximum(i - nc, 0), 0))]
        args += [a_c, a_l]
    in_specs += [modspec(2), modspec(3), modspec(4), full(gpm), full(gpf),
                 full(wpa), full(wpb), full(wo), full(wr), full(br)]
    args += [mod3, mod3, mod3, gpm, gpf, wpa, wpb, wo, wr, br]
    return pl.pallas_call(
        functools.partial(_postmix_kernel, n_ctx_tiles=nc),
        grid=(nc + nl,),
        in_specs=in_specs,
        out_specs=[pl.BlockSpec((tm, d), lambda i: (i, 0)),
                   pl.BlockSpec((tm * sub, LANES), lambda i: (i, 0)),
                   pl.BlockSpec((tm, LANES), lambda i: (i, 0))],
        out_shape=[jax.ShapeDtypeStruct((t_all, d), F32),
                   jax.ShapeDtypeStruct((t_all * sub, LANES), F32),
                   jax.ShapeDtypeStruct((t_all, LANES), F32)],
        compiler_params=pltpu.CompilerParams(dimension_semantics=("arbitrary",)),
        name="postmix",
    )(*args)


def _moe_kernel(be_ref, nact_ref, src0_ref, srcn_ref, dst_ref, h_hbm, w1_ref, w3_ref, w2_ref, y_hbm,
                xbuf, obuf, w1b, w3b, w2b, gsem, ssem, *, rows, sub, trash_row, n_groups):
    i = pl.program_id(0)
    nact = nact_ref[0]
    slot = i % 2
    nslot = 1 - slot
    blk = rows * sub
    per_group = rows // n_groups

    def gather_copy(idx_ref, r, s):
        src = pl.multiple_of(idx_ref[0, 0, r] * sub, sub)
        return pltpu.make_async_copy(h_hbm.at[pl.ds(src, sub)],
                                     xbuf.at[pl.ds(pl.multiple_of(s * blk + r * sub, sub), sub)], gsem.at[s])

    def scatter_copy(r, s, to_trash):
        dst = jnp.where(to_trash, trash_row + r, dst_ref[0, 0, r])
        return pltpu.make_async_copy(obuf.at[pl.ds(pl.multiple_of(s * blk + r * sub, sub), sub)],
                                     y_hbm.at[pl.ds(pl.multiple_of(dst * sub, sub), sub)], ssem.at[s])

    def wait_gathers(s):
        pltpu.make_async_copy(h_hbm.at[pl.ds(0, blk)], xbuf.at[pl.ds(pl.multiple_of(s * blk, blk), blk)],
                              gsem.at[s]).wait()

    def wait_scatters(s):
        pltpu.make_async_copy(obuf.at[pl.ds(pl.multiple_of(s * blk, blk), blk)], y_hbm.at[pl.ds(0, blk)],
                              ssem.at[s]).wait()

    @pl.when(i == 0)
    def _():
        obuf[...] = jnp.zeros_like(obuf)
        for r in range(rows):
            gather_copy(src0_ref, r, 0).start(priority=r % 2)

    @pl.when(i < nact)
    def _():
        changed = jnp.logical_or(i == 0, be_ref[i] != be_ref[jnp.maximum(i - 1, 0)])

        @pl.when(changed)
        def _():
            w1b[...] = w1_ref[0].astype(BF16)
            w3b[...] = w3_ref[0].astype(BF16)
            w2b[...] = w2_ref[0].astype(BF16)

        wait_gathers(slot)
        x = _load_row_tiles(xbuf, slot * blk, rows, sub).astype(BF16)
        first = i == 0

        def dma_group(g):
            for r in range(g * per_group, (g + 1) * per_group):
                gather_copy(srcn_ref, r, nslot).start(priority=0)
                scatter_copy(r, nslot, first).start(priority=1)

        dma_group(0)
        a = jnp.dot(x, w1b[...], preferred_element_type=F32)
        dma_group(1)
        b = jnp.dot(x, w3b[...], preferred_element_type=F32)
        dma_group(2)
        hmid = (a * jax.nn.sigmoid(a) * b).astype(BF16)
        y = jnp.dot(hmid, w2b[...], preferred_element_type=F32)
        dma_group(3)
        _store_row_tiles(obuf, slot * blk, y)
        wait_scatters(nslot)

    @pl.when(i == nact)
    def _():
        wait_gathers(slot)
        for r in range(rows):
            scatter_copy(r, nslot, False).start(priority=r % 2)
        wait_scatters(nslot)


def _moe(blk_expert, nact, src_tok, dst_row, h2t, w1, w3, w2, n_out_rows):
    nblk = src_tok.shape[0]
    d, de = w1.shape[1], w1.shape[2]
    sub = d // LANES
    rows = MOE_ROWS
    last = nblk - 1
    grid_spec = pltpu.PrefetchScalarGridSpec(
        num_scalar_prefetch=2,
        grid=(nblk + 1,),
        in_specs=[
            pl.BlockSpec((1, 1, rows), lambda i, be, na: (0, 0, 0), memory_space=pltpu.SMEM),
            pl.BlockSpec((1, 1, rows), lambda i, be, na: (jnp.minimum(i + 1, last), 0, 0), memory_space=pltpu.SMEM),
            pl.BlockSpec((1, 1, rows), lambda i, be, na: (jnp.maximum(i - 1, 0), 0, 0), memory_space=pltpu.SMEM),
            pl.BlockSpec(memory_space=pl.ANY),
            pl.BlockSpec((1, d, de), lambda i, be, na: (be[i], 0, 0)),
            pl.BlockSpec((1, d, de), lambda i, be, na: (be[i], 0, 0)),
            pl.BlockSpec((1, de, d), lambda i, be, na: (be[i], 0, 0)),
        ],
        out_specs=pl.BlockSpec(memory_space=pl.ANY),
        scratch_shapes=[
            pltpu.VMEM((2 * rows * sub, LANES), F32),
            pltpu.VMEM((2 * rows * sub, LANES), F32),
            pltpu.VMEM((d, de), BF16),
            pltpu.VMEM((d, de), BF16),
            pltpu.VMEM((de, d), BF16),
            pltpu.SemaphoreType.DMA((2,)),
            pltpu.SemaphoreType.DMA((2,)),
        ],
    )
    return pl.pallas_call(
        functools.partial(_moe_kernel, rows=rows, sub=sub, trash_row=n_out_rows - rows, n_groups=4),
        grid_spec=grid_spec,
        out_shape=jax.ShapeDtypeStruct((n_out_rows * sub, LANES), F32),
        compiler_params=pltpu.CompilerParams(dimension_semantics=("arbitrary",)),
        name="expert_mlp",
    )(blk_expert, nact, src_tok, src_tok, dst_row, h2t, w1, w3, w2)


def _combine_kernel(x1_ref, y0_ref, y1_ref, r_ref, g2_ref, gpost_ref, o_ref):
    tm, d = x1_ref.shape
    sub = d // LANES
    r = r_ref[...]
    y = (_load_row_tiles(y0_ref, 0, tm, sub) * r[:, 2:3] + _load_row_tiles(y1_ref, 0, tm, sub) * r[:, 3:4])
    o_ref[...] = x1_ref[...] + g2_ref[0] * _rms(y, gpost_ref[...])


def _combine(x1, y2t, route, mod3, mod_row0, gpost, *, t, seq, tm, tok_off, is_ctx):
    t_all, d = x1.shape
    per = seq // tm
    sub = d // LANES
    boff = tok_off // tm
    koff = t_all // tm

    def row(i):
        return (i // per) if not is_ctx else 0

    return pl.pallas_call(
        _combine_kernel,
        grid=(t // tm,),
        in_specs=[
            pl.BlockSpec((tm, d), lambda i: (boff + i, 0)),
            pl.BlockSpec((tm * sub, LANES), lambda i: (boff + i, 0)),
            pl.BlockSpec((tm * sub, LANES), lambda i: (koff + boff + i, 0)),
            pl.BlockSpec((tm, LANES), lambda i: (boff + i, 0)),
            pl.BlockSpec((1, 1, d), lambda i: (mod_row0 + row(i), 0, 5)),
            pl.BlockSpec((1, d), lambda i: (0, 0)),
        ],
        out_specs=pl.BlockSpec((tm, d), lambda i: (i, 0)),
        out_shape=jax.ShapeDtypeStruct((t, d), F32),
        compiler_params=pltpu.CompilerParams(dimension_semantics=("arbitrary",)),
        name="combine_ctx" if is_ctx else "combine_lat",
    )(x1, y2t, y2t, route, mod3, gpost)


def _routing_tables(route, rows):
    t = route.shape[0]
    n_asg = t * TOP_K
    assert n_asg < 0xFFFF and n_asg % rows == 0
    pad_id = 0xFFFF
    eid = route[:, 0:TOP_K].astype(jnp.int32).reshape(n_asg)
    ex = jnp.arange(N_EXPERTS, dtype=jnp.int32)
    counts = jnp.sum((eid[:, None] == ex[None, :]).astype(jnp.int32), axis=0)
    npad = (-counts) % rows
    real = eid * 65536 + jnp.arange(n_asg, dtype=jnp.int32)
    cand = jnp.where(jnp.arange(rows, dtype=jnp.int32)[None, :] < npad[:, None],
                     ex[:, None] * 65536 + pad_id, jnp.int32(0x7FFFFFFF))
    keys = jnp.sort(jnp.concatenate([real, cand.reshape(-1)]))
    n_blocks = n_asg // rows + N_EXPERTS
    a = keys & 0xFFFF
    is_pad = a == pad_id
    pos = jnp.arange(n_blocks * rows, dtype=jnp.int32) % rows
    src_tok = jnp.where(is_pad, 0, a // TOP_K).reshape(n_blocks, 1, rows)
    dst_row = jnp.where(is_pad, n_asg + pos, (a % TOP_K) * t + a // TOP_K).reshape(n_blocks, 1, rows)
    blk_expert = jnp.minimum(keys.reshape(n_blocks, rows)[:, 0] >> 16, N_EXPERTS - 1)
    blk_expert = jnp.concatenate([blk_expert, blk_expert[-1:]])
    nact = (jnp.sum(counts + npad) // rows).reshape(1)
    return blk_expert, nact, src_tok, dst_row, n_asg + rows


def _rope_tables(n_tok):
    n_rows = n_tok // GRID_W
    rows = jnp.repeat(jnp.arange(n_rows), GRID_W).astype(F32)
    cols = jnp.tile(jnp.arange(GRID_W), n_rows).astype(F32)
    quarter = A_HEAD_DIM // 4
    inv = ROPE_BASE ** (-jnp.arange(quarter, dtype=F32) / quarter)
    ang = jnp.concatenate([rows[:, None] * inv, cols[:, None] * inv], axis=-1)
    cos, sin = jnp.cos(ang), jnp.sin(ang)
    cos_t = jnp.tile(jnp.concatenate([cos, cos], axis=-1), (1, LANES // A_HEAD_DIM))
    sin_t = jnp.tile(jnp.concatenate([-sin, sin], axis=-1), (1, LANES // A_HEAD_DIM))
    return cos_t, sin_t


def kernel(x_prompt, x_sample, c, cache_diff_k, cache_diff_v, cache_swa_k, cache_swa_v, c_ctx, w_ada, b_ada, g_pre_mix, g_post_mix, g_pre_ffn, g_post_ffn, w_in, lam_q1, lam_k1, lam_q2, lam_k2, g_diff_head, sink, w_proj_a, w_proj_b, w_out, w_router_group, b_router_group, w_router_expert, b_router_expert, w_e1, w_e3, w_e2):
    depth = w_in.shape[0]
    assert depth == 1
    l = 0
    bp, sp, d = x_prompt.shape
    bs, ss, _ = x_sample.shape
    lambda_init = 0.8 - 0.6 * math.exp(-0.3 * l)
    assert A_HEAD_DIM == B_HEAD_DIM and ss % GRID_W == 0 and bs + 1 <= MOD_ROWS

    c_all = jnp.concatenate([c_ctx[None, :], c, jnp.zeros((MOD_ROWS - 1 - bs, d), F32)], axis=0)
    mod = _modulation(c_all, w_ada[l], b_ada[l][None, :])
    mod3 = mod.reshape(MOD_ROWS, 1, 6 * d)

    w_in_b = w_in[l].astype(BF16)
    wpa = w_proj_a[l].astype(BF16)
    wpb = w_proj_b[l].astype(BF16)
    wo = w_out[l].astype(BF16)
    n_r = N_GROUPS + N_EXPERTS
    wr = jnp.concatenate([w_router_group[l], w_router_expert[l], jnp.zeros((d, LANES - n_r), F32)], axis=1)
    br = jnp.concatenate([b_router_group[l], b_router_expert[l], jnp.zeros((LANES - n_r,), F32)])[None, :]
    lam_p = jnp.stack([lam_q1[l], lam_k1[l], lam_q2[l], lam_k2[l]], axis=0)
    g_head = g_diff_head[l][None, :]
    sink_l = sink[l]
    cos_t, sin_t = _rope_tables(ss)

    xp2 = x_prompt.reshape(bp * sp, d)
    xs2 = x_sample.reshape(bs * ss, d)
    gpre = g_pre_mix[l][None, :]

    (qa_c, ka_c, va_c, qb_c, kb2_c, vb_c, sga_c, sgb_c, kaf, vaf, kbf, vbf) = _inproj(
        xp2, mod3, 0, gpre, cos_t, sin_t, w_in_b, seq=sp, tm=sp, is_ctx=True)
    r3 = lambda a, b_: a.reshape(b_, -1, a.shape[-1])
    oa_c = _diff_attention(lam_p, g_head, r3(qa_c, bp), [(r3(ka_c, bp), r3(va_c, bp))],
                           tq=sp, lambda_init=lambda_init)
    ob_c = _swa_attention(sink_l, r3(qb_c, bp),
                          [("rep", r3(kb2_c, bp), r3(vb_c, bp), sp, lambda i: 0)], tq=sp)

    (qa_s, ka_s, va_s, qb_s, kb2_s, vb_s, sga_s, sgb_s) = _inproj(
        xs2, mod3, 1, gpre, cos_t, sin_t, w_in_b, seq=ss, tm=512, is_ctx=False)
    past = cache_diff_k.shape[2]
    ck = cache_diff_k[:, l].reshape(bs, past, -1)
    cv = cache_diff_v[:, l].reshape(bs, past, -1)
    oa_s = _diff_attention(lam_p, g_head, r3(qa_s, bs),
                           [(ck, cv), (r3(ka_s, bs), r3(va_s, bs))], tq=256, lambda_init=lambda_init)
    sk = cache_swa_k[:, l].reshape(bs, past, -1)
    sv = cache_swa_v[:, l].reshape(bs, past, -1)
    nqb = ss // SWA_Q
    kb2_3, vb_3 = r3(kb2_s, bs), r3(vb_s, bs)
    ob_s = _swa_attention(sink_l, r3(qb_s, bs), [
        ("cache", sk, sv, past, lambda i: 0),
        ("left", kb2_3, vb_3, SWA_Q, lambda i: jnp.maximum(i - 1, 0)),
        ("center", kb2_3, vb_3, SWA_Q, lambda i: i),
        ("right", kb2_3, vb_3, SWA_Q, lambda i: jnp.minimum(i + 1, nqb - 1)),
    ], tq=SWA_Q)

    gpm = g_post_mix[l][None, :]
    gpf = g_pre_ffn[l][None, :]
    t_ctx, t_lat = bp * sp, bs * ss
    x1, h2t, route = _postmix(
        (xp2, oa_c.reshape(t_ctx, -1), ob_c.reshape(t_ctx, -1), sga_c, sgb_c),
        (xs2, oa_s.reshape(t_lat, -1), ob_s.reshape(t_lat, -1), sga_s, sgb_s),
        mod3, gpm, gpf, wpa, wpb, wo, wr, br, lat_seq=ss, tm=512)

    blk_expert, nact, src_tok, dst_row, n_out_rows = _routing_tables(route, MOE_ROWS)
    y2t = _moe(blk_expert, nact, src_tok, dst_row, h2t, w_e1[l], w_e3[l], w_e2[l], n_out_rows)

    gpost = g_post_ffn[l][None, :]
    tmc = 256
    y_p = _combine(x1, y2t, route, mod3, 0, gpost, t=t_ctx, seq=sp, tm=tmc, tok_off=0, is_ctx=True)
    y_s = _combine(x1, y2t, route, mod3, 1, gpost, t=t_lat, seq=ss, tm=tmc, tok_off=t_ctx, is_ctx=False)

    ha = A_HEADS
    return (y_p.reshape(bp, sp, d), y_s.reshape(bs, ss, d),
            kaf.reshape(bp, 1, sp, ha, 2, A_HEAD_DIM), vaf.reshape(bp, 1, sp, ha, A_V_DIM),
            kbf.reshape(bp, 1, sp, B_KV_HEADS, B_HEAD_DIM), vbf.reshape(bp, 1, sp, B_KV_HEADS, B_HEAD_DIM))
```

```python
import functools
import math

import jax
import jax.numpy as jnp
from jax import lax
from jax.experimental import pallas as pl
from jax.experimental.pallas import tpu as pltpu

F32 = jnp.float32
BF16 = jnp.bfloat16
HIGHEST = lax.Precision.HIGHEST

GRID_W = 64
ROPE_BASE = 10000.0
EPS = 1e-6
NEG_INF = -1e30
A_HEADS = 4
A_HEAD_DIM = 64
A_V_DIM = 2 * A_HEAD_DIM
B_HEADS = 8
B_KV_HEADS = 2
B_GROUP = B_HEADS // B_KV_HEADS
B_HEAD_DIM = 64
WINDOW = 128
N_GROUPS = 4
EXPERTS_PER_GROUP = 8
N_EXPERTS = N_GROUPS * EXPERTS_PER_GROUP
TOP_K = 2

LANES = 128
MOD_ROWS = 16
MOE_ROWS = 256
SWA_Q = 128

_QA = 0
_KA = _QA + A_HEADS * 2 * A_HEAD_DIM
_VA = _KA + A_HEADS * 2 * A_HEAD_DIM
_QB = _VA + A_HEADS * A_V_DIM
_KB = _QB + B_HEADS * B_HEAD_DIM
_VB = _KB + B_KV_HEADS * B_HEAD_DIM
_GA = _VB + B_KV_HEADS * B_HEAD_DIM


def _rms(x, g):
    return x * lax.rsqrt(jnp.mean(x * x, axis=-1, keepdims=True) + EPS) * g


def _store_row_tiles(ref, base, val):
    sub = val.shape[1] // LANES
    for s in range(sub):
        ref[pl.ds(base + s, val.shape[0], stride=sub), :] = val[:, s * LANES:(s + 1) * LANES]


def _load_row_tiles(ref, base, n_rows, sub):
    return jnp.concatenate([ref[pl.ds(base + s, n_rows, stride=sub), :] for s in range(sub)], axis=1)


def _mod_kernel(c_ref, w_ref, b_ref, o_ref):
    c = c_ref[...]
    a = c * jax.nn.sigmoid(c)
    o_ref[...] = jnp.dot(a, w_ref[...], precision=HIGHEST, preferred_element_type=F32) + b_ref[...]


def _modulation(c_all, w_ada, b_ada):
    d, n = w_ada.shape
    tn = 512
    return pl.pallas_call(
        _mod_kernel,
        grid=(n // tn,),
        in_specs=[
            pl.BlockSpec((MOD_ROWS, d), lambda j: (0, 0)),
            pl.BlockSpec((d, tn), lambda j: (0, j)),
            pl.BlockSpec((1, tn), lambda j: (0, j)),
        ],
        out_specs=pl.BlockSpec((MOD_ROWS, tn), lambda j: (0, j)),
        out_shape=jax.ShapeDtypeStruct((MOD_ROWS, n), F32),
        name="modulation",
    )(c_all, w_ada, b_ada)


def _rope128(z, cos, sin_signed, first_half):
    rot = jnp.where(first_half, pltpu.roll(z, 96, 1), pltpu.roll(z, 32, 1))
    return z * cos + rot * sin_signed


def _inproj_kernel(x_ref, sh_ref, sc_ref, g_ref, cos_ref, sin_ref, w_ref, *outs, is_ctx):
    x = x_ref[...]
    h = _rms(x, g_ref[...]) * (1.0 + sc_ref[0]) + sh_ref[0]
    hb = h.astype(BF16)
    lane = lax.broadcasted_iota(jnp.int32, (1, LANES), 1)
    first_half = (lane % 64) < 32
    low = lane < 64

    def seg(lo, hi):
        return jnp.dot(hb, w_ref[:, lo:hi], preferred_element_type=F32)

    def rope(z):
        if is_ctx:
            return z
        cos = cos_ref[...]
        sin = sin_ref[...]
        parts = [_rope128(z[:, j:j + LANES], cos, sin, first_half) for j in range(0, z.shape[1], LANES)]
        return parts[0] if len(parts) == 1 else jnp.concatenate(parts, axis=1)

    if is_ctx:
        qa_o, ka_o, va_o, qb_o, kb2_o, vb_o, sga_o, sgb_o, kaf_o, vaf_o, kbf_o, vbf_o = outs
    else:
        qa_o, ka_o, va_o, qb_o, kb2_o, vb_o, sga_o, sgb_o = outs

    scale = A_HEAD_DIM ** -0.5
    qa_o[...] = (rope(seg(_QA, _KA)) * scale).astype(BF16)
    ka = rope(seg(_KA, _VA))
    if is_ctx:
        ka_o[...] = ka.astype(BF16)
    else:
        ka_o[0] = ka.T.astype(BF16)
    va = seg(_VA, _QB)
    va_o[...] = va.astype(BF16)
    qb_o[...] = (rope(seg(_QB, _KB)) * (B_HEAD_DIM ** -0.5)).astype(BF16)
    kb = rope(seg(_KB, _VB))
    kb_sw = pltpu.roll(kb, 64, 1)
    kb2_o[:, 0:LANES] = jnp.where(low, kb, kb_sw).astype(BF16)
    kb2_o[:, LANES:2 * LANES] = jnp.where(low, kb_sw, kb).astype(BF16)
    vb = seg(_VB, _GA)
    vb_o[...] = vb.astype(BF16)
    d = x.shape[1]
    sga_o[...] = jax.nn.sigmoid(seg(_GA, _GA + d)).astype(BF16)
    sgb_o[...] = jax.nn.sigmoid(seg(_GA + d, _GA + 2 * d)).astype(BF16)
    if is_ctx:
        kaf_o[...] = ka
        vaf_o[...] = va
        kbf_o[...] = kb
        vbf_o[...] = vb


def _inproj(x2, mod3, mod_row0, g_pre, cos_t, sin_t, w_in_b, *, seq, tm, is_ctx):
    t, d = x2.shape
    per = seq // tm
    n_in = w_in_b.shape[1]
    wa = A_HEADS * 2 * A_HEAD_DIM
    wkb = B_KV_HEADS * B_HEAD_DIM

    def row(i):
        return (i // per) if not is_ctx else 0

    tok = lambda w: pl.BlockSpec((tm, w), lambda i: (i, 0))
    out_shape = [
        jax.ShapeDtypeStruct((t, wa), BF16), jax.ShapeDtypeStruct((t, wa), BF16),
        jax.ShapeDtypeStruct((t, wa), BF16), jax.ShapeDtypeStruct((t, wa), BF16),
        jax.ShapeDtypeStruct((t, 2 * wkb), BF16), jax.ShapeDtypeStruct((t, wkb), BF16),
        jax.ShapeDtypeStruct((t, d), BF16), jax.ShapeDtypeStruct((t, d), BF16),
    ]
    out_specs = [tok(wa), tok(wa), tok(wa), tok(wa), tok(2 * wkb), tok(wkb), tok(d), tok(d)]
    if not is_ctx:
        out_shape[1] = jax.ShapeDtypeStruct((t // seq, wa, seq), BF16)
        out_specs[1] = pl.BlockSpec((1, wa, tm), lambda i: (i // per, 0, i % per))
    if is_ctx:
        out_shape += [jax.ShapeDtypeStruct((t, wa), F32), jax.ShapeDtypeStruct((t, wa), F32),
                      jax.ShapeDtypeStruct((t, wkb), F32), jax.ShapeDtypeStruct((t, wkb), F32)]
        out_specs += [tok(wa), tok(wa), tok(wkb), tok(wkb)]
    return pl.pallas_call(
        functools.partial(_inproj_kernel, is_ctx=is_ctx),
        grid=(t // tm,),
        in_specs=[
            pl.BlockSpec((tm, d), lambda i: (i, 0)),
            pl.BlockSpec((1, 1, d), lambda i: (mod_row0 + row(i), 0, 0)),
            pl.BlockSpec((1, 1, d), lambda i: (mod_row0 + row(i), 0, 1)),
            pl.BlockSpec((1, d), lambda i: (0, 0)),
            pl.BlockSpec((tm, LANES), lambda i: (i % per, 0)),
            pl.BlockSpec((tm, LANES), lambda i: (i % per, 0)),
            pl.BlockSpec((d, n_in), lambda i: (0, 0)),
        ],
        out_specs=out_specs,
        out_shape=out_shape,
        compiler_params=pltpu.CompilerParams(dimension_semantics=("arbitrary",)),
        name="inproj_ctx" if is_ctx else "inproj_lat",
    )(x2, mod3, mod3, g_pre, cos_t, sin_t, w_in_b)


def _nt(a, b):
    return lax.dot_general(a, b, (((1,), (1,)), ((), ())), preferred_element_type=F32)


def _diff_kernel(lam_ref, g_ref, q_ref, *refs, n_seg, lambda_init):
    kv = refs[:2 * n_seg]
    o_ref = refs[2 * n_seg]
    lp = lam_ref[...]
    lam = (jnp.exp(jnp.sum(lp[0:1] * lp[1:2], axis=-1, keepdims=True))
           - jnp.exp(jnp.sum(lp[2:3] * lp[3:4], axis=-1, keepdims=True)) + lambda_init)
    q = q_ref[0]
    tq = q.shape[0]
    lane = lax.broadcasted_iota(jnp.int32, (1, LANES), 1)
    q2 = jnp.concatenate([q * (lane < 64).astype(BF16), q * (lane >= 64).astype(BF16)], axis=0)
    s = [_nt(q2, kv[2 * j][0].astype(BF16)) for j in range(n_seg)]
    mx = functools.reduce(jnp.maximum, [jnp.max(x, axis=-1, keepdims=True) for x in s])
    acc = None
    for j in range(n_seg):
        v = kv[2 * j + 1][0].astype(BF16)
        v_ext = jnp.concatenate([v, jnp.ones_like(v)], axis=1)
        aj = jnp.dot(jnp.exp(s[j] - mx).astype(BF16), v_ext, preferred_element_type=F32)
        acc = aj if acc is None else acc + aj
    on = acc[:, 0:LANES] / acc[:, LANES:2 * LANES]
    o = on[0:tq] - lam * on[tq:2 * tq]
    o = _rms(o, g_ref[...]) * (1.0 - lambda_init)
    o_ref[0] = o.astype(BF16)


def _diff_attention(lam_p, g_head, q, segs, *, tq, lambda_init):
    b, s, w = q.shape
    in_specs = [
        pl.BlockSpec((4, A_HEAD_DIM), lambda bi, h, qi: (0, 0)),
        pl.BlockSpec((1, A_V_DIM), lambda bi, h, qi: (0, 0)),
        pl.BlockSpec((1, tq, LANES), lambda bi, h, qi: (bi, qi, h)),
    ]
    args = [lam_p, g_head, q]
    for k, v in segs:
        nk = k.shape[1]
        in_specs += [pl.BlockSpec((1, nk, LANES), lambda bi, h, qi: (bi, 0, h)),
                     pl.BlockSpec((1, nk, LANES), lambda bi, h, qi: (bi, 0, h))]
        args += [k, v]
    return pl.pallas_call(
        functools.partial(_diff_kernel, n_seg=len(segs), lambda_init=lambda_init),
        grid=(b, A_HEADS, s // tq),
        in_specs=in_specs,
        out_specs=pl.BlockSpec((1, tq, LANES), lambda bi, h, qi: (bi, qi, h)),
        out_shape=jax.ShapeDtypeStruct((b, s, w), BF16),
        compiler_params=pltpu.CompilerParams(
            dimension_semantics=("arbitrary", "arbitrary", "arbitrary")),
        name="diff_attn_%d" % len(segs),
    )(*args)


def _diff_lat_kernel(lam_ref, g_ref, q_ref, kc_ref, kt_ref, vc_ref, v_ref, o_ref, s_scr, m_scr, *, lambda_init):
    t = pl.program_id(0)
    slot = t % 2
    pslot = 1 - slot
    tq = q_ref.shape[1]
    nkc = kc_ref.shape[1]
    nkn = kt_ref.shape[2]

    @pl.when(t == 0)
    def _():
        s_scr[...] = jnp.zeros_like(s_scr)
        m_scr[...] = jnp.zeros_like(m_scr)

    lp = lam_ref[...]
    lam = (jnp.exp(jnp.sum(lp[0:1] * lp[1:2], axis=-1, keepdims=True))
           - jnp.exp(jnp.sum(lp[2:3] * lp[3:4], axis=-1, keepdims=True)) + lambda_init)
    lane = lax.broadcasted_iota(jnp.int32, (1, LANES), 1)

    mp = m_scr[pslot]
    v_all = jnp.concatenate([vc_ref[0].astype(BF16), v_ref[0]], axis=0)
    v_ext = jnp.concatenate([v_all, jnp.ones_like(v_all)], axis=1)
    p = jnp.concatenate(
        [jnp.exp(s_scr[pslot, :, c:c + LANES] - mp).astype(BF16) for c in range(0, nkc + nkn, LANES)], axis=1)
    acc = jnp.dot(p, v_ext, preferred_element_type=F32)
    on = acc[:, 0:LANES] / acc[:, LANES:2 * LANES]
    o = on[0:tq] - lam * on[tq:2 * tq]
    o_ref[0] = (_rms(o, g_ref[...]) * (1.0 - lambda_init)).astype(BF16)

    q = q_ref[0]
    q2 = jnp.concatenate([q * (lane < 64).astype(BF16), q * (lane >= 64).astype(BF16)], axis=0)
    sc = _nt(q2, kc_ref[0].astype(BF16))
    sn = jnp.dot(q2, kt_ref[0], preferred_element_type=F32)
    mx = jnp.maximum(jnp.max(sc, axis=-1, keepdims=True), jnp.max(sn, axis=-1, keepdims=True))
    s_scr[slot, :, 0:nkc] = sc
    s_scr[slot, :, nkc:nkc + nkn] = sn
    m_scr[slot] = jnp.broadcast_to(mx, (2 * tq, LANES))


def _diff_attention_lat(lam_p, g_head, q, kc, kt, vc, v, *, tq, lambda_init):
    b, s, w = q.shape
    past = kc.shape[1]
    nq = s // tq
    n_units = b * A_HEADS * nq
    last = n_units - 1

    def unit(u):
        return u // (A_HEADS * nq), (u // nq) % A_HEADS, u % nq

    def cur(t):
        return unit(jnp.minimum(t, last))

    def prev(t):
        return unit(jnp.maximum(t - 1, 0))

    return pl.pallas_call(
        functools.partial(_diff_lat_kernel, lambda_init=lambda_init),
        grid=(n_units + 1,),
        in_specs=[
            pl.BlockSpec((4, A_HEAD_DIM), lambda t: (0, 0)),
            pl.BlockSpec((1, A_V_DIM), lambda t: (0, 0)),
            pl.BlockSpec((1, tq, LANES), lambda t: (cur(t)[0], cur(t)[2], cur(t)[1])),
            pl.BlockSpec((1, past, LANES), lambda t: (cur(t)[0], 0, cur(t)[1])),
            pl.BlockSpec((1, LANES, s), lambda t: (cur(t)[0], cur(t)[1], 0)),
            pl.BlockSpec((1, past, LANES), lambda t: (prev(t)[0], 0, prev(t)[1])),
            pl.BlockSpec((1, s, LANES), lambda t: (prev(t)[0], 0, prev(t)[1])),
        ],
        out_specs=pl.BlockSpec((1, tq, LANES), lambda t: (prev(t)[0], prev(t)[2], prev(t)[1])),
        out_shape=jax.ShapeDtypeStruct((b, s, w), BF16),
        scratch_shapes=[pltpu.VMEM((2, 2 * tq, past + s), F32), pltpu.VMEM((2, 2 * tq, LANES), F32)],
        compiler_params=pltpu.CompilerParams(dimension_semantics=("arbitrary",),
                                             vmem_limit_bytes=56 * 1024 * 1024),
        name="diff_attn_lat",
    )(lam_p, g_head, q, kc, kt, vc, v)


def _swa_kernel(sink_ref, q_ref, *refs, tq, seg_kinds):
    n_seg = len(seg_kinds)
    k_refs = refs[:n_seg]
    v_refs = refs[n_seg:2 * n_seg]
    o_ref = refs[2 * n_seg]
    i = pl.program_id(1)
    nb = pl.num_programs(1)
    lane = lax.broadcasted_iota(jnp.int32, (1, LANES), 1)
    low = lane < 64
    lane2 = lax.broadcasted_iota(jnp.int32, (1, 2 * LANES), 1)
    head_masks = [((lane2 // 64) == g).astype(BF16) for g in range(B_GROUP)]
    qi = lax.broadcasted_iota(jnp.int32, (B_GROUP * tq, SWA_Q), 0) & (tq - 1)
    kj = lax.broadcasted_iota(jnp.int32, (B_GROUP * tq, SWA_Q), 1)
    far = 2 * SWA_Q
    gw = B_GROUP * B_HEAD_DIM

    vs = [v_ref[0].astype(BF16) for v_ref in v_refs]
    v_all = vs[0] if n_seg == 1 else jnp.concatenate(vs, axis=0)
    v_ext = jnp.concatenate([v_all, jnp.ones_like(v_all)], axis=1)

    for n in range(B_KV_HEADS):
        q = q_ref[0, :, n * gw:(n + 1) * gw]
        qs = jnp.concatenate([q * hm for hm in head_masks], axis=0)
        ks = []
        for kind, k_ref in zip(seg_kinds, k_refs):
            if kind == "cache":
                k = k_ref[0]
                sw = pltpu.roll(k, 64, 1)
                k2 = (jnp.where(low, k, sw) if n == 0 else jnp.where(low, sw, k)).astype(BF16)
            else:
                k2 = k_ref[0, :, n * LANES:(n + 1) * LANES]
            ks.append(jnp.concatenate([k2, k2], axis=1))
        k_all = ks[0] if n_seg == 1 else jnp.concatenate(ks, axis=0)
        s = _nt(qs, k_all)
        chunks, col = [], 0
        for kind, k in zip(seg_kinds, ks):
            for c in range(0, k.shape[0], LANES):
                sc = s[:, col + c:col + c + LANES]
                if kind == "left":
                    sc = jnp.where(kj >= qi + jnp.where(i > 0, 0, far), sc, NEG_INF)
                elif kind == "right":
                    sc = jnp.where(kj <= qi - jnp.where(i < nb - 1, 0, far), sc, NEG_INF)
                chunks.append(sc)
            col += k.shape[0]
        sinkcol = jnp.concatenate(
            [jnp.full((tq, 1), sink_ref[n * B_GROUP + g], F32) for g in range(B_GROUP)], axis=0)
        mx = jnp.maximum(jnp.max(functools.reduce(jnp.maximum, chunks), axis=-1, keepdims=True), sinkcol)
        p = jnp.concatenate([jnp.exp(c - mx).astype(BF16) for c in chunks], axis=1)
        acc = jnp.dot(p, v_ext, preferred_element_type=F32)
        o = acc[:, 0:LANES] / (acc[:, LANES:2 * LANES] + jnp.exp(sinkcol - mx))
        osw = pltpu.roll(o, 64, 1)
        for j in range(B_GROUP // 2):
            ra = slice((2 * j) * tq, (2 * j + 1) * tq)
            rb = slice((2 * j + 1) * tq, (2 * j + 2) * tq)
            pair = jnp.where(low, o[ra], osw[rb]) if n == 0 else jnp.where(low, osw[ra], o[rb])
            o_ref[0, :, n * gw + j * LANES:n * gw + (j + 1) * LANES] = pair.astype(BF16)


def _swa_attention(sink, q, segs, *, tq):
    b, s, w = q.shape
    in_specs = [
        pl.BlockSpec(memory_space=pltpu.SMEM),
        pl.BlockSpec((1, tq, w), lambda bi, i: (bi, i, 0)),
    ]
    k_specs, v_specs, k_args, v_args, kinds = [], [], [], [], []
    for kind, k, v, rows, idx in segs:
        k_specs.append(pl.BlockSpec((1, rows, k.shape[2]), lambda bi, i, idx=idx: (bi, idx(i), 0)))
        v_specs.append(pl.BlockSpec((1, rows, v.shape[2]), lambda bi, i, idx=idx: (bi, idx(i), 0)))
        k_args.append(k)
        v_args.append(v)
        kinds.append(kind)
    return pl.pallas_call(
        functools.partial(_swa_kernel, tq=tq, seg_kinds=tuple(kinds)),
        grid=(b, s // tq),
        in_specs=in_specs + k_specs + v_specs,
        out_specs=pl.BlockSpec((1, tq, w), lambda bi, i: (bi, i, 0)),
        out_shape=jax.ShapeDtypeStruct((b, s, w), BF16),
        compiler_params=pltpu.CompilerParams(dimension_semantics=("arbitrary", "arbitrary")),
        name="swa_attn_%d" % len(segs),
    )(sink, q, *k_args, *v_args)


def _postmix_kernel(*refs, n_ctx_tiles):
    (xc, xl, oac, oal, obc, obl, sgac, sgal, sgbc, sgbl, g1_ref, sh2_ref, sc2_ref, gpm_ref, gpf_ref,
     wpa_ref, wpb_ref, wo_ref, wr_ref, wrl_ref, br_ref, x1_o, h2_o, route_o) = refs
    is_ctx = pl.program_id(0) < n_ctx_tiles
    pick = lambda a, b: jnp.where(is_ctx, a[...], b[...])
    pa = jnp.dot(pick(oac, oal), wpa_ref[...], preferred_element_type=F32)
    pb = jnp.dot(pick(obc, obl), wpb_ref[...], preferred_element_type=F32)
    mix = pick(sgac, sgal).astype(F32) * pa + pick(sgbc, sgbl).astype(F32) * pb
    m2 = jnp.dot(mix.astype(BF16), wo_ref[...], preferred_element_type=F32)
    x1 = pick(xc, xl) + g1_ref[0] * _rms(m2, gpm_ref[...])
    x1_o[...] = x1
    h2 = _rms(x1, gpf_ref[...]) * (1.0 + sc2_ref[0]) + sh2_ref[0]
    _store_row_tiles(h2_o, 0, h2)

    h_hi = h2.astype(BF16)
    h_lo = (h2 - h_hi.astype(F32)).astype(BF16)
    logits = (jnp.dot(h_hi, wr_ref[...], preferred_element_type=F32)
              + jnp.dot(h_lo, wr_ref[...], preferred_element_type=F32)
              + jnp.dot(h_hi, wrl_ref[...], preferred_element_type=F32) + br_ref[...])
    tm = logits.shape[0]
    lane_i = lax.broadcasted_iota(jnp.int32, (tm, LANES), 1)
    lane = lane_i.astype(F32)
    big = float(LANES)
    is_g = lane_i < N_GROUPS
    lg = jnp.where(is_g, logits, -jnp.inf)
    mg = jnp.max(lg, axis=-1, keepdims=True)
    g_sel = jnp.min(jnp.where(is_g & (logits == mg), lane, big), axis=-1, keepdims=True)
    g_w = 1.0 / jnp.sum(jnp.where(is_g, jnp.exp(lg - mg), 0.0), axis=-1, keepdims=True)
    lane_grp = lax.shift_right_arithmetic(lane_i - N_GROUPS, 3).astype(F32)
    in_grp = (lane_grp == g_sel) & (lane_i < N_GROUPS + N_EXPERTS)
    le = jnp.where(in_grp, logits, -jnp.inf)
    v0 = jnp.max(le, axis=-1, keepdims=True)
    i0 = jnp.min(jnp.where(in_grp & (logits == v0), lane, big), axis=-1, keepdims=True)
    rest = in_grp & (lane != i0)
    le1 = jnp.where(rest, logits, -jnp.inf)
    v1 = jnp.max(le1, axis=-1, keepdims=True)
    i1 = jnp.min(jnp.where(rest & (logits == v1), lane, big), axis=-1, keepdims=True)
    e = jnp.exp(v1 - v0)
    w0 = g_w / (1.0 + e)
    w1 = g_w * e / (1.0 + e)
    e0 = i0 - N_GROUPS
    e1 = i1 - N_GROUPS
    route_o[...] = jnp.where(lane_i == 0, e0, jnp.where(lane_i == 1, e1,
                             jnp.where(lane_i == 2, w0, jnp.where(lane_i == 3, w1, 0.0))))


def _postmix(ctx_in, lat_in, mod3, gpm, gpf, wpa, wpb, wo, wr, wrl, br, *, lat_seq, tm):
    t_ctx, d = ctx_in[0].shape
    t_lat = lat_in[0].shape[0]
    assert t_ctx % tm == 0 and lat_seq % tm == 0
    nc = t_ctx // tm
    nl = t_lat // tm
    per = lat_seq // tm
    sub = d // LANES
    t_all = t_ctx + t_lat

    mod_row = lambda i: jnp.where(i < nc, 0, 1 + jnp.maximum(i - nc, 0) // per)
    full = lambda a: pl.BlockSpec(a.shape, lambda i: (0,) * a.ndim)
    modspec = lambda c: pl.BlockSpec((1, 1, d), lambda i: (mod_row(i), 0, c))
    in_specs, args = [], []
    for a_c, a_l in zip(ctx_in, lat_in):
        w = a_c.shape[1]
        in_specs += [pl.BlockSpec((tm, w), lambda i: (jnp.minimum(i, nc - 1), 0)),
                     pl.BlockSpec((tm, w), lambda i: (jnp.maximum(i - nc, 0), 0))]
        args += [a_c, a_l]
    in_specs += [modspec(2), modspec(3), modspec(4), full(gpm), full(gpf),
                 full(wpa), full(wpb), full(wo), full(wr), full(wrl), full(br)]
    args += [mod3, mod3, mod3, gpm, gpf, wpa, wpb, wo, wr, wrl, br]
    return pl.pallas_call(
        functools.partial(_postmix_kernel, n_ctx_tiles=nc),
        grid=(nc + nl,),
        in_specs=in_specs,
        out_specs=[pl.BlockSpec((tm, d), lambda i: (i, 0)),
                   pl.BlockSpec((tm * sub, LANES), lambda i: (i, 0)),
                   pl.BlockSpec((tm, LANES), lambda i: (i, 0))],
        out_shape=[jax.ShapeDtypeStruct((t_all, d), F32),
                   jax.ShapeDtypeStruct((t_all * sub, LANES), F32),
                   jax.ShapeDtypeStruct((t_all, LANES), F32)],
        compiler_params=pltpu.CompilerParams(dimension_semantics=("arbitrary",)),
        name="postmix",
    )(*args)


def _moe_kernel(be_ref, nact_ref, src0_ref, srcn_ref, dst_ref, h_hbm, w1_ref, w3_ref, w2_ref, y_hbm,
                xbuf, obuf, w1b, w3b, w2b, gsem, ssem, *, rows, sub, trash_row):
    i = pl.program_id(0)
    nact = nact_ref[0]
    slot = i % 2
    nslot = 1 - slot
    blk = rows * sub
    half = rows // 2

    def gather_copy(idx_ref, r, s):
        src = pl.multiple_of(idx_ref[0, 0, r] * sub, sub)
        return pltpu.make_async_copy(h_hbm.at[pl.ds(src, sub)],
                                     xbuf.at[pl.ds(pl.multiple_of(s * blk + r * sub, sub), sub)], gsem.at[s])

    def scatter_copy(r, s, dst):
        return pltpu.make_async_copy(obuf.at[pl.ds(pl.multiple_of(s * blk + r * sub, sub), sub)],
                                     y_hbm.at[pl.ds(pl.multiple_of(dst * sub, sub), sub)], ssem.at[s])

    def wait_gathers(s):
        pltpu.make_async_copy(h_hbm.at[pl.ds(0, blk)], xbuf.at[pl.ds(pl.multiple_of(s * blk, blk), blk)],
                              gsem.at[s]).wait()

    def wait_scatters(s):
        pltpu.make_async_copy(obuf.at[pl.ds(pl.multiple_of(s * blk, blk), blk)], y_hbm.at[pl.ds(0, blk)],
                              ssem.at[s]).wait()

    @pl.when(i == 0)
    def _():
        obuf[...] = jnp.zeros_like(obuf)
        for r in range(rows):
            gather_copy(src0_ref, r, 0).start(priority=r % 2)
            scatter_copy(r, 0, trash_row + r).start(priority=(r + 1) % 2)

    @pl.when(i < nact)
    def _():
        changed = jnp.logical_or(i == 0, be_ref[i] != be_ref[jnp.maximum(i - 1, 0)])

        @pl.when(changed)
        def _():
            w1b[...] = w1_ref[0].astype(BF16)
            w3b[...] = w3_ref[0].astype(BF16)
            w2b[...] = w2_ref[0].astype(BF16)

        wait_gathers(slot)
        x = _load_row_tiles(xbuf, slot * blk, rows, sub).astype(BF16)
        first = i == 0

        def gathers(lo, hi):
            for r in range(lo, hi):
                gather_copy(srcn_ref, r, nslot).start(priority=r % 2)

        def scatters(lo, hi):
            for r in range(lo, hi):
                dst = jnp.where(first, trash_row + rows + r, dst_ref[0, 0, r])
                scatter_copy(r, nslot, dst).start(priority=r % 2)

        gathers(0, half)
        a = jnp.dot(x, w1b[...], preferred_element_type=F32)
        gathers(half, rows)
        b = jnp.dot(x, w3b[...], preferred_element_type=F32)
        scatters(0, half)
        hmid = (a * jax.nn.sigmoid(a) * b).astype(BF16)
        y = jnp.dot(hmid, w2b[...], preferred_element_type=F32)
        scatters(half, rows)
        wait_scatters(slot)
        _store_row_tiles(obuf, slot * blk, y)

    @pl.when(i == nact)
    def _():
        wait_gathers(slot)
        for r in range(rows):
            scatter_copy(r, nslot, dst_ref[0, 0, r]).start(priority=r % 2)
        wait_scatters(slot)
        wait_scatters(nslot)


def _moe(blk_expert, nact, src_tok, dst_row, h2t, w1, w3, w2, n_out_rows):
    nblk = src_tok.shape[0]
    d, de = w1.shape[1], w1.shape[2]
    sub = d // LANES
    rows = MOE_ROWS
    last = nblk - 1
    grid_spec = pltpu.PrefetchScalarGridSpec(
        num_scalar_prefetch=2,
        grid=(nblk + 1,),
        in_specs=[
            pl.BlockSpec((1, 1, rows), lambda i, be, na: (0, 0, 0), memory_space=pltpu.SMEM),
            pl.BlockSpec((1, 1, rows), lambda i, be, na: (jnp.minimum(i + 1, last), 0, 0), memory_space=pltpu.SMEM),
            pl.BlockSpec((1, 1, rows), lambda i, be, na: (jnp.maximum(i - 1, 0), 0, 0), memory_space=pltpu.SMEM),
            pl.BlockSpec(memory_space=pl.ANY),
            pl.BlockSpec((1, d, de), lambda i, be, na: (be[i], 0, 0)),
            pl.BlockSpec((1, d, de), lambda i, be, na: (be[i], 0, 0)),
            pl.BlockSpec((1, de, d), lambda i, be, na: (be[i], 0, 0)),
        ],
        out_specs=pl.BlockSpec(memory_space=pl.ANY),
        scratch_shapes=[
            pltpu.VMEM((2 * rows * sub, LANES), F32),
            pltpu.VMEM((2 * rows * sub, LANES), F32),
            pltpu.VMEM((d, de), BF16),
            pltpu.VMEM((d, de), BF16),
            pltpu.VMEM((de, d), BF16),
            pltpu.SemaphoreType.DMA((2,)),
            pltpu.SemaphoreType.DMA((2,)),
        ],
    )
    return pl.pallas_call(
        functools.partial(_moe_kernel, rows=rows, sub=sub, trash_row=n_out_rows - 2 * rows),
        grid_spec=grid_spec,
        out_shape=jax.ShapeDtypeStruct((n_out_rows * sub, LANES), F32),
        compiler_params=pltpu.CompilerParams(dimension_semantics=("arbitrary",)),
        name="expert_mlp",
    )(blk_expert, nact, src_tok, src_tok, dst_row, h2t, w1, w3, w2)


def _combine_kernel(x1_ref, y0_ref, y1_ref, r_ref, g2_ref, gpost_ref, o_ref):
    tm, d = x1_ref.shape
    sub = d // LANES
    r = r_ref[...]
    y = (_load_row_tiles(y0_ref, 0, tm, sub) * r[:, 2:3] + _load_row_tiles(y1_ref, 0, tm, sub) * r[:, 3:4])
    o_ref[...] = x1_ref[...] + g2_ref[0] * _rms(y, gpost_ref[...])


def _combine(x1, y2t, route, mod3, mod_row0, gpost, *, t, seq, tm, tok_off, is_ctx):
    t_all, d = x1.shape
    per = seq // tm
    sub = d // LANES
    boff = tok_off // tm
    koff = t_all // tm

    def row(i):
        return (i // per) if not is_ctx else 0

    return pl.pallas_call(
        _combine_kernel,
        grid=(t // tm,),
        in_specs=[
            pl.BlockSpec((tm, d), lambda i: (boff + i, 0)),
            pl.BlockSpec((tm * sub, LANES), lambda i: (boff + i, 0)),
            pl.BlockSpec((tm * sub, LANES), lambda i: (koff + boff + i, 0)),
            pl.BlockSpec((tm, LANES), lambda i: (boff + i, 0)),
            pl.BlockSpec((1, 1, d), lambda i: (mod_row0 + row(i), 0, 5)),
            pl.BlockSpec((1, d), lambda i: (0, 0)),
        ],
        out_specs=pl.BlockSpec((tm, d), lambda i: (i, 0)),
        out_shape=jax.ShapeDtypeStruct((t, d), F32),
        compiler_params=pltpu.CompilerParams(dimension_semantics=("arbitrary",)),
        name="combine_ctx" if is_ctx else "combine_lat",
    )(x1, y2t, y2t, route, mod3, gpost)


def _routing_tables(route, rows):
    t = route.shape[0]
    n_asg = t * TOP_K
    assert n_asg < 0xFFFF and n_asg % rows == 0
    pad_id = 0xFFFF
    eid = route[:, 0:TOP_K].astype(jnp.int32).reshape(n_asg)
    ex = jnp.arange(N_EXPERTS, dtype=jnp.int32)
    counts = jnp.sum((eid[:, None] == ex[None, :]).astype(jnp.int32), axis=0)
    npad = (-counts) % rows
    real = eid * 65536 + jnp.arange(n_asg, dtype=jnp.int32)
    cand = jnp.where(jnp.arange(rows, dtype=jnp.int32)[None, :] < npad[:, None],
                     ex[:, None] * 65536 + pad_id, jnp.int32(0x7FFFFFFF))
    keys = jnp.sort(jnp.concatenate([real, cand.reshape(-1)]))
    n_blocks = n_asg // rows + N_EXPERTS
    a = keys & 0xFFFF
    is_pad = a == pad_id
    trash = n_asg + jnp.arange(n_blocks * rows, dtype=jnp.int32) % (2 * rows)
    src_tok = jnp.where(is_pad, 0, a // TOP_K).reshape(n_blocks, 1, rows)
    dst_row = jnp.where(is_pad, trash, (a % TOP_K) * t + a // TOP_K).reshape(n_blocks, 1, rows)
    blk_expert = jnp.minimum(keys.reshape(n_blocks, rows)[:, 0] >> 16, N_EXPERTS - 1)
    blk_expert = jnp.concatenate([blk_expert, blk_expert[-1:]])
    nact = (jnp.sum(counts + npad) // rows).reshape(1)
    return blk_expert, nact, src_tok, dst_row, n_asg + 2 * rows


def _rope_tables(n_tok):
    n_rows = n_tok // GRID_W
    rows = jnp.repeat(jnp.arange(n_rows), GRID_W).astype(F32)
    cols = jnp.tile(jnp.arange(GRID_W), n_rows).astype(F32)
    quarter = A_HEAD_DIM // 4
    inv = ROPE_BASE ** (-jnp.arange(quarter, dtype=F32) / quarter)
    ang = jnp.concatenate([rows[:, None] * inv, cols[:, None] * inv], axis=-1)
    cos, sin = jnp.cos(ang), jnp.sin(ang)
    cos_t = jnp.tile(jnp.concatenate([cos, cos], axis=-1), (1, LANES // A_HEAD_DIM))
    sin_t = jnp.tile(jnp.concatenate([-sin, sin], axis=-1), (1, LANES // A_HEAD_DIM))
    return cos_t, sin_t


def kernel(x_prompt, x_sample, c, cache_diff_k, cache_diff_v, cache_swa_k, cache_swa_v, c_ctx, w_ada, b_ada, g_pre_mix, g_post_mix, g_pre_ffn, g_post_ffn, w_in, lam_q1, lam_k1, lam_q2, lam_k2, g_diff_head, sink, w_proj_a, w_proj_b, w_out, w_router_group, b_router_group, w_router_expert, b_router_expert, w_e1, w_e3, w_e2):
    depth = w_in.shape[0]
    assert depth == 1
    l = 0
    bp, sp, d = x_prompt.shape
    bs, ss, _ = x_sample.shape
    lambda_init = 0.8 - 0.6 * math.exp(-0.3 * l)
    assert A_HEAD_DIM == B_HEAD_DIM and ss % GRID_W == 0 and bs + 1 <= MOD_ROWS

    c_all = jnp.concatenate([c_ctx[None, :], c, jnp.zeros((MOD_ROWS - 1 - bs, d), F32)], axis=0)
    mod = _modulation(c_all, w_ada[l], b_ada[l][None, :])
    mod3 = mod.reshape(MOD_ROWS, 1, 6 * d)

    w_in_b = w_in[l].astype(BF16)
    wpa = w_proj_a[l].astype(BF16)
    wpb = w_proj_b[l].astype(BF16)
    wo = w_out[l].astype(BF16)
    n_r = N_GROUPS + N_EXPERTS
    wr = jnp.concatenate([w_router_group[l], w_router_expert[l], jnp.zeros((d, LANES - n_r), F32)], axis=1)
    br = jnp.concatenate([b_router_group[l], b_router_expert[l], jnp.zeros((LANES - n_r,), F32)])[None, :]
    wr_hi = wr.astype(BF16)
    wr_lo = (wr - wr_hi.astype(F32)).astype(BF16)
    lam_p = jnp.stack([lam_q1[l], lam_k1[l], lam_q2[l], lam_k2[l]], axis=0)
    g_head = g_diff_head[l][None, :]
    sink_l = sink[l]
    cos_t, sin_t = _rope_tables(ss)

    xp2 = x_prompt.reshape(bp * sp, d)
    xs2 = x_sample.reshape(bs * ss, d)
    gpre = g_pre_mix[l][None, :]

    (qa_c, ka_c, va_c, qb_c, kb2_c, vb_c, sga_c, sgb_c, kaf, vaf, kbf, vbf) = _inproj(
        xp2, mod3, 0, gpre, cos_t, sin_t, w_in_b, seq=sp, tm=sp, is_ctx=True)
    r3 = lambda a, b_: a.reshape(b_, -1, a.shape[-1])
    oa_c = _diff_attention(lam_p, g_head, r3(qa_c, bp), [(r3(ka_c, bp), r3(va_c, bp))],
                           tq=sp, lambda_init=lambda_init)
    ob_c = _swa_attention(sink_l, r3(qb_c, bp),
                          [("rep", r3(kb2_c, bp), r3(vb_c, bp), sp, lambda i: 0)], tq=sp)

    (qa_s, ka_s, va_s, qb_s, kb2_s, vb_s, sga_s, sgb_s) = _inproj(
        xs2, mod3, 1, gpre, cos_t, sin_t, w_in_b, seq=ss, tm=512, is_ctx=False)
    past = cache_diff_k.shape[2]
    ck = cache_diff_k[:, l].reshape(bs, past, -1)
    cv = cache_diff_v[:, l].reshape(bs, past, -1)
    oa_s = _diff_attention_lat(lam_p, g_head, r3(qa_s, bs), ck, ka_s, cv, r3(va_s, bs),
                               tq=512, lambda_init=lambda_init)
    sk = cache_swa_k[:, l].reshape(bs, past, -1)
    sv = cache_swa_v[:, l].reshape(bs, past, -1)
    nqb = ss // SWA_Q
    kb2_3, vb_3 = r3(kb2_s, bs), r3(vb_s, bs)
    ob_s = _swa_attention(sink_l, r3(qb_s, bs), [
        ("cache", sk, sv, past, lambda i: 0),
        ("left", kb2_3, vb_3, SWA_Q, lambda i: jnp.maximum(i - 1, 0)),
        ("center", kb2_3, vb_3, SWA_Q, lambda i: i),
        ("right", kb2_3, vb_3, SWA_Q, lambda i: jnp.minimum(i + 1, nqb - 1)),
    ], tq=SWA_Q)

    gpm = g_post_mix[l][None, :]
    gpf = g_pre_ffn[l][None, :]
    t_ctx, t_lat = bp * sp, bs * ss
    x1, h2t, route = _postmix(
        (xp2, oa_c.reshape(t_ctx, -1), ob_c.reshape(t_ctx, -1), sga_c, sgb_c),
        (xs2, oa_s.reshape(t_lat, -1), ob_s.reshape(t_lat, -1), sga_s, sgb_s),
        mod3, gpm, gpf, wpa, wpb, wo, wr_hi, wr_lo, br, lat_seq=ss, tm=512)

    blk_expert, nact, src_tok, dst_row, n_out_rows = _routing_tables(route, MOE_ROWS)
    y2t = _moe(blk_expert, nact, src_tok, dst_row, h2t, w_e1[l], w_e3[l], w_e2[l], n_out_rows)

    gpost = g_post_ffn[l][None, :]
    tmc = 256
    y_p = _combine(x1, y2t, route, mod3, 0, gpost, t=t_ctx, seq=sp, tm=tmc, tok_off=0, is_ctx=True)
    y_s = _combine(x1, y2t, route, mod3, 1, gpost, t=t_lat, seq=ss, tm=tmc, tok_off=t_ctx, is_ctx=False)

    ha = A_HEADS
    return (y_p.reshape(bp, sp, d), y_s.reshape(bs, ss, d),
            kaf.reshape(bp, 1, sp, ha, 2, A_HEAD_DIM), vaf.reshape(bp, 1, sp, ha, A_V_DIM),
            kbf.reshape(bp, 1, sp, B_KV_HEADS, B_HEAD_DIM), vbf.reshape(bp, 1, sp, B_KV_HEADS, B_HEAD_DIM))
```

```python
import functools
import math

import jax
import jax.numpy as jnp
from jax import lax
from jax.experimental import pallas as pl
from jax.experimental.pallas import tpu as pltpu

F32 = jnp.float32
BF16 = jnp.bfloat16
HIGHEST = lax.Precision.HIGHEST

GRID_W = 64
ROPE_BASE = 10000.0
EPS = 1e-6
NEG_INF = -1e30
A_HEADS = 4
A_HEAD_DIM = 64
A_V_DIM = 2 * A_HEAD_DIM
B_HEADS = 8
B_KV_HEADS = 2
B_GROUP = B_HEADS // B_KV_HEADS
B_HEAD_DIM = 64
WINDOW = 128
N_GROUPS = 4
EXPERTS_PER_GROUP = 8
N_EXPERTS = N_GROUPS * EXPERTS_PER_GROUP
TOP_K = 2

LANES = 128
MOD_ROWS = 16
MOE_ROWS = 256
MOE_TILE = 512
SWA_Q = 128

_QA = 0
_KA = _QA + A_HEADS * 2 * A_HEAD_DIM
_VA = _KA + A_HEADS * 2 * A_HEAD_DIM
_QB = _VA + A_HEADS * A_V_DIM
_KB = _QB + B_HEADS * B_HEAD_DIM
_VB = _KB + B_KV_HEADS * B_HEAD_DIM
_GA = _VB + B_KV_HEADS * B_HEAD_DIM


def _rms(x, g):
    return x * lax.rsqrt(jnp.mean(x * x, axis=-1, keepdims=True) + EPS) * g


def _store_row_tiles(ref, base, val):
    sub = val.shape[1] // LANES
    for s in range(sub):
        ref[pl.ds(base + s, val.shape[0], stride=sub), :] = val[:, s * LANES:(s + 1) * LANES]


def _load_row_tiles(ref, base, n_rows, sub):
    return jnp.concatenate([ref[pl.ds(base + s, n_rows, stride=sub), :] for s in range(sub)], axis=1)


def _mod_kernel(c_ref, w_ref, b_ref, o_ref):
    c = c_ref[...]
    a = c * jax.nn.sigmoid(c)
    o_ref[...] = jnp.dot(a, w_ref[...], precision=HIGHEST, preferred_element_type=F32) + b_ref[...]


def _modulation(c_all, w_ada, b_ada):
    d, n = w_ada.shape
    tn = 512
    return pl.pallas_call(
        _mod_kernel,
        grid=(n // tn,),
        in_specs=[
            pl.BlockSpec((MOD_ROWS, d), lambda j: (0, 0)),
            pl.BlockSpec((d, tn), lambda j: (0, j)),
            pl.BlockSpec((1, tn), lambda j: (0, j)),
        ],
        out_specs=pl.BlockSpec((MOD_ROWS, tn), lambda j: (0, j)),
        out_shape=jax.ShapeDtypeStruct((MOD_ROWS, n), F32),
        name="modulation",
    )(c_all, w_ada, b_ada)


def _rope128(z, cos, sin_signed, first_half):
    rot = jnp.where(first_half, pltpu.roll(z, 96, 1), pltpu.roll(z, 32, 1))
    return z * cos + rot * sin_signed


def _inproj_kernel(x_ref, sh_ref, sc_ref, g_ref, cos_ref, sin_ref, w_ref, *outs, is_ctx):
    x = x_ref[...]
    h = _rms(x, g_ref[...]) * (1.0 + sc_ref[0]) + sh_ref[0]
    hb = h.astype(BF16)
    lane = lax.broadcasted_iota(jnp.int32, (1, LANES), 1)
    first_half = (lane % 64) < 32
    low = lane < 64

    def seg(lo, hi):
        return jnp.dot(hb, w_ref[:, lo:hi], preferred_element_type=F32)

    def rope(z):
        if is_ctx:
            return z
        cos = cos_ref[...]
        sin = sin_ref[...]
        parts = [_rope128(z[:, j:j + LANES], cos, sin, first_half) for j in range(0, z.shape[1], LANES)]
        return parts[0] if len(parts) == 1 else jnp.concatenate(parts, axis=1)

    if is_ctx:
        qa_o, ka_o, va_o, qb_o, kb2_o, vb_o, sga_o, sgb_o, kaf_o, vaf_o, kbf_o, vbf_o = outs
    else:
        qa_o, ka_o, va_o, qb_o, kb2_o, vb_o, sga_o, sgb_o = outs

    scale = A_HEAD_DIM ** -0.5
    qa_o[...] = (rope(seg(_QA, _KA)) * scale).astype(BF16)
    ka = rope(seg(_KA, _VA))
    if is_ctx:
        ka_o[...] = ka.astype(BF16)
    else:
        ka_o[0] = ka.T.astype(BF16)
    va = seg(_VA, _QB)
    va_o[...] = va.astype(BF16)
    qb_o[...] = (rope(seg(_QB, _KB)) * (B_HEAD_DIM ** -0.5)).astype(BF16)
    kb = rope(seg(_KB, _VB))
    kb_sw = pltpu.roll(kb, 64, 1)
    kb2_o[:, 0:LANES] = jnp.where(low, kb, kb_sw).astype(BF16)
    kb2_o[:, LANES:2 * LANES] = jnp.where(low, kb_sw, kb).astype(BF16)
    vb = seg(_VB, _GA)
    vb_o[...] = vb.astype(BF16)
    d = x.shape[1]
    sga_o[...] = jax.nn.sigmoid(seg(_GA, _GA + d)).astype(BF16)
    sgb_o[...] = jax.nn.sigmoid(seg(_GA + d, _GA + 2 * d)).astype(BF16)
    if is_ctx:
        kaf_o[...] = ka
        vaf_o[...] = va
        kbf_o[...] = kb
        vbf_o[...] = vb


def _inproj(x2, mod3, mod_row0, g_pre, cos_t, sin_t, w_in_b, *, seq, tm, is_ctx):
    t, d = x2.shape
    per = seq // tm
    n_in = w_in_b.shape[1]
    wa = A_HEADS * 2 * A_HEAD_DIM
    wkb = B_KV_HEADS * B_HEAD_DIM

    def row(i):
        return (i // per) if not is_ctx else 0

    tok = lambda w: pl.BlockSpec((tm, w), lambda i: (i, 0))
    out_shape = [
        jax.ShapeDtypeStruct((t, wa), BF16), jax.ShapeDtypeStruct((t, wa), BF16),
        jax.ShapeDtypeStruct((t, wa), BF16), jax.ShapeDtypeStruct((t, wa), BF16),
        jax.ShapeDtypeStruct((t, 2 * wkb), BF16), jax.ShapeDtypeStruct((t, wkb), BF16),
        jax.ShapeDtypeStruct((t, d), BF16), jax.ShapeDtypeStruct((t, d), BF16),
    ]
    out_specs = [tok(wa), tok(wa), tok(wa), tok(wa), tok(2 * wkb), tok(wkb), tok(d), tok(d)]
    if not is_ctx:
        out_shape[1] = jax.ShapeDtypeStruct((t // seq, wa, seq), BF16)
        out_specs[1] = pl.BlockSpec((1, wa, tm), lambda i: (i // per, 0, i % per))
    if is_ctx:
        out_shape += [jax.ShapeDtypeStruct((t, wa), F32), jax.ShapeDtypeStruct((t, wa), F32),
                      jax.ShapeDtypeStruct((t, wkb), F32), jax.ShapeDtypeStruct((t, wkb), F32)]
        out_specs += [tok(wa), tok(wa), tok(wkb), tok(wkb)]
    return pl.pallas_call(
        functools.partial(_inproj_kernel, is_ctx=is_ctx),
        grid=(t // tm,),
        in_specs=[
            pl.BlockSpec((tm, d), lambda i: (i, 0)),
            pl.BlockSpec((1, 1, d), lambda i: (mod_row0 + row(i), 0, 0)),
            pl.BlockSpec((1, 1, d), lambda i: (mod_row0 + row(i), 0, 1)),
            pl.BlockSpec((1, d), lambda i: (0, 0)),
            pl.BlockSpec((tm, LANES), lambda i: (i % per, 0)),
            pl.BlockSpec((tm, LANES), lambda i: (i % per, 0)),
            pl.BlockSpec((d, n_in), lambda i: (0, 0)),
        ],
        out_specs=out_specs,
        out_shape=out_shape,
        compiler_params=pltpu.CompilerParams(dimension_semantics=("arbitrary",)),
        name="inproj_ctx" if is_ctx else "inproj_lat",
    )(x2, mod3, mod3, g_pre, cos_t, sin_t, w_in_b)


def _nt(a, b):
    return lax.dot_general(a, b, (((1,), (1,)), ((), ())), preferred_element_type=F32)


def _diff_kernel(lam_ref, g_ref, q_ref, *refs, n_seg, lambda_init):
    kv = refs[:2 * n_seg]
    o_ref = refs[2 * n_seg]
    lp = lam_ref[...]
    lam = (jnp.exp(jnp.sum(lp[0:1] * lp[1:2], axis=-1, keepdims=True))
           - jnp.exp(jnp.sum(lp[2:3] * lp[3:4], axis=-1, keepdims=True)) + lambda_init)
    q = q_ref[0]
    tq = q.shape[0]
    lane = lax.broadcasted_iota(jnp.int32, (1, LANES), 1)
    q2 = jnp.concatenate([q * (lane < 64).astype(BF16), q * (lane >= 64).astype(BF16)], axis=0)
    s = [_nt(q2, kv[2 * j][0].astype(BF16)) for j in range(n_seg)]
    mx = functools.reduce(jnp.maximum, [jnp.max(x, axis=-1, keepdims=True) for x in s])
    acc = None
    for j in range(n_seg):
        v = kv[2 * j + 1][0].astype(BF16)
        v_ext = jnp.concatenate([v, jnp.ones_like(v)], axis=1)
        aj = jnp.dot(jnp.exp(s[j] - mx).astype(BF16), v_ext, preferred_element_type=F32)
        acc = aj if acc is None else acc + aj
    on = acc[:, 0:LANES] / acc[:, LANES:2 * LANES]
    o = on[0:tq] - lam * on[tq:2 * tq]
    o = _rms(o, g_ref[...]) * (1.0 - lambda_init)
    o_ref[0] = o.astype(BF16)


def _diff_attention(lam_p, g_head, q, segs, *, tq, lambda_init):
    b, s, w = q.shape
    in_specs = [
        pl.BlockSpec((4, A_HEAD_DIM), lambda bi, h, qi: (0, 0)),
        pl.BlockSpec((1, A_V_DIM), lambda bi, h, qi: (0, 0)),
        pl.BlockSpec((1, tq, LANES), lambda bi, h, qi: (bi, qi, h)),
    ]
    args = [lam_p, g_head, q]
    for k, v in segs:
        nk = k.shape[1]
        in_specs += [pl.BlockSpec((1, nk, LANES), lambda bi, h, qi: (bi, 0, h)),
                     pl.BlockSpec((1, nk, LANES), lambda bi, h, qi: (bi, 0, h))]
        args += [k, v]
    return pl.pallas_call(
        functools.partial(_diff_kernel, n_seg=len(segs), lambda_init=lambda_init),
        grid=(b, A_HEADS, s // tq),
        in_specs=in_specs,
        out_specs=pl.BlockSpec((1, tq, LANES), lambda bi, h, qi: (bi, qi, h)),
        out_shape=jax.ShapeDtypeStruct((b, s, w), BF16),
        compiler_params=pltpu.CompilerParams(
            dimension_semantics=("arbitrary", "arbitrary", "arbitrary")),
        name="diff_attn_%d" % len(segs),
    )(*args)


def _diff_lat_kernel(lam_ref, g_ref, q_ref, kc_ref, kt_ref, vc_ref, v_ref, o_ref, s_scr, m_scr, *, lambda_init):
    t = pl.program_id(0)
    slot = t % 2
    pslot = 1 - slot
    tq = q_ref.shape[1]
    nkc = kc_ref.shape[1]
    nkn = kt_ref.shape[2]

    @pl.when(t == 0)
    def _():
        s_scr[...] = jnp.zeros_like(s_scr)
        m_scr[...] = jnp.zeros_like(m_scr)

    lp = lam_ref[...]
    lam = (jnp.exp(jnp.sum(lp[0:1] * lp[1:2], axis=-1, keepdims=True))
           - jnp.exp(jnp.sum(lp[2:3] * lp[3:4], axis=-1, keepdims=True)) + lambda_init)
    lane = lax.broadcasted_iota(jnp.int32, (1, LANES), 1)

    mp = m_scr[pslot]
    v_all = jnp.concatenate([vc_ref[0].astype(BF16), v_ref[0]], axis=0)
    v_ext = jnp.concatenate([v_all, jnp.ones_like(v_all)], axis=1)
    p = jnp.concatenate(
        [jnp.exp(s_scr[pslot, :, c:c + LANES] - mp).astype(BF16) for c in range(0, nkc + nkn, LANES)], axis=1)
    acc = jnp.dot(p, v_ext, preferred_element_type=F32)
    on = acc[:, 0:LANES] / acc[:, LANES:2 * LANES]
    o = on[0:tq] - lam * on[tq:2 * tq]
    o_ref[0] = (_rms(o, g_ref[...]) * (1.0 - lambda_init)).astype(BF16)

    q = q_ref[0]
    q2 = jnp.concatenate([q * (lane < 64).astype(BF16), q * (lane >= 64).astype(BF16)], axis=0)
    sc = _nt(q2, kc_ref[0].astype(BF16))
    sn = jnp.dot(q2, kt_ref[0], preferred_element_type=F32)
    mx = jnp.maximum(jnp.max(sc, axis=-1, keepdims=True), jnp.max(sn, axis=-1, keepdims=True))
    s_scr[slot, :, 0:nkc] = sc
    s_scr[slot, :, nkc:nkc + nkn] = sn
    m_scr[slot] = jnp.broadcast_to(mx, (2 * tq, LANES))


def _diff_attention_lat(lam_p, g_head, q, kc, kt, vc, v, *, tq, lambda_init):
    b, s, w = q.shape
    past = kc.shape[1]
    nq = s // tq
    n_units = b * A_HEADS * nq
    last = n_units - 1

    def unit(u):
        return u // (A_HEADS * nq), (u // nq) % A_HEADS, u % nq

    def cur(t):
        return unit(jnp.minimum(t, last))

    def prev(t):
        return unit(jnp.maximum(t - 1, 0))

    return pl.pallas_call(
        functools.partial(_diff_lat_kernel, lambda_init=lambda_init),
        grid=(n_units + 1,),
        in_specs=[
            pl.BlockSpec((4, A_HEAD_DIM), lambda t: (0, 0)),
            pl.BlockSpec((1, A_V_DIM), lambda t: (0, 0)),
            pl.BlockSpec((1, tq, LANES), lambda t: (cur(t)[0], cur(t)[2], cur(t)[1])),
            pl.BlockSpec((1, past, LANES), lambda t: (cur(t)[0], 0, cur(t)[1])),
            pl.BlockSpec((1, LANES, s), lambda t: (cur(t)[0], cur(t)[1], 0)),
            pl.BlockSpec((1, past, LANES), lambda t: (prev(t)[0], 0, prev(t)[1])),
            pl.BlockSpec((1, s, LANES), lambda t: (prev(t)[0], 0, prev(t)[1])),
        ],
        out_specs=pl.BlockSpec((1, tq, LANES), lambda t: (prev(t)[0], prev(t)[2], prev(t)[1])),
        out_shape=jax.ShapeDtypeStruct((b, s, w), BF16),
        scratch_shapes=[pltpu.VMEM((2, 2 * tq, past + s), F32), pltpu.VMEM((2, 2 * tq, LANES), F32)],
        compiler_params=pltpu.CompilerParams(dimension_semantics=("arbitrary",),
                                             vmem_limit_bytes=56 * 1024 * 1024),
        name="diff_attn_lat",
    )(lam_p, g_head, q, kc, kt, vc, v)


def _swa_kernel(sink_ref, q_ref, *refs, tq, seg_kinds):
    n_seg = len(seg_kinds)
    k_refs = refs[:n_seg]
    v_refs = refs[n_seg:2 * n_seg]
    o_ref = refs[2 * n_seg]
    i = pl.program_id(1)
    nb = pl.num_programs(1)
    lane = lax.broadcasted_iota(jnp.int32, (1, LANES), 1)
    low = lane < 64
    lane2 = lax.broadcasted_iota(jnp.int32, (1, 2 * LANES), 1)
    head_masks = [((lane2 // 64) == g).astype(BF16) for g in range(B_GROUP)]
    qi = lax.broadcasted_iota(jnp.int32, (B_GROUP * tq, SWA_Q), 0) & (tq - 1)
    kj = lax.broadcasted_iota(jnp.int32, (B_GROUP * tq, SWA_Q), 1)
    far = 2 * SWA_Q
    gw = B_GROUP * B_HEAD_DIM

    vs = [v_ref[0].astype(BF16) for v_ref in v_refs]
    v_all = vs[0] if n_seg == 1 else jnp.concatenate(vs, axis=0)
    v_ext = jnp.concatenate([v_all, jnp.ones_like(v_all)], axis=1)

    for n in range(B_KV_HEADS):
        q = q_ref[0, :, n * gw:(n + 1) * gw]
        qs = jnp.concatenate([q * hm for hm in head_masks], axis=0)
        ks = []
        for kind, k_ref in zip(seg_kinds, k_refs):
            if kind == "cache":
                k = k_ref[0]
                sw = pltpu.roll(k, 64, 1)
                k2 = (jnp.where(low, k, sw) if n == 0 else jnp.where(low, sw, k)).astype(BF16)
            else:
                k2 = k_ref[0, :, n * LANES:(n + 1) * LANES]
            ks.append(jnp.concatenate([k2, k2], axis=1))
        k_all = ks[0] if n_seg == 1 else jnp.concatenate(ks, axis=0)
        s = _nt(qs, k_all)
        chunks, col = [], 0
        for kind, k in zip(seg_kinds, ks):
            for c in range(0, k.shape[0], LANES):
                sc = s[:, col + c:col + c + LANES]
                if kind == "left":
                    sc = jnp.where(kj >= qi + jnp.where(i > 0, 0, far), sc, NEG_INF)
                elif kind == "right":
                    sc = jnp.where(kj <= qi - jnp.where(i < nb - 1, 0, far), sc, NEG_INF)
                chunks.append(sc)
            col += k.shape[0]
        sinkcol = jnp.concatenate(
            [jnp.full((tq, 1), sink_ref[n * B_GROUP + g], F32) for g in range(B_GROUP)], axis=0)
        mx = jnp.maximum(jnp.max(functools.reduce(jnp.maximum, chunks), axis=-1, keepdims=True), sinkcol)
        p = jnp.concatenate([jnp.exp(c - mx).astype(BF16) for c in chunks], axis=1)
        acc = jnp.dot(p, v_ext, preferred_element_type=F32)
        o = acc[:, 0:LANES] / (acc[:, LANES:2 * LANES] + jnp.exp(sinkcol - mx))
        osw = pltpu.roll(o, 64, 1)
        for j in range(B_GROUP // 2):
            ra = slice((2 * j) * tq, (2 * j + 1) * tq)
            rb = slice((2 * j + 1) * tq, (2 * j + 2) * tq)
            pair = jnp.where(low, o[ra], osw[rb]) if n == 0 else jnp.where(low, osw[ra], o[rb])
            o_ref[0, :, n * gw + j * LANES:n * gw + (j + 1) * LANES] = pair.astype(BF16)


def _swa_attention(sink, q, segs, *, tq):
    b, s, w = q.shape
    in_specs = [
        pl.BlockSpec(memory_space=pltpu.SMEM),
        pl.BlockSpec((1, tq, w), lambda bi, i: (bi, i, 0)),
    ]
    k_specs, v_specs, k_args, v_args, kinds = [], [], [], [], []
    for kind, k, v, rows, idx in segs:
        k_specs.append(pl.BlockSpec((1, rows, k.shape[2]), lambda bi, i, idx=idx: (bi, idx(i), 0)))
        v_specs.append(pl.BlockSpec((1, rows, v.shape[2]), lambda bi, i, idx=idx: (bi, idx(i), 0)))
        k_args.append(k)
        v_args.append(v)
        kinds.append(kind)
    return pl.pallas_call(
        functools.partial(_swa_kernel, tq=tq, seg_kinds=tuple(kinds)),
        grid=(b, s // tq),
        in_specs=in_specs + k_specs + v_specs,
        out_specs=pl.BlockSpec((1, tq, w), lambda bi, i: (bi, i, 0)),
        out_shape=jax.ShapeDtypeStruct((b, s, w), BF16),
        compiler_params=pltpu.CompilerParams(dimension_semantics=("arbitrary", "arbitrary")),
        name="swa_attn_%d" % len(segs),
    )(sink, q, *k_args, *v_args)


def _postmix_kernel(*refs, n_ctx_tiles):
    (xc, xl, oac, oal, obc, obl, sgac, sgal, sgbc, sgbl, g1_ref, sh2_ref, sc2_ref, gpm_ref, gpf_ref,
     wpa_ref, wpb_ref, wo_ref, wr_ref, wrl_ref, br_ref, x1_o, h2_o, route_o) = refs
    is_ctx = pl.program_id(0) < n_ctx_tiles
    pick = lambda a, b: jnp.where(is_ctx, a[...], b[...])
    pa = jnp.dot(pick(oac, oal), wpa_ref[...], preferred_element_type=F32)
    pb = jnp.dot(pick(obc, obl), wpb_ref[...], preferred_element_type=F32)
    mix = pick(sgac, sgal).astype(F32) * pa + pick(sgbc, sgbl).astype(F32) * pb
    m2 = jnp.dot(mix.astype(BF16), wo_ref[...], preferred_element_type=F32)
    x1 = pick(xc, xl) + g1_ref[0] * _rms(m2, gpm_ref[...])
    x1_o[...] = x1
    h2 = _rms(x1, gpf_ref[...]) * (1.0 + sc2_ref[0]) + sh2_ref[0]
    h2_o[...] = h2.astype(BF16)

    h_hi = h2.astype(BF16)
    h_lo = (h2 - h_hi.astype(F32)).astype(BF16)
    logits = (jnp.dot(h_hi, wr_ref[...], preferred_element_type=F32)
              + jnp.dot(h_lo, wr_ref[...], preferred_element_type=F32)
              + jnp.dot(h_hi, wrl_ref[...], preferred_element_type=F32) + br_ref[...])
    tm = logits.shape[0]
    lane_i = lax.broadcasted_iota(jnp.int32, (tm, LANES), 1)
    lane = lane_i.astype(F32)
    big = float(LANES)
    is_g = lane_i < N_GROUPS
    lg = jnp.where(is_g, logits, -jnp.inf)
    mg = jnp.max(lg, axis=-1, keepdims=True)
    g_sel = jnp.min(jnp.where(is_g & (logits == mg), lane, big), axis=-1, keepdims=True)
    g_w = 1.0 / jnp.sum(jnp.where(is_g, jnp.exp(lg - mg), 0.0), axis=-1, keepdims=True)
    lane_grp = lax.shift_right_arithmetic(lane_i - N_GROUPS, 3).astype(F32)
    in_grp = (lane_grp == g_sel) & (lane_i < N_GROUPS + N_EXPERTS)
    le = jnp.where(in_grp, logits, -jnp.inf)
    v0 = jnp.max(le, axis=-1, keepdims=True)
    i0 = jnp.min(jnp.where(in_grp & (logits == v0), lane, big), axis=-1, keepdims=True)
    rest = in_grp & (lane != i0)
    le1 = jnp.where(rest, logits, -jnp.inf)
    v1 = jnp.max(le1, axis=-1, keepdims=True)
    i1 = jnp.min(jnp.where(rest & (logits == v1), lane, big), axis=-1, keepdims=True)
    e = jnp.exp(v1 - v0)
    w0 = g_w / (1.0 + e)
    w1 = g_w * e / (1.0 + e)
    e0 = i0 - N_GROUPS
    e1 = i1 - N_GROUPS
    route_o[...] = jnp.where(lane_i == 0, e0, jnp.where(lane_i == 1, e1,
                             jnp.where(lane_i == 2, w0, jnp.where(lane_i == 3, w1, 0.0))))


def _postmix(ctx_in, lat_in, mod3, gpm, gpf, wpa, wpb, wo, wr, wrl, br, *, lat_seq, tm):
    t_ctx, d = ctx_in[0].shape
    t_lat = lat_in[0].shape[0]
    assert t_ctx % tm == 0 and lat_seq % tm == 0
    nc = t_ctx // tm
    nl = t_lat // tm
    per = lat_seq // tm
    sub = d // LANES
    t_all = t_ctx + t_lat

    mod_row = lambda i: jnp.where(i < nc, 0, 1 + jnp.maximum(i - nc, 0) // per)
    full = lambda a: pl.BlockSpec(a.shape, lambda i: (0,) * a.ndim)
    modspec = lambda c: pl.BlockSpec((1, 1, d), lambda i: (mod_row(i), 0, c))
    in_specs, args = [], []
    for a_c, a_l in zip(ctx_in, lat_in):
        w = a_c.shape[1]
        in_specs += [pl.BlockSpec((tm, w), lambda i: (jnp.minimum(i, nc - 1), 0)),
                     pl.BlockSpec((tm, w), lambda i: (jnp.maximum(i - nc, 0), 0))]
        args += [a_c, a_l]
    in_specs += [modspec(2), modspec(3), modspec(4), full(gpm), full(gpf),
                 full(wpa), full(wpb), full(wo), full(wr), full(wrl), full(br)]
    args += [mod3, mod3, mod3, gpm, gpf, wpa, wpb, wo, wr, wrl, br]
    return pl.pallas_call(
        functools.partial(_postmix_kernel, n_ctx_tiles=nc),
        grid=(nc + nl,),
        in_specs=in_specs,
        out_specs=[pl.BlockSpec((tm, d), lambda i: (i, 0)),
                   pl.BlockSpec((tm, d), lambda i: (i, 0)),
                   pl.BlockSpec((tm, LANES), lambda i: (i, 0))],
        out_shape=[jax.ShapeDtypeStruct((t_all, d), F32),
                   jax.ShapeDtypeStruct((t_all, d), BF16),
                   jax.ShapeDtypeStruct((t_all, LANES), F32)],
        compiler_params=pltpu.CompilerParams(dimension_semantics=("arbitrary",)),
        name="postmix",
    )(*args)


def _segment_copies(src, src_row, dst, dst_row, n, sub, sem):
    piece = 64

    def copy(off, size):
        return pltpu.make_async_copy(src.at[pl.ds(pl.multiple_of((src_row + off) * sub, sub), size * sub)],
                                     dst.at[pl.ds(pl.multiple_of((dst_row + off) * sub, sub), size * sub)], sem)

    def full(c, carry):
        copy(c * piece, piece).start()
        return carry
    lax.fori_loop(0, n // piece, full, 0)
    size = piece // 2
    while size >= 1:
        @pl.when((n & size) != 0)
        def _(size=size):
            copy(n & ~(2 * size - 1), size).start()
        size //= 2


def _local_positions(route, tile_base):
    tm = route.shape[0]
    lane = lax.broadcasted_iota(jnp.int32, (tm, LANES), 1).astype(F32)
    is0 = lane == route[:, 0:1]
    is1 = lane == route[:, 1:2]
    earlier = (lax.broadcasted_iota(jnp.int32, (tm, tm), 1)
               < lax.broadcasted_iota(jnp.int32, (tm, tm), 0)).astype(BF16)
    pre0 = jnp.dot(earlier, is0.astype(BF16), preferred_element_type=F32)
    pre1 = jnp.dot(earlier, is1.astype(BF16), preferred_element_type=F32)
    cnt0 = jnp.sum(is0.astype(F32), axis=0, keepdims=True)
    lpos0 = jnp.sum(jnp.where(is0, tile_base + pre0, 0.0), axis=-1, keepdims=True)
    lpos1 = jnp.sum(jnp.where(is1, tile_base + cnt0 + pre1, 0.0), axis=-1, keepdims=True)
    return lpos0.astype(jnp.int32), lpos1.astype(jnp.int32)


def _dispatch_kernel(ss_ref, sl_ref, tb_ref, h_ref, r_ref, tbv_ref, xs_hbm, pbuf, zbuf, sem, zsem, *, n_asg, rows):
    i = pl.program_id(0)
    nt = pl.num_programs(0)
    tm, d = h_ref.shape
    sub = d // LANES
    nrow = TOP_K * tm
    slot = i % 2

    def wait_slot(s):
        pltpu.make_async_copy(pbuf.at[pl.ds(pl.multiple_of(s * nrow * sub, nrow * sub), nrow * sub)],
                              xs_hbm.at[pl.ds(0, nrow * sub)], sem.at[s]).wait()

    def slack_copy():
        return pltpu.make_async_copy(zbuf, xs_hbm.at[pl.ds(n_asg * sub, rows * sub)], zsem.at[0])

    @pl.when(i == 0)
    def _():
        zbuf[...] = jnp.zeros_like(zbuf)
        slack_copy().start()

    lpos0, lpos1 = _local_positions(r_ref[...], tbv_ref[0])
    p = lax.broadcasted_iota(jnp.int32, (tm, nrow), 1)
    sel = ((p == lpos0) | (p == lpos1)).astype(BF16)
    xp = lax.dot_general(sel, h_ref[...].astype(BF16), (((0,), (0,)), ((), ())),
                         preferred_element_type=F32)

    @pl.when(i >= 2)
    def _():
        wait_slot(slot)

    _store_row_tiles(pbuf, slot * nrow * sub, xp)

    def seg(e, c):
        k = i * N_EXPERTS + e
        _segment_copies(pbuf, slot * nrow + tb_ref[k], xs_hbm, ss_ref[k], sl_ref[k], sub, sem.at[slot])
        return c
    lax.fori_loop(0, N_EXPERTS, seg, 0)

    @pl.when(i == nt - 1)
    def _():
        wait_slot(slot)

        @pl.when(nt >= 2)
        def _():
            wait_slot(1 - slot)
        slack_copy().wait()


def _dispatch(h2, route, tables, *, tm, rows):
    seg_start, seg_len, tile_base, tile_base_v = tables
    t, d = h2.shape
    sub = d // LANES
    n_asg = t * TOP_K
    grid_spec = pltpu.PrefetchScalarGridSpec(
        num_scalar_prefetch=3,
        grid=(t // tm,),
        in_specs=[
            pl.BlockSpec((tm, d), lambda i, *_: (i, 0)),
            pl.BlockSpec((tm, LANES), lambda i, *_: (i, 0)),
            pl.BlockSpec((1, 1, LANES), lambda i, *_: (i, 0, 0)),
        ],
        out_specs=pl.BlockSpec(memory_space=pl.ANY),
        scratch_shapes=[pltpu.VMEM((2 * TOP_K * tm * sub, LANES), F32), pltpu.VMEM((rows * sub, LANES), F32),
                        pltpu.SemaphoreType.DMA((2,)), pltpu.SemaphoreType.DMA((1,))],
    )
    return pl.pallas_call(
        functools.partial(_dispatch_kernel, n_asg=n_asg, rows=rows),
        grid_spec=grid_spec,
        out_shape=jax.ShapeDtypeStruct(((n_asg + rows) * sub, LANES), F32),
        compiler_params=pltpu.CompilerParams(dimension_semantics=("arbitrary",)),
        name="dispatch",
    )(seg_start, seg_len, tile_base, h2, route, tile_base_v)


def _moe_kernel(be_ref, row0_ref, nact_ref, xs_hbm, w1_ref, w3_ref, w2_ref, ys_hbm,
                xbuf, obuf, w1b, w3b, w2b, rsem, wsem, *, rows, sub):
    i = pl.program_id(0)
    nact = nact_ref[0]
    slot = i % 2
    nslot = 1 - slot
    blk = rows * sub

    def read(j, s):
        return pltpu.make_async_copy(xs_hbm.at[pl.ds(pl.multiple_of(row0_ref[j] * sub, sub), blk)],
                                     xbuf.at[pl.ds(pl.multiple_of(s * blk, blk), blk)], rsem.at[s])

    def write(j, s):
        return pltpu.make_async_copy(obuf.at[pl.ds(pl.multiple_of(s * blk, blk), blk)],
                                     ys_hbm.at[pl.ds(pl.multiple_of(row0_ref[j] * sub, sub), blk)], wsem.at[s])

    @pl.when(i == 0)
    def _():
        read(0, 0).start()

    @pl.when(i < nact)
    def _():
        @pl.when(i + 1 < nact)
        def _():
            read(i + 1, nslot).start()

        changed = jnp.logical_or(i == 0, be_ref[i] != be_ref[jnp.maximum(i - 1, 0)])

        @pl.when(changed)
        def _():
            w1b[...] = w1_ref[0].astype(BF16)
            w3b[...] = w3_ref[0].astype(BF16)
            w2b[...] = w2_ref[0].astype(BF16)

        read(i, slot).wait()
        x = _load_row_tiles(xbuf, slot * blk, rows, sub).astype(BF16)
        a = jnp.dot(x, w1b[...], preferred_element_type=F32)
        b = jnp.dot(x, w3b[...], preferred_element_type=F32)
        hmid = (a * jax.nn.sigmoid(a) * b).astype(BF16)
        y = jnp.dot(hmid, w2b[...], preferred_element_type=F32)
        _store_row_tiles(obuf, slot * blk, y)

        @pl.when(i >= 1)
        def _():
            write(i - 1, nslot).wait()
        write(i, slot).start()

    @pl.when(i == nact)
    def _():
        write(i - 1, nslot).wait()
        obuf[pl.ds(pl.multiple_of(slot * blk, blk), blk), :] = jnp.zeros((blk, LANES), F32)
        tail = pltpu.make_async_copy(obuf.at[pl.ds(pl.multiple_of(slot * blk, blk), blk)],
                                     ys_hbm.at[pl.ds(ys_hbm.shape[0] - blk, blk)], wsem.at[slot])
        tail.start()
        tail.wait()


def _moe(blk_expert, row0, nact, xs, w1, w3, w2):
    nblk = blk_expert.shape[0] - 1
    d, de = w1.shape[1], w1.shape[2]
    sub = d // LANES
    rows = MOE_ROWS
    grid_spec = pltpu.PrefetchScalarGridSpec(
        num_scalar_prefetch=3,
        grid=(nblk + 1,),
        in_specs=[
            pl.BlockSpec(memory_space=pl.ANY),
            pl.BlockSpec((1, d, de), lambda i, be, r0, na: (be[i], 0, 0)),
            pl.BlockSpec((1, d, de), lambda i, be, r0, na: (be[i], 0, 0)),
            pl.BlockSpec((1, de, d), lambda i, be, r0, na: (be[i], 0, 0)),
        ],
        out_specs=pl.BlockSpec(memory_space=pl.ANY),
        scratch_shapes=[
            pltpu.VMEM((2 * rows * sub, LANES), F32),
            pltpu.VMEM((2 * rows * sub, LANES), F32),
            pltpu.VMEM((d, de), BF16),
            pltpu.VMEM((d, de), BF16),
            pltpu.VMEM((de, d), BF16),
            pltpu.SemaphoreType.DMA((2,)),
            pltpu.SemaphoreType.DMA((2,)),
        ],
    )
    return pl.pallas_call(
        functools.partial(_moe_kernel, rows=rows, sub=sub),
        grid_spec=grid_spec,
        out_shape=jax.ShapeDtypeStruct(xs.shape, F32),
        compiler_params=pltpu.CompilerParams(dimension_semantics=("arbitrary",)),
        name="expert_mlp",
    )(blk_expert, row0, nact, xs, w1, w3, w2)


def _combine_kernel(ss_ref, sl_ref, tb_ref, x1_ref, r_ref, tbv_ref, g2_ref, gpost_ref, ys_hbm, o_ref,
                    ybuf, sem, *, tile0):
    i = pl.program_id(0)
    nt = pl.num_programs(0)
    tm, d = x1_ref.shape
    sub = d // LANES
    nrow = TOP_K * tm
    slot = i % 2

    def fetch(tile, s):
        def seg(e, c):
            k = tile * N_EXPERTS + e
            _segment_copies(ys_hbm, ss_ref[k], ybuf, s * nrow + tb_ref[k], sl_ref[k], sub, sem.at[s])
            return c
        lax.fori_loop(0, N_EXPERTS, seg, 0)

    @pl.when(i == 0)
    def _():
        fetch(tile0, 0)

    @pl.when(i + 1 < nt)
    def _():
        fetch(tile0 + i + 1, 1 - slot)

    pltpu.make_async_copy(ys_hbm.at[pl.ds(0, nrow * sub)],
                          ybuf.at[pl.ds(pl.multiple_of(slot * nrow * sub, nrow * sub), nrow * sub)],
                          sem.at[slot]).wait()
    r = r_ref[...]
    lpos0, lpos1 = _local_positions(r, tbv_ref[0])
    p = lax.broadcasted_iota(jnp.int32, (tm, nrow), 1)
    q = (jnp.where(p == lpos0, r[:, 2:3], 0.0) + jnp.where(p == lpos1, r[:, 3:4], 0.0)).astype(BF16)
    ysort = _load_row_tiles(ybuf, slot * nrow * sub, nrow, sub).astype(BF16)
    y = jnp.dot(q, ysort, preferred_element_type=F32)
    o_ref[...] = x1_ref[...] + g2_ref[0] * _rms(y, gpost_ref[...])


def _combine(x1, ys, route, tables, mod3, mod_row0, gpost, *, t, seq, tm, tok_off, is_ctx):
    seg_start, seg_len, tile_base, tile_base_v = tables
    t_all, d = x1.shape
    per = seq // tm
    sub = d // LANES
    boff = tok_off // tm

    def row(i):
        return (i // per) if not is_ctx else 0

    grid_spec = pltpu.PrefetchScalarGridSpec(
        num_scalar_prefetch=3,
        grid=(t // tm,),
        in_specs=[
            pl.BlockSpec((tm, d), lambda i, *_: (boff + i, 0)),
            pl.BlockSpec((tm, LANES), lambda i, *_: (boff + i, 0)),
            pl.BlockSpec((1, 1, LANES), lambda i, *_: (boff + i, 0, 0)),
            pl.BlockSpec((1, 1, d), lambda i, *_: (mod_row0 + row(i), 0, 5)),
            pl.BlockSpec((1, d), lambda i, *_: (0, 0)),
            pl.BlockSpec(memory_space=pl.ANY),
        ],
        out_specs=pl.BlockSpec((tm, d), lambda i, *_: (i, 0)),
        scratch_shapes=[pltpu.VMEM((2 * TOP_K * tm * sub, LANES), F32), pltpu.SemaphoreType.DMA((2,))],
    )
    return pl.pallas_call(
        functools.partial(_combine_kernel, tile0=boff),
        grid_spec=grid_spec,
        out_shape=jax.ShapeDtypeStruct((t, d), F32),
        compiler_params=pltpu.CompilerParams(dimension_semantics=("arbitrary",)),
        name="combine_ctx" if is_ctx else "combine_lat",
    )(seg_start, seg_len, tile_base, x1, route, tile_base_v, mod3, gpost, ys)


def _routing_tables(route, tm, rows):
    t = route.shape[0]
    nt = t // tm
    n_asg = t * TOP_K
    ex = jnp.arange(N_EXPERTS, dtype=jnp.int32)
    e01 = route[:, 0:TOP_K].astype(jnp.int32)
    cnt_te = jnp.sum((e01.reshape(nt, tm * TOP_K, 1) == ex).astype(jnp.int32), axis=1)
    cnt_e = jnp.sum(cnt_te, axis=0)
    start_e = jnp.cumsum(cnt_e) - cnt_e
    seg_start = start_e[None, :] + jnp.cumsum(cnt_te, axis=0) - cnt_te
    tile_base = jnp.cumsum(cnt_te, axis=1) - cnt_te
    tile_base_v = jnp.pad(tile_base.astype(F32), ((0, 0), (0, LANES - N_EXPERTS))).reshape(nt, 1, LANES)

    nblk_e = (cnt_e + rows - 1) // rows
    blk_end = jnp.cumsum(nblk_e)
    blk_start = blk_end - nblk_e
    n_blocks = n_asg // rows + N_EXPERTS
    b = jnp.arange(n_blocks + 1, dtype=jnp.int32)
    be = jnp.minimum(jnp.sum((blk_end[None, :] <= b[:, None]).astype(jnp.int32), axis=1), N_EXPERTS - 1)
    first = jnp.sum(jnp.where(be[:, None] == ex[None, :], (start_e - blk_start * rows)[None, :], 0), axis=1)
    row0 = jnp.clip(first + b * rows, 0, n_asg)
    nact = blk_end[-1:].astype(jnp.int32)
    seg = (seg_start.reshape(-1).astype(jnp.int32), cnt_te.reshape(-1).astype(jnp.int32),
           tile_base.reshape(-1).astype(jnp.int32), tile_base_v)
    return seg, be.astype(jnp.int32), row0.astype(jnp.int32), nact


def _rope_tables(n_tok):
    n_rows = n_tok // GRID_W
    rows = jnp.repeat(jnp.arange(n_rows), GRID_W).astype(F32)
    cols = jnp.tile(jnp.arange(GRID_W), n_rows).astype(F32)
    quarter = A_HEAD_DIM // 4
    inv = ROPE_BASE ** (-jnp.arange(quarter, dtype=F32) / quarter)
    ang = jnp.concatenate([rows[:, None] * inv, cols[:, None] * inv], axis=-1)
    cos, sin = jnp.cos(ang), jnp.sin(ang)
    cos_t = jnp.tile(jnp.concatenate([cos, cos], axis=-1), (1, LANES // A_HEAD_DIM))
    sin_t = jnp.tile(jnp.concatenate([-sin, sin], axis=-1), (1, LANES // A_HEAD_DIM))
    return cos_t, sin_t


def kernel(x_prompt, x_sample, c, cache_diff_k, cache_diff_v, cache_swa_k, cache_swa_v, c_ctx, w_ada, b_ada, g_pre_mix, g_post_mix, g_pre_ffn, g_post_ffn, w_in, lam_q1, lam_k1, lam_q2, lam_k2, g_diff_head, sink, w_proj_a, w_proj_b, w_out, w_router_group, b_router_group, w_router_expert, b_router_expert, w_e1, w_e3, w_e2):
    depth = w_in.shape[0]
    assert depth == 1
    l = 0
    bp, sp, d = x_prompt.shape
    bs, ss, _ = x_sample.shape
    lambda_init = 0.8 - 0.6 * math.exp(-0.3 * l)
    assert A_HEAD_DIM == B_HEAD_DIM and ss % GRID_W == 0 and bs + 1 <= MOD_ROWS

    c_all = jnp.concatenate([c_ctx[None, :], c, jnp.zeros((MOD_ROWS - 1 - bs, d), F32)], axis=0)
    mod = _modulation(c_all, w_ada[l], b_ada[l][None, :])
    mod3 = mod.reshape(MOD_ROWS, 1, 6 * d)

    w_in_b = w_in[l].astype(BF16)
    wpa = w_proj_a[l].astype(BF16)
    wpb = w_proj_b[l].astype(BF16)
    wo = w_out[l].astype(BF16)
    n_r = N_GROUPS + N_EXPERTS
    wr = jnp.concatenate([w_router_group[l], w_router_expert[l], jnp.zeros((d, LANES - n_r), F32)], axis=1)
    br = jnp.concatenate([b_router_group[l], b_router_expert[l], jnp.zeros((LANES - n_r,), F32)])[None, :]
    wr_hi = wr.astype(BF16)
    wr_lo = (wr - wr_hi.astype(F32)).astype(BF16)
    lam_p = jnp.stack([lam_q1[l], lam_k1[l], lam_q2[l], lam_k2[l]], axis=0)
    g_head = g_diff_head[l][None, :]
    sink_l = sink[l]
    cos_t, sin_t = _rope_tables(ss)

    xp2 = x_prompt.reshape(bp * sp, d)
    xs2 = x_sample.reshape(bs * ss, d)
    gpre = g_pre_mix[l][None, :]

    (qa_c, ka_c, va_c, qb_c, kb2_c, vb_c, sga_c, sgb_c, kaf, vaf, kbf, vbf) = _inproj(
        xp2, mod3, 0, gpre, cos_t, sin_t, w_in_b, seq=sp, tm=sp, is_ctx=True)
    r3 = lambda a, b_: a.reshape(b_, -1, a.shape[-1])
    oa_c = _diff_attention(lam_p, g_head, r3(qa_c, bp), [(r3(ka_c, bp), r3(va_c, bp))],
                           tq=sp, lambda_init=lambda_init)
    ob_c = _swa_attention(sink_l, r3(qb_c, bp),
                          [("rep", r3(kb2_c, bp), r3(vb_c, bp), sp, lambda i: 0)], tq=sp)

    (qa_s, ka_s, va_s, qb_s, kb2_s, vb_s, sga_s, sgb_s) = _inproj(
        xs2, mod3, 1, gpre, cos_t, sin_t, w_in_b, seq=ss, tm=512, is_ctx=False)
    past = cache_diff_k.shape[2]
    ck = cache_diff_k[:, l].reshape(bs, past, -1)
    cv = cache_diff_v[:, l].reshape(bs, past, -1)
    oa_s = _diff_attention_lat(lam_p, g_head, r3(qa_s, bs), ck, ka_s, cv, r3(va_s, bs),
                               tq=512, lambda_init=lambda_init)
    sk = cache_swa_k[:, l].reshape(bs, past, -1)
    sv = cache_swa_v[:, l].reshape(bs, past, -1)
    nqb = ss // SWA_Q
    kb2_3, vb_3 = r3(kb2_s, bs), r3(vb_s, bs)
    ob_s = _swa_attention(sink_l, r3(qb_s, bs), [
        ("cache", sk, sv, past, lambda i: 0),
        ("left", kb2_3, vb_3, SWA_Q, lambda i: jnp.maximum(i - 1, 0)),
        ("center", kb2_3, vb_3, SWA_Q, lambda i: i),
        ("right", kb2_3, vb_3, SWA_Q, lambda i: jnp.minimum(i + 1, nqb - 1)),
    ], tq=SWA_Q)

    gpm = g_post_mix[l][None, :]
    gpf = g_pre_ffn[l][None, :]
    t_ctx, t_lat = bp * sp, bs * ss
    x1, h2t, route = _postmix(
        (xp2, oa_c.reshape(t_ctx, -1), ob_c.reshape(t_ctx, -1), sga_c, sgb_c),
        (xs2, oa_s.reshape(t_lat, -1), ob_s.reshape(t_lat, -1), sga_s, sgb_s),
        mod3, gpm, gpf, wpa, wpb, wo, wr_hi, wr_lo, br, lat_seq=ss, tm=512)

    tables, blk_expert, row0, nact = _routing_tables(route, MOE_TILE, MOE_ROWS)
    xs = _dispatch(h2t, route, tables, tm=MOE_TILE, rows=MOE_ROWS)
    ys = _moe(blk_expert, row0, nact, xs, w_e1[l], w_e3[l], w_e2[l])

    gpost = g_post_ffn[l][None, :]
    y_p = _combine(x1, ys, route, tables, mod3, 0, gpost, t=t_ctx, seq=sp, tm=MOE_TILE, tok_off=0, is_ctx=True)
    y_s = _combine(x1, ys, route, tables, mod3, 1, gpost, t=t_lat, seq=ss, tm=MOE_TILE, tok_off=t_ctx,
                   is_ctx=False)

    ha = A_HEADS
    return (y_p.reshape(bp, sp, d), y_s.reshape(bs, ss, d),
            kaf.reshape(bp, 1, sp, ha, 2, A_HEAD_DIM), vaf.reshape(bp, 1, sp, ha, A_V_DIM),
            kbf.reshape(bp, 1, sp, B_KV_HEADS, B_HEAD_DIM), vbf.reshape(bp, 1, sp, B_KV_HEADS, B_HEAD_DIM))
```

```python
import functools
import math

import jax
import jax.numpy as jnp
from jax import lax
from jax.experimental import pallas as pl
from jax.experimental.pallas import tpu as pltpu

F32 = jnp.float32
BF16 = jnp.bfloat16
HIGHEST = lax.Precision.HIGHEST

GRID_W = 64
ROPE_BASE = 10000.0
EPS = 1e-6
NEG_INF = -1e30
A_HEADS = 4
A_HEAD_DIM = 64
A_V_DIM = 2 * A_HEAD_DIM
B_HEADS = 8
B_KV_HEADS = 2
B_GROUP = B_HEADS // B_KV_HEADS
B_HEAD_DIM = 64
WINDOW = 128
N_GROUPS = 4
EXPERTS_PER_GROUP = 8
N_EXPERTS = N_GROUPS * EXPERTS_PER_GROUP
TOP_K = 2

LANES = 128
MOD_ROWS = 16
MOE_ROWS = 256
MOE_TILE = 512
SWA_Q = 128

_QA = 0
_KA = _QA + A_HEADS * 2 * A_HEAD_DIM
_VA = _KA + A_HEADS * 2 * A_HEAD_DIM
_QB = _VA + A_HEADS * A_V_DIM
_KB = _QB + B_HEADS * B_HEAD_DIM
_VB = _KB + B_KV_HEADS * B_HEAD_DIM
_GA = _VB + B_KV_HEADS * B_HEAD_DIM


def _rms(x, g):
    return x * lax.rsqrt(jnp.mean(x * x, axis=-1, keepdims=True) + EPS) * g


def _store_row_tiles(ref, base, val):
    sub = val.shape[1] // LANES
    for s in range(sub):
        ref[pl.ds(base + s, val.shape[0], stride=sub), :] = val[:, s * LANES:(s + 1) * LANES]


def _load_row_tiles(ref, base, n_rows, sub):
    return jnp.concatenate([ref[pl.ds(base + s, n_rows, stride=sub), :] for s in range(sub)], axis=1)


def _mod_kernel(c_ref, w_ref, b_ref, o_ref):
    c = c_ref[...]
    a = c * jax.nn.sigmoid(c)
    o_ref[...] = jnp.dot(a, w_ref[...], precision=HIGHEST, preferred_element_type=F32) + b_ref[...]


def _modulation(c_all, w_ada, b_ada):
    d, n = w_ada.shape
    tn = 512
    return pl.pallas_call(
        _mod_kernel,
        grid=(n // tn,),
        in_specs=[
            pl.BlockSpec((MOD_ROWS, d), lambda j: (0, 0)),
            pl.BlockSpec((d, tn), lambda j: (0, j)),
            pl.BlockSpec((1, tn), lambda j: (0, j)),
        ],
        out_specs=pl.BlockSpec((MOD_ROWS, tn), lambda j: (0, j)),
        out_shape=jax.ShapeDtypeStruct((MOD_ROWS, n), F32),
        name="modulation",
    )(c_all, w_ada, b_ada)


def _rope128(z, cos, sin_signed, first_half):
    rot = jnp.where(first_half, pltpu.roll(z, 96, 1), pltpu.roll(z, 32, 1))
    return z * cos + rot * sin_signed


def _inproj_kernel(x_ref, sh_ref, sc_ref, g_ref, cos_ref, sin_ref, w_ref, *outs, is_ctx):
    x = x_ref[...]
    h = _rms(x, g_ref[...]) * (1.0 + sc_ref[0]) + sh_ref[0]
    hb = h.astype(BF16)
    lane = lax.broadcasted_iota(jnp.int32, (1, LANES), 1)
    first_half = (lane % 64) < 32
    low = lane < 64

    def seg(lo, hi):
        return jnp.dot(hb, w_ref[:, lo:hi], preferred_element_type=F32)

    def rope(z):
        if is_ctx:
            return z
        cos = cos_ref[...]
        sin = sin_ref[...]
        parts = [_rope128(z[:, j:j + LANES], cos, sin, first_half) for j in range(0, z.shape[1], LANES)]
        return parts[0] if len(parts) == 1 else jnp.concatenate(parts, axis=1)

    if is_ctx:
        qa_o, ka_o, va_o, qb_o, kb2_o, vb_o, sga_o, sgb_o, kaf_o, vaf_o, kbf_o, vbf_o = outs
    else:
        qa_o, ka_o, va_o, qb_o, kb2_o, vb_o, sga_o, sgb_o = outs

    scale = A_HEAD_DIM ** -0.5
    qa_o[...] = (rope(seg(_QA, _KA)) * scale).astype(BF16)
    ka = rope(seg(_KA, _VA))
    if is_ctx:
        ka_o[...] = ka.astype(BF16)
    else:
        ka_o[0] = ka.T.astype(BF16)
    va = seg(_VA, _QB)
    va_o[...] = va.astype(BF16)
    qb_o[...] = (rope(seg(_QB, _KB)) * (B_HEAD_DIM ** -0.5)).astype(BF16)
    kb = rope(seg(_KB, _VB))
    kb_sw = pltpu.roll(kb, 64, 1)
    kb2_o[:, 0:LANES] = jnp.where(low, kb, kb_sw).astype(BF16)
    kb2_o[:, LANES:2 * LANES] = jnp.where(low, kb_sw, kb).astype(BF16)
    vb = seg(_VB, _GA)
    vb_o[...] = vb.astype(BF16)
    d = x.shape[1]
    sga_o[...] = jax.nn.sigmoid(seg(_GA, _GA + d)).astype(BF16)
    sgb_o[...] = jax.nn.sigmoid(seg(_GA + d, _GA + 2 * d)).astype(BF16)
    if is_ctx:
        kaf_o[...] = ka
        vaf_o[...] = va
        kbf_o[...] = kb
        vbf_o[...] = vb


def _inproj(x2, mod3, mod_row0, g_pre, cos_t, sin_t, w_in_b, *, seq, tm, is_ctx):
    t, d = x2.shape
    per = seq // tm
    n_in = w_in_b.shape[1]
    wa = A_HEADS * 2 * A_HEAD_DIM
    wkb = B_KV_HEADS * B_HEAD_DIM

    def row(i):
        return (i // per) if not is_ctx else 0

    tok = lambda w: pl.BlockSpec((tm, w), lambda i: (i, 0))
    out_shape = [
        jax.ShapeDtypeStruct((t, wa), BF16), jax.ShapeDtypeStruct((t, wa), BF16),
        jax.ShapeDtypeStruct((t, wa), BF16), jax.ShapeDtypeStruct((t, wa), BF16),
        jax.ShapeDtypeStruct((t, 2 * wkb), BF16), jax.ShapeDtypeStruct((t, wkb), BF16),
        jax.ShapeDtypeStruct((t, d), BF16), jax.ShapeDtypeStruct((t, d), BF16),
    ]
    out_specs = [tok(wa), tok(wa), tok(wa), tok(wa), tok(2 * wkb), tok(wkb), tok(d), tok(d)]
    if not is_ctx:
        out_shape[1] = jax.ShapeDtypeStruct((t // seq, wa, seq), BF16)
        out_specs[1] = pl.BlockSpec((1, wa, tm), lambda i: (i // per, 0, i % per))
    if is_ctx:
        out_shape += [jax.ShapeDtypeStruct((t, wa), F32), jax.ShapeDtypeStruct((t, wa), F32),
                      jax.ShapeDtypeStruct((t, wkb), F32), jax.ShapeDtypeStruct((t, wkb), F32)]
        out_specs += [tok(wa), tok(wa), tok(wkb), tok(wkb)]
    return pl.pallas_call(
        functools.partial(_inproj_kernel, is_ctx=is_ctx),
        grid=(t // tm,),
        in_specs=[
            pl.BlockSpec((tm, d), lambda i: (i, 0)),
            pl.BlockSpec((1, 1, d), lambda i: (mod_row0 + row(i), 0, 0)),
            pl.BlockSpec((1, 1, d), lambda i: (mod_row0 + row(i), 0, 1)),
            pl.BlockSpec((1, d), lambda i: (0, 0)),
            pl.BlockSpec((tm, LANES), lambda i: (i % per, 0)),
            pl.BlockSpec((tm, LANES), lambda i: (i % per, 0)),
            pl.BlockSpec((d, n_in), lambda i: (0, 0)),
        ],
        out_specs=out_specs,
        out_shape=out_shape,
        compiler_params=pltpu.CompilerParams(dimension_semantics=("arbitrary",)),
        name="inproj_ctx" if is_ctx else "inproj_lat",
    )(x2, mod3, mod3, g_pre, cos_t, sin_t, w_in_b)


def _nt(a, b):
    return lax.dot_general(a, b, (((1,), (1,)), ((), ())), preferred_element_type=F32)


def _diff_kernel(lam_ref, g_ref, q_ref, *refs, n_seg, lambda_init):
    kv = refs[:2 * n_seg]
    o_ref = refs[2 * n_seg]
    lp = lam_ref[...]
    lam = (jnp.exp(jnp.sum(lp[0:1] * lp[1:2], axis=-1, keepdims=True))
           - jnp.exp(jnp.sum(lp[2:3] * lp[3:4], axis=-1, keepdims=True)) + lambda_init)
    q = q_ref[0]
    tq = q.shape[0]
    lane = lax.broadcasted_iota(jnp.int32, (1, LANES), 1)
    q2 = jnp.concatenate([q * (lane < 64).astype(BF16), q * (lane >= 64).astype(BF16)], axis=0)
    s = [_nt(q2, kv[2 * j][0].astype(BF16)) for j in range(n_seg)]
    mx = functools.reduce(jnp.maximum, [jnp.max(x, axis=-1, keepdims=True) for x in s])
    acc = None
    for j in range(n_seg):
        v = kv[2 * j + 1][0].astype(BF16)
        v_ext = jnp.concatenate([v, jnp.ones_like(v)], axis=1)
        aj = jnp.dot(jnp.exp(s[j] - mx).astype(BF16), v_ext, preferred_element_type=F32)
        acc = aj if acc is None else acc + aj
    on = acc[:, 0:LANES] / acc[:, LANES:2 * LANES]
    o = on[0:tq] - lam * on[tq:2 * tq]
    o = _rms(o, g_ref[...]) * (1.0 - lambda_init)
    o_ref[0] = o.astype(BF16)


def _diff_attention(lam_p, g_head, q, segs, *, tq, lambda_init):
    b, s, w = q.shape
    in_specs = [
        pl.BlockSpec((4, A_HEAD_DIM), lambda bi, h, qi: (0, 0)),
        pl.BlockSpec((1, A_V_DIM), lambda bi, h, qi: (0, 0)),
        pl.BlockSpec((1, tq, LANES), lambda bi, h, qi: (bi, qi, h)),
    ]
    args = [lam_p, g_head, q]
    for k, v in segs:
        nk = k.shape[1]
        in_specs += [pl.BlockSpec((1, nk, LANES), lambda bi, h, qi: (bi, 0, h)),
                     pl.BlockSpec((1, nk, LANES), lambda bi, h, qi: (bi, 0, h))]
        args += [k, v]
    return pl.pallas_call(
        functools.partial(_diff_kernel, n_seg=len(segs), lambda_init=lambda_init),
        grid=(b, A_HEADS, s // tq),
        in_specs=in_specs,
        out_specs=pl.BlockSpec((1, tq, LANES), lambda bi, h, qi: (bi, qi, h)),
        out_shape=jax.ShapeDtypeStruct((b, s, w), BF16),
        compiler_params=pltpu.CompilerParams(
            dimension_semantics=("arbitrary", "arbitrary", "arbitrary")),
        name="diff_attn_%d" % len(segs),
    )(*args)


def _diff_lat_kernel(lam_ref, g_ref, q_ref, kc_ref, kt_ref, vc_ref, v_ref, o_ref, s_a, m_a, s_b, m_b, *,
                     lambda_init):
    t = pl.program_id(0)
    tq = q_ref.shape[1]
    nkc = kc_ref.shape[1]
    nkn = kt_ref.shape[2]

    @pl.when(t == 0)
    def _():
        s_b[...] = jnp.zeros_like(s_b)
        m_b[...] = jnp.zeros_like(m_b)

    def body(s_w, m_w, s_r, m_r):
        lp = lam_ref[...]
        lam = (jnp.exp(jnp.sum(lp[0:1] * lp[1:2], axis=-1, keepdims=True))
               - jnp.exp(jnp.sum(lp[2:3] * lp[3:4], axis=-1, keepdims=True)) + lambda_init)
        lane = lax.broadcasted_iota(jnp.int32, (1, LANES), 1)

        q = q_ref[0]
        q2 = jnp.concatenate([q * (lane < 64).astype(BF16), q * (lane >= 64).astype(BF16)], axis=0)
        sc = _nt(q2, kc_ref[0].astype(BF16))
        sn = jnp.dot(q2, kt_ref[0], preferred_element_type=F32)
        mx = jnp.maximum(jnp.max(sc, axis=-1, keepdims=True), jnp.max(sn, axis=-1, keepdims=True))
        s_w[:, 0:nkc] = sc
        s_w[:, nkc:nkc + nkn] = sn
        m_w[...] = jnp.broadcast_to(mx, (2 * tq, LANES))

        mp = m_r[...]
        v_all = jnp.concatenate([vc_ref[0].astype(BF16), v_ref[0]], axis=0)
        v_ext = jnp.concatenate([v_all, jnp.ones_like(v_all)], axis=1)
        p = jnp.concatenate(
            [jnp.exp(s_r[:, c:c + LANES] - mp).astype(BF16) for c in range(0, nkc + nkn, LANES)], axis=1)
        acc = jnp.dot(p, v_ext, preferred_element_type=F32)
        on = acc[:, 0:LANES] / acc[:, LANES:2 * LANES]
        o = on[0:tq] - lam * on[tq:2 * tq]
        o_ref[0] = (_rms(o, g_ref[...]) * (1.0 - lambda_init)).astype(BF16)

    @pl.when(t % 2 == 0)
    def _():
        body(s_a, m_a, s_b, m_b)

    @pl.when(t % 2 == 1)
    def _():
        body(s_b, m_b, s_a, m_a)


def _diff_attention_lat(lam_p, g_head, q, kc, kt, vc, v, *, tq, lambda_init):
    b, s, w = q.shape
    past = kc.shape[1]
    nq = s // tq
    n_units = b * A_HEADS * nq
    last = n_units - 1

    def unit(u):
        return u // (A_HEADS * nq), (u // nq) % A_HEADS, u % nq

    def cur(t):
        return unit(jnp.minimum(t, last))

    def prev(t):
        return unit(jnp.maximum(t - 1, 0))

    return pl.pallas_call(
        functools.partial(_diff_lat_kernel, lambda_init=lambda_init),
        grid=(n_units + 1,),
        in_specs=[
            pl.BlockSpec((4, A_HEAD_DIM), lambda t: (0, 0)),
            pl.BlockSpec((1, A_V_DIM), lambda t: (0, 0)),
            pl.BlockSpec((1, tq, LANES), lambda t: (cur(t)[0], cur(t)[2], cur(t)[1])),
            pl.BlockSpec((1, past, LANES), lambda t: (cur(t)[0], 0, cur(t)[1])),
            pl.BlockSpec((1, LANES, s), lambda t: (cur(t)[0], cur(t)[1], 0)),
            pl.BlockSpec((1, past, LANES), lambda t: (prev(t)[0], 0, prev(t)[1])),
            pl.BlockSpec((1, s, LANES), lambda t: (prev(t)[0], 0, prev(t)[1])),
        ],
        out_specs=pl.BlockSpec((1, tq, LANES), lambda t: (prev(t)[0], prev(t)[2], prev(t)[1])),
        out_shape=jax.ShapeDtypeStruct((b, s, w), BF16),
        scratch_shapes=[pltpu.VMEM((2 * tq, past + s), F32), pltpu.VMEM((2 * tq, LANES), F32),
                        pltpu.VMEM((2 * tq, past + s), F32), pltpu.VMEM((2 * tq, LANES), F32)],
        compiler_params=pltpu.CompilerParams(dimension_semantics=("arbitrary",),
                                             vmem_limit_bytes=56 * 1024 * 1024),
        name="diff_attn_lat",
    )(lam_p, g_head, q, kc, kt, vc, v)


def _swa_kernel(sink_ref, q_ref, *refs, tq, seg_kinds):
    n_seg = len(seg_kinds)
    k_refs = refs[:n_seg]
    v_refs = refs[n_seg:2 * n_seg]
    o_ref = refs[2 * n_seg]
    i = pl.program_id(1)
    nb = pl.num_programs(1)
    lane = lax.broadcasted_iota(jnp.int32, (1, LANES), 1)
    low = lane < 64
    lane2 = lax.broadcasted_iota(jnp.int32, (1, 2 * LANES), 1)
    head_masks = [((lane2 // 64) == g).astype(BF16) for g in range(B_GROUP)]
    qi = lax.broadcasted_iota(jnp.int32, (B_GROUP * tq, SWA_Q), 0) & (tq - 1)
    kj = lax.broadcasted_iota(jnp.int32, (B_GROUP * tq, SWA_Q), 1)
    far = 2 * SWA_Q
    gw = B_GROUP * B_HEAD_DIM

    vs = [v_ref[0].astype(BF16) for v_ref in v_refs]
    v_all = vs[0] if n_seg == 1 else jnp.concatenate(vs, axis=0)
    v_ext = jnp.concatenate([v_all, jnp.ones_like(v_all)], axis=1)

    for n in range(B_KV_HEADS):
        q = q_ref[0, :, n * gw:(n + 1) * gw]
        qs = jnp.concatenate([q * hm for hm in head_masks], axis=0)
        ks = []
        for kind, k_ref in zip(seg_kinds, k_refs):
            if kind == "cache":
                k = k_ref[0]
                sw = pltpu.roll(k, 64, 1)
                k2 = (jnp.where(low, k, sw) if n == 0 else jnp.where(low, sw, k)).astype(BF16)
            else:
                k2 = k_ref[0, :, n * LANES:(n + 1) * LANES]
            ks.append(jnp.concatenate([k2, k2], axis=1))
        k_all = ks[0] if n_seg == 1 else jnp.concatenate(ks, axis=0)
        s = _nt(qs, k_all)
        chunks, col = [], 0
        for kind, k in zip(seg_kinds, ks):
            for c in range(0, k.shape[0], LANES):
                sc = s[:, col + c:col + c + LANES]
                if kind == "left":
                    sc = jnp.where(kj >= qi + jnp.where(i > 0, 0, far), sc, NEG_INF)
                elif kind == "right":
                    sc = jnp.where(kj <= qi - jnp.where(i < nb - 1, 0, far), sc, NEG_INF)
                chunks.append(sc)
            col += k.shape[0]
        sinkcol = jnp.concatenate(
            [jnp.full((tq, 1), sink_ref[n * B_GROUP + g], F32) for g in range(B_GROUP)], axis=0)
        mx = jnp.maximum(jnp.max(functools.reduce(jnp.maximum, chunks), axis=-1, keepdims=True), sinkcol)
        p = jnp.concatenate([jnp.exp(c - mx).astype(BF16) for c in chunks], axis=1)
        acc = jnp.dot(p, v_ext, preferred_element_type=F32)
        o = acc[:, 0:LANES] / (acc[:, LANES:2 * LANES] + jnp.exp(sinkcol - mx))
        osw = pltpu.roll(o, 64, 1)
        for j in range(B_GROUP // 2):
            ra = slice((2 * j) * tq, (2 * j + 1) * tq)
            rb = slice((2 * j + 1) * tq, (2 * j + 2) * tq)
            pair = jnp.where(low, o[ra], osw[rb]) if n == 0 else jnp.where(low, osw[ra], o[rb])
            o_ref[0, :, n * gw + j * LANES:n * gw + (j + 1) * LANES] = pair.astype(BF16)


def _swa_attention(sink, q, segs, *, tq):
    b, s, w = q.shape
    in_specs = [
        pl.BlockSpec(memory_space=pltpu.SMEM),
        pl.BlockSpec((1, tq, w), lambda bi, i: (bi, i, 0)),
    ]
    k_specs, v_specs, k_args, v_args, kinds = [], [], [], [], []
    for kind, k, v, rows, idx in segs:
        k_specs.append(pl.BlockSpec((1, rows, k.shape[2]), lambda bi, i, idx=idx: (bi, idx(i), 0)))
        v_specs.append(pl.BlockSpec((1, rows, v.shape[2]), lambda bi, i, idx=idx: (bi, idx(i), 0)))
        k_args.append(k)
        v_args.append(v)
        kinds.append(kind)
    return pl.pallas_call(
        functools.partial(_swa_kernel, tq=tq, seg_kinds=tuple(kinds)),
        grid=(b, s // tq),
        in_specs=in_specs + k_specs + v_specs,
        out_specs=pl.BlockSpec((1, tq, w), lambda bi, i: (bi, i, 0)),
        out_shape=jax.ShapeDtypeStruct((b, s, w), BF16),
        compiler_params=pltpu.CompilerParams(dimension_semantics=("arbitrary", "arbitrary")),
        name="swa_attn_%d" % len(segs),
    )(sink, q, *k_args, *v_args)


def _postmix_kernel(*refs, n_ctx_tiles):
    (xc, xl, oac, oal, obc, obl, sgac, sgal, sgbc, sgbl, g1_ref, sh2_ref, sc2_ref, gpm_ref, gpf_ref,
     wpa_ref, wpb_ref, wo_ref, wr_ref, wrl_ref, br_ref, x1_o, h2_o, route_o) = refs
    is_ctx = pl.program_id(0) < n_ctx_tiles
    pick = lambda a, b: jnp.where(is_ctx, a[...], b[...])
    pa = jnp.dot(pick(oac, oal), wpa_ref[...], preferred_element_type=F32)
    pb = jnp.dot(pick(obc, obl), wpb_ref[...], preferred_element_type=F32)
    mix = pick(sgac, sgal).astype(F32) * pa + pick(sgbc, sgbl).astype(F32) * pb
    m2 = jnp.dot(mix.astype(BF16), wo_ref[...], preferred_element_type=F32)
    x1 = pick(xc, xl) + g1_ref[0] * _rms(m2, gpm_ref[...])
    x1_o[...] = x1
    h2 = _rms(x1, gpf_ref[...]) * (1.0 + sc2_ref[0]) + sh2_ref[0]
    h2_o[...] = h2.astype(BF16)

    h_hi = h2.astype(BF16)
    h_lo = (h2 - h_hi.astype(F32)).astype(BF16)
    logits = (jnp.dot(h_hi, wr_ref[...], preferred_element_type=F32)
              + jnp.dot(h_lo, wr_ref[...], preferred_element_type=F32)
              + jnp.dot(h_hi, wrl_ref[...], preferred_element_type=F32) + br_ref[...])
    tm = logits.shape[0]
    lane_i = lax.broadcasted_iota(jnp.int32, (tm, LANES), 1)
    lane = lane_i.astype(F32)
    big = float(LANES)
    is_g = lane_i < N_GROUPS
    lg = jnp.where(is_g, logits, -jnp.inf)
    mg = jnp.max(lg, axis=-1, keepdims=True)
    g_sel = jnp.min(jnp.where(is_g & (logits == mg), lane, big), axis=-1, keepdims=True)
    g_w = 1.0 / jnp.sum(jnp.where(is_g, jnp.exp(lg - mg), 0.0), axis=-1, keepdims=True)
    lane_grp = lax.shift_right_arithmetic(lane_i - N_GROUPS, 3).astype(F32)
    in_grp = (lane_grp == g_sel) & (lane_i < N_GROUPS + N_EXPERTS)
    le = jnp.where(in_grp, logits, -jnp.inf)
    v0 = jnp.max(le, axis=-1, keepdims=True)
    i0 = jnp.min(jnp.where(in_grp & (logits == v0), lane, big), axis=-1, keepdims=True)
    rest = in_grp & (lane != i0)
    le1 = jnp.where(rest, logits, -jnp.inf)
    v1 = jnp.max(le1, axis=-1, keepdims=True)
    i1 = jnp.min(jnp.where(rest & (logits == v1), lane, big), axis=-1, keepdims=True)
    e = jnp.exp(v1 - v0)
    w0 = g_w / (1.0 + e)
    w1 = g_w * e / (1.0 + e)
    e0 = i0 - N_GROUPS
    e1 = i1 - N_GROUPS
    route_o[...] = jnp.where(lane_i == 0, e0, jnp.where(lane_i == 1, e1,
                             jnp.where(lane_i == 2, w0, jnp.where(lane_i == 3, w1, 0.0))))


def _postmix(ctx_in, lat_in, mod3, gpm, gpf, wpa, wpb, wo, wr, wrl, br, *, lat_seq, tm):
    t_ctx, d = ctx_in[0].shape
    t_lat = lat_in[0].shape[0]
    assert t_ctx % tm == 0 and lat_seq % tm == 0
    nc = t_ctx // tm
    nl = t_lat // tm
    per = lat_seq // tm
    sub = d // LANES
    t_all = t_ctx + t_lat

    mod_row = lambda i: jnp.where(i < nc, 0, 1 + jnp.maximum(i - nc, 0) // per)
    full = lambda a: pl.BlockSpec(a.shape, lambda i: (0,) * a.ndim)
    modspec = lambda c: pl.BlockSpec((1, 1, d), lambda i: (mod_row(i), 0, c))
    in_specs, args = [], []
    for a_c, a_l in zip(ctx_in, lat_in):
        w = a_c.shape[1]
        in_specs += [pl.BlockSpec((tm, w), lambda i: (jnp.minimum(i, nc - 1), 0)),
                     pl.BlockSpec((tm, w), lambda i: (jnp.maximum(i - nc, 0), 0))]
        args += [a_c, a_l]
    in_specs += [modspec(2), modspec(3), modspec(4), full(gpm), full(gpf),
                 full(wpa), full(wpb), full(wo), full(wr), full(wrl), full(br)]
    args += [mod3, mod3, mod3, gpm, gpf, wpa, wpb, wo, wr, wrl, br]
    return pl.pallas_call(
        functools.partial(_postmix_kernel, n_ctx_tiles=nc),
        grid=(nc + nl,),
        in_specs=in_specs,
        out_specs=[pl.BlockSpec((tm, d), lambda i: (i, 0)),
                   pl.BlockSpec((tm, d), lambda i: (i, 0)),
                   pl.BlockSpec((tm, LANES), lambda i: (i, 0))],
        out_shape=[jax.ShapeDtypeStruct((t_all, d), F32),
                   jax.ShapeDtypeStruct((t_all, d), BF16),
                   jax.ShapeDtypeStruct((t_all, LANES), F32)],
        compiler_params=pltpu.CompilerParams(dimension_semantics=("arbitrary",)),
        name="postmix",
    )(*args)


def _segment_copies(src, src_row, dst, dst_row, n, sub, sem):
    @pl.when(n > 0)
    def _():
        pltpu.make_async_copy(src.at[pl.ds(pl.multiple_of(src_row * sub, sub), n * sub)],
                              dst.at[pl.ds(pl.multiple_of(dst_row * sub, sub), n * sub)], sem).start()


def _local_positions(route, tile_base):
    tm = route.shape[0]
    lane = lax.broadcasted_iota(jnp.int32, (tm, LANES), 1).astype(F32)
    is0 = lane == route[:, 0:1]
    is1 = lane == route[:, 1:2]
    earlier = (lax.broadcasted_iota(jnp.int32, (tm, tm), 1)
               < lax.broadcasted_iota(jnp.int32, (tm, tm), 0)).astype(BF16)
    pre0 = jnp.dot(earlier, is0.astype(BF16), preferred_element_type=F32)
    pre1 = jnp.dot(earlier, is1.astype(BF16), preferred_element_type=F32)
    cnt0 = jnp.sum(is0.astype(F32), axis=0, keepdims=True)
    lpos0 = jnp.sum(jnp.where(is0, tile_base + pre0, 0.0), axis=-1, keepdims=True)
    lpos1 = jnp.sum(jnp.where(is1, tile_base + cnt0 + pre1, 0.0), axis=-1, keepdims=True)
    return lpos0.astype(jnp.int32), lpos1.astype(jnp.int32)


def _dispatch_kernel(ss_ref, sl_ref, tb_ref, h_ref, r_ref, tbv_ref, xs_hbm, pbuf, zbuf, sem, zsem, *, n_asg, rows):
    i = pl.program_id(0)
    nt = pl.num_programs(0)
    tm, d = h_ref.shape
    sub = d // LANES
    nrow = TOP_K * tm
    slot = i % 2

    def wait_slot(s):
        pltpu.make_async_copy(pbuf.at[pl.ds(pl.multiple_of(s * nrow * sub, nrow * sub), nrow * sub)],
                              xs_hbm.at[pl.ds(0, nrow * sub)], sem.at[s]).wait()

    def slack_copy():
        return pltpu.make_async_copy(zbuf, xs_hbm.at[pl.ds(n_asg * sub, rows * sub)], zsem.at[0])

    @pl.when(i == 0)
    def _():
        zbuf[...] = jnp.zeros_like(zbuf)
        slack_copy().start()

    lpos0, lpos1 = _local_positions(r_ref[...], tbv_ref[0])
    p = lax.broadcasted_iota(jnp.int32, (tm, nrow), 1)
    sel = ((p == lpos0) | (p == lpos1)).astype(BF16)
    xp = lax.dot_general(sel, h_ref[...].astype(BF16), (((0,), (0,)), ((), ())),
                         preferred_element_type=F32)

    @pl.when(i >= 2)
    def _():
        wait_slot(slot)

    _store_row_tiles(pbuf, slot * nrow * sub, xp)

    def seg(e, c):
        k = i * N_EXPERTS + e
        _segment_copies(pbuf, slot * nrow + tb_ref[k], xs_hbm, ss_ref[k], sl_ref[k], sub, sem.at[slot])
        return c
    lax.fori_loop(0, N_EXPERTS, seg, 0)

    @pl.when(i == nt - 1)
    def _():
        wait_slot(slot)

        @pl.when(nt >= 2)
        def _():
            wait_slot(1 - slot)
        slack_copy().wait()


def _dispatch(h2, route, tables, *, tm, rows):
    seg_start, seg_len, tile_base, tile_base_v = tables
    t, d = h2.shape
    sub = d // LANES
    n_asg = t * TOP_K
    grid_spec = pltpu.PrefetchScalarGridSpec(
        num_scalar_prefetch=3,
        grid=(t // tm,),
        in_specs=[
            pl.BlockSpec((tm, d), lambda i, *_: (i, 0)),
            pl.BlockSpec((tm, LANES), lambda i, *_: (i, 0)),
            pl.BlockSpec((1, 1, LANES), lambda i, *_: (i, 0, 0)),
        ],
        out_specs=pl.BlockSpec(memory_space=pl.ANY),
        scratch_shapes=[pltpu.VMEM((2 * TOP_K * tm * sub, LANES), F32), pltpu.VMEM((rows * sub, LANES), F32),
                        pltpu.SemaphoreType.DMA((2,)), pltpu.SemaphoreType.DMA((1,))],
    )
    return pl.pallas_call(
        functools.partial(_dispatch_kernel, n_asg=n_asg, rows=rows),
        grid_spec=grid_spec,
        out_shape=jax.ShapeDtypeStruct(((n_asg + rows) * sub, LANES), F32),
        compiler_params=pltpu.CompilerParams(dimension_semantics=("arbitrary",)),
        name="dispatch",
    )(seg_start, seg_len, tile_base, h2, route, tile_base_v)


def _moe_kernel(be_ref, row0_ref, nact_ref, xs_hbm, w1_ref, w3_ref, w2_ref, ys_hbm,
                xbuf, obuf, w1b, w3b, w2b, rsem, wsem, *, rows, sub):
    i = pl.program_id(0)
    nact = nact_ref[0]
    slot = i % 2
    nslot = 1 - slot
    blk = rows * sub

    def read(j, s):
        return pltpu.make_async_copy(xs_hbm.at[pl.ds(pl.multiple_of(row0_ref[j] * sub, sub), blk)],
                                     xbuf.at[pl.ds(pl.multiple_of(s * blk, blk), blk)], rsem.at[s])

    def write(j, s):
        return pltpu.make_async_copy(obuf.at[pl.ds(pl.multiple_of(s * blk, blk), blk)],
                                     ys_hbm.at[pl.ds(pl.multiple_of(row0_ref[j] * sub, sub), blk)], wsem.at[s])

    @pl.when(i == 0)
    def _():
        read(0, 0).start()

    @pl.when(i < nact)
    def _():
        @pl.when(i + 1 < nact)
        def _():
            read(i + 1, nslot).start()

        changed = jnp.logical_or(i == 0, be_ref[i] != be_ref[jnp.maximum(i - 1, 0)])

        @pl.when(changed)
        def _():
            w1b[...] = w1_ref[0].astype(BF16)
            w3b[...] = w3_ref[0].astype(BF16)
            w2b[...] = w2_ref[0].astype(BF16)

        read(i, slot).wait()
        x = _load_row_tiles(xbuf, slot * blk, rows, sub).astype(BF16)
        a = jnp.dot(x, w1b[...], preferred_element_type=F32)
        b = jnp.dot(x, w3b[...], preferred_element_type=F32)
        hmid = (a * jax.nn.sigmoid(a) * b).astype(BF16)
        y = jnp.dot(hmid, w2b[...], preferred_element_type=F32)
        _store_row_tiles(obuf, slot * blk, y)

        @pl.when(i >= 1)
        def _():
            write(i - 1, nslot).wait()
        write(i, slot).start()

    @pl.when(i == nact)
    def _():
        write(i - 1, nslot).wait()
        obuf[pl.ds(pl.multiple_of(slot * blk, blk), blk), :] = jnp.zeros((blk, LANES), F32)
        tail = pltpu.make_async_copy(obuf.at[pl.ds(pl.multiple_of(slot * blk, blk), blk)],
                                     ys_hbm.at[pl.ds(ys_hbm.shape[0] - blk, blk)], wsem.at[slot])
        tail.start()
        tail.wait()


def _moe(blk_expert, row0, nact, xs, w1, w3, w2):
    nblk = blk_expert.shape[0] - 1
    d, de = w1.shape[1], w1.shape[2]
    sub = d // LANES
    rows = MOE_ROWS
    grid_spec = pltpu.PrefetchScalarGridSpec(
        num_scalar_prefetch=3,
        grid=(nblk + 1,),
        in_specs=[
            pl.BlockSpec(memory_space=pl.ANY),
            pl.BlockSpec((1, d, de), lambda i, be, r0, na: (be[i], 0, 0)),
            pl.BlockSpec((1, d, de), lambda i, be, r0, na: (be[i], 0, 0)),
            pl.BlockSpec((1, de, d), lambda i, be, r0, na: (be[i], 0, 0)),
        ],
        out_specs=pl.BlockSpec(memory_space=pl.ANY),
        scratch_shapes=[
            pltpu.VMEM((2 * rows * sub, LANES), F32),
            pltpu.VMEM((2 * rows * sub, LANES), F32),
            pltpu.VMEM((d, de), BF16),
            pltpu.VMEM((d, de), BF16),
            pltpu.VMEM((de, d), BF16),
            pltpu.SemaphoreType.DMA((2,)),
            pltpu.SemaphoreType.DMA((2,)),
        ],
    )
    return pl.pallas_call(
        functools.partial(_moe_kernel, rows=rows, sub=sub),
        grid_spec=grid_spec,
        out_shape=jax.ShapeDtypeStruct(xs.shape, F32),
        compiler_params=pltpu.CompilerParams(dimension_semantics=("arbitrary",)),
        name="expert_mlp",
    )(blk_expert, row0, nact, xs, w1, w3, w2)


def _combine_kernel(ss_ref, sl_ref, tb_ref, x1_ref, r_ref, tbv_ref, g2_ref, gpost_ref, ys_hbm, o_ref,
                    ybuf, sem, *, tile0):
    i = pl.program_id(0)
    nt = pl.num_programs(0)
    tm, d = x1_ref.shape
    sub = d // LANES
    nrow = TOP_K * tm
    slot = i % 2

    def fetch(tile, s):
        def seg(e, c):
            k = tile * N_EXPERTS + e
            _segment_copies(ys_hbm, ss_ref[k], ybuf, s * nrow + tb_ref[k], sl_ref[k], sub, sem.at[s])
            return c
        lax.fori_loop(0, N_EXPERTS, seg, 0)

    @pl.when(i == 0)
    def _():
        fetch(tile0, 0)

    @pl.when(i + 1 < nt)
    def _():
        fetch(tile0 + i + 1, 1 - slot)

    pltpu.make_async_copy(ys_hbm.at[pl.ds(0, nrow * sub)],
                          ybuf.at[pl.ds(pl.multiple_of(slot * nrow * sub, nrow * sub), nrow * sub)],
                          sem.at[slot]).wait()
    r = r_ref[...]
    lpos0, lpos1 = _local_positions(r, tbv_ref[0])
    p = lax.broadcasted_iota(jnp.int32, (tm, nrow), 1)
    q = (jnp.where(p == lpos0, r[:, 2:3], 0.0) + jnp.where(p == lpos1, r[:, 3:4], 0.0)).astype(BF16)
    ysort = _load_row_tiles(ybuf, slot * nrow * sub, nrow, sub).astype(BF16)
    y = jnp.dot(q, ysort, preferred_element_type=F32)
    o_ref[...] = x1_ref[...] + g2_ref[0] * _rms(y, gpost_ref[...])


def _combine(x1, ys, route, tables, mod3, mod_row0, gpost, *, t, seq, tm, tok_off, is_ctx):
    seg_start, seg_len, tile_base, tile_base_v = tables
    t_all, d = x1.shape
    per = seq // tm
    sub = d // LANES
    boff = tok_off // tm

    def row(i):
        return (i // per) if not is_ctx else 0

    grid_spec = pltpu.PrefetchScalarGridSpec(
        num_scalar_prefetch=3,
        grid=(t // tm,),
        in_specs=[
            pl.BlockSpec((tm, d), lambda i, *_: (boff + i, 0)),
            pl.BlockSpec((tm, LANES), lambda i, *_: (boff + i, 0)),
            pl.BlockSpec((1, 1, LANES), lambda i, *_: (boff + i, 0, 0)),
            pl.BlockSpec((1, 1, d), lambda i, *_: (mod_row0 + row(i), 0, 5)),
            pl.BlockSpec((1, d), lambda i, *_: (0, 0)),
            pl.BlockSpec(memory_space=pl.ANY),
        ],
        out_specs=pl.BlockSpec((tm, d), lambda i, *_: (i, 0)),
        scratch_shapes=[pltpu.VMEM((2 * TOP_K * tm * sub, LANES), F32), pltpu.SemaphoreType.DMA((2,))],
    )
    return pl.pallas_call(
        functools.partial(_combine_kernel, tile0=boff),
        grid_spec=grid_spec,
        out_shape=jax.ShapeDtypeStruct((t, d), F32),
        compiler_params=pltpu.CompilerParams(dimension_semantics=("arbitrary",)),
        name="combine_ctx" if is_ctx else "combine_lat",
    )(seg_start, seg_len, tile_base, x1, route, tile_base_v, mod3, gpost, ys)


def _routing_tables(route, tm, rows):
    t = route.shape[0]
    nt = t // tm
    n_asg = t * TOP_K
    ex = jnp.arange(N_EXPERTS, dtype=jnp.int32)
    e01 = route[:, 0:TOP_K].astype(jnp.int32)
    cnt_te = jnp.sum((e01.reshape(nt, tm * TOP_K, 1) == ex).astype(jnp.int32), axis=1)
    cnt_e = jnp.sum(cnt_te, axis=0)
    start_e = jnp.cumsum(cnt_e) - cnt_e
    seg_start = start_e[None, :] + jnp.cumsum(cnt_te, axis=0) - cnt_te
    tile_base = jnp.cumsum(cnt_te, axis=1) - cnt_te
    tile_base_v = jnp.pad(tile_base.astype(F32), ((0, 0), (0, LANES - N_EXPERTS))).reshape(nt, 1, LANES)

    nblk_e = (cnt_e + rows - 1) // rows
    blk_end = jnp.cumsum(nblk_e)
    blk_start = blk_end - nblk_e
    n_blocks = n_asg // rows + N_EXPERTS
    b = jnp.arange(n_blocks + 1, dtype=jnp.int32)
    be = jnp.minimum(jnp.sum((blk_end[None, :] <= b[:, None]).astype(jnp.int32), axis=1), N_EXPERTS - 1)
    first = jnp.sum(jnp.where(be[:, None] == ex[None, :], (start_e - blk_start * rows)[None, :], 0), axis=1)
    row0 = jnp.clip(first + b * rows, 0, n_asg)
    nact = blk_end[-1:].astype(jnp.int32)
    seg = (seg_start.reshape(-1).astype(jnp.int32), cnt_te.reshape(-1).astype(jnp.int32),
           tile_base.reshape(-1).astype(jnp.int32), tile_base_v)
    return seg, be.astype(jnp.int32), row0.astype(jnp.int32), nact


def _rope_tables(n_tok):
    n_rows = n_tok // GRID_W
    rows = jnp.repeat(jnp.arange(n_rows), GRID_W).astype(F32)
    cols = jnp.tile(jnp.arange(GRID_W), n_rows).astype(F32)
    quarter = A_HEAD_DIM // 4
    inv = ROPE_BASE ** (-jnp.arange(quarter, dtype=F32) / quarter)
    ang = jnp.concatenate([rows[:, None] * inv, cols[:, None] * inv], axis=-1)
    cos, sin = jnp.cos(ang), jnp.sin(ang)
    cos_t = jnp.tile(jnp.concatenate([cos, cos], axis=-1), (1, LANES // A_HEAD_DIM))
    sin_t = jnp.tile(jnp.concatenate([-sin, sin], axis=-1), (1, LANES // A_HEAD_DIM))
    return cos_t, sin_t


def kernel(x_prompt, x_sample, c, cache_diff_k, cache_diff_v, cache_swa_k, cache_swa_v, c_ctx, w_ada, b_ada, g_pre_mix, g_post_mix, g_pre_ffn, g_post_ffn, w_in, lam_q1, lam_k1, lam_q2, lam_k2, g_diff_head, sink, w_proj_a, w_proj_b, w_out, w_router_group, b_router_group, w_router_expert, b_router_expert, w_e1, w_e3, w_e2):
    depth = w_in.shape[0]
    assert depth == 1
    l = 0
    bp, sp, d = x_prompt.shape
    bs, ss, _ = x_sample.shape
    lambda_init = 0.8 - 0.6 * math.exp(-0.3 * l)
    assert A_HEAD_DIM == B_HEAD_DIM and ss % GRID_W == 0 and bs + 1 <= MOD_ROWS

    c_all = jnp.concatenate([c_ctx[None, :], c, jnp.zeros((MOD_ROWS - 1 - bs, d), F32)], axis=0)
    mod = _modulation(c_all, w_ada[l], b_ada[l][None, :])
    mod3 = mod.reshape(MOD_ROWS, 1, 6 * d)

    w_in_b = w_in[l].astype(BF16)
    wpa = w_proj_a[l].astype(BF16)
    wpb = w_proj_b[l].astype(BF16)
    wo = w_out[l].astype(BF16)
    n_r = N_GROUPS + N_EXPERTS
    wr = jnp.concatenate([w_router_group[l], w_router_expert[l], jnp.zeros((d, LANES - n_r), F32)], axis=1)
    br = jnp.concatenate([b_router_group[l], b_router_expert[l], jnp.zeros((LANES - n_r,), F32)])[None, :]
    wr_hi = wr.astype(BF16)
    wr_lo = (wr - wr_hi.astype(F32)).astype(BF16)
    lam_p = jnp.stack([lam_q1[l], lam_k1[l], lam_q2[l], lam_k2[l]], axis=0)
    g_head = g_diff_head[l][None, :]
    sink_l = sink[l]
    cos_t, sin_t = _rope_tables(ss)

    xp2 = x_prompt.reshape(bp * sp, d)
    xs2 = x_sample.reshape(bs * ss, d)
    gpre = g_pre_mix[l][None, :]

    (qa_c, ka_c, va_c, qb_c, kb2_c, vb_c, sga_c, sgb_c, kaf, vaf, kbf, vbf) = _inproj(
        xp2, mod3, 0, gpre, cos_t, sin_t, w_in_b, seq=sp, tm=sp, is_ctx=True)
    r3 = lambda a, b_: a.reshape(b_, -1, a.shape[-1])
    oa_c = _diff_attention(lam_p, g_head, r3(qa_c, bp), [(r3(ka_c, bp), r3(va_c, bp))],
                           tq=sp, lambda_init=lambda_init)
    ob_c = _swa_attention(sink_l, r3(qb_c, bp),
                          [("rep", r3(kb2_c, bp), r3(vb_c, bp), sp, lambda i: 0)], tq=sp)

    (qa_s, ka_s, va_s, qb_s, kb2_s, vb_s, sga_s, sgb_s) = _inproj(
        xs2, mod3, 1, gpre, cos_t, sin_t, w_in_b, seq=ss, tm=512, is_ctx=False)
    past = cache_diff_k.shape[2]
    ck = cache_diff_k[:, l].reshape(bs, past, -1)
    cv = cache_diff_v[:, l].reshape(bs, past, -1)
    oa_s = _diff_attention_lat(lam_p, g_head, r3(qa_s, bs), ck, ka_s, cv, r3(va_s, bs),
                               tq=512, lambda_init=lambda_init)
    sk = cache_swa_k[:, l].reshape(bs, past, -1)
    sv = cache_swa_v[:, l].reshape(bs, past, -1)
    nqb = ss // SWA_Q
    kb2_3, vb_3 = r3(kb2_s, bs), r3(vb_s, bs)
    ob_s = _swa_attention(sink_l, r3(qb_s, bs), [
        ("cache", sk, sv, past, lambda i: 0),
        ("left", kb2_3, vb_3, SWA_Q, lambda i: jnp.maximum(i - 1, 0)),
        ("center", kb2_3, vb_3, SWA_Q, lambda i: i),
        ("right", kb2_3, vb_3, SWA_Q, lambda i: jnp.minimum(i + 1, nqb - 1)),
    ], tq=SWA_Q)

    gpm = g_post_mix[l][None, :]
    gpf = g_pre_ffn[l][None, :]
    t_ctx, t_lat = bp * sp, bs * ss
    x1, h2t, route = _postmix(
        (xp2, oa_c.reshape(t_ctx, -1), ob_c.reshape(t_ctx, -1), sga_c, sgb_c),
        (xs2, oa_s.reshape(t_lat, -1), ob_s.reshape(t_lat, -1), sga_s, sgb_s),
        mod3, gpm, gpf, wpa, wpb, wo, wr_hi, wr_lo, br, lat_seq=ss, tm=512)

    tables, blk_expert, row0, nact = _routing_tables(route, MOE_TILE, MOE_ROWS)
    xs = _dispatch(h2t, route, tables, tm=MOE_TILE, rows=MOE_ROWS)
    ys = _moe(blk_expert, row0, nact, xs, w_e1[l], w_e3[l], w_e2[l])

    gpost = g_post_ffn[l][None, :]
    y_p = _combine(x1, ys, route, tables, mod3, 0, gpost, t=t_ctx, seq=sp, tm=MOE_TILE, tok_off=0, is_ctx=True)
    y_s = _combine(x1, ys, route, tables, mod3, 1, gpost, t=t_lat, seq=ss, tm=MOE_TILE, tok_off=t_ctx,
                   is_ctx=False)

    ha = A_HEADS
    return (y_p.reshape(bp, sp, d), y_s.reshape(bs, ss, d),
            kaf.reshape(bp, 1, sp, ha, 2, A_HEAD_DIM), vaf.reshape(bp, 1, sp, ha, A_V_DIM),
            kbf.reshape(bp, 1, sp, B_KV_HEADS, B_HEAD_DIM), vbf.reshape(bp, 1, sp, B_KV_HEADS, B_HEAD_DIM))
```

```python
import functools
import math

import jax
import jax.numpy as jnp
from jax import lax
from jax.experimental import pallas as pl
from jax.experimental.pallas import tpu as pltpu

F32 = jnp.float32
BF16 = jnp.bfloat16
HIGHEST = lax.Precision.HIGHEST

GRID_W = 64
ROPE_BASE = 10000.0
EPS = 1e-6
NEG_INF = -1e30
A_HEADS = 4
A_HEAD_DIM = 64
A_V_DIM = 2 * A_HEAD_DIM
B_HEADS = 8
B_KV_HEADS = 2
B_GROUP = B_HEADS // B_KV_HEADS
B_HEAD_DIM = 64
WINDOW = 128
N_GROUPS = 4
EXPERTS_PER_GROUP = 8
N_EXPERTS = N_GROUPS * EXPERTS_PER_GROUP
TOP_K = 2

LANES = 128
MOD_ROWS = 16
MOE_ROWS = 256
MOE_TILE = 512
SWA_Q = 128

_QA = 0
_KA = _QA + A_HEADS * 2 * A_HEAD_DIM
_VA = _KA + A_HEADS * 2 * A_HEAD_DIM
_QB = _VA + A_HEADS * A_V_DIM
_KB = _QB + B_HEADS * B_HEAD_DIM
_VB = _KB + B_KV_HEADS * B_HEAD_DIM
_GA = _VB + B_KV_HEADS * B_HEAD_DIM


def _rms(x, g):
    return x * lax.rsqrt(jnp.mean(x * x, axis=-1, keepdims=True) + EPS) * g


def _store_row_tiles(ref, base, val):
    sub = val.shape[1] // LANES
    for s in range(sub):
        ref[pl.ds(base + s, val.shape[0], stride=sub), :] = val[:, s * LANES:(s + 1) * LANES]


def _load_row_tiles(ref, base, n_rows, sub):
    return jnp.concatenate([ref[pl.ds(base + s, n_rows, stride=sub), :] for s in range(sub)], axis=1)


def _mod_kernel(c_ref, w_ref, b_ref, o_ref):
    c = c_ref[...]
    a = c * jax.nn.sigmoid(c)
    o_ref[...] = jnp.dot(a, w_ref[...], precision=HIGHEST, preferred_element_type=F32) + b_ref[...]


def _modulation(c_all, w_ada, b_ada):
    d, n = w_ada.shape
    tn = 512
    return pl.pallas_call(
        _mod_kernel,
        grid=(n // tn,),
        in_specs=[
            pl.BlockSpec((MOD_ROWS, d), lambda j: (0, 0)),
            pl.BlockSpec((d, tn), lambda j: (0, j)),
            pl.BlockSpec((1, tn), lambda j: (0, j)),
        ],
        out_specs=pl.BlockSpec((MOD_ROWS, tn), lambda j: (0, j)),
        out_shape=jax.ShapeDtypeStruct((MOD_ROWS, n), F32),
        name="modulation",
    )(c_all, w_ada, b_ada)


def _rope128(z, cos, sin_signed, first_half):
    rot = jnp.where(first_half, pltpu.roll(z, 96, 1), pltpu.roll(z, 32, 1))
    return z * cos + rot * sin_signed


def _inproj_kernel(x_ref, sh_ref, sc_ref, g_ref, cos_ref, sin_ref, w_ref, *outs, is_ctx):
    x = x_ref[...]
    h = _rms(x, g_ref[...]) * (1.0 + sc_ref[0]) + sh_ref[0]
    hb = h.astype(BF16)
    lane = lax.broadcasted_iota(jnp.int32, (1, LANES), 1)
    first_half = (lane % 64) < 32
    low = lane < 64

    def seg(lo, hi):
        return jnp.dot(hb, w_ref[:, lo:hi], preferred_element_type=F32)

    def rope(z):
        if is_ctx:
            return z
        cos = cos_ref[...]
        sin = sin_ref[...]
        parts = [_rope128(z[:, j:j + LANES], cos, sin, first_half) for j in range(0, z.shape[1], LANES)]
        return parts[0] if len(parts) == 1 else jnp.concatenate(parts, axis=1)

    if is_ctx:
        qa_o, ka_o, va_o, qb_o, kb2_o, vb_o, sga_o, sgb_o, kaf_o, vaf_o, kbf_o, vbf_o = outs
    else:
        qa_o, ka_o, va_o, qb_o, kb2_o, vb_o, sga_o, sgb_o = outs

    scale = A_HEAD_DIM ** -0.5
    qa_o[...] = (rope(seg(_QA, _KA)) * scale).astype(BF16)
    ka = rope(seg(_KA, _VA))
    if is_ctx:
        ka_o[...] = ka.astype(BF16)
    else:
        ka_o[0] = ka.T.astype(BF16)
    va = seg(_VA, _QB)
    va_o[...] = va.astype(BF16)
    qb_o[...] = (rope(seg(_QB, _KB)) * (B_HEAD_DIM ** -0.5)).astype(BF16)
    kb = rope(seg(_KB, _VB))
    kb_sw = pltpu.roll(kb, 64, 1)
    kb2_o[:, 0:LANES] = jnp.where(low, kb, kb_sw).astype(BF16)
    kb2_o[:, LANES:2 * LANES] = jnp.where(low, kb_sw, kb).astype(BF16)
    vb = seg(_VB, _GA)
    vb_o[...] = vb.astype(BF16)
    d = x.shape[1]
    sga_o[...] = jax.nn.sigmoid(seg(_GA, _GA + d)).astype(BF16)
    sgb_o[...] = jax.nn.sigmoid(seg(_GA + d, _GA + 2 * d)).astype(BF16)
    if is_ctx:
        kaf_o[...] = ka
        vaf_o[...] = va
        kbf_o[...] = kb
        vbf_o[...] = vb


def _inproj(x2, mod3, mod_row0, g_pre, cos_t, sin_t, w_in_b, *, seq, tm, is_ctx):
    t, d = x2.shape
    per = seq // tm
    n_in = w_in_b.shape[1]
    wa = A_HEADS * 2 * A_HEAD_DIM
    wkb = B_KV_HEADS * B_HEAD_DIM

    def row(i):
        return (i // per) if not is_ctx else 0

    tok = lambda w: pl.BlockSpec((tm, w), lambda i: (i, 0))
    out_shape = [
        jax.ShapeDtypeStruct((t, wa), BF16), jax.ShapeDtypeStruct((t, wa), BF16),
        jax.ShapeDtypeStruct((t, wa), BF16), jax.ShapeDtypeStruct((t, wa), BF16),
        jax.ShapeDtypeStruct((t, 2 * wkb), BF16), jax.ShapeDtypeStruct((t, wkb), BF16),
        jax.ShapeDtypeStruct((t, d), BF16), jax.ShapeDtypeStruct((t, d), BF16),
    ]
    out_specs = [tok(wa), tok(wa), tok(wa), tok(wa), tok(2 * wkb), tok(wkb), tok(d), tok(d)]
    if not is_ctx:
        out_shape[1] = jax.ShapeDtypeStruct((t // seq, wa, seq), BF16)
        out_specs[1] = pl.BlockSpec((1, wa, tm), lambda i: (i // per, 0, i % per))
    if is_ctx:
        out_shape += [jax.ShapeDtypeStruct((t, wa), F32), jax.ShapeDtypeStruct((t, wa), F32),
                      jax.ShapeDtypeStruct((t, wkb), F32), jax.ShapeDtypeStruct((t, wkb), F32)]
        out_specs += [tok(wa), tok(wa), tok(wkb), tok(wkb)]
    return pl.pallas_call(
        functools.partial(_inproj_kernel, is_ctx=is_ctx),
        grid=(t // tm,),
        in_specs=[
            pl.BlockSpec((tm, d), lambda i: (i, 0)),
            pl.BlockSpec((1, 1, d), lambda i: (mod_row0 + row(i), 0, 0)),
            pl.BlockSpec((1, 1, d), lambda i: (mod_row0 + row(i), 0, 1)),
            pl.BlockSpec((1, d), lambda i: (0, 0)),
            pl.BlockSpec((tm, LANES), lambda i: (i % per, 0)),
            pl.BlockSpec((tm, LANES), lambda i: (i % per, 0)),
            pl.BlockSpec((d, n_in), lambda i: (0, 0)),
        ],
        out_specs=out_specs,
        out_shape=out_shape,
        compiler_params=pltpu.CompilerParams(dimension_semantics=("arbitrary",)),
        name="inproj_ctx" if is_ctx else "inproj_lat",
    )(x2, mod3, mod3, g_pre, cos_t, sin_t, w_in_b)


def _nt(a, b):
    return lax.dot_general(a, b, (((1,), (1,)), ((), ())), preferred_element_type=F32)


def _diff_kernel(lam_ref, g_ref, q_ref, *refs, n_seg, lambda_init):
    kv = refs[:2 * n_seg]
    o_ref = refs[2 * n_seg]
    lp = lam_ref[...]
    lam = (jnp.exp(jnp.sum(lp[0:1] * lp[1:2], axis=-1, keepdims=True))
           - jnp.exp(jnp.sum(lp[2:3] * lp[3:4], axis=-1, keepdims=True)) + lambda_init)
    q = q_ref[0]
    tq = q.shape[0]
    lane = lax.broadcasted_iota(jnp.int32, (1, LANES), 1)
    q2 = jnp.concatenate([q * (lane < 64).astype(BF16), q * (lane >= 64).astype(BF16)], axis=0)
    s = [_nt(q2, kv[2 * j][0].astype(BF16)) for j in range(n_seg)]
    mx = functools.reduce(jnp.maximum, [jnp.max(x, axis=-1, keepdims=True) for x in s])
    acc = None
    for j in range(n_seg):
        v = kv[2 * j + 1][0].astype(BF16)
        v_ext = jnp.concatenate([v, jnp.ones_like(v)], axis=1)
        aj = jnp.dot(jnp.exp(s[j] - mx).astype(BF16), v_ext, preferred_element_type=F32)
        acc = aj if acc is None else acc + aj
    on = acc[:, 0:LANES] / acc[:, LANES:2 * LANES]
    o = on[0:tq] - lam * on[tq:2 * tq]
    o = _rms(o, g_ref[...]) * (1.0 - lambda_init)
    o_ref[0] = o.astype(BF16)


def _diff_attention(lam_p, g_head, q, segs, *, tq, lambda_init):
    b, s, w = q.shape
    in_specs = [
        pl.BlockSpec((4, A_HEAD_DIM), lambda bi, h, qi: (0, 0)),
        pl.BlockSpec((1, A_V_DIM), lambda bi, h, qi: (0, 0)),
        pl.BlockSpec((1, tq, LANES), lambda bi, h, qi: (bi, qi, h)),
    ]
    args = [lam_p, g_head, q]
    for k, v in segs:
        nk = k.shape[1]
        in_specs += [pl.BlockSpec((1, nk, LANES), lambda bi, h, qi: (bi, 0, h)),
                     pl.BlockSpec((1, nk, LANES), lambda bi, h, qi: (bi, 0, h))]
        args += [k, v]
    return pl.pallas_call(
        functools.partial(_diff_kernel, n_seg=len(segs), lambda_init=lambda_init),
        grid=(b, A_HEADS, s // tq),
        in_specs=in_specs,
        out_specs=pl.BlockSpec((1, tq, LANES), lambda bi, h, qi: (bi, qi, h)),
        out_shape=jax.ShapeDtypeStruct((b, s, w), BF16),
        compiler_params=pltpu.CompilerParams(
            dimension_semantics=("arbitrary", "arbitrary", "arbitrary")),
        name="diff_attn_%d" % len(segs),
    )(*args)


def _diff_lat_kernel(lam_ref, g_ref, q_ref, kc_ref, kt_ref, vc_ref, v_ref, o_ref, s_a, m_a, s_b, m_b, *,
                     lambda_init):
    t = pl.program_id(0)
    tq = q_ref.shape[1]
    nkc = kc_ref.shape[1]
    nkn = kt_ref.shape[2]

    @pl.when(t == 0)
    def _():
        s_b[...] = jnp.zeros_like(s_b)
        m_b[...] = jnp.zeros_like(m_b)

    def body(s_w, m_w, s_r, m_r):
        lp = lam_ref[...]
        lam = (jnp.exp(jnp.sum(lp[0:1] * lp[1:2], axis=-1, keepdims=True))
               - jnp.exp(jnp.sum(lp[2:3] * lp[3:4], axis=-1, keepdims=True)) + lambda_init)
        lane = lax.broadcasted_iota(jnp.int32, (1, LANES), 1)

        q = q_ref[0]
        q2 = jnp.concatenate([q * (lane < 64).astype(BF16), q * (lane >= 64).astype(BF16)], axis=0)
        sc = _nt(q2, kc_ref[0].astype(BF16))
        sn = jnp.dot(q2, kt_ref[0], preferred_element_type=F32)
        mx = jnp.maximum(jnp.max(sc, axis=-1, keepdims=True), jnp.max(sn, axis=-1, keepdims=True))
        s_w[:, 0:nkc] = sc
        s_w[:, nkc:nkc + nkn] = sn
        m_w[...] = jnp.broadcast_to(mx, (2 * tq, LANES))

        mp = m_r[...]
        v_all = jnp.concatenate([vc_ref[0].astype(BF16), v_ref[0]], axis=0)
        v_ext = jnp.concatenate([v_all, jnp.ones_like(v_all)], axis=1)
        p = jnp.concatenate(
            [jnp.exp(s_r[:, c:c + LANES] - mp).astype(BF16) for c in range(0, nkc + nkn, LANES)], axis=1)
        acc = jnp.dot(p, v_ext, preferred_element_type=F32)
        on = acc[:, 0:LANES] / acc[:, LANES:2 * LANES]
        o = on[0:tq] - lam * on[tq:2 * tq]
        o_ref[0] = (_rms(o, g_ref[...]) * (1.0 - lambda_init)).astype(BF16)

    @pl.when(t % 2 == 0)
    def _():
        body(s_a, m_a, s_b, m_b)

    @pl.when(t % 2 == 1)
    def _():
        body(s_b, m_b, s_a, m_a)


def _diff_attention_lat(lam_p, g_head, q, kc, kt, vc, v, *, tq, lambda_init):
    b, s, w = q.shape
    past = kc.shape[1]
    nq = s // tq
    n_units = b * A_HEADS * nq
    last = n_units - 1

    def unit(u):
        return u // (A_HEADS * nq), (u // nq) % A_HEADS, u % nq

    def cur(t):
        return unit(jnp.minimum(t, last))

    def prev(t):
        return unit(jnp.maximum(t - 1, 0))

    return pl.pallas_call(
        functools.partial(_diff_lat_kernel, lambda_init=lambda_init),
        grid=(n_units + 1,),
        in_specs=[
            pl.BlockSpec((4, A_HEAD_DIM), lambda t: (0, 0)),
            pl.BlockSpec((1, A_V_DIM), lambda t: (0, 0)),
            pl.BlockSpec((1, tq, LANES), lambda t: (cur(t)[0], cur(t)[2], cur(t)[1])),
            pl.BlockSpec((1, past, LANES), lambda t: (cur(t)[0], 0, cur(t)[1])),
            pl.BlockSpec((1, LANES, s), lambda t: (cur(t)[0], cur(t)[1], 0)),
            pl.BlockSpec((1, past, LANES), lambda t: (prev(t)[0], 0, prev(t)[1])),
            pl.BlockSpec((1, s, LANES), lambda t: (prev(t)[0], 0, prev(t)[1])),
        ],
        out_specs=pl.BlockSpec((1, tq, LANES), lambda t: (prev(t)[0], prev(t)[2], prev(t)[1])),
        out_shape=jax.ShapeDtypeStruct((b, s, w), BF16),
        scratch_shapes=[pltpu.VMEM((2 * tq, past + s), F32), pltpu.VMEM((2 * tq, LANES), F32),
                        pltpu.VMEM((2 * tq, past + s), F32), pltpu.VMEM((2 * tq, LANES), F32)],
        compiler_params=pltpu.CompilerParams(dimension_semantics=("arbitrary",),
                                             vmem_limit_bytes=56 * 1024 * 1024),
        name="diff_attn_lat",
    )(lam_p, g_head, q, kc, kt, vc, v)


def _swa_kernel(sink_ref, q_ref, *refs, tq, seg_kinds):
    n_seg = len(seg_kinds)
    k_refs = refs[:n_seg]
    v_refs = refs[n_seg:2 * n_seg]
    o_ref = refs[2 * n_seg]
    i = pl.program_id(1)
    nb = pl.num_programs(1)
    lane = lax.broadcasted_iota(jnp.int32, (1, LANES), 1)
    low = lane < 64
    lane2 = lax.broadcasted_iota(jnp.int32, (1, 2 * LANES), 1)
    head_masks = [((lane2 // 64) == g).astype(BF16) for g in range(B_GROUP)]
    qi = lax.broadcasted_iota(jnp.int32, (B_GROUP * tq, SWA_Q), 0) & (tq - 1)
    kj = lax.broadcasted_iota(jnp.int32, (B_GROUP * tq, SWA_Q), 1)
    far = 2 * SWA_Q
    gw = B_GROUP * B_HEAD_DIM

    vs = [v_ref[0].astype(BF16) for v_ref in v_refs]
    v_all = vs[0] if n_seg == 1 else jnp.concatenate(vs, axis=0)
    v_ext = jnp.concatenate([v_all, jnp.ones_like(v_all)], axis=1)

    for n in range(B_KV_HEADS):
        q = q_ref[0, :, n * gw:(n + 1) * gw]
        qs = jnp.concatenate([q * hm for hm in head_masks], axis=0)
        ks = []
        for kind, k_ref in zip(seg_kinds, k_refs):
            if kind == "cache":
                k = k_ref[0]
                sw = pltpu.roll(k, 64, 1)
                k2 = (jnp.where(low, k, sw) if n == 0 else jnp.where(low, sw, k)).astype(BF16)
            else:
                k2 = k_ref[0, :, n * LANES:(n + 1) * LANES]
            ks.append(jnp.concatenate([k2, k2], axis=1))
        k_all = ks[0] if n_seg == 1 else jnp.concatenate(ks, axis=0)
        s = _nt(qs, k_all)
        chunks, col = [], 0
        for kind, k in zip(seg_kinds, ks):
            for c in range(0, k.shape[0], LANES):
                sc = s[:, col + c:col + c + LANES]
                if kind == "left":
                    sc = jnp.where(kj >= qi + jnp.where(i > 0, 0, far), sc, NEG_INF)
                elif kind == "right":
                    sc = jnp.where(kj <= qi - jnp.where(i < nb - 1, 0, far), sc, NEG_INF)
                chunks.append(sc)
            col += k.shape[0]
        sinkcol = jnp.concatenate(
            [jnp.full((tq, 1), sink_ref[n * B_GROUP + g], F32) for g in range(B_GROUP)], axis=0)
        mx = jnp.maximum(jnp.max(functools.reduce(jnp.maximum, chunks), axis=-1, keepdims=True), sinkcol)
        p = jnp.concatenate([jnp.exp(c - mx).astype(BF16) for c in chunks], axis=1)
        acc = jnp.dot(p, v_ext, preferred_element_type=F32)
        o = acc[:, 0:LANES] / (acc[:, LANES:2 * LANES] + jnp.exp(sinkcol - mx))
        osw = pltpu.roll(o, 64, 1)
        for j in range(B_GROUP // 2):
            ra = slice((2 * j) * tq, (2 * j + 1) * tq)
            rb = slice((2 * j + 1) * tq, (2 * j + 2) * tq)
            pair = jnp.where(low, o[ra], osw[rb]) if n == 0 else jnp.where(low, osw[ra], o[rb])
            o_ref[0, :, n * gw + j * LANES:n * gw + (j + 1) * LANES] = pair.astype(BF16)


def _swa_attention(sink, q, segs, *, tq):
    b, s, w = q.shape
    in_specs = [
        pl.BlockSpec(memory_space=pltpu.SMEM),
        pl.BlockSpec((1, tq, w), lambda bi, i: (bi, i, 0)),
    ]
    k_specs, v_specs, k_args, v_args, kinds = [], [], [], [], []
    for kind, k, v, rows, idx in segs:
        k_specs.append(pl.BlockSpec((1, rows, k.shape[2]), lambda bi, i, idx=idx: (bi, idx(i), 0)))
        v_specs.append(pl.BlockSpec((1, rows, v.shape[2]), lambda bi, i, idx=idx: (bi, idx(i), 0)))
        k_args.append(k)
        v_args.append(v)
        kinds.append(kind)
    return pl.pallas_call(
        functools.partial(_swa_kernel, tq=tq, seg_kinds=tuple(kinds)),
        grid=(b, s // tq),
        in_specs=in_specs + k_specs + v_specs,
        out_specs=pl.BlockSpec((1, tq, w), lambda bi, i: (bi, i, 0)),
        out_shape=jax.ShapeDtypeStruct((b, s, w), BF16),
        compiler_params=pltpu.CompilerParams(dimension_semantics=("arbitrary", "arbitrary")),
        name="swa_attn_%d" % len(segs),
    )(sink, q, *k_args, *v_args)


def _swa_lat_kernel(sink_ref, q_ref, kc_ref, kl_ref, km_ref, kr_ref, vc_ref, vl_ref, vm_ref, vr_ref, o_ref,
                    s_a, m_a, s_b, m_b, *, nqb):
    t = pl.program_id(0)
    n_units = pl.num_programs(0) - 1
    tq = q_ref.shape[1]
    gw = B_GROUP * B_HEAD_DIM
    i_cur = jnp.minimum(t, n_units - 1) % nqb
    nkc = kc_ref.shape[1]
    rows = B_GROUP * tq

    @pl.when(t == 0)
    def _():
        s_b[...] = jnp.zeros_like(s_b)
        m_b[...] = jnp.zeros_like(m_b)

    def sink_col(n):
        return jnp.concatenate(
            [jnp.full((tq, 1), sink_ref[n * B_GROUP + g], F32) for g in range(B_GROUP)], axis=0)

    def body(s_w, m_w, s_r, m_r):
        lane = lax.broadcasted_iota(jnp.int32, (1, LANES), 1)
        low = lane < 64
        lane2 = lax.broadcasted_iota(jnp.int32, (1, 2 * LANES), 1)
        head_masks = [((lane2 // 64) == g).astype(BF16) for g in range(B_GROUP)]
        qi = lax.broadcasted_iota(jnp.int32, (rows, SWA_Q), 0) & (tq - 1)
        kj = lax.broadcasted_iota(jnp.int32, (rows, SWA_Q), 1)
        far = 2 * SWA_Q
        left_ok = kj >= qi + jnp.where(i_cur > 0, 0, far)
        right_ok = kj <= qi - jnp.where(i_cur < nqb - 1, 0, far)

        kc = kc_ref[0]
        kc_sw = pltpu.roll(kc, 64, 1)
        for n in range(B_KV_HEADS):
            q = q_ref[0, :, n * gw:(n + 1) * gw]
            qs = jnp.concatenate([q * hm for hm in head_masks], axis=0)
            kc2 = (jnp.where(low, kc, kc_sw) if n == 0 else jnp.where(low, kc_sw, kc)).astype(BF16)
            ks = [kc2] + [r[0, :, n * LANES:(n + 1) * LANES] for r in (kl_ref, km_ref, kr_ref)]
            k_all = jnp.concatenate([jnp.concatenate([k, k], axis=1) for k in ks], axis=0)
            s = _nt(qs, k_all)
            chunks = [s[:, c:c + LANES] for c in range(0, s.shape[1], LANES)]
            il = nkc // LANES
            chunks[il] = jnp.where(left_ok, chunks[il], NEG_INF)
            chunks[il + 2] = jnp.where(right_ok, chunks[il + 2], NEG_INF)
            mx = jnp.maximum(jnp.max(functools.reduce(jnp.maximum, chunks), axis=-1, keepdims=True), sink_col(n))
            for c, ch in enumerate(chunks):
                s_w[n * rows:(n + 1) * rows, c * LANES:(c + 1) * LANES] = ch
            m_w[n * rows:(n + 1) * rows, :] = jnp.broadcast_to(mx, (rows, LANES))

        v_all = jnp.concatenate([vc_ref[0].astype(BF16), vl_ref[0], vm_ref[0], vr_ref[0]], axis=0)
        v_ext = jnp.concatenate([v_all, jnp.ones_like(v_all)], axis=1)
        nk = v_all.shape[0]
        for n in range(B_KV_HEADS):
            mp = m_r[n * rows:(n + 1) * rows, :]
            p = jnp.concatenate([jnp.exp(s_r[n * rows:(n + 1) * rows, c:c + LANES] - mp).astype(BF16)
                                 for c in range(0, nk, LANES)], axis=1)
            acc = jnp.dot(p, v_ext, preferred_element_type=F32)
            o = acc[:, 0:LANES] / (acc[:, LANES:2 * LANES] + jnp.exp(sink_col(n) - mp))
            osw = pltpu.roll(o, 64, 1)
            for j in range(B_GROUP // 2):
                ra = slice((2 * j) * tq, (2 * j + 1) * tq)
                rb = slice((2 * j + 1) * tq, (2 * j + 2) * tq)
                pair = jnp.where(low, o[ra], osw[rb]) if n == 0 else jnp.where(low, osw[ra], o[rb])
                o_ref[0, :, n * gw + j * LANES:n * gw + (j + 1) * LANES] = pair.astype(BF16)

    @pl.when(t % 2 == 0)
    def _():
        body(s_a, m_a, s_b, m_b)

    @pl.when(t % 2 == 1)
    def _():
        body(s_b, m_b, s_a, m_a)


def _swa_attention_lat(sink, q, kc, k2, vc, v):
    b, s, w = q.shape
    past = kc.shape[1]
    tq = SWA_Q
    nqb = s // tq
    n_units = b * nqb
    last = n_units - 1
    nk = past + 3 * tq

    def cur(t):
        u = jnp.minimum(t, last)
        return u // nqb, u % nqb

    def prev(t):
        u = jnp.maximum(t - 1, 0)
        return u // nqb, u % nqb

    lo = lambda i: jnp.maximum(i - 1, 0)
    hi = lambda i: jnp.minimum(i + 1, nqb - 1)
    kspec = lambda f: pl.BlockSpec((1, tq, 2 * LANES), lambda t: (cur(t)[0], f(cur(t)[1]), 0))
    vspec = lambda f: pl.BlockSpec((1, tq, LANES), lambda t: (prev(t)[0], f(prev(t)[1]), 0))
    same = lambda i: i
    rows = B_KV_HEADS * B_GROUP * tq
    return pl.pallas_call(
        functools.partial(_swa_lat_kernel, nqb=nqb),
        grid=(n_units + 1,),
        in_specs=[
            pl.BlockSpec(memory_space=pltpu.SMEM),
            pl.BlockSpec((1, tq, w), lambda t: (cur(t)[0], cur(t)[1], 0)),
            pl.BlockSpec((1, past, LANES), lambda t: (cur(t)[0], 0, 0)),
            kspec(lo), kspec(same), kspec(hi),
            pl.BlockSpec((1, past, LANES), lambda t: (prev(t)[0], 0, 0)),
            vspec(lo), vspec(same), vspec(hi),
        ],
        out_specs=pl.BlockSpec((1, tq, w), lambda t: (prev(t)[0], prev(t)[1], 0)),
        out_shape=jax.ShapeDtypeStruct((b, s, w), BF16),
        scratch_shapes=[pltpu.VMEM((rows, nk), F32), pltpu.VMEM((rows, LANES), F32),
                        pltpu.VMEM((rows, nk), F32), pltpu.VMEM((rows, LANES), F32)],
        compiler_params=pltpu.CompilerParams(dimension_semantics=("arbitrary",)),
        name="swa_attn_lat",
    )(sink, q, kc, k2, k2, k2, vc, v, v, v)


def _postmix_kernel(*refs, n_ctx_tiles):
    (xc, xl, oac, oal, obc, obl, sgac, sgal, sgbc, sgbl, g1_ref, sh2_ref, sc2_ref, gpm_ref, gpf_ref,
     wpa_ref, wpb_ref, wo_ref, wr_ref, wrl_ref, br_ref, x1_o, h2_o, route_o) = refs
    is_ctx = pl.program_id(0) < n_ctx_tiles
    pick = lambda a, b: jnp.where(is_ctx, a[...], b[...])
    pa = jnp.dot(pick(oac, oal), wpa_ref[...], preferred_element_type=F32)
    pb = jnp.dot(pick(obc, obl), wpb_ref[...], preferred_element_type=F32)
    mix = pick(sgac, sgal).astype(F32) * pa + pick(sgbc, sgbl).astype(F32) * pb
    m2 = jnp.dot(mix.astype(BF16), wo_ref[...], preferred_element_type=F32)
    x1 = pick(xc, xl) + g1_ref[0] * _rms(m2, gpm_ref[...])
    x1_o[...] = x1
    h2 = _rms(x1, gpf_ref[...]) * (1.0 + sc2_ref[0]) + sh2_ref[0]
    h2_o[...] = h2.astype(BF16)

    h_hi = h2.astype(BF16)
    h_lo = (h2 - h_hi.astype(F32)).astype(BF16)
    logits = (jnp.dot(h_hi, wr_ref[...], preferred_element_type=F32)
              + jnp.dot(h_lo, wr_ref[...], preferred_element_type=F32)
              + jnp.dot(h_hi, wrl_ref[...], preferred_element_type=F32) + br_ref[...])
    tm = logits.shape[0]
    lt = logits.T
    row = lax.broadcasted_iota(jnp.int32, (EXPERTS_PER_GROUP, tm), 0).astype(F32)
    none = float(EXPERTS_PER_GROUP)
    lg = jnp.where(row < N_GROUPS, lt[N_EXPERTS:N_EXPERTS + EXPERTS_PER_GROUP], -jnp.inf)
    mg = jnp.max(lg, axis=0, keepdims=True)
    g_sel = jnp.min(jnp.where(lg == mg, row, none), axis=0, keepdims=True)
    g_w = 1.0 / jnp.sum(jnp.exp(lg - mg), axis=0, keepdims=True)
    le = lt[0:EXPERTS_PER_GROUP]
    for g in range(1, N_GROUPS):
        le = jnp.where(g_sel == g, lt[g * EXPERTS_PER_GROUP:(g + 1) * EXPERTS_PER_GROUP], le)
    v0 = jnp.max(le, axis=0, keepdims=True)
    i0 = jnp.min(jnp.where(le == v0, row, none), axis=0, keepdims=True)
    le1 = jnp.where(row == i0, -jnp.inf, le)
    v1 = jnp.max(le1, axis=0, keepdims=True)
    i1 = jnp.min(jnp.where(le1 == v1, row, none), axis=0, keepdims=True)
    e = jnp.exp(v1 - v0)
    w0 = g_w / (1.0 + e)
    w1 = g_w * e / (1.0 + e)
    e0 = g_sel * EXPERTS_PER_GROUP + i0
    e1 = g_sel * EXPERTS_PER_GROUP + i1
    rt = jnp.where(row == 0, e0, jnp.where(row == 1, e1, jnp.where(row == 2, w0, jnp.where(row == 3, w1, 0.0))))
    rt = jnp.concatenate([rt, jnp.zeros((LANES - EXPERTS_PER_GROUP, tm), F32)], axis=0)
    route_o[...] = rt.T


def _postmix(ctx_in, lat_in, mod3, gpm, gpf, wpa, wpb, wo, wr, wrl, br, *, lat_seq, tm):
    t_ctx, d = ctx_in[0].shape
    t_lat = lat_in[0].shape[0]
    assert t_ctx % tm == 0 and lat_seq % tm == 0
    nc = t_ctx // tm
    nl = t_lat // tm
    per = lat_seq // tm
    sub = d // LANES
    t_all = t_ctx + t_lat

    mod_row = lambda i: jnp.where(i < nc, 0, 1 + jnp.maximum(i - nc, 0) // per)
    full = lambda a: pl.BlockSpec(a.shape, lambda i: (0,) * a.ndim)
    modspec = lambda c: pl.BlockSpec((1, 1, d), lambda i: (mod_row(i), 0, c))
    in_specs, args = [], []
    for a_c, a_l in zip(ctx_in, lat_in):
        w = a_c.shape[1]
        in_specs += [pl.BlockSpec((tm, w), lambda i: (jnp.minimum(i, nc - 1), 0)),
                     pl.BlockSpec((tm, w), lambda i: (jnp.maximum(i - nc, 0), 0))]
        args += [a_c, a_l]
    in_specs += [modspec(2), modspec(3), modspec(4), full(gpm), full(gpf),
                 full(wpa), full(wpb), full(wo), full(wr), full(wrl), full(br)]
    args += [mod3, mod3, mod3, gpm, gpf, wpa, wpb, wo, wr, wrl, br]
    return pl.pallas_call(
        functools.partial(_postmix_kernel, n_ctx_tiles=nc),
        grid=(nc + nl,),
        in_specs=in_specs,
        out_specs=[pl.BlockSpec((tm, d), lambda i: (i, 0)),
                   pl.BlockSpec((tm, d), lambda i: (i, 0)),
                   pl.BlockSpec((tm, LANES), lambda i: (i, 0))],
        out_shape=[jax.ShapeDtypeStruct((t_all, d), F32),
                   jax.ShapeDtypeStruct((t_all, d), BF16),
                   jax.ShapeDtypeStruct((t_all, LANES), F32)],
        compiler_params=pltpu.CompilerParams(dimension_semantics=("arbitrary",)),
        name="postmix",
    )(*args)


def _segment_copies(src, src_row, dst, dst_row, n, sub, sem):
    @pl.when(n > 0)
    def _():
        pltpu.make_async_copy(src.at[pl.ds(pl.multiple_of(src_row * sub, sub), n * sub)],
                              dst.at[pl.ds(pl.multiple_of(dst_row * sub, sub), n * sub)], sem).start()


def _local_positions(route, tile_base):
    tm = route.shape[0]
    lane = lax.broadcasted_iota(jnp.int32, (tm, LANES), 1).astype(F32)
    is0 = lane == route[:, 0:1]
    is1 = lane == route[:, 1:2]
    earlier = (lax.broadcasted_iota(jnp.int32, (tm, tm), 1)
               < lax.broadcasted_iota(jnp.int32, (tm, tm), 0)).astype(BF16)
    pre0 = jnp.dot(earlier, is0.astype(BF16), preferred_element_type=F32)
    pre1 = jnp.dot(earlier, is1.astype(BF16), preferred_element_type=F32)
    cnt0 = jnp.sum(is0.astype(F32), axis=0, keepdims=True)
    lpos0 = jnp.sum(jnp.where(is0, tile_base + pre0, 0.0), axis=-1, keepdims=True)
    lpos1 = jnp.sum(jnp.where(is1, tile_base + cnt0 + pre1, 0.0), axis=-1, keepdims=True)
    return lpos0.astype(jnp.int32), lpos1.astype(jnp.int32)


def _dispatch_kernel(ss_ref, sl_ref, tb_ref, h_ref, r_ref, tbv_ref, xs_hbm, pbuf, zbuf, sem, zsem, *, n_asg, rows):
    i = pl.program_id(0)
    nt = pl.num_programs(0)
    tm, d = h_ref.shape
    sub = d // LANES
    nrow = TOP_K * tm
    slot = i % 2

    def wait_slot(s):
        pltpu.make_async_copy(pbuf.at[pl.ds(pl.multiple_of(s * nrow * sub, nrow * sub), nrow * sub)],
                              xs_hbm.at[pl.ds(0, nrow * sub)], sem.at[s]).wait()

    def slack_copy():
        return pltpu.make_async_copy(zbuf, xs_hbm.at[pl.ds(n_asg * sub, rows * sub)], zsem.at[0])

    @pl.when(i == 0)
    def _():
        zbuf[...] = jnp.zeros_like(zbuf)
        slack_copy().start()

    lpos0, lpos1 = _local_positions(r_ref[...], tbv_ref[0])
    p = lax.broadcasted_iota(jnp.int32, (tm, nrow), 1)
    sel = ((p == lpos0) | (p == lpos1)).astype(BF16)
    xp = lax.dot_general(sel, h_ref[...].astype(BF16), (((0,), (0,)), ((), ())),
                         preferred_element_type=F32)

    @pl.when(i >= 2)
    def _():
        wait_slot(slot)

    _store_row_tiles(pbuf, slot * nrow * sub, xp)

    def seg(e, c):
        k = i * N_EXPERTS + e
        _segment_copies(pbuf, slot * nrow + tb_ref[k], xs_hbm, ss_ref[k], sl_ref[k], sub, sem.at[slot])
        return c
    lax.fori_loop(0, N_EXPERTS, seg, 0)

    @pl.when(i == nt - 1)
    def _():
        wait_slot(slot)

        @pl.when(nt >= 2)
        def _():
            wait_slot(1 - slot)
        slack_copy().wait()


def _dispatch(h2, route, tables, *, tm, rows):
    seg_start, seg_len, tile_base, tile_base_v = tables
    t, d = h2.shape
    sub = d // LANES
    n_asg = t * TOP_K
    grid_spec = pltpu.PrefetchScalarGridSpec(
        num_scalar_prefetch=3,
        grid=(t // tm,),
        in_specs=[
            pl.BlockSpec((tm, d), lambda i, *_: (i, 0)),
            pl.BlockSpec((tm, LANES), lambda i, *_: (i, 0)),
            pl.BlockSpec((1, 1, LANES), lambda i, *_: (i, 0, 0)),
        ],
        out_specs=pl.BlockSpec(memory_space=pl.ANY),
        scratch_shapes=[pltpu.VMEM((2 * TOP_K * tm * sub, LANES), F32), pltpu.VMEM((rows * sub, LANES), F32),
                        pltpu.SemaphoreType.DMA((2,)), pltpu.SemaphoreType.DMA((1,))],
    )
    return pl.pallas_call(
        functools.partial(_dispatch_kernel, n_asg=n_asg, rows=rows),
        grid_spec=grid_spec,
        out_shape=jax.ShapeDtypeStruct(((n_asg + rows) * sub, LANES), F32),
        compiler_params=pltpu.CompilerParams(dimension_semantics=("arbitrary",)),
        name="dispatch",
    )(seg_start, seg_len, tile_base, h2, route, tile_base_v)


def _moe_kernel(be_ref, row0_ref, nact_ref, par_ref, nxt_ref, xs_hbm, w1_hbm, w3_hbm, w2_hbm, ys_hbm,
                xbuf, obuf, wf1, wf3, wf2, w1b, w3b, w2b, rsem, wsem, gsem, *, rows, sub):
    i = pl.program_id(0)
    nact = nact_ref[0]
    slot = i % 2
    nslot = 1 - slot
    blk = rows * sub

    def weight_copies(e, s):
        return [pltpu.make_async_copy(w_hbm.at[e], wf.at[s], gsem.at[s])
                for w_hbm, wf in ((w1_hbm, wf1), (w3_hbm, wf3), (w2_hbm, wf2))]

    def read(j, s):
        return pltpu.make_async_copy(xs_hbm.at[pl.ds(pl.multiple_of(row0_ref[j] * sub, sub), blk)],
                                     xbuf.at[pl.ds(pl.multiple_of(s * blk, blk), blk)], rsem.at[s])

    def write(j, s):
        return pltpu.make_async_copy(obuf.at[pl.ds(pl.multiple_of(s * blk, blk), blk)],
                                     ys_hbm.at[pl.ds(pl.multiple_of(row0_ref[j] * sub, sub), blk)], wsem.at[s])

    @pl.when(i == 0)
    def _():
        read(0, 0).start()
        for c in weight_copies(be_ref[0], 0):
            c.start()

    @pl.when(i < nact)
    def _():
        @pl.when(i + 1 < nact)
        def _():
            read(i + 1, nslot).start()

        changed = jnp.logical_or(i == 0, be_ref[i] != be_ref[jnp.maximum(i - 1, 0)])

        @pl.when(changed)
        def _():
            s = par_ref[i]
            for c in weight_copies(be_ref[i], s):
                c.wait()
            w1b[...] = wf1[s].astype(BF16)
            w3b[...] = wf3[s].astype(BF16)
            w2b[...] = wf2[s].astype(BF16)

            @pl.when(nxt_ref[i] >= 0)
            def _():
                for c in weight_copies(nxt_ref[i], 1 - s):
                    c.start()

        read(i, slot).wait()
        x = _load_row_tiles(xbuf, slot * blk, rows, sub).astype(BF16)
        a = jnp.dot(x, w1b[...], preferred_element_type=F32)
        b = jnp.dot(x, w3b[...], preferred_element_type=F32)
        hmid = (a * jax.nn.sigmoid(a) * b).astype(BF16)
        y = jnp.dot(hmid, w2b[...], preferred_element_type=F32)
        _store_row_tiles(obuf, slot * blk, y)

        @pl.when(i >= 1)
        def _():
            write(i - 1, nslot).wait()
        write(i, slot).start()

    @pl.when(i == nact)
    def _():
        write(i - 1, nslot).wait()
        obuf[pl.ds(pl.multiple_of(slot * blk, blk), blk), :] = jnp.zeros((blk, LANES), F32)
        tail = pltpu.make_async_copy(obuf.at[pl.ds(pl.multiple_of(slot * blk, blk), blk)],
                                     ys_hbm.at[pl.ds(ys_hbm.shape[0] - blk, blk)], wsem.at[slot])
        tail.start()
        tail.wait()


def _moe(blk_tables, xs, w1, w3, w2):
    blk_expert, row0, nact, parity, nxt = blk_tables
    nblk = blk_expert.shape[0] - 1
    d, de = w1.shape[1], w1.shape[2]
    sub = d // LANES
    rows = MOE_ROWS
    anyspec = pl.BlockSpec(memory_space=pl.ANY)
    grid_spec = pltpu.PrefetchScalarGridSpec(
        num_scalar_prefetch=5,
        grid=(nblk + 1,),
        in_specs=[anyspec, anyspec, anyspec, anyspec],
        out_specs=anyspec,
        scratch_shapes=[
            pltpu.VMEM((2 * rows * sub, LANES), F32),
            pltpu.VMEM((2 * rows * sub, LANES), F32),
            pltpu.VMEM((2, d, de), F32),
            pltpu.VMEM((2, d, de), F32),
            pltpu.VMEM((2, de, d), F32),
            pltpu.VMEM((d, de), BF16),
            pltpu.VMEM((d, de), BF16),
            pltpu.VMEM((de, d), BF16),
            pltpu.SemaphoreType.DMA((2,)),
            pltpu.SemaphoreType.DMA((2,)),
            pltpu.SemaphoreType.DMA((2,)),
        ],
    )
    return pl.pallas_call(
        functools.partial(_moe_kernel, rows=rows, sub=sub),
        grid_spec=grid_spec,
        out_shape=jax.ShapeDtypeStruct(xs.shape, F32),
        compiler_params=pltpu.CompilerParams(dimension_semantics=("arbitrary",)),
        name="expert_mlp",
    )(blk_expert, row0, nact, parity, nxt, xs, w1, w3, w2)


def _combine_kernel(ss_ref, sl_ref, tb_ref, x1_ref, r_ref, tbv_ref, g2_ref, gpost_ref, ys_hbm, o_ref,
                    ybuf, sem, *, tile0):
    i = pl.program_id(0)
    nt = pl.num_programs(0)
    tm, d = x1_ref.shape
    sub = d // LANES
    nrow = TOP_K * tm
    slot = i % 2

    def fetch(tile, s):
        def seg(e, c):
            k = tile * N_EXPERTS + e
            _segment_copies(ys_hbm, ss_ref[k], ybuf, s * nrow + tb_ref[k], sl_ref[k], sub, sem.at[s])
            return c
        lax.fori_loop(0, N_EXPERTS, seg, 0)

    @pl.when(i == 0)
    def _():
        fetch(tile0, 0)

    @pl.when(i + 1 < nt)
    def _():
        fetch(tile0 + i + 1, 1 - slot)

    pltpu.make_async_copy(ys_hbm.at[pl.ds(0, nrow * sub)],
                          ybuf.at[pl.ds(pl.multiple_of(slot * nrow * sub, nrow * sub), nrow * sub)],
                          sem.at[slot]).wait()
    r = r_ref[...]
    lpos0, lpos1 = _local_positions(r, tbv_ref[0])
    p = lax.broadcasted_iota(jnp.int32, (tm, nrow), 1)
    q = (jnp.where(p == lpos0, r[:, 2:3], 0.0) + jnp.where(p == lpos1, r[:, 3:4], 0.0)).astype(BF16)
    ysort = _load_row_tiles(ybuf, slot * nrow * sub, nrow, sub).astype(BF16)
    y = jnp.dot(q, ysort, preferred_element_type=F32)
    o_ref[...] = x1_ref[...] + g2_ref[0] * _rms(y, gpost_ref[...])


def _combine(x1, ys, route, tables, mod3, mod_row0, gpost, *, t, seq, tm, tok_off, is_ctx):
    seg_start, seg_len, tile_base, tile_base_v = tables
    t_all, d = x1.shape
    per = seq // tm
    sub = d // LANES
    boff = tok_off // tm

    def row(i):
        return (i // per) if not is_ctx else 0

    grid_spec = pltpu.PrefetchScalarGridSpec(
        num_scalar_prefetch=3,
        grid=(t // tm,),
        in_specs=[
            pl.BlockSpec((tm, d), lambda i, *_: (boff + i, 0)),
            pl.BlockSpec((tm, LANES), lambda i, *_: (boff + i, 0)),
            pl.BlockSpec((1, 1, LANES), lambda i, *_: (boff + i, 0, 0)),
            pl.BlockSpec((1, 1, d), lambda i, *_: (mod_row0 + row(i), 0, 5)),
            pl.BlockSpec((1, d), lambda i, *_: (0, 0)),
            pl.BlockSpec(memory_space=pl.ANY),
        ],
        out_specs=pl.BlockSpec((tm, d), lambda i, *_: (i, 0)),
        scratch_shapes=[pltpu.VMEM((2 * TOP_K * tm * sub, LANES), F32), pltpu.SemaphoreType.DMA((2,))],
    )
    return pl.pallas_call(
        functools.partial(_combine_kernel, tile0=boff),
        grid_spec=grid_spec,
        out_shape=jax.ShapeDtypeStruct((t, d), F32),
        compiler_params=pltpu.CompilerParams(dimension_semantics=("arbitrary",)),
        name="combine_ctx" if is_ctx else "combine_lat",
    )(seg_start, seg_len, tile_base, x1, route, tile_base_v, mod3, gpost, ys)


def _routing_tables(route, tm, rows):
    t = route.shape[0]
    nt = t // tm
    n_asg = t * TOP_K
    ex = jnp.arange(N_EXPERTS, dtype=jnp.int32)
    e01 = route[:, 0:TOP_K].astype(jnp.int32)
    cnt_te = jnp.sum((e01.reshape(nt, tm * TOP_K, 1) == ex).astype(jnp.int32), axis=1)
    cnt_e = jnp.sum(cnt_te, axis=0)
    start_e = jnp.cumsum(cnt_e) - cnt_e
    seg_start = start_e[None, :] + jnp.cumsum(cnt_te, axis=0) - cnt_te
    tile_base = jnp.cumsum(cnt_te, axis=1) - cnt_te
    tile_base_v = jnp.pad(tile_base.astype(F32), ((0, 0), (0, LANES - N_EXPERTS))).reshape(nt, 1, LANES)

    nblk_e = (cnt_e + rows - 1) // rows
    blk_end = jnp.cumsum(nblk_e)
    blk_start = blk_end - nblk_e
    n_blocks = n_asg // rows + N_EXPERTS
    b = jnp.arange(n_blocks + 1, dtype=jnp.int32)
    be = jnp.minimum(jnp.sum((blk_end[None, :] <= b[:, None]).astype(jnp.int32), axis=1), N_EXPERTS - 1)
    first = jnp.sum(jnp.where(be[:, None] == ex[None, :], (start_e - blk_start * rows)[None, :], 0), axis=1)
    row0 = jnp.clip(first + b * rows, 0, n_asg)
    nact = blk_end[-1:].astype(jnp.int32)
    used = cnt_e > 0
    parity_e = (jnp.cumsum(used.astype(jnp.int32)) - 1) % 2
    later = (ex[None, :] > ex[:, None]) & used[None, :]
    nxt_e = jnp.min(jnp.where(later, ex[None, :], N_EXPERTS), axis=1)
    nxt_e = jnp.where(nxt_e == N_EXPERTS, -1, nxt_e)
    pick = lambda tab: jnp.sum(jnp.where(be[:, None] == ex[None, :], tab[None, :], 0), axis=1).astype(jnp.int32)
    seg = (seg_start.reshape(-1).astype(jnp.int32), cnt_te.reshape(-1).astype(jnp.int32),
           tile_base.reshape(-1).astype(jnp.int32), tile_base_v)
    blk = (be.astype(jnp.int32), row0.astype(jnp.int32), nact, pick(parity_e), pick(nxt_e))
    return seg, blk


def _rope_tables(n_tok):
    n_rows = n_tok // GRID_W
    rows = jnp.repeat(jnp.arange(n_rows), GRID_W).astype(F32)
    cols = jnp.tile(jnp.arange(GRID_W), n_rows).astype(F32)
    quarter = A_HEAD_DIM // 4
    inv = ROPE_BASE ** (-jnp.arange(quarter, dtype=F32) / quarter)
    ang = jnp.concatenate([rows[:, None] * inv, cols[:, None] * inv], axis=-1)
    cos, sin = jnp.cos(ang), jnp.sin(ang)
    cos_t = jnp.tile(jnp.concatenate([cos, cos], axis=-1), (1, LANES // A_HEAD_DIM))
    sin_t = jnp.tile(jnp.concatenate([-sin, sin], axis=-1), (1, LANES // A_HEAD_DIM))
    return cos_t, sin_t


def kernel(x_prompt, x_sample, c, cache_diff_k, cache_diff_v, cache_swa_k, cache_swa_v, c_ctx, w_ada, b_ada, g_pre_mix, g_post_mix, g_pre_ffn, g_post_ffn, w_in, lam_q1, lam_k1, lam_q2, lam_k2, g_diff_head, sink, w_proj_a, w_proj_b, w_out, w_router_group, b_router_group, w_router_expert, b_router_expert, w_e1, w_e3, w_e2):
    depth = w_in.shape[0]
    assert depth == 1
    l = 0
    bp, sp, d = x_prompt.shape
    bs, ss, _ = x_sample.shape
    lambda_init = 0.8 - 0.6 * math.exp(-0.3 * l)
    assert A_HEAD_DIM == B_HEAD_DIM and ss % GRID_W == 0 and bs + 1 <= MOD_ROWS

    c_all = jnp.concatenate([c_ctx[None, :], c, jnp.zeros((MOD_ROWS - 1 - bs, d), F32)], axis=0)
    mod = _modulation(c_all, w_ada[l], b_ada[l][None, :])
    mod3 = mod.reshape(MOD_ROWS, 1, 6 * d)

    w_in_b = w_in[l].astype(BF16)
    wpa = w_proj_a[l].astype(BF16)
    wpb = w_proj_b[l].astype(BF16)
    wo = w_out[l].astype(BF16)
    n_r = N_GROUPS + N_EXPERTS
    wr = jnp.concatenate([w_router_expert[l], w_router_group[l], jnp.zeros((d, LANES - n_r), F32)], axis=1)
    br = jnp.concatenate([b_router_expert[l], b_router_group[l], jnp.zeros((LANES - n_r,), F32)])[None, :]
    wr_hi = wr.astype(BF16)
    wr_lo = (wr - wr_hi.astype(F32)).astype(BF16)
    lam_p = jnp.stack([lam_q1[l], lam_k1[l], lam_q2[l], lam_k2[l]], axis=0)
    g_head = g_diff_head[l][None, :]
    sink_l = sink[l]
    cos_t, sin_t = _rope_tables(ss)

    xp2 = x_prompt.reshape(bp * sp, d)
    xs2 = x_sample.reshape(bs * ss, d)
    gpre = g_pre_mix[l][None, :]

    (qa_c, ka_c, va_c, qb_c, kb2_c, vb_c, sga_c, sgb_c, kaf, vaf, kbf, vbf) = _inproj(
        xp2, mod3, 0, gpre, cos_t, sin_t, w_in_b, seq=sp, tm=sp, is_ctx=True)
    r3 = lambda a, b_: a.reshape(b_, -1, a.shape[-1])
    oa_c = _diff_attention(lam_p, g_head, r3(qa_c, bp), [(r3(ka_c, bp), r3(va_c, bp))],
                           tq=sp, lambda_init=lambda_init)
    ob_c = _swa_attention(sink_l, r3(qb_c, bp),
                          [("rep", r3(kb2_c, bp), r3(vb_c, bp), sp, lambda i: 0)], tq=sp)

    (qa_s, ka_s, va_s, qb_s, kb2_s, vb_s, sga_s, sgb_s) = _inproj(
        xs2, mod3, 1, gpre, cos_t, sin_t, w_in_b, seq=ss, tm=512, is_ctx=False)
    past = cache_diff_k.shape[2]
    ck = cache_diff_k[:, l].reshape(bs, past, -1)
    cv = cache_diff_v[:, l].reshape(bs, past, -1)
    oa_s = _diff_attention_lat(lam_p, g_head, r3(qa_s, bs), ck, ka_s, cv, r3(va_s, bs),
                               tq=512, lambda_init=lambda_init)
    sk = cache_swa_k[:, l].reshape(bs, past, -1)
    sv = cache_swa_v[:, l].reshape(bs, past, -1)
    nqb = ss // SWA_Q
    kb2_3, vb_3 = r3(kb2_s, bs), r3(vb_s, bs)
    ob_s = _swa_attention_lat(sink_l, r3(qb_s, bs), sk, kb2_3, sv, vb_3)

    gpm = g_post_mix[l][None, :]
    gpf = g_pre_ffn[l][None, :]
    t_ctx, t_lat = bp * sp, bs * ss
    x1, h2t, route = _postmix(
        (xp2, oa_c.reshape(t_ctx, -1), ob_c.reshape(t_ctx, -1), sga_c, sgb_c),
        (xs2, oa_s.reshape(t_lat, -1), ob_s.reshape(t_lat, -1), sga_s, sgb_s),
        mod3, gpm, gpf, wpa, wpb, wo, wr_hi, wr_lo, br, lat_seq=ss, tm=512)

    tables, blk_tables = _routing_tables(route, MOE_TILE, MOE_ROWS)
    xs = _dispatch(h2t, route, tables, tm=MOE_TILE, rows=MOE_ROWS)
    ys = _moe(blk_tables, xs, w_e1[l], w_e3[l], w_e2[l])

    gpost = g_post_ffn[l][None, :]
    y_p = _combine(x1, ys, route, tables, mod3, 0, gpost, t=t_ctx, seq=sp, tm=MOE_TILE, tok_off=0, is_ctx=True)
    y_s = _combine(x1, ys, route, tables, mod3, 1, gpost, t=t_lat, seq=ss, tm=MOE_TILE, tok_off=t_ctx,
                   is_ctx=False)

    ha = A_HEADS
    return (y_p.reshape(bp, sp, d), y_s.reshape(bs, ss, d),
            kaf.reshape(bp, 1, sp, ha, 2, A_HEAD_DIM), vaf.reshape(bp, 1, sp, ha, A_V_DIM),
            kbf.reshape(bp, 1, sp, B_KV_HEADS, B_HEAD_DIM), vbf.reshape(bp, 1, sp, B_KV_HEADS, B_HEAD_DIM))
```

```python
import functools
import math

import jax
import jax.numpy as jnp
from jax import lax
from jax.experimental import pallas as pl
from jax.experimental.pallas import tpu as pltpu

F32 = jnp.float32
BF16 = jnp.bfloat16
HIGHEST = lax.Precision.HIGHEST

GRID_W = 64
ROPE_BASE = 10000.0
EPS = 1e-6
NEG_INF = -1e30
A_HEADS = 4
A_HEAD_DIM = 64
A_V_DIM = 2 * A_HEAD_DIM
B_HEADS = 8
B_KV_HEADS = 2
B_GROUP = B_HEADS // B_KV_HEADS
B_HEAD_DIM = 64
WINDOW = 128
N_GROUPS = 4
EXPERTS_PER_GROUP = 8
N_EXPERTS = N_GROUPS * EXPERTS_PER_GROUP
TOP_K = 2

LANES = 128
MOD_ROWS = 16
MOE_ROWS = 256
MOE_TILE = 512
SWA_Q = 128

_QA = 0
_KA = _QA + A_HEADS * 2 * A_HEAD_DIM
_VA = _KA + A_HEADS * 2 * A_HEAD_DIM
_QB = _VA + A_HEADS * A_V_DIM
_KB = _QB + B_HEADS * B_HEAD_DIM
_VB = _KB + B_KV_HEADS * B_HEAD_DIM
_GA = _VB + B_KV_HEADS * B_HEAD_DIM


def _rms(x, g):
    return x * lax.rsqrt(jnp.mean(x * x, axis=-1, keepdims=True) + EPS) * g


def _store_row_tiles(ref, base, val):
    sub = val.shape[1] // LANES
    for s in range(sub):
        ref[pl.ds(base + s, val.shape[0], stride=sub), :] = val[:, s * LANES:(s + 1) * LANES]


def _load_row_tiles(ref, base, n_rows, sub):
    return jnp.concatenate([ref[pl.ds(base + s, n_rows, stride=sub), :] for s in range(sub)], axis=1)


def _mod_kernel(c_ref, w_ref, b_ref, o_ref):
    c = c_ref[...]
    a = c * jax.nn.sigmoid(c)
    o_ref[...] = jnp.dot(a, w_ref[...], precision=HIGHEST, preferred_element_type=F32) + b_ref[...]


def _modulation(c_all, w_ada, b_ada):
    d, n = w_ada.shape
    tn = 512
    return pl.pallas_call(
        _mod_kernel,
        grid=(n // tn,),
        in_specs=[
            pl.BlockSpec((MOD_ROWS, d), lambda j: (0, 0)),
            pl.BlockSpec((d, tn), lambda j: (0, j)),
            pl.BlockSpec((1, tn), lambda j: (0, j)),
        ],
        out_specs=pl.BlockSpec((MOD_ROWS, tn), lambda j: (0, j)),
        out_shape=jax.ShapeDtypeStruct((MOD_ROWS, n), F32),
        name="modulation",
    )(c_all, w_ada, b_ada)


def _rope128(z, cos, sin_signed, first_half):
    rot = jnp.where(first_half, pltpu.roll(z, 96, 1), pltpu.roll(z, 32, 1))
    return z * cos + rot * sin_signed


def _inproj_kernel(x_ref, sh_ref, sc_ref, g_ref, cos_ref, sin_ref, w_ref, *outs, is_ctx):
    x = x_ref[...]
    h = _rms(x, g_ref[...]) * (1.0 + sc_ref[0]) + sh_ref[0]
    hb = h.astype(BF16)
    lane = lax.broadcasted_iota(jnp.int32, (1, LANES), 1)
    first_half = (lane % 64) < 32
    low = lane < 64

    def seg(lo, hi):
        return jnp.dot(hb, w_ref[:, lo:hi], preferred_element_type=F32)

    def rope(z):
        if is_ctx:
            return z
        cos = cos_ref[...]
        sin = sin_ref[...]
        parts = [_rope128(z[:, j:j + LANES], cos, sin, first_half) for j in range(0, z.shape[1], LANES)]
        return parts[0] if len(parts) == 1 else jnp.concatenate(parts, axis=1)

    if is_ctx:
        qa_o, ka_o, va_o, qb_o, kb2_o, vb_o, sga_o, sgb_o, kaf_o, vaf_o, kbf_o, vbf_o = outs
    else:
        qa_o, ka_o, va_o, qb_o, kb2_o, vb_o, sga_o, sgb_o = outs

    scale = A_HEAD_DIM ** -0.5
    qa_o[...] = (rope(seg(_QA, _KA)) * scale).astype(BF16)
    ka = rope(seg(_KA, _VA))
    if is_ctx:
        ka_o[...] = ka.astype(BF16)
    else:
        ka_o[0] = ka.T.astype(BF16)
    va = seg(_VA, _QB)
    va_o[...] = va.astype(BF16)
    qb_o[...] = (rope(seg(_QB, _KB)) * (B_HEAD_DIM ** -0.5)).astype(BF16)
    kb = rope(seg(_KB, _VB))
    kb_sw = pltpu.roll(kb, 64, 1)
    kb2_o[:, 0:LANES] = jnp.where(low, kb, kb_sw).astype(BF16)
    kb2_o[:, LANES:2 * LANES] = jnp.where(low, kb_sw, kb).astype(BF16)
    vb = seg(_VB, _GA)
    if is_ctx:
        vb_o[...] = vb.astype(BF16)
    else:
        vb_sw = pltpu.roll(vb, 64, 1)
        vb_o[:, 0:LANES] = jnp.where(low, vb, vb_sw).astype(BF16)
        vb_o[:, LANES:2 * LANES] = jnp.where(low, vb_sw, vb).astype(BF16)
    d = x.shape[1]
    sga_o[...] = jax.nn.sigmoid(seg(_GA, _GA + d)).astype(BF16)
    sgb_o[...] = jax.nn.sigmoid(seg(_GA + d, _GA + 2 * d)).astype(BF16)
    if is_ctx:
        kaf_o[...] = ka
        vaf_o[...] = va
        kbf_o[...] = kb
        vbf_o[...] = vb


def _inproj(x2, mod3, mod_row0, g_pre, cos_t, sin_t, w_in_b, *, seq, tm, is_ctx):
    t, d = x2.shape
    per = seq // tm
    n_in = w_in_b.shape[1]
    wa = A_HEADS * 2 * A_HEAD_DIM
    wkb = B_KV_HEADS * B_HEAD_DIM

    def row(i):
        return (i // per) if not is_ctx else 0

    tok = lambda w: pl.BlockSpec((tm, w), lambda i: (i, 0))
    out_shape = [
        jax.ShapeDtypeStruct((t, wa), BF16), jax.ShapeDtypeStruct((t, wa), BF16),
        jax.ShapeDtypeStruct((t, wa), BF16), jax.ShapeDtypeStruct((t, wa), BF16),
        jax.ShapeDtypeStruct((t, 2 * wkb), BF16), jax.ShapeDtypeStruct((t, wkb), BF16),
        jax.ShapeDtypeStruct((t, d), BF16), jax.ShapeDtypeStruct((t, d), BF16),
    ]
    out_specs = [tok(wa), tok(wa), tok(wa), tok(wa), tok(2 * wkb), tok(wkb), tok(d), tok(d)]
    if not is_ctx:
        out_shape[5] = jax.ShapeDtypeStruct((t, 2 * wkb), BF16)
        out_specs[5] = tok(2 * wkb)
        out_shape[1] = jax.ShapeDtypeStruct((t // seq, wa, seq), BF16)
        out_specs[1] = pl.BlockSpec((1, wa, tm), lambda i: (i // per, 0, i % per))
    if is_ctx:
        out_shape += [jax.ShapeDtypeStruct((t, wa), F32), jax.ShapeDtypeStruct((t, wa), F32),
                      jax.ShapeDtypeStruct((t, wkb), F32), jax.ShapeDtypeStruct((t, wkb), F32)]
        out_specs += [tok(wa), tok(wa), tok(wkb), tok(wkb)]
    return pl.pallas_call(
        functools.partial(_inproj_kernel, is_ctx=is_ctx),
        grid=(t // tm,),
        in_specs=[
            pl.BlockSpec((tm, d), lambda i: (i, 0)),
            pl.BlockSpec((1, 1, d), lambda i: (mod_row0 + row(i), 0, 0)),
            pl.BlockSpec((1, 1, d), lambda i: (mod_row0 + row(i), 0, 1)),
            pl.BlockSpec((1, d), lambda i: (0, 0)),
            pl.BlockSpec((tm, LANES), lambda i: (i % per, 0)),
            pl.BlockSpec((tm, LANES), lambda i: (i % per, 0)),
            pl.BlockSpec((d, n_in), lambda i: (0, 0)),
        ],
        out_specs=out_specs,
        out_shape=out_shape,
        compiler_params=pltpu.CompilerParams(dimension_semantics=("arbitrary",)),
        name="inproj_ctx" if is_ctx else "inproj_lat",
    )(x2, mod3, mod3, g_pre, cos_t, sin_t, w_in_b)


def _nt(a, b):
    return lax.dot_general(a, b, (((1,), (1,)), ((), ())), preferred_element_type=F32)


def _diff_kernel(lam_ref, g_ref, q_ref, *refs, n_seg, lambda_init):
    kv = refs[:2 * n_seg]
    o_ref = refs[2 * n_seg]
    lp = lam_ref[...]
    lam = (jnp.exp(jnp.sum(lp[0:1] * lp[1:2], axis=-1, keepdims=True))
           - jnp.exp(jnp.sum(lp[2:3] * lp[3:4], axis=-1, keepdims=True)) + lambda_init)
    q = q_ref[0]
    tq = q.shape[0]
    lane = lax.broadcasted_iota(jnp.int32, (1, LANES), 1)
    q2 = jnp.concatenate([q * (lane < 64).astype(BF16), q * (lane >= 64).astype(BF16)], axis=0)
    s = [_nt(q2, kv[2 * j][0].astype(BF16)) for j in range(n_seg)]
    mx = functools.reduce(jnp.maximum, [jnp.max(x, axis=-1, keepdims=True) for x in s])
    acc = None
    for j in range(n_seg):
        v = kv[2 * j + 1][0].astype(BF16)
        v_ext = jnp.concatenate([v, jnp.ones_like(v)], axis=1)
        aj = jnp.dot(jnp.exp(s[j] - mx).astype(BF16), v_ext, preferred_element_type=F32)
        acc = aj if acc is None else acc + aj
    on = acc[:, 0:LANES] / acc[:, LANES:2 * LANES]
    o = on[0:tq] - lam * on[tq:2 * tq]
    o = _rms(o, g_ref[...]) * (1.0 - lambda_init)
    o_ref[0] = o.astype(BF16)


def _diff_attention(lam_p, g_head, q, segs, *, tq, lambda_init):
    b, s, w = q.shape
    in_specs = [
        pl.BlockSpec((4, A_HEAD_DIM), lambda bi, h, qi: (0, 0)),
        pl.BlockSpec((1, A_V_DIM), lambda bi, h, qi: (0, 0)),
        pl.BlockSpec((1, tq, LANES), lambda bi, h, qi: (bi, qi, h)),
    ]
    args = [lam_p, g_head, q]
    for k, v in segs:
        nk = k.shape[1]
        in_specs += [pl.BlockSpec((1, nk, LANES), lambda bi, h, qi: (bi, 0, h)),
                     pl.BlockSpec((1, nk, LANES), lambda bi, h, qi: (bi, 0, h))]
        args += [k, v]
    return pl.pallas_call(
        functools.partial(_diff_kernel, n_seg=len(segs), lambda_init=lambda_init),
        grid=(b, A_HEADS, s // tq),
        in_specs=in_specs,
        out_specs=pl.BlockSpec((1, tq, LANES), lambda bi, h, qi: (bi, qi, h)),
        out_shape=jax.ShapeDtypeStruct((b, s, w), BF16),
        compiler_params=pltpu.CompilerParams(
            dimension_semantics=("arbitrary", "arbitrary", "arbitrary")),
        name="diff_attn_%d" % len(segs),
    )(*args)


def _diff_lat_kernel(lam_ref, g_ref, q_ref, kc_ref, kt_ref, vc_ref, v_ref, o_ref, s_a, m_a, s_b, m_b, *,
                     lambda_init):
    t = pl.program_id(0)
    tq = q_ref.shape[1]
    nkc = kc_ref.shape[1]
    nkn = kt_ref.shape[2]

    @pl.when(t == 0)
    def _():
        s_b[...] = jnp.zeros_like(s_b)
        m_b[...] = jnp.zeros_like(m_b)

    def body(s_w, m_w, s_r, m_r):
        lp = lam_ref[...]
        lam = (jnp.exp(jnp.sum(lp[0:1] * lp[1:2], axis=-1, keepdims=True))
               - jnp.exp(jnp.sum(lp[2:3] * lp[3:4], axis=-1, keepdims=True)) + lambda_init)
        lane = lax.broadcasted_iota(jnp.int32, (1, LANES), 1)

        q = q_ref[0]
        q2 = jnp.concatenate([q * (lane < 64).astype(BF16), q * (lane >= 64).astype(BF16)], axis=0)
        sc = _nt(q2, kc_ref[0].astype(BF16))
        sn = jnp.dot(q2, kt_ref[0], preferred_element_type=F32)
        mx = jnp.maximum(jnp.max(sc, axis=-1, keepdims=True), jnp.max(sn, axis=-1, keepdims=True))
        s_w[:, 0:nkc] = sc
        s_w[:, nkc:nkc + nkn] = sn
        m_w[...] = jnp.broadcast_to(mx, (2 * tq, LANES))

        mp = m_r[...]
        v_all = jnp.concatenate([vc_ref[0].astype(BF16), v_ref[0]], axis=0)
        v_ext = jnp.concatenate([v_all, jnp.ones_like(v_all)], axis=1)
        p = jnp.concatenate(
            [jnp.exp(s_r[:, c:c + LANES] - mp).astype(BF16) for c in range(0, nkc + nkn, LANES)], axis=1)
        acc = jnp.dot(p, v_ext, preferred_element_type=F32)
        on = acc[:, 0:LANES] / acc[:, LANES:2 * LANES]
        o = on[0:tq] - lam * on[tq:2 * tq]
        o_ref[0] = (_rms(o, g_ref[...]) * (1.0 - lambda_init)).astype(BF16)

    @pl.when(t % 2 == 0)
    def _():
        body(s_a, m_a, s_b, m_b)

    @pl.when(t % 2 == 1)
    def _():
        body(s_b, m_b, s_a, m_a)


def _diff_attention_lat(lam_p, g_head, q, kc, kt, vc, v, *, tq, lambda_init):
    b, s, w = q.shape
    past = kc.shape[1]
    nq = s // tq
    n_units = b * A_HEADS * nq
    last = n_units - 1

    def unit(u):
        return u // (A_HEADS * nq), (u // nq) % A_HEADS, u % nq

    def cur(t):
        return unit(jnp.minimum(t, last))

    def prev(t):
        return unit(jnp.maximum(t - 1, 0))

    return pl.pallas_call(
        functools.partial(_diff_lat_kernel, lambda_init=lambda_init),
        grid=(n_units + 1,),
        in_specs=[
            pl.BlockSpec((4, A_HEAD_DIM), lambda t: (0, 0)),
            pl.BlockSpec((1, A_V_DIM), lambda t: (0, 0)),
            pl.BlockSpec((1, tq, LANES), lambda t: (cur(t)[0], cur(t)[2], cur(t)[1])),
            pl.BlockSpec((1, past, LANES), lambda t: (cur(t)[0], 0, cur(t)[1])),
            pl.BlockSpec((1, LANES, s), lambda t: (cur(t)[0], cur(t)[1], 0)),
            pl.BlockSpec((1, past, LANES), lambda t: (prev(t)[0], 0, prev(t)[1])),
            pl.BlockSpec((1, s, LANES), lambda t: (prev(t)[0], 0, prev(t)[1])),
        ],
        out_specs=pl.BlockSpec((1, tq, LANES), lambda t: (prev(t)[0], prev(t)[2], prev(t)[1])),
        out_shape=jax.ShapeDtypeStruct((b, s, w), BF16),
        scratch_shapes=[pltpu.VMEM((2 * tq, past + s), F32), pltpu.VMEM((2 * tq, LANES), F32),
                        pltpu.VMEM((2 * tq, past + s), F32), pltpu.VMEM((2 * tq, LANES), F32)],
        compiler_params=pltpu.CompilerParams(dimension_semantics=("arbitrary",),
                                             vmem_limit_bytes=56 * 1024 * 1024),
        name="diff_attn_lat",
    )(lam_p, g_head, q, kc, kt, vc, v)


def _swa_kernel(sink_ref, q_ref, *refs, tq, seg_kinds):
    n_seg = len(seg_kinds)
    k_refs = refs[:n_seg]
    v_refs = refs[n_seg:2 * n_seg]
    o_ref = refs[2 * n_seg]
    i = pl.program_id(1)
    nb = pl.num_programs(1)
    lane = lax.broadcasted_iota(jnp.int32, (1, LANES), 1)
    low = lane < 64
    lane2 = lax.broadcasted_iota(jnp.int32, (1, 2 * LANES), 1)
    head_masks = [((lane2 // 64) == g).astype(BF16) for g in range(B_GROUP)]
    qi = lax.broadcasted_iota(jnp.int32, (B_GROUP * tq, SWA_Q), 0) & (tq - 1)
    kj = lax.broadcasted_iota(jnp.int32, (B_GROUP * tq, SWA_Q), 1)
    far = 2 * SWA_Q
    gw = B_GROUP * B_HEAD_DIM

    vs = [v_ref[0].astype(BF16) for v_ref in v_refs]
    v_all = vs[0] if n_seg == 1 else jnp.concatenate(vs, axis=0)
    v_ext = jnp.concatenate([v_all, jnp.ones_like(v_all)], axis=1)

    for n in range(B_KV_HEADS):
        q = q_ref[0, :, n * gw:(n + 1) * gw]
        qs = jnp.concatenate([q * hm for hm in head_masks], axis=0)
        ks = []
        for kind, k_ref in zip(seg_kinds, k_refs):
            if kind == "cache":
                k = k_ref[0]
                sw = pltpu.roll(k, 64, 1)
                k2 = (jnp.where(low, k, sw) if n == 0 else jnp.where(low, sw, k)).astype(BF16)
            else:
                k2 = k_ref[0, :, n * LANES:(n + 1) * LANES]
            ks.append(jnp.concatenate([k2, k2], axis=1))
        k_all = ks[0] if n_seg == 1 else jnp.concatenate(ks, axis=0)
        s = _nt(qs, k_all)
        chunks, col = [], 0
        for kind, k in zip(seg_kinds, ks):
            for c in range(0, k.shape[0], LANES):
                sc = s[:, col + c:col + c + LANES]
                if kind == "left":
                    sc = jnp.where(kj >= qi + jnp.where(i > 0, 0, far), sc, NEG_INF)
                elif kind == "right":
                    sc = jnp.where(kj <= qi - jnp.where(i < nb - 1, 0, far), sc, NEG_INF)
                chunks.append(sc)
            col += k.shape[0]
        sinkcol = jnp.concatenate(
            [jnp.full((tq, 1), sink_ref[n * B_GROUP + g], F32) for g in range(B_GROUP)], axis=0)
        mx = jnp.maximum(jnp.max(functools.reduce(jnp.maximum, chunks), axis=-1, keepdims=True), sinkcol)
        p = jnp.concatenate([jnp.exp(c - mx).astype(BF16) for c in chunks], axis=1)
        acc = jnp.dot(p, v_ext, preferred_element_type=F32)
        o = acc[:, 0:LANES] / (acc[:, LANES:2 * LANES] + jnp.exp(sinkcol - mx))
        osw = pltpu.roll(o, 64, 1)
        for j in range(B_GROUP // 2):
            ra = slice((2 * j) * tq, (2 * j + 1) * tq)
            rb = slice((2 * j + 1) * tq, (2 * j + 2) * tq)
            pair = jnp.where(low, o[ra], osw[rb]) if n == 0 else jnp.where(low, osw[ra], o[rb])
            o_ref[0, :, n * gw + j * LANES:n * gw + (j + 1) * LANES] = pair.astype(BF16)


def _swa_attention(sink, q, segs, *, tq):
    b, s, w = q.shape
    in_specs = [
        pl.BlockSpec(memory_space=pltpu.SMEM),
        pl.BlockSpec((1, tq, w), lambda bi, i: (bi, i, 0)),
    ]
    k_specs, v_specs, k_args, v_args, kinds = [], [], [], [], []
    for kind, k, v, rows, idx in segs:
        k_specs.append(pl.BlockSpec((1, rows, k.shape[2]), lambda bi, i, idx=idx: (bi, idx(i), 0)))
        v_specs.append(pl.BlockSpec((1, rows, v.shape[2]), lambda bi, i, idx=idx: (bi, idx(i), 0)))
        k_args.append(k)
        v_args.append(v)
        kinds.append(kind)
    return pl.pallas_call(
        functools.partial(_swa_kernel, tq=tq, seg_kinds=tuple(kinds)),
        grid=(b, s // tq),
        in_specs=in_specs + k_specs + v_specs,
        out_specs=pl.BlockSpec((1, tq, w), lambda bi, i: (bi, i, 0)),
        out_shape=jax.ShapeDtypeStruct((b, s, w), BF16),
        compiler_params=pltpu.CompilerParams(dimension_semantics=("arbitrary", "arbitrary")),
        name="swa_attn_%d" % len(segs),
    )(sink, q, *k_args, *v_args)


def _swa_lat_kernel(sink_ref, q_ref, kc_ref, kl_ref, km_ref, kr_ref, vc_ref, vl_ref, vm_ref, vr_ref, o_ref,
                    s_a, m_a, s_b, m_b, *, nqb):
    t = pl.program_id(0)
    n_units = pl.num_programs(0) - 1
    tq = q_ref.shape[1]
    gw = B_GROUP * B_HEAD_DIM
    i_cur = jnp.minimum(t, n_units - 1) % nqb
    nkc = kc_ref.shape[1]
    rows = B_GROUP * tq

    @pl.when(t == 0)
    def _():
        s_b[...] = jnp.zeros_like(s_b)
        m_b[...] = jnp.zeros_like(m_b)

    def sink_col(n):
        return jnp.concatenate(
            [jnp.full((tq, 1), sink_ref[n * B_GROUP + g], F32) for g in range(B_GROUP)], axis=0)

    def body(s_w, m_w, s_r, m_r):
        lane = lax.broadcasted_iota(jnp.int32, (1, LANES), 1)
        low = lane < 64
        lane2 = lax.broadcasted_iota(jnp.int32, (1, 2 * LANES), 1)
        head_masks = [((lane2 // 64) == g).astype(BF16) for g in range(B_GROUP)]
        qi = lax.broadcasted_iota(jnp.int32, (rows, SWA_Q), 0) & (tq - 1)
        kj = lax.broadcasted_iota(jnp.int32, (rows, SWA_Q), 1)
        far = 2 * SWA_Q
        left_ok = kj >= qi + jnp.where(i_cur > 0, 0, far)
        right_ok = kj <= qi - jnp.where(i_cur < nqb - 1, 0, far)

        kc = kc_ref[0]
        kc_sw = pltpu.roll(kc, 64, 1)
        for n in range(B_KV_HEADS):
            q = q_ref[0, :, n * gw:(n + 1) * gw]
            qs = jnp.concatenate([q * hm for hm in head_masks], axis=0)
            kc2 = (jnp.where(low, kc, kc_sw) if n == 0 else jnp.where(low, kc_sw, kc)).astype(BF16)
            ks = [kc2] + [r[0, :, n * LANES:(n + 1) * LANES] for r in (kl_ref, km_ref, kr_ref)]
            k_all = jnp.concatenate([jnp.concatenate([k, k], axis=1) for k in ks], axis=0)
            s = _nt(qs, k_all)
            chunks = [s[:, c:c + LANES] for c in range(0, s.shape[1], LANES)]
            il = nkc // LANES
            chunks[il] = jnp.where(left_ok, chunks[il], NEG_INF)
            chunks[il + 2] = jnp.where(right_ok, chunks[il + 2], NEG_INF)
            top = jnp.max(functools.reduce(jnp.maximum, chunks), keepdims=True)
            for g in range(B_GROUP):
                top = jnp.maximum(top, sink_ref[n * B_GROUP + g])
            for c, ch in enumerate(chunks):
                s_w[n * rows:(n + 1) * rows, c * LANES:(c + 1) * LANES] = ch
            m_w[n * rows:(n + 1) * rows, :] = jnp.broadcast_to(top, (rows, LANES))

        vc = vc_ref[0]
        vc_sw = pltpu.roll(vc, 64, 1)
        nk = nkc + 3 * tq
        finishers = []
        for n in range(B_KV_HEADS):
            vc2 =(jnp.where(low, vc, vc_sw) if n == 0 else jnp.where(low, vc_sw, vc)).astype(BF16)
            v_all = jnp.concatenate(
                [vc2] + [r[0, :, n * LANES:(n + 1) * LANES] for r in (vl_ref, vm_ref, vr_ref)], axis=0)
            v_ext = jnp.concatenate([v_all, jnp.ones_like(v_all)], axis=1)
            sink = sink_col(n)

            def finish(mp, n=n, v_ext=v_ext, sink=sink):
                p = jnp.concatenate([jnp.exp(s_r[n * rows:(n + 1) * rows, c:c + LANES] - mp).astype(BF16)
                                     for c in range(0, nk, LANES)], axis=1)
                acc = jnp.dot(p, v_ext, preferred_element_type=F32)
                den = acc[:, LANES:2 * LANES] + jnp.exp(sink - mp)
                o = acc[:, 0:LANES] / den
                for j in range(B_GROUP // 2):
                    pair = jnp.where(low, o[(2 * j) * tq:(2 * j + 1) * tq], o[(2 * j + 1) * tq:(2 * j + 2) * tq])
                    o_ref[0, :, n * gw + j * LANES:n * gw + (j + 1) * LANES] = pair.astype(BF16)
                return jnp.min(den)

            finishers.append((n, finish, sink, finish(m_r[n * rows:(n + 1) * rows, :])))

        smallest = functools.reduce(jnp.minimum, [f[3] for f in finishers])

        @pl.when(jnp.logical_not(smallest >= 1e-30))
        def _():
            for n, finish, sink, _ in finishers:
                row_max = functools.reduce(
                    jnp.maximum, [s_r[n * rows:(n + 1) * rows, c:c + LANES] for c in range(0, nk, LANES)])
                mx = jnp.maximum(jnp.max(row_max, axis=-1, keepdims=True), sink)
                finish(jnp.broadcast_to(mx, (rows, LANES)))

    @pl.when(t % 2 == 0)
    def _():
        body(s_a, m_a, s_b, m_b)

    @pl.when(t % 2 == 1)
    def _():
        body(s_b, m_b, s_a, m_a)


def _swa_attention_lat(sink, q, kc, k2, vc, v):
    b, s, w = q.shape
    past = kc.shape[1]
    tq = SWA_Q
    nqb = s // tq
    n_units = b * nqb
    last = n_units - 1
    nk = past + 3 * tq

    def cur(t):
        u = jnp.minimum(t, last)
        return u // nqb, u % nqb

    def prev(t):
        u = jnp.maximum(t - 1, 0)
        return u // nqb, u % nqb

    lo = lambda i: jnp.maximum(i - 1, 0)
    hi = lambda i: jnp.minimum(i + 1, nqb - 1)
    kspec = lambda f: pl.BlockSpec((1, tq, 2 * LANES), lambda t: (cur(t)[0], f(cur(t)[1]), 0))
    vspec = lambda f: pl.BlockSpec((1, tq, 2 * LANES), lambda t: (prev(t)[0], f(prev(t)[1]), 0))
    same = lambda i: i
    rows = B_KV_HEADS * B_GROUP * tq
    return pl.pallas_call(
        functools.partial(_swa_lat_kernel, nqb=nqb),
        grid=(n_units + 1,),
        in_specs=[
            pl.BlockSpec(memory_space=pltpu.SMEM),
            pl.BlockSpec((1, tq, w), lambda t: (cur(t)[0], cur(t)[1], 0)),
            pl.BlockSpec((1, past, LANES), lambda t: (cur(t)[0], 0, 0)),
            kspec(lo), kspec(same), kspec(hi),
            pl.BlockSpec((1, past, LANES), lambda t: (prev(t)[0], 0, 0)),
            vspec(lo), vspec(same), vspec(hi),
        ],
        out_specs=pl.BlockSpec((1, tq, w), lambda t: (prev(t)[0], prev(t)[1], 0)),
        out_shape=jax.ShapeDtypeStruct((b, s, w), BF16),
        scratch_shapes=[pltpu.VMEM((rows, nk), F32), pltpu.VMEM((rows, LANES), F32),
                        pltpu.VMEM((rows, nk), F32), pltpu.VMEM((rows, LANES), F32)],
        compiler_params=pltpu.CompilerParams(dimension_semantics=("arbitrary",)),
        name="swa_attn_lat",
    )(sink, q, kc, k2, k2, k2, vc, v, v, v)


def _postmix_kernel(*refs, n_ctx_tiles):
    (xc, xl, oac, oal, obc, obl, sgac, sgal, sgbc, sgbl, g1_ref, sh2_ref, sc2_ref, gpm_ref, gpf_ref,
     wpa_ref, wpb_ref, wo_ref, wr_ref, wrl_ref, br_ref, x1_o, h2_o, route_o) = refs
    is_ctx = pl.program_id(0) < n_ctx_tiles
    pick = lambda a, b: jnp.where(is_ctx, a[...], b[...])
    pa = jnp.dot(pick(oac, oal), wpa_ref[...], preferred_element_type=F32)
    pb = jnp.dot(pick(obc, obl), wpb_ref[...], preferred_element_type=F32)
    mix = pick(sgac, sgal).astype(F32) * pa + pick(sgbc, sgbl).astype(F32) * pb
    m2 = jnp.dot(mix.astype(BF16), wo_ref[...], preferred_element_type=F32)
    x1 = pick(xc, xl) + g1_ref[0] * _rms(m2, gpm_ref[...])
    x1_o[...] = x1
    h2 = _rms(x1, gpf_ref[...]) * (1.0 + sc2_ref[0]) + sh2_ref[0]
    h2_o[...] = h2.astype(BF16)

    h_hi = h2.astype(BF16)
    h_lo = (h2 - h_hi.astype(F32)).astype(BF16)
    logits = (jnp.dot(h_hi, wr_ref[...], preferred_element_type=F32)
              + jnp.dot(h_lo, wr_ref[...], preferred_element_type=F32)
              + jnp.dot(h_hi, wrl_ref[...], preferred_element_type=F32) + br_ref[...])
    tm = logits.shape[0]
    lt = logits.T
    row = lax.broadcasted_iota(jnp.int32, (EXPERTS_PER_GROUP, tm), 0).astype(F32)
    none = float(EXPERTS_PER_GROUP)
    lg = jnp.where(row < N_GROUPS, lt[N_EXPERTS:N_EXPERTS + EXPERTS_PER_GROUP], -jnp.inf)
    mg = jnp.max(lg, axis=0, keepdims=True)
    g_sel = jnp.min(jnp.where(lg == mg, row, none), axis=0, keepdims=True)
    g_w = 1.0 / jnp.sum(jnp.exp(lg - mg), axis=0, keepdims=True)
    le = lt[0:EXPERTS_PER_GROUP]
    for g in range(1, N_GROUPS):
        le = jnp.where(g_sel == g, lt[g * EXPERTS_PER_GROUP:(g + 1) * EXPERTS_PER_GROUP], le)
    v0 = jnp.max(le, axis=0, keepdims=True)
    i0 = jnp.min(jnp.where(le == v0, row, none), axis=0, keepdims=True)
    le1 = jnp.where(row == i0, -jnp.inf, le)
    v1 = jnp.max(le1, axis=0, keepdims=True)
    i1 = jnp.min(jnp.where(le1 == v1, row, none), axis=0, keepdims=True)
    e = jnp.exp(v1 - v0)
    w0 = g_w / (1.0 + e)
    w1 = g_w * e / (1.0 + e)
    e0 = g_sel * EXPERTS_PER_GROUP + i0
    e1 = g_sel * EXPERTS_PER_GROUP + i1
    rt = jnp.where(row == 0, e0, jnp.where(row == 1, e1, jnp.where(row == 2, w0, jnp.where(row == 3, w1, 0.0))))
    rt = jnp.concatenate([rt, jnp.zeros((LANES - EXPERTS_PER_GROUP, tm), F32)], axis=0)
    route_o[...] = rt.T


def _postmix(ctx_in, lat_in, mod3, gpm, gpf, wpa, wpb, wo, wr, wrl, br, *, lat_seq, tm):
    t_ctx, d = ctx_in[0].shape
    t_lat = lat_in[0].shape[0]
    assert t_ctx % tm == 0 and lat_seq % tm == 0
    nc = t_ctx // tm
    nl = t_lat // tm
    per = lat_seq // tm
    sub = d // LANES
    t_all = t_ctx + t_lat

    mod_row = lambda i: jnp.where(i < nc, 0, 1 + jnp.maximum(i - nc, 0) // per)
    full = lambda a: pl.BlockSpec(a.shape, lambda i: (0,) * a.ndim)
    modspec = lambda c: pl.BlockSpec((1, 1, d), lambda i: (mod_row(i), 0, c))
    in_specs, args = [], []
    for a_c, a_l in zip(ctx_in, lat_in):
        w = a_c.shape[1]
        in_specs += [pl.BlockSpec((tm, w), lambda i: (jnp.minimum(i, nc - 1), 0)),
                     pl.BlockSpec((tm, w), lambda i: (jnp.maximum(i - nc, 0), 0))]
        args += [a_c, a_l]
    in_specs += [modspec(2), modspec(3), modspec(4), full(gpm), full(gpf),
                 full(wpa), full(wpb), full(wo), full(wr), full(wrl), full(br)]
    args += [mod3, mod3, mod3, gpm, gpf, wpa, wpb, wo, wr, wrl, br]
    return pl.pallas_call(
        functools.partial(_postmix_kernel, n_ctx_tiles=nc),
        grid=(nc + nl,),
        in_specs=in_specs,
        out_specs=[pl.BlockSpec((tm, d), lambda i: (i, 0)),
                   pl.BlockSpec((tm, d), lambda i: (i, 0)),
                   pl.BlockSpec((tm, LANES), lambda i: (i, 0))],
        out_shape=[jax.ShapeDtypeStruct((t_all, d), F32),
                   jax.ShapeDtypeStruct((t_all, d), BF16),
                   jax.ShapeDtypeStruct((t_all, LANES), F32)],
        compiler_params=pltpu.CompilerParams(dimension_semantics=("arbitrary",)),
        name="postmix",
    )(*args)


def _segment_copies(src, src_row, dst, dst_row, n, sub, sem):
    @pl.when(n > 0)
    def _():
        pltpu.make_async_copy(src.at[pl.ds(pl.multiple_of(src_row * sub, sub), n * sub)],
                              dst.at[pl.ds(pl.multiple_of(dst_row * sub, sub), n * sub)], sem).start()


def _local_positions(route, tile_base):
    tm = route.shape[0]
    lane = lax.broadcasted_iota(jnp.int32, (tm, LANES), 1).astype(F32)
    is0 = lane == route[:, 0:1]
    is1 = lane == route[:, 1:2]
    earlier = (lax.broadcasted_iota(jnp.int32, (tm, tm), 1)
               < lax.broadcasted_iota(jnp.int32, (tm, tm), 0)).astype(BF16)
    pre0 = jnp.dot(earlier, is0.astype(BF16), preferred_element_type=F32)
    pre1 = jnp.dot(earlier, is1.astype(BF16), preferred_element_type=F32)
    cnt0 = jnp.sum(is0.astype(F32), axis=0, keepdims=True)
    lpos0 = jnp.sum(jnp.where(is0, tile_base + pre0, 0.0), axis=-1, keepdims=True)
    lpos1 = jnp.sum(jnp.where(is1, tile_base + cnt0 + pre1, 0.0), axis=-1, keepdims=True)
    return lpos0.astype(jnp.int32), lpos1.astype(jnp.int32)


def _dispatch_kernel(ss_ref, sl_ref, tb_ref, h_ref, r_ref, tbv_ref, xs_hbm, pbuf, zbuf, sem, zsem, *, n_asg, rows):
    i = pl.program_id(0)
    nt = pl.num_programs(0)
    tm, d = h_ref.shape
    sub = d // LANES
    nrow = TOP_K * tm
    slot = i % 2

    def wait_slot(s):
        pltpu.make_async_copy(pbuf.at[pl.ds(pl.multiple_of(s * nrow * sub, nrow * sub), nrow * sub)],
                              xs_hbm.at[pl.ds(0, nrow * sub)], sem.at[s]).wait()

    def slack_copy():
        return pltpu.make_async_copy(zbuf, xs_hbm.at[pl.ds(n_asg * sub, rows * sub)], zsem.at[0])

    @pl.when(i == 0)
    def _():
        zbuf[...] = jnp.zeros_like(zbuf)
        slack_copy().start()

    lpos0, lpos1 = _local_positions(r_ref[...], tbv_ref[0])
    p = lax.broadcasted_iota(jnp.int32, (tm, nrow), 1)
    sel = ((p == lpos0) | (p == lpos1)).astype(BF16)
    xp = lax.dot_general(sel, h_ref[...].astype(BF16), (((0,), (0,)), ((), ())),
                         preferred_element_type=F32)

    @pl.when(i >= 2)
    def _():
        wait_slot(slot)

    _store_row_tiles(pbuf, slot * nrow * sub, xp)

    def seg(e, c):
        k = i * N_EXPERTS + e
        _segment_copies(pbuf, slot * nrow + tb_ref[k], xs_hbm, ss_ref[k], sl_ref[k], sub, sem.at[slot])
        return c
    lax.fori_loop(0, N_EXPERTS, seg, 0)

    @pl.when(i == nt - 1)
    def _():
        wait_slot(slot)

        @pl.when(nt >= 2)
        def _():
            wait_slot(1 - slot)
        slack_copy().wait()


def _dispatch(h2, route, tables, *, tm, rows):
    seg_start, seg_len, tile_base, tile_base_v = tables
    t, d = h2.shape
    sub = d // LANES
    n_asg = t * TOP_K
    grid_spec = pltpu.PrefetchScalarGridSpec(
        num_scalar_prefetch=3,
        grid=(t // tm,),
        in_specs=[
            pl.BlockSpec((tm, d), lambda i, *_: (i, 0)),
            pl.BlockSpec((tm, LANES), lambda i, *_: (i, 0)),
            pl.BlockSpec((1, 1, LANES), lambda i, *_: (i, 0, 0)),
        ],
        out_specs=pl.BlockSpec(memory_space=pl.ANY),
        scratch_shapes=[pltpu.VMEM((2 * TOP_K * tm * sub, LANES), F32), pltpu.VMEM((rows * sub, LANES), F32),
                        pltpu.SemaphoreType.DMA((2,)), pltpu.SemaphoreType.DMA((1,))],
    )
    return pl.pallas_call(
        functools.partial(_dispatch_kernel, n_asg=n_asg, rows=rows),
        grid_spec=grid_spec,
        out_shape=jax.ShapeDtypeStruct(((n_asg + rows) * sub, LANES), F32),
        compiler_params=pltpu.CompilerParams(dimension_semantics=("arbitrary",)),
        name="dispatch",
    )(seg_start, seg_len, tile_base, h2, route, tile_base_v)


def _moe_kernel(be_ref, row0_ref, nact_ref, par_ref, nxt_ref, xs_hbm, w1_hbm, w3_hbm, w2_hbm, ys_hbm,
                xbuf, obuf, wf1, wf3, wf2, w1b, w3b, w2b, rsem, wsem, gsem, *, rows, sub):
    i = pl.program_id(0)
    nact = nact_ref[0]
    slot = i % 2
    nslot = 1 - slot
    blk = rows * sub

    def weight_copies(e, s):
        return [pltpu.make_async_copy(w_hbm.at[e], wf.at[s], gsem.at[s])
                for w_hbm, wf in ((w1_hbm, wf1), (w3_hbm, wf3), (w2_hbm, wf2))]

    def read(j, s):
        return pltpu.make_async_copy(xs_hbm.at[pl.ds(pl.multiple_of(row0_ref[j] * sub, sub), blk)],
                                     xbuf.at[pl.ds(pl.multiple_of(s * blk, blk), blk)], rsem.at[s])

    def write(j, s):
        return pltpu.make_async_copy(obuf.at[pl.ds(pl.multiple_of(s * blk, blk), blk)],
                                     ys_hbm.at[pl.ds(pl.multiple_of(row0_ref[j] * sub, sub), blk)], wsem.at[s])

    @pl.when(i == 0)
    def _():
        read(0, 0).start()
        for c in weight_copies(be_ref[0], 0):
            c.start()

    @pl.when(i < nact)
    def _():
        @pl.when(i + 1 < nact)
        def _():
            read(i + 1, nslot).start()

        changed = jnp.logical_or(i == 0, be_ref[i] != be_ref[jnp.maximum(i - 1, 0)])

        @pl.when(changed)
        def _():
            s = par_ref[i]
            for c in weight_copies(be_ref[i], s):
                c.wait()
            w1b[...] = wf1[s].astype(BF16)
            w3b[...] = wf3[s].astype(BF16)
            w2b[...] = wf2[s].astype(BF16)

            @pl.when(nxt_ref[i] >= 0)
            def _():
                for c in weight_copies(nxt_ref[i], 1 - s):
                    c.start()

        read(i, slot).wait()
        x = _load_row_tiles(xbuf, slot * blk, rows, sub).astype(BF16)
        a = jnp.dot(x, w1b[...], preferred_element_type=F32)
        b = jnp.dot(x, w3b[...], preferred_element_type=F32)
        hmid = (a * jax.nn.sigmoid(a) * b).astype(BF16)
        y = jnp.dot(hmid, w2b[...], preferred_element_type=F32)
        _store_row_tiles(obuf, slot * blk, y)

        @pl.when(i >= 1)
        def _():
            write(i - 1, nslot).wait()
        write(i, slot).start()

    @pl.when(i == nact)
    def _():
        write(i - 1, nslot).wait()
        obuf[pl.ds(pl.multiple_of(slot * blk, blk), blk), :] = jnp.zeros((blk, LANES), F32)
        tail = pltpu.make_async_copy(obuf.at[pl.ds(pl.multiple_of(slot * blk, blk), blk)],
                                     ys_hbm.at[pl.ds(ys_hbm.shape[0] - blk, blk)], wsem.at[slot])
        tail.start()
        tail.wait()


def _moe(blk_tables, xs, w1, w3, w2):
    blk_expert, row0, nact, parity, nxt = blk_tables
    nblk = blk_expert.shape[0] - 1
    d, de = w1.shape[1], w1.shape[2]
    sub = d // LANES
    rows = MOE_ROWS
    anyspec = pl.BlockSpec(memory_space=pl.ANY)
    grid_spec = pltpu.PrefetchScalarGridSpec(
        num_scalar_prefetch=5,
        grid=(nblk + 1,),
        in_specs=[anyspec, anyspec, anyspec, anyspec],
        out_specs=anyspec,
        scratch_shapes=[
            pltpu.VMEM((2 * rows * sub, LANES), F32),
            pltpu.VMEM((2 * rows * sub, LANES), F32),
            pltpu.VMEM((2, d, de), F32),
            pltpu.VMEM((2, d, de), F32),
            pltpu.VMEM((2, de, d), F32),
            pltpu.VMEM((d, de), BF16),
            pltpu.VMEM((d, de), BF16),
            pltpu.VMEM((de, d), BF16),
            pltpu.SemaphoreType.DMA((2,)),
            pltpu.SemaphoreType.DMA((2,)),
            pltpu.SemaphoreType.DMA((2,)),
        ],
    )
    return pl.pallas_call(
        functools.partial(_moe_kernel, rows=rows, sub=sub),
        grid_spec=grid_spec,
        out_shape=jax.ShapeDtypeStruct(xs.shape, F32),
        compiler_params=pltpu.CompilerParams(dimension_semantics=("arbitrary",)),
        name="expert_mlp",
    )(blk_expert, row0, nact, parity, nxt, xs, w1, w3, w2)


def _combine_kernel(ss_ref, sl_ref, tb_ref, x1_ref, r_ref, tbv_ref, g2_ref, gpost_ref, ys_hbm, o_ref,
                    ybuf, sem, *, tile0):
    i = pl.program_id(0)
    nt = pl.num_programs(0)
    tm, d = x1_ref.shape
    sub = d // LANES
    nrow = TOP_K * tm
    slot = i % 2

    def fetch(tile, s):
        def seg(e, c):
            k = tile * N_EXPERTS + e
            _segment_copies(ys_hbm, ss_ref[k], ybuf, s * nrow + tb_ref[k], sl_ref[k], sub, sem.at[s])
            return c
        lax.fori_loop(0, N_EXPERTS, seg, 0)

    @pl.when(i == 0)
    def _():
        fetch(tile0, 0)

    @pl.when(i + 1 < nt)
    def _():
        fetch(tile0 + i + 1, 1 - slot)

    pltpu.make_async_copy(ys_hbm.at[pl.ds(0, nrow * sub)],
                          ybuf.at[pl.ds(pl.multiple_of(slot * nrow * sub, nrow * sub), nrow * sub)],
                          sem.at[slot]).wait()
    r = r_ref[...]
    lpos0, lpos1 = _local_positions(r, tbv_ref[0])
    p = lax.broadcasted_iota(jnp.int32, (tm, nrow), 1)
    q = (jnp.where(p == lpos0, r[:, 2:3], 0.0) + jnp.where(p == lpos1, r[:, 3:4], 0.0)).astype(BF16)
    ysort = _load_row_tiles(ybuf, slot * nrow * sub, nrow, sub).astype(BF16)
    y = jnp.dot(q, ysort, preferred_element_type=F32)
    o_ref[...] = x1_ref[...] + g2_ref[0] * _rms(y, gpost_ref[...])


def _combine(x1, ys, route, tables, mod3, mod_row0, gpost, *, t, seq, tm, tok_off, is_ctx):
    seg_start, seg_len, tile_base, tile_base_v = tables
    t_all, d = x1.shape
    per = seq // tm
    sub = d // LANES
    boff = tok_off // tm

    def row(i):
        return (i // per) if not is_ctx else 0

    grid_spec = pltpu.PrefetchScalarGridSpec(
        num_scalar_prefetch=3,
        grid=(t // tm,),
        in_specs=[
            pl.BlockSpec((tm, d), lambda i, *_: (boff + i, 0)),
            pl.BlockSpec((tm, LANES), lambda i, *_: (boff + i, 0)),
            pl.BlockSpec((1, 1, LANES), lambda i, *_: (boff + i, 0, 0)),
            pl.BlockSpec((1, 1, d), lambda i, *_: (mod_row0 + row(i), 0, 5)),
            pl.BlockSpec((1, d), lambda i, *_: (0, 0)),
            pl.BlockSpec(memory_space=pl.ANY),
        ],
        out_specs=pl.BlockSpec((tm, d), lambda i, *_: (i, 0)),
        scratch_shapes=[pltpu.VMEM((2 * TOP_K * tm * sub, LANES), F32), pltpu.SemaphoreType.DMA((2,))],
    )
    return pl.pallas_call(
        functools.partial(_combine_kernel, tile0=boff),
        grid_spec=grid_spec,
        out_shape=jax.ShapeDtypeStruct((t, d), F32),
        compiler_params=pltpu.CompilerParams(dimension_semantics=("arbitrary",)),
        name="combine_ctx" if is_ctx else "combine_lat",
    )(seg_start, seg_len, tile_base, x1, route, tile_base_v, mod3, gpost, ys)


def _routing_tables(route, tm, rows):
    t = route.shape[0]
    nt = t // tm
    n_asg = t * TOP_K
    ex = jnp.arange(N_EXPERTS, dtype=jnp.int32)
    e01 = route[:, 0:TOP_K].astype(jnp.int32)
    cnt_te = jnp.sum((e01.reshape(nt, tm * TOP_K, 1) == ex).astype(jnp.int32), axis=1)
    cnt_e = jnp.sum(cnt_te, axis=0)
    start_e = jnp.cumsum(cnt_e) - cnt_e
    seg_start = start_e[None, :] + jnp.cumsum(cnt_te, axis=0) - cnt_te
    tile_base = jnp.cumsum(cnt_te, axis=1) - cnt_te
    tile_base_v = jnp.pad(tile_base.astype(F32), ((0, 0), (0, LANES - N_EXPERTS))).reshape(nt, 1, LANES)

    nblk_e = (cnt_e + rows - 1) // rows
    blk_end = jnp.cumsum(nblk_e)
    blk_start = blk_end - nblk_e
    n_blocks = n_asg // rows + N_EXPERTS
    b = jnp.arange(n_blocks + 1, dtype=jnp.int32)
    be = jnp.minimum(jnp.sum((blk_end[None, :] <= b[:, None]).astype(jnp.int32), axis=1), N_EXPERTS - 1)
    first = jnp.sum(jnp.where(be[:, None] == ex[None, :], (start_e - blk_start * rows)[None, :], 0), axis=1)
    row0 = jnp.clip(first + b * rows, 0, n_asg)
    nact = blk_end[-1:].astype(jnp.int32)
    used = cnt_e > 0
    parity_e = (jnp.cumsum(used.astype(jnp.int32)) - 1) % 2
    later = (ex[None, :] > ex[:, None]) & used[None, :]
    nxt_e = jnp.min(jnp.where(later, ex[None, :], N_EXPERTS), axis=1)
    nxt_e = jnp.where(nxt_e == N_EXPERTS, -1, nxt_e)
    pick = lambda tab: jnp.sum(jnp.where(be[:, None] == ex[None, :], tab[None, :], 0), axis=1).astype(jnp.int32)
    seg = (seg_start.reshape(-1).astype(jnp.int32), cnt_te.reshape(-1).astype(jnp.int32),
           tile_base.reshape(-1).astype(jnp.int32), tile_base_v)
    blk = (be.astype(jnp.int32), row0.astype(jnp.int32), nact, pick(parity_e), pick(nxt_e))
    return seg, blk


def _rope_tables(n_tok):
    n_rows = n_tok // GRID_W
    rows = jnp.repeat(jnp.arange(n_rows), GRID_W).astype(F32)
    cols = jnp.tile(jnp.arange(GRID_W), n_rows).astype(F32)
    quarter = A_HEAD_DIM // 4
    inv = ROPE_BASE ** (-jnp.arange(quarter, dtype=F32) / quarter)
    ang = jnp.concatenate([rows[:, None] * inv, cols[:, None] * inv], axis=-1)
    cos, sin = jnp.cos(ang), jnp.sin(ang)
    cos_t = jnp.tile(jnp.concatenate([cos, cos], axis=-1), (1, LANES // A_HEAD_DIM))
    sin_t = jnp.tile(jnp.concatenate([-sin, sin], axis=-1), (1, LANES // A_HEAD_DIM))
    return cos_t, sin_t


def kernel(x_prompt, x_sample, c, cache_diff_k, cache_diff_v, cache_swa_k, cache_swa_v, c_ctx, w_ada, b_ada, g_pre_mix, g_post_mix, g_pre_ffn, g_post_ffn, w_in, lam_q1, lam_k1, lam_q2, lam_k2, g_diff_head, sink, w_proj_a, w_proj_b, w_out, w_router_group, b_router_group, w_router_expert, b_router_expert, w_e1, w_e3, w_e2):
    depth = w_in.shape[0]
    assert depth == 1
    l = 0
    bp, sp, d = x_prompt.shape
    bs, ss, _ = x_sample.shape
    lambda_init = 0.8 - 0.6 * math.exp(-0.3 * l)
    assert A_HEAD_DIM == B_HEAD_DIM and ss % GRID_W == 0 and bs + 1 <= MOD_ROWS

    c_all = jnp.concatenate([c_ctx[None, :], c, jnp.zeros((MOD_ROWS - 1 - bs, d), F32)], axis=0)
    mod = _modulation(c_all, w_ada[l], b_ada[l][None, :])
    mod3 = mod.reshape(MOD_ROWS, 1, 6 * d)

    w_in_b = w_in[l].astype(BF16)
    wpa = w_proj_a[l].astype(BF16)
    wpb = w_proj_b[l].astype(BF16)
    wo = w_out[l].astype(BF16)
    n_r = N_GROUPS + N_EXPERTS
    wr = jnp.concatenate([w_router_expert[l], w_router_group[l], jnp.zeros((d, LANES - n_r), F32)], axis=1)
    br = jnp.concatenate([b_router_expert[l], b_router_group[l], jnp.zeros((LANES - n_r,), F32)])[None, :]
    wr_hi = wr.astype(BF16)
    wr_lo = (wr - wr_hi.astype(F32)).astype(BF16)
    lam_p = jnp.stack([lam_q1[l], lam_k1[l], lam_q2[l], lam_k2[l]], axis=0)
    g_head = g_diff_head[l][None, :]
    sink_l = sink[l]
    cos_t, sin_t = _rope_tables(ss)

    xp2 = x_prompt.reshape(bp * sp, d)
    xs2 = x_sample.reshape(bs * ss, d)
    gpre = g_pre_mix[l][None, :]

    (qa_c, ka_c, va_c, qb_c, kb2_c, vb_c, sga_c, sgb_c, kaf, vaf, kbf, vbf) = _inproj(
        xp2, mod3, 0, gpre, cos_t, sin_t, w_in_b, seq=sp, tm=sp, is_ctx=True)
    r3 = lambda a, b_: a.reshape(b_, -1, a.shape[-1])
    oa_c = _diff_attention(lam_p, g_head, r3(qa_c, bp), [(r3(ka_c, bp), r3(va_c, bp))],
                           tq=sp, lambda_init=lambda_init)
    ob_c = _swa_attention(sink_l, r3(qb_c, bp),
                          [("rep", r3(kb2_c, bp), r3(vb_c, bp), sp, lambda i: 0)], tq=sp)

    (qa_s, ka_s, va_s, qb_s, kb2_s, vb_s, sga_s, sgb_s) = _inproj(
        xs2, mod3, 1, gpre, cos_t, sin_t, w_in_b, seq=ss, tm=512, is_ctx=False)
    past = cache_diff_k.shape[2]
    ck = cache_diff_k[:, l].reshape(bs, past, -1)
    cv = cache_diff_v[:, l].reshape(bs, past, -1)
    oa_s = _diff_attention_lat(lam_p, g_head, r3(qa_s, bs), ck, ka_s, cv, r3(va_s, bs),
                               tq=512, lambda_init=lambda_init)
    sk = cache_swa_k[:, l].reshape(bs, past, -1)
    sv = cache_swa_v[:, l].reshape(bs, past, -1)
    nqb = ss // SWA_Q
    kb2_3, vb_3 = r3(kb2_s, bs), r3(vb_s, bs)
    ob_s = _swa_attention_lat(sink_l, r3(qb_s, bs), sk, kb2_3, sv, vb_3)

    gpm = g_post_mix[l][None, :]
    gpf = g_pre_ffn[l][None, :]
    t_ctx, t_lat = bp * sp, bs * ss
    x1, h2t, route = _postmix(
        (xp2, oa_c.reshape(t_ctx, -1), ob_c.reshape(t_ctx, -1), sga_c, sgb_c),
        (xs2, oa_s.reshape(t_lat, -1), ob_s.reshape(t_lat, -1), sga_s, sgb_s),
        mod3, gpm, gpf, wpa, wpb, wo, wr_hi, wr_lo, br, lat_seq=ss, tm=512)

    tables, blk_tables = _routing_tables(route, MOE_TILE, MOE_ROWS)
    xs = _dispatch(h2t, route, tables, tm=MOE_TILE, rows=MOE_ROWS)
    ys = _moe(blk_tables, xs, w_e1[l], w_e3[l], w_e2[l])

    gpost = g_post_ffn[l][None, :]
    y_p = _combine(x1, ys, route, tables, mod3, 0, gpost, t=t_ctx, seq=sp, tm=MOE_TILE, tok_off=0, is_ctx=True)
    y_s = _combine(x1, ys, route, tables, mod3, 1, gpost, t=t_lat, seq=ss, tm=MOE_TILE, tok_off=t_ctx,
                   is_ctx=False)

    ha = A_HEADS
    return (y_p.reshape(bp, sp, d), y_s.reshape(bs, ss, d),
            kaf.reshape(bp, 1, sp, ha, 2, A_HEAD_DIM), vaf.reshape(bp, 1, sp, ha, A_V_DIM),
            kbf.reshape(bp, 1, sp, B_KV_HEADS, B_HEAD_DIM), vbf.reshape(bp, 1, sp, B_KV_HEADS, B_HEAD_DIM))
```

```python
import functools
import math

import jax
import jax.numpy as jnp
from jax import lax
from jax.experimental import pallas as pl
from jax.experimental.pallas import tpu as pltpu

F32 = jnp.float32
BF16 = jnp.bfloat16
HIGHEST = lax.Precision.HIGHEST

GRID_W = 64
ROPE_BASE = 10000.0
EPS = 1e-6
NEG_INF = -1e30
A_HEADS = 4
A_HEAD_DIM = 64
A_V_DIM = 2 * A_HEAD_DIM
B_HEADS = 8
B_KV_HEADS = 2
B_GROUP = B_HEADS // B_KV_HEADS
B_HEAD_DIM = 64
WINDOW = 128
N_GROUPS = 4
EXPERTS_PER_GROUP = 8
N_EXPERTS = N_GROUPS * EXPERTS_PER_GROUP
TOP_K = 2

LANES = 128
MOD_ROWS = 16
MOE_ROWS = 512
MOE_TILE = 512
SWA_Q = 128

_QA = 0
_KA = _QA + A_HEADS * 2 * A_HEAD_DIM
_VA = _KA + A_HEADS * 2 * A_HEAD_DIM
_QB = _VA + A_HEADS * A_V_DIM
_KB = _QB + B_HEADS * B_HEAD_DIM
_VB = _KB + B_KV_HEADS * B_HEAD_DIM
_GA = _VB + B_KV_HEADS * B_HEAD_DIM


def _rms(x, g):
    return x * lax.rsqrt(jnp.mean(x * x, axis=-1, keepdims=True) + EPS) * g


def _store_row_tiles(ref, base, val):
    sub = val.shape[1] // LANES
    for s in range(sub):
        ref[pl.ds(base + s, val.shape[0], stride=sub), :] = val[:, s * LANES:(s + 1) * LANES]


def _load_row_tiles(ref, base, n_rows, sub):
    return jnp.concatenate([ref[pl.ds(base + s, n_rows, stride=sub), :] for s in range(sub)], axis=1)


def _mod_kernel(c_ref, w_ref, b_ref, o_ref):
    c = c_ref[...]
    a = c * jax.nn.sigmoid(c)
    o_ref[...] = jnp.dot(a, w_ref[...], precision=HIGHEST, preferred_element_type=F32) + b_ref[...]


def _modulation(c_all, w_ada, b_ada):
    d, n = w_ada.shape
    tn = 512
    return pl.pallas_call(
        _mod_kernel,
        grid=(n // tn,),
        in_specs=[
            pl.BlockSpec((MOD_ROWS, d), lambda j: (0, 0)),
            pl.BlockSpec((d, tn), lambda j: (0, j)),
            pl.BlockSpec((1, tn), lambda j: (0, j)),
        ],
        out_specs=pl.BlockSpec((MOD_ROWS, tn), lambda j: (0, j)),
        out_shape=jax.ShapeDtypeStruct((MOD_ROWS, n), F32),
        name="modulation",
    )(c_all, w_ada, b_ada)


def _rope128(z, cos, sin_signed, first_half):
    rot = jnp.where(first_half, pltpu.roll(z, 96, 1), pltpu.roll(z, 32, 1))
    return z * cos + rot * sin_signed


def _inproj_kernel(x_ref, sh_ref, sc_ref, g_ref, cos_ref, sin_ref, w_ref, *outs, is_ctx):
    x = x_ref[...]
    h = _rms(x, g_ref[...]) * (1.0 + sc_ref[0]) + sh_ref[0]
    hb = h.astype(BF16)
    lane = lax.broadcasted_iota(jnp.int32, (1, LANES), 1)
    first_half = (lane % 64) < 32
    low = lane < 64

    def seg(lo, hi):
        return jnp.dot(hb, w_ref[:, lo:hi], preferred_element_type=F32)

    def rope(z):
        if is_ctx:
            return z
        cos = cos_ref[...]
        sin = sin_ref[...]
        parts = [_rope128(z[:, j:j + LANES], cos, sin, first_half) for j in range(0, z.shape[1], LANES)]
        return parts[0] if len(parts) == 1 else jnp.concatenate(parts, axis=1)

    if is_ctx:
        qa_o, ka_o, va_o, qb_o, kb2_o, vb_o, sga_o, sgb_o, kaf_o, vaf_o, kbf_o, vbf_o = outs
    else:
        qa_o, ka_o, va_o, qb_o, kb2_o, vb_o, sga_o, sgb_o = outs

    scale = A_HEAD_DIM ** -0.5
    qa_o[...] = (rope(seg(_QA, _KA)) * scale).astype(BF16)
    ka = rope(seg(_KA, _VA))
    if is_ctx:
        ka_o[...] = ka.astype(BF16)
    else:
        ka_o[0] = ka.T.astype(BF16)
    va = seg(_VA, _QB)
    va_o[...] = va.astype(BF16)
    qb_o[...] = (rope(seg(_QB, _KB)) * (B_HEAD_DIM ** -0.5)).astype(BF16)
    kb = rope(seg(_KB, _VB))
    kb_sw = pltpu.roll(kb, 64, 1)
    kb2_o[:, 0:LANES] = jnp.where(low, kb, kb_sw).astype(BF16)
    kb2_o[:, LANES:2 * LANES] = jnp.where(low, kb_sw, kb).astype(BF16)
    vb = seg(_VB, _GA)
    if is_ctx:
        vb_o[...] = vb.astype(BF16)
    else:
        vb_sw = pltpu.roll(vb, 64, 1)
        vb_o[:, 0:LANES] = jnp.where(low, vb, vb_sw).astype(BF16)
        vb_o[:, LANES:2 * LANES] = jnp.where(low, vb_sw, vb).astype(BF16)
    d = x.shape[1]
    sga_o[...] = jax.nn.sigmoid(seg(_GA, _GA + d)).astype(BF16)
    sgb_o[...] = jax.nn.sigmoid(seg(_GA + d, _GA + 2 * d)).astype(BF16)
    if is_ctx:
        kaf_o[...] = ka
        vaf_o[...] = va
        kbf_o[...] = kb
        vbf_o[...] = vb


def _inproj(x2, mod3, mod_row0, g_pre, cos_t, sin_t, w_in_b, *, seq, tm, is_ctx):
    t, d = x2.shape
    per = seq // tm
    n_in = w_in_b.shape[1]
    wa = A_HEADS * 2 * A_HEAD_DIM
    wkb = B_KV_HEADS * B_HEAD_DIM

    def row(i):
        return (i // per) if not is_ctx else 0

    tok = lambda w: pl.BlockSpec((tm, w), lambda i: (i, 0))
    out_shape = [
        jax.ShapeDtypeStruct((t, wa), BF16), jax.ShapeDtypeStruct((t, wa), BF16),
        jax.ShapeDtypeStruct((t, wa), BF16), jax.ShapeDtypeStruct((t, wa), BF16),
        jax.ShapeDtypeStruct((t, 2 * wkb), BF16), jax.ShapeDtypeStruct((t, wkb), BF16),
        jax.ShapeDtypeStruct((t, d), BF16), jax.ShapeDtypeStruct((t, d), BF16),
    ]
    out_specs = [tok(wa), tok(wa), tok(wa), tok(wa), tok(2 * wkb), tok(wkb), tok(d), tok(d)]
    if not is_ctx:
        out_shape[5] = jax.ShapeDtypeStruct((t, 2 * wkb), BF16)
        out_specs[5] = tok(2 * wkb)
        out_shape[1] = jax.ShapeDtypeStruct((t // seq, wa, seq), BF16)
        out_specs[1] = pl.BlockSpec((1, wa, tm), lambda i: (i // per, 0, i % per))
    if is_ctx:
        out_shape += [jax.ShapeDtypeStruct((t, wa), F32), jax.ShapeDtypeStruct((t, wa), F32),
                      jax.ShapeDtypeStruct((t, wkb), F32), jax.ShapeDtypeStruct((t, wkb), F32)]
        out_specs += [tok(wa), tok(wa), tok(wkb), tok(wkb)]
    return pl.pallas_call(
        functools.partial(_inproj_kernel, is_ctx=is_ctx),
        grid=(t // tm,),
        in_specs=[
            pl.BlockSpec((tm, d), lambda i: (i, 0)),
            pl.BlockSpec((1, 1, d), lambda i: (mod_row0 + row(i), 0, 0)),
            pl.BlockSpec((1, 1, d), lambda i: (mod_row0 + row(i), 0, 1)),
            pl.BlockSpec((1, d), lambda i: (0, 0)),
            pl.BlockSpec((tm, LANES), lambda i: (i % per, 0)),
            pl.BlockSpec((tm, LANES), lambda i: (i % per, 0)),
            pl.BlockSpec((d, n_in), lambda i: (0, 0)),
        ],
        out_specs=out_specs,
        out_shape=out_shape,
        compiler_params=pltpu.CompilerParams(dimension_semantics=("arbitrary",)),
        name="inproj_ctx" if is_ctx else "inproj_lat",
    )(x2, mod3, mod3, g_pre, cos_t, sin_t, w_in_b)


def _nt(a, b):
    return lax.dot_general(a, b, (((1,), (1,)), ((), ())), preferred_element_type=F32)


def _diff_kernel(lam_ref, g_ref, q_ref, k_ref, v_ref, o_ref, *, lambda_init):
    lp = lam_ref[...]
    lam = (jnp.exp(jnp.sum(lp[0:1] * lp[1:2], axis=-1, keepdims=True))
           - jnp.exp(jnp.sum(lp[2:3] * lp[3:4], axis=-1, keepdims=True)) + lambda_init)
    tq = q_ref.shape[1]
    lane = lax.broadcasted_iota(jnp.int32, (1, LANES), 1)
    for h in range(A_HEADS):
        cols = slice(h * LANES, (h + 1) * LANES)
        q = q_ref[0, :, cols]
        q2 = jnp.concatenate([q * (lane < 64).astype(BF16), q * (lane >= 64).astype(BF16)], axis=0)
        s = _nt(q2, k_ref[0, :, cols])
        mx = jnp.max(s, axis=-1, keepdims=True)
        v = v_ref[0, :, cols]
        v_ext = jnp.concatenate([v, jnp.ones_like(v)], axis=1)
        acc = jnp.dot(jnp.exp(s - mx).astype(BF16), v_ext, preferred_element_type=F32)
        on = acc[:, 0:LANES] / acc[:, LANES:2 * LANES]
        o = on[0:tq] - lam * on[tq:2 * tq]
        o_ref[0, :, cols] = (_rms(o, g_ref[...]) * (1.0 - lambda_init)).astype(BF16)


def _diff_attention(lam_p, g_head, q, k, v, *, lambda_init):
    b, s, w = q.shape
    seq = pl.BlockSpec((1, s, w), lambda bi: (bi, 0, 0))
    return pl.pallas_call(
        functools.partial(_diff_kernel, lambda_init=lambda_init),
        grid=(b,),
        in_specs=[pl.BlockSpec((4, A_HEAD_DIM), lambda bi: (0, 0)),
                  pl.BlockSpec((1, A_V_DIM), lambda bi: (0, 0)), seq, seq, seq],
        out_specs=seq,
        out_shape=jax.ShapeDtypeStruct((b, s, w), BF16),
        compiler_params=pltpu.CompilerParams(dimension_semantics=("arbitrary",)),
        name="diff_attn_ctx",
    )(lam_p, g_head, q, k, v)


def _diff_lat_kernel(lam_ref, g_ref, q_ref, kc_ref, kt_ref, vc_ref, v_ref, o_ref, s_a, m_a, s_b, m_b, *,
                     lambda_init):
    t = pl.program_id(0)
    tq = q_ref.shape[1]
    nkc = kc_ref.shape[1]
    nkn = kt_ref.shape[2]

    @pl.when(t == 0)
    def _():
        s_b[...] = jnp.zeros_like(s_b)
        m_b[...] = jnp.zeros_like(m_b)

    def body(s_w, m_w, s_r, m_r):
        lp = lam_ref[...]
        lam = (jnp.exp(jnp.sum(lp[0:1] * lp[1:2], axis=-1, keepdims=True))
               - jnp.exp(jnp.sum(lp[2:3] * lp[3:4], axis=-1, keepdims=True)) + lambda_init)
        lane = lax.broadcasted_iota(jnp.int32, (1, LANES), 1)

        q = q_ref[0]
        q2 = jnp.concatenate([q * (lane < 64).astype(BF16), q * (lane >= 64).astype(BF16)], axis=0)
        sc = _nt(q2, kc_ref[0].astype(BF16))
        sn = jnp.dot(q2, kt_ref[0], preferred_element_type=F32)
        mx = jnp.maximum(jnp.max(sc, axis=-1, keepdims=True), jnp.max(sn, axis=-1, keepdims=True))
        s_w[:, 0:nkc] = sc
        s_w[:, nkc:nkc + nkn] = sn
        m_w[...] = jnp.broadcast_to(mx, (2 * tq, LANES))

        mp = m_r[...]
        v_all = jnp.concatenate([vc_ref[0].astype(BF16), v_ref[0]], axis=0)
        v_ext = jnp.concatenate([v_all, jnp.ones_like(v_all)], axis=1)
        p = jnp.concatenate(
            [jnp.exp(s_r[:, c:c + LANES] - mp).astype(BF16) for c in range(0, nkc + nkn, LANES)], axis=1)
        acc = jnp.dot(p, v_ext, preferred_element_type=F32)
        on = acc[:, 0:LANES] / acc[:, LANES:2 * LANES]
        o = on[0:tq] - lam * on[tq:2 * tq]
        o_ref[0] = (_rms(o, g_ref[...]) * (1.0 - lambda_init)).astype(BF16)

    @pl.when(t % 2 == 0)
    def _():
        body(s_a, m_a, s_b, m_b)

    @pl.when(t % 2 == 1)
    def _():
        body(s_b, m_b, s_a, m_a)


def _diff_attention_lat(lam_p, g_head, q, kc, kt, vc, v, *, tq, lambda_init):
    b, s, w = q.shape
    past = kc.shape[1]
    nq = s // tq
    n_units = b * A_HEADS * nq
    last = n_units - 1

    def unit(u):
        return u // (A_HEADS * nq), (u // nq) % A_HEADS, u % nq

    def cur(t):
        return unit(jnp.minimum(t, last))

    def prev(t):
        return unit(jnp.maximum(t - 1, 0))

    return pl.pallas_call(
        functools.partial(_diff_lat_kernel, lambda_init=lambda_init),
        grid=(n_units + 1,),
        in_specs=[
            pl.BlockSpec((4, A_HEAD_DIM), lambda t: (0, 0)),
            pl.BlockSpec((1, A_V_DIM), lambda t: (0, 0)),
            pl.BlockSpec((1, tq, LANES), lambda t: (cur(t)[0], cur(t)[2], cur(t)[1])),
            pl.BlockSpec((1, past, LANES), lambda t: (cur(t)[0], 0, cur(t)[1])),
            pl.BlockSpec((1, LANES, s), lambda t: (cur(t)[0], cur(t)[1], 0)),
            pl.BlockSpec((1, past, LANES), lambda t: (prev(t)[0], 0, prev(t)[1])),
            pl.BlockSpec((1, s, LANES), lambda t: (prev(t)[0], 0, prev(t)[1])),
        ],
        out_specs=pl.BlockSpec((1, tq, LANES), lambda t: (prev(t)[0], prev(t)[2], prev(t)[1])),
        out_shape=jax.ShapeDtypeStruct((b, s, w), BF16),
        scratch_shapes=[pltpu.VMEM((2 * tq, past + s), F32), pltpu.VMEM((2 * tq, LANES), F32),
                        pltpu.VMEM((2 * tq, past + s), F32), pltpu.VMEM((2 * tq, LANES), F32)],
        compiler_params=pltpu.CompilerParams(dimension_semantics=("arbitrary",),
                                             vmem_limit_bytes=56 * 1024 * 1024),
        name="diff_attn_lat",
    )(lam_p, g_head, q, kc, kt, vc, v)


def _swa_kernel(sink_ref, q_ref, k_ref, v_ref, o_ref):
    tq = q_ref.shape[1]
    rows = B_GROUP * tq
    lane = lax.broadcasted_iota(jnp.int32, (1, LANES), 1)
    low = lane < 64
    lane2 = lax.broadcasted_iota(jnp.int32, (1, 2 * LANES), 1)
    head_masks = [((lane2 // 64) == g).astype(BF16) for g in range(B_GROUP)]
    gw = B_GROUP * B_HEAD_DIM
    v = v_ref[0]
    v_ext = jnp.concatenate([v, jnp.ones_like(v)], axis=1)
    finishers = []
    for n in range(B_KV_HEADS):
        q = q_ref[0, :, n * gw:(n + 1) * gw]
        qs = jnp.concatenate([q * hm for hm in head_masks], axis=0)
        k2 = k_ref[0, :, n * LANES:(n + 1) * LANES]
        s = _nt(qs, jnp.concatenate([k2, k2], axis=1))
        sink = jnp.concatenate(
            [jnp.full((tq, 1), sink_ref[n * B_GROUP + g], F32) for g in range(B_GROUP)], axis=0)
        top = jnp.max(s, keepdims=True)
        for g in range(B_GROUP):
            top = jnp.maximum(top, sink_ref[n * B_GROUP + g])

        def finish(mp, n=n, s=s, sink=sink):
            acc = jnp.dot(jnp.exp(s - mp).astype(BF16), v_ext, preferred_element_type=F32)
            den = acc[:, LANES:2 * LANES] + jnp.exp(sink - mp)
            o = acc[:, 0:LANES] / den
            osw = pltpu.roll(o, 64, 1)
            for j in range(B_GROUP // 2):
                ra = slice((2 * j) * tq, (2 * j + 1) * tq)
                rb = slice((2 * j + 1) * tq, (2 * j + 2) * tq)
                pair = jnp.where(low, o[ra], osw[rb]) if n == 0 else jnp.where(low, osw[ra], o[rb])
                o_ref[0, :, n * gw + j * LANES:n * gw + (j + 1) * LANES] = pair.astype(BF16)
            return jnp.min(den)

        finishers.append((finish, s, sink, finish(top)))

    smallest = functools.reduce(jnp.minimum, [f[3] for f in finishers])

    @pl.when(jnp.logical_not(smallest >= 1e-30))
    def _():
        for finish, s, sink, _ in finishers:
            finish(jnp.maximum(jnp.max(s, axis=-1, keepdims=True), sink))


def _swa_attention(sink, q, k2, v):
    b, s, w = q.shape
    seq = lambda a: pl.BlockSpec((1, s, a.shape[2]), lambda bi: (bi, 0, 0))
    return pl.pallas_call(
        _swa_kernel,
        grid=(b,),
        in_specs=[pl.BlockSpec(memory_space=pltpu.SMEM), seq(q), seq(k2), seq(v)],
        out_specs=seq(q),
        out_shape=jax.ShapeDtypeStruct((b, s, w), BF16),
        compiler_params=pltpu.CompilerParams(dimension_semantics=("arbitrary",)),
        name="swa_attn_ctx",
    )(sink, q, k2, v)


def _swa_lat_kernel(sink_ref, q_ref, kc_ref, kl_ref, km_ref, kr_ref, vc_ref, vl_ref, vm_ref, vr_ref, o_ref,
                    s_a, m_a, s_b, m_b, *, nqb):
    t = pl.program_id(0)
    n_units = pl.num_programs(0) - 1
    tq = q_ref.shape[1]
    gw = B_GROUP * B_HEAD_DIM
    i_cur = jnp.minimum(t, n_units - 1) % nqb
    nkc = kc_ref.shape[1]
    rows = B_GROUP * tq

    @pl.when(t == 0)
    def _():
        s_b[...] = jnp.zeros_like(s_b)
        m_b[...] = jnp.zeros_like(m_b)

    def sink_col(n):
        return jnp.concatenate(
            [jnp.full((tq, 1), sink_ref[n * B_GROUP + g], F32) for g in range(B_GROUP)], axis=0)

    def body(s_w, m_w, s_r, m_r):
        lane = lax.broadcasted_iota(jnp.int32, (1, LANES), 1)
        low = lane < 64
        lane2 = lax.broadcasted_iota(jnp.int32, (1, 2 * LANES), 1)
        head_masks = [((lane2 // 64) == g).astype(BF16) for g in range(B_GROUP)]
        qi = lax.broadcasted_iota(jnp.int32, (rows, SWA_Q), 0) & (tq - 1)
        kj = lax.broadcasted_iota(jnp.int32, (rows, SWA_Q), 1)
        far = 2 * SWA_Q
        left_ok = kj >= qi + jnp.where(i_cur > 0, 0, far)
        right_ok = kj <= qi - jnp.where(i_cur < nqb - 1, 0, far)

        kc = kc_ref[0]
        kc_sw = pltpu.roll(kc, 64, 1)
        for n in range(B_KV_HEADS):
            q = q_ref[0, :, n * gw:(n + 1) * gw]
            qs = jnp.concatenate([q * hm for hm in head_masks], axis=0)
            kc2 = (jnp.where(low, kc, kc_sw) if n == 0 else jnp.where(low, kc_sw, kc)).astype(BF16)
            ks = [kc2] + [r[0, :, n * LANES:(n + 1) * LANES] for r in (kl_ref, km_ref, kr_ref)]
            k_all = jnp.concatenate([jnp.concatenate([k, k], axis=1) for k in ks], axis=0)
            s = _nt(qs, k_all)
            chunks = [s[:, c:c + LANES] for c in range(0, s.shape[1], LANES)]
            il = nkc // LANES
            chunks[il] = jnp.where(left_ok, chunks[il], NEG_INF)
            chunks[il + 2] = jnp.where(right_ok, chunks[il + 2], NEG_INF)
            top = jnp.max(functools.reduce(jnp.maximum, chunks), keepdims=True)
            for g in range(B_GROUP):
                top = jnp.maximum(top, sink_ref[n * B_GROUP + g])
            for c, ch in enumerate(chunks):
                s_w[n * rows:(n + 1) * rows, c * LANES:(c + 1) * LANES] = ch
            m_w[n * rows:(n + 1) * rows, :] = jnp.broadcast_to(top, (rows, LANES))

        vc = vc_ref[0]
        vc_sw = pltpu.roll(vc, 64, 1)
        nk = nkc + 3 * tq
        finishers = []
        for n in range(B_KV_HEADS):
            vc2 =(jnp.where(low, vc, vc_sw) if n == 0 else jnp.where(low, vc_sw, vc)).astype(BF16)
            v_all = jnp.concatenate(
                [vc2] + [r[0, :, n * LANES:(n + 1) * LANES] for r in (vl_ref, vm_ref, vr_ref)], axis=0)
            v_ext = jnp.concatenate([v_all, jnp.ones_like(v_all)], axis=1)
            sink = sink_col(n)

            def finish(mp, n=n, v_ext=v_ext, sink=sink):
                p = jnp.concatenate([jnp.exp(s_r[n * rows:(n + 1) * rows, c:c + LANES] - mp).astype(BF16)
                                     for c in range(0, nk, LANES)], axis=1)
                acc = jnp.dot(p, v_ext, preferred_element_type=F32)
                den = acc[:, LANES:2 * LANES] + jnp.exp(sink - mp)
                o = acc[:, 0:LANES] / den
                for j in range(B_GROUP // 2):
                    pair = jnp.where(low, o[(2 * j) * tq:(2 * j + 1) * tq], o[(2 * j + 1) * tq:(2 * j + 2) * tq])
                    o_ref[0, :, n * gw + j * LANES:n * gw + (j + 1) * LANES] = pair.astype(BF16)
                return jnp.min(den)

            finishers.append((n, finish, sink, finish(m_r[n * rows:(n + 1) * rows, :])))

        smallest = functools.reduce(jnp.minimum, [f[3] for f in finishers])

        @pl.when(jnp.logical_not(smallest >= 1e-30))
        def _():
            for n, finish, sink, _ in finishers:
                row_max = functools.reduce(
                    jnp.maximum, [s_r[n * rows:(n + 1) * rows, c:c + LANES] for c in range(0, nk, LANES)])
                mx = jnp.maximum(jnp.max(row_max, axis=-1, keepdims=True), sink)
                finish(jnp.broadcast_to(mx, (rows, LANES)))

    @pl.when(t % 2 == 0)
    def _():
        body(s_a, m_a, s_b, m_b)

    @pl.when(t % 2 == 1)
    def _():
        body(s_b, m_b, s_a, m_a)


def _swa_attention_lat(sink, q, kc, k2, vc, v):
    b, s, w = q.shape
    past = kc.shape[1]
    tq = SWA_Q
    nqb = s // tq
    n_units = b * nqb
    last = n_units - 1
    nk = past + 3 * tq

    def cur(t):
        u = jnp.minimum(t, last)
        return u // nqb, u % nqb

    def prev(t):
        u = jnp.maximum(t - 1, 0)
        return u // nqb, u % nqb

    lo = lambda i: jnp.maximum(i - 1, 0)
    hi = lambda i: jnp.minimum(i + 1, nqb - 1)
    kspec = lambda f: pl.BlockSpec((1, tq, 2 * LANES), lambda t: (cur(t)[0], f(cur(t)[1]), 0))
    vspec = lambda f: pl.BlockSpec((1, tq, 2 * LANES), lambda t: (prev(t)[0], f(prev(t)[1]), 0))
    same = lambda i: i
    rows = B_KV_HEADS * B_GROUP * tq
    return pl.pallas_call(
        functools.partial(_swa_lat_kernel, nqb=nqb),
        grid=(n_units + 1,),
        in_specs=[
            pl.BlockSpec(memory_space=pltpu.SMEM),
            pl.BlockSpec((1, tq, w), lambda t: (cur(t)[0], cur(t)[1], 0)),
            pl.BlockSpec((1, past, LANES), lambda t: (cur(t)[0], 0, 0)),
            kspec(lo), kspec(same), kspec(hi),
            pl.BlockSpec((1, past, LANES), lambda t: (prev(t)[0], 0, 0)),
            vspec(lo), vspec(same), vspec(hi),
        ],
        out_specs=pl.BlockSpec((1, tq, w), lambda t: (prev(t)[0], prev(t)[1], 0)),
        out_shape=jax.ShapeDtypeStruct((b, s, w), BF16),
        scratch_shapes=[pltpu.VMEM((rows, nk), F32), pltpu.VMEM((rows, LANES), F32),
                        pltpu.VMEM((rows, nk), F32), pltpu.VMEM((rows, LANES), F32)],
        compiler_params=pltpu.CompilerParams(dimension_semantics=("arbitrary",)),
        name="swa_attn_lat",
    )(sink, q, kc, k2, k2, k2, vc, v, v, v)


def _postmix_kernel(*refs, n_ctx_tiles):
    (xc, xl, oac, oal, obc, obl, sgac, sgal, sgbc, sgbl, g1_ref, sh2_ref, sc2_ref, gpm_ref, gpf_ref,
     wpa_ref, wpb_ref, wo_ref, wr_ref, wrl_ref, br_ref, x1_o, h2_o, route_o) = refs
    is_ctx = pl.program_id(0) < n_ctx_tiles
    pick = lambda a, b: jnp.where(is_ctx, a[...], b[...])
    pa = jnp.dot(pick(oac, oal), wpa_ref[...], preferred_element_type=F32)
    pb = jnp.dot(pick(obc, obl), wpb_ref[...], preferred_element_type=F32)
    mix = pick(sgac, sgal).astype(F32) * pa + pick(sgbc, sgbl).astype(F32) * pb
    m2 = jnp.dot(mix.astype(BF16), wo_ref[...], preferred_element_type=F32)
    x1 = pick(xc, xl) + g1_ref[0] * _rms(m2, gpm_ref[...])
    x1_o[...] = x1
    h2 = _rms(x1, gpf_ref[...]) * (1.0 + sc2_ref[0]) + sh2_ref[0]
    h2_o[...] = h2.astype(BF16)

    h_hi = h2.astype(BF16)
    h_lo = (h2 - h_hi.astype(F32)).astype(BF16)
    logits = (jnp.dot(h_hi, wr_ref[...], preferred_element_type=F32)
              + jnp.dot(h_lo, wr_ref[...], preferred_element_type=F32)
              + jnp.dot(h_hi, wrl_ref[...], preferred_element_type=F32) + br_ref[...])
    tm = logits.shape[0]
    lt = logits.T
    row = lax.broadcasted_iota(jnp.int32, (EXPERTS_PER_GROUP, tm), 0).astype(F32)
    none = float(EXPERTS_PER_GROUP)
    lg = jnp.where(row < N_GROUPS, lt[N_EXPERTS:N_EXPERTS + EXPERTS_PER_GROUP], -jnp.inf)
    mg = jnp.max(lg, axis=0, keepdims=True)
    g_sel = jnp.min(jnp.where(lg == mg, row, none), axis=0, keepdims=True)
    g_w = 1.0 / jnp.sum(jnp.exp(lg - mg), axis=0, keepdims=True)
    le = lt[0:EXPERTS_PER_GROUP]
    for g in range(1, N_GROUPS):
        le = jnp.where(g_sel == g, lt[g * EXPERTS_PER_GROUP:(g + 1) * EXPERTS_PER_GROUP], le)
    v0 = jnp.max(le, axis=0, keepdims=True)
    i0 = jnp.min(jnp.where(le == v0, row, none), axis=0, keepdims=True)
    le1 = jnp.where(row == i0, -jnp.inf, le)
    v1 = jnp.max(le1, axis=0, keepdims=True)
    i1 = jnp.min(jnp.where(le1 == v1, row, none), axis=0, keepdims=True)
    e = jnp.exp(v1 - v0)
    w0 = g_w / (1.0 + e)
    w1 = g_w * e / (1.0 + e)
    e0 = g_sel * EXPERTS_PER_GROUP + i0
    e1 = g_sel * EXPERTS_PER_GROUP + i1
    rt = jnp.where(row == 0, e0, jnp.where(row == 1, e1, jnp.where(row == 2, w0, jnp.where(row == 3, w1, 0.0))))
    rt = jnp.concatenate([rt, jnp.zeros((LANES - EXPERTS_PER_GROUP, tm), F32)], axis=0)
    route_o[...] = rt.T


def _postmix(ctx_in, lat_in, mod3, gpm, gpf, wpa, wpb, wo, wr, wrl, br, *, lat_seq, tm):
    t_ctx, d = ctx_in[0].shape
    t_lat = lat_in[0].shape[0]
    assert t_ctx % tm == 0 and lat_seq % tm == 0
    nc = t_ctx // tm
    nl = t_lat // tm
    per = lat_seq // tm
    sub = d // LANES
    t_all = t_ctx + t_lat

    mod_row = lambda i: jnp.where(i < nc, 0, 1 + jnp.maximum(i - nc, 0) // per)
    full = lambda a: pl.BlockSpec(a.shape, lambda i: (0,) * a.ndim)
    modspec = lambda c: pl.BlockSpec((1, 1, d), lambda i: (mod_row(i), 0, c))
    in_specs, args = [], []
    for a_c, a_l in zip(ctx_in, lat_in):
        w = a_c.shape[1]
        in_specs += [pl.BlockSpec((tm, w), lambda i: (jnp.minimum(i, nc - 1), 0)),
                     pl.BlockSpec((tm, w), lambda i: (jnp.maximum(i - nc, 0), 0))]
        args += [a_c, a_l]
    in_specs += [modspec(2), modspec(3), modspec(4), full(gpm), full(gpf),
                 full(wpa), full(wpb), full(wo), full(wr), full(wrl), full(br)]
    args += [mod3, mod3, mod3, gpm, gpf, wpa, wpb, wo, wr, wrl, br]
    return pl.pallas_call(
        functools.partial(_postmix_kernel, n_ctx_tiles=nc),
        grid=(nc + nl,),
        in_specs=in_specs,
        out_specs=[pl.BlockSpec((tm, d), lambda i: (i, 0)),
                   pl.BlockSpec((tm, d), lambda i: (i, 0)),
                   pl.BlockSpec((tm, LANES), lambda i: (i, 0))],
        out_shape=[jax.ShapeDtypeStruct((t_all, d), F32),
                   jax.ShapeDtypeStruct((t_all, d), BF16),
                   jax.ShapeDtypeStruct((t_all, LANES), F32)],
        compiler_params=pltpu.CompilerParams(dimension_semantics=("arbitrary",)),
        name="postmix",
    )(*args)


def _segment_copies(src, src_row, dst, dst_row, n, sub, sem):
    @pl.when(n > 0)
    def _():
        pltpu.make_async_copy(src.at[pl.ds(pl.multiple_of(src_row * sub, sub), n * sub)],
                              dst.at[pl.ds(pl.multiple_of(dst_row * sub, sub), n * sub)], sem).start()


def _local_positions(route, tile_base):
    tm = route.shape[0]
    lane = lax.broadcasted_iota(jnp.int32, (tm, LANES), 1).astype(F32)
    is0 = lane == route[:, 0:1]
    is1 = lane == route[:, 1:2]
    earlier = (lax.broadcasted_iota(jnp.int32, (tm, tm), 1)
               < lax.broadcasted_iota(jnp.int32, (tm, tm), 0)).astype(BF16)
    pre0 = jnp.dot(earlier, is0.astype(BF16), preferred_element_type=F32)
    pre1 = jnp.dot(earlier, is1.astype(BF16), preferred_element_type=F32)
    cnt0 = jnp.sum(is0.astype(F32), axis=0, keepdims=True)
    lpos0 = jnp.sum(jnp.where(is0, tile_base + pre0, 0.0), axis=-1, keepdims=True)
    lpos1 = jnp.sum(jnp.where(is1, tile_base + cnt0 + pre1, 0.0), axis=-1, keepdims=True)
    return lpos0.astype(jnp.int32), lpos1.astype(jnp.int32)


def _dispatch_kernel(ss_ref, sl_ref, tb_ref, h_ref, r_ref, tbv_ref, xs_hbm, pbuf, zbuf, sem, zsem, *, n_asg, rows):
    i = pl.program_id(0)
    nt = pl.num_programs(0)
    tm, d = h_ref.shape
    sub = d // LANES
    nrow = TOP_K * tm
    slot = i % 2

    def wait_slot(s):
        pltpu.make_async_copy(pbuf.at[pl.ds(pl.multiple_of(s * nrow * sub, nrow * sub), nrow * sub)],
                              xs_hbm.at[pl.ds(0, nrow * sub)], sem.at[s]).wait()

    def slack_copy():
        return pltpu.make_async_copy(zbuf, xs_hbm.at[pl.ds(n_asg * sub, rows * sub)], zsem.at[0])

    @pl.when(i == 0)
    def _():
        zbuf[...] = jnp.zeros_like(zbuf)
        slack_copy().start()

    lpos0, lpos1 = _local_positions(r_ref[...], tbv_ref[0])
    p = lax.broadcasted_iota(jnp.int32, (tm, nrow), 1)
    sel = ((p == lpos0) | (p == lpos1)).astype(BF16)
    xp = lax.dot_general(sel, h_ref[...].astype(BF16), (((0,), (0,)), ((), ())),
                         preferred_element_type=F32)

    @pl.when(i >= 2)
    def _():
        wait_slot(slot)

    _store_row_tiles(pbuf, slot * nrow * sub, xp)

    def seg(e, c):
        k = i * N_EXPERTS + e
        _segment_copies(pbuf, slot * nrow + tb_ref[k], xs_hbm, ss_ref[k], sl_ref[k], sub, sem.at[slot])
        return c
    lax.fori_loop(0, N_EXPERTS, seg, 0)

    @pl.when(i == nt - 1)
    def _():
        wait_slot(slot)

        @pl.when(nt >= 2)
        def _():
            wait_slot(1 - slot)
        slack_copy().wait()


def _dispatch(h2, route, tables, *, tm, rows):
    seg_start, seg_len, tile_base, tile_base_v = tables
    t, d = h2.shape
    sub = d // LANES
    n_asg = t * TOP_K
    grid_spec = pltpu.PrefetchScalarGridSpec(
        num_scalar_prefetch=3,
        grid=(t // tm,),
        in_specs=[
            pl.BlockSpec((tm, d), lambda i, *_: (i, 0)),
            pl.BlockSpec((tm, LANES), lambda i, *_: (i, 0)),
            pl.BlockSpec((1, 1, LANES), lambda i, *_: (i, 0, 0)),
        ],
        out_specs=pl.BlockSpec(memory_space=pl.ANY),
        scratch_shapes=[pltpu.VMEM((2 * TOP_K * tm * sub, LANES), F32), pltpu.VMEM((rows * sub, LANES), F32),
                        pltpu.SemaphoreType.DMA((2,)), pltpu.SemaphoreType.DMA((1,))],
    )
    return pl.pallas_call(
        functools.partial(_dispatch_kernel, n_asg=n_asg, rows=rows),
        grid_spec=grid_spec,
        out_shape=jax.ShapeDtypeStruct(((n_asg + rows) * sub, LANES), F32),
        compiler_params=pltpu.CompilerParams(dimension_semantics=("arbitrary",)),
        name="dispatch",
    )(seg_start, seg_len, tile_base, h2, route, tile_base_v)


def _moe_kernel(be_ref, row0_ref, nact_ref, par_ref, nxt_ref, xs_hbm, w1_hbm, w3_hbm, w2_hbm, ys_hbm,
                xbuf, obuf, wf1, wf3, wf2, w1b, w3b, w2b, rsem, wsem, gsem, *, rows, sub):
    i = pl.program_id(0)
    nact = nact_ref[0]
    slot = i % 2
    nslot = 1 - slot
    blk = rows * sub

    def weight_copies(e, s):
        return [pltpu.make_async_copy(w_hbm.at[e], wf.at[s], gsem.at[s])
                for w_hbm, wf in ((w1_hbm, wf1), (w3_hbm, wf3), (w2_hbm, wf2))]

    def read(j, s):
        return pltpu.make_async_copy(xs_hbm.at[pl.ds(pl.multiple_of(row0_ref[j] * sub, sub), blk)],
                                     xbuf.at[pl.ds(pl.multiple_of(s * blk, blk), blk)], rsem.at[s])

    def write(j, s):
        return pltpu.make_async_copy(obuf.at[pl.ds(pl.multiple_of(s * blk, blk), blk)],
                                     ys_hbm.at[pl.ds(pl.multiple_of(row0_ref[j] * sub, sub), blk)], wsem.at[s])

    @pl.when(i == 0)
    def _():
        read(0, 0).start()
        for c in weight_copies(be_ref[0], 0):
            c.start()

    @pl.when(i < nact)
    def _():
        @pl.when(i + 1 < nact)
        def _():
            read(i + 1, nslot).start()

        changed = jnp.logical_or(i == 0, be_ref[i] != be_ref[jnp.maximum(i - 1, 0)])

        @pl.when(changed)
        def _():
            s = par_ref[i]
            for c in weight_copies(be_ref[i], s):
                c.wait()
            w1b[...] = wf1[s].astype(BF16)
            w3b[...] = wf3[s].astype(BF16)
            w2b[...] = wf2[s].astype(BF16)

            @pl.when(nxt_ref[i] >= 0)
            def _():
                for c in weight_copies(nxt_ref[i], 1 - s):
                    c.start()

        read(i, slot).wait()
        x = _load_row_tiles(xbuf, slot * blk, rows, sub).astype(BF16)
        a = jnp.dot(x, w1b[...], preferred_element_type=F32)
        b = jnp.dot(x, w3b[...], preferred_element_type=F32)
        hmid = (a * jax.nn.sigmoid(a) * b).astype(BF16)
        y = jnp.dot(hmid, w2b[...], preferred_element_type=F32)
        _store_row_tiles(obuf, slot * blk, y)

        @pl.when(i >= 1)
        def _():
            write(i - 1, nslot).wait()
        write(i, slot).start()

    @pl.when(i == nact)
    def _():
        write(i - 1, nslot).wait()
        obuf[pl.ds(pl.multiple_of(slot * blk, blk), blk), :] = jnp.zeros((blk, LANES), F32)
        tail = pltpu.make_async_copy(obuf.at[pl.ds(pl.multiple_of(slot * blk, blk), blk)],
                                     ys_hbm.at[pl.ds(ys_hbm.shape[0] - blk, blk)], wsem.at[slot])
        tail.start()
        tail.wait()


def _moe(blk_tables, xs, w1, w3, w2):
    blk_expert, row0, nact, parity, nxt = blk_tables
    nblk = blk_expert.shape[0] - 1
    d, de = w1.shape[1], w1.shape[2]
    sub = d // LANES
    rows = MOE_ROWS
    anyspec = pl.BlockSpec(memory_space=pl.ANY)
    grid_spec = pltpu.PrefetchScalarGridSpec(
        num_scalar_prefetch=5,
        grid=(nblk + 1,),
        in_specs=[anyspec, anyspec, anyspec, anyspec],
        out_specs=anyspec,
        scratch_shapes=[
            pltpu.VMEM((2 * rows * sub, LANES), F32),
            pltpu.VMEM((2 * rows * sub, LANES), F32),
            pltpu.VMEM((2, d, de), F32),
            pltpu.VMEM((2, d, de), F32),
            pltpu.VMEM((2, de, d), F32),
            pltpu.VMEM((d, de), BF16),
            pltpu.VMEM((d, de), BF16),
            pltpu.VMEM((de, d), BF16),
            pltpu.SemaphoreType.DMA((2,)),
            pltpu.SemaphoreType.DMA((2,)),
            pltpu.SemaphoreType.DMA((2,)),
        ],
    )
    return pl.pallas_call(
        functools.partial(_moe_kernel, rows=rows, sub=sub),
        grid_spec=grid_spec,
        out_shape=jax.ShapeDtypeStruct(xs.shape, F32),
        compiler_params=pltpu.CompilerParams(dimension_semantics=("arbitrary",)),
        name="expert_mlp",
    )(blk_expert, row0, nact, parity, nxt, xs, w1, w3, w2)


def _combine_kernel(ss_ref, sl_ref, tb_ref, x1_ref, r_ref, tbv_ref, g2_ref, gpost_ref, ys_hbm, o_ref,
                    ybuf, sem, *, tile0):
    i = pl.program_id(0)
    nt = pl.num_programs(0)
    tm, d = x1_ref.shape
    sub = d // LANES
    nrow = TOP_K * tm
    slot = i % 2

    def fetch(tile, s):
        def seg(e, c):
            k = tile * N_EXPERTS + e
            _segment_copies(ys_hbm, ss_ref[k], ybuf, s * nrow + tb_ref[k], sl_ref[k], sub, sem.at[s])
            return c
        lax.fori_loop(0, N_EXPERTS, seg, 0)

    @pl.when(i == 0)
    def _():
        fetch(tile0, 0)

    @pl.when(i + 1 < nt)
    def _():
        fetch(tile0 + i + 1, 1 - slot)

    pltpu.make_async_copy(ys_hbm.at[pl.ds(0, nrow * sub)],
                          ybuf.at[pl.ds(pl.multiple_of(slot * nrow * sub, nrow * sub), nrow * sub)],
                          sem.at[slot]).wait()
    r = r_ref[...]
    lpos0, lpos1 = _local_positions(r, tbv_ref[0])
    p = lax.broadcasted_iota(jnp.int32, (tm, nrow), 1)
    q = (jnp.where(p == lpos0, r[:, 2:3], 0.0) + jnp.where(p == lpos1, r[:, 3:4], 0.0)).astype(BF16)
    ysort = _load_row_tiles(ybuf, slot * nrow * sub, nrow, sub).astype(BF16)
    y = jnp.dot(q, ysort, preferred_element_type=F32)
    o_ref[...] = x1_ref[...] + g2_ref[0] * _rms(y, gpost_ref[...])


def _combine(x1, ys, route, tables, mod3, mod_row0, gpost, *, t, seq, tm, tok_off, is_ctx):
    seg_start, seg_len, tile_base, tile_base_v = tables
    t_all, d = x1.shape
    per = seq // tm
    sub = d // LANES
    boff = tok_off // tm

    def row(i):
        return (i // per) if not is_ctx else 0

    grid_spec = pltpu.PrefetchScalarGridSpec(
        num_scalar_prefetch=3,
        grid=(t // tm,),
        in_specs=[
            pl.BlockSpec((tm, d), lambda i, *_: (boff + i, 0)),
            pl.BlockSpec((tm, LANES), lambda i, *_: (boff + i, 0)),
            pl.BlockSpec((1, 1, LANES), lambda i, *_: (boff + i, 0, 0)),
            pl.BlockSpec((1, 1, d), lambda i, *_: (mod_row0 + row(i), 0, 5)),
            pl.BlockSpec((1, d), lambda i, *_: (0, 0)),
            pl.BlockSpec(memory_space=pl.ANY),
        ],
        out_specs=pl.BlockSpec((tm, d), lambda i, *_: (i, 0)),
        scratch_shapes=[pltpu.VMEM((2 * TOP_K * tm * sub, LANES), F32), pltpu.SemaphoreType.DMA((2,))],
    )
    return pl.pallas_call(
        functools.partial(_combine_kernel, tile0=boff),
        grid_spec=grid_spec,
        out_shape=jax.ShapeDtypeStruct((t, d), F32),
        compiler_params=pltpu.CompilerParams(dimension_semantics=("arbitrary",)),
        name="combine_ctx" if is_ctx else "combine_lat",
    )(seg_start, seg_len, tile_base, x1, route, tile_base_v, mod3, gpost, ys)


def _routing_tables(route, tm, rows):
    t = route.shape[0]
    nt = t // tm
    n_asg = t * TOP_K
    ex = jnp.arange(N_EXPERTS, dtype=jnp.int32)
    e01 = route[:, 0:TOP_K].astype(jnp.int32)
    cnt_te = jnp.sum((e01.reshape(nt, tm * TOP_K, 1) == ex).astype(jnp.int32), axis=1)
    cnt_e = jnp.sum(cnt_te, axis=0)
    start_e = jnp.cumsum(cnt_e) - cnt_e
    seg_start = start_e[None, :] + jnp.cumsum(cnt_te, axis=0) - cnt_te
    tile_base = jnp.cumsum(cnt_te, axis=1) - cnt_te
    tile_base_v = jnp.pad(tile_base.astype(F32), ((0, 0), (0, LANES - N_EXPERTS))).reshape(nt, 1, LANES)

    nblk_e = (cnt_e + rows - 1) // rows
    blk_end = jnp.cumsum(nblk_e)
    blk_start = blk_end - nblk_e
    n_blocks = n_asg // rows + N_EXPERTS
    b = jnp.arange(n_blocks + 1, dtype=jnp.int32)
    be = jnp.minimum(jnp.sum((blk_end[None, :] <= b[:, None]).astype(jnp.int32), axis=1), N_EXPERTS - 1)
    first = jnp.sum(jnp.where(be[:, None] == ex[None, :], (start_e - blk_start * rows)[None, :], 0), axis=1)
    row0 = jnp.clip(first + b * rows, 0, n_asg)
    nact = blk_end[-1:].astype(jnp.int32)
    used = cnt_e > 0
    parity_e = (jnp.cumsum(used.astype(jnp.int32)) - 1) % 2
    later = (ex[None, :] > ex[:, None]) & used[None, :]
    nxt_e = jnp.min(jnp.where(later, ex[None, :], N_EXPERTS), axis=1)
    nxt_e = jnp.where(nxt_e == N_EXPERTS, -1, nxt_e)
    pick = lambda tab: jnp.sum(jnp.where(be[:, None] == ex[None, :], tab[None, :], 0), axis=1).astype(jnp.int32)
    seg = (seg_start.reshape(-1).astype(jnp.int32), cnt_te.reshape(-1).astype(jnp.int32),
           tile_base.reshape(-1).astype(jnp.int32), tile_base_v)
    blk = (be.astype(jnp.int32), row0.astype(jnp.int32), nact, pick(parity_e), pick(nxt_e))
    return seg, blk


def _rope_tables(n_tok):
    n_rows = n_tok // GRID_W
    rows = jnp.repeat(jnp.arange(n_rows), GRID_W).astype(F32)
    cols = jnp.tile(jnp.arange(GRID_W), n_rows).astype(F32)
    quarter = A_HEAD_DIM // 4
    inv = ROPE_BASE ** (-jnp.arange(quarter, dtype=F32) / quarter)
    ang = jnp.concatenate([rows[:, None] * inv, cols[:, None] * inv], axis=-1)
    cos, sin = jnp.cos(ang), jnp.sin(ang)
    cos_t = jnp.tile(jnp.concatenate([cos, cos], axis=-1), (1, LANES // A_HEAD_DIM))
    sin_t = jnp.tile(jnp.concatenate([-sin, sin], axis=-1), (1, LANES // A_HEAD_DIM))
    return cos_t, sin_t


def kernel(x_prompt, x_sample, c, cache_diff_k, cache_diff_v, cache_swa_k, cache_swa_v, c_ctx, w_ada, b_ada, g_pre_mix, g_post_mix, g_pre_ffn, g_post_ffn, w_in, lam_q1, lam_k1, lam_q2, lam_k2, g_diff_head, sink, w_proj_a, w_proj_b, w_out, w_router_group, b_router_group, w_router_expert, b_router_expert, w_e1, w_e3, w_e2):
    depth = w_in.shape[0]
    assert depth == 1
    l = 0
    bp, sp, d = x_prompt.shape
    bs, ss, _ = x_sample.shape
    lambda_init = 0.8 - 0.6 * math.exp(-0.3 * l)
    assert A_HEAD_DIM == B_HEAD_DIM and ss % GRID_W == 0 and bs + 1 <= MOD_ROWS

    c_all = jnp.concatenate([c_ctx[None, :], c, jnp.zeros((MOD_ROWS - 1 - bs, d), F32)], axis=0)
    mod = _modulation(c_all, w_ada[l], b_ada[l][None, :])
    mod3 = mod.reshape(MOD_ROWS, 1, 6 * d)

    w_in_b = w_in[l].astype(BF16)
    wpa = w_proj_a[l].astype(BF16)
    wpb = w_proj_b[l].astype(BF16)
    wo = w_out[l].astype(BF16)
    n_r = N_GROUPS + N_EXPERTS
    wr = jnp.concatenate([w_router_expert[l], w_router_group[l], jnp.zeros((d, LANES - n_r), F32)], axis=1)
    br = jnp.concatenate([b_router_expert[l], b_router_group[l], jnp.zeros((LANES - n_r,), F32)])[None, :]
    wr_hi = wr.astype(BF16)
    wr_lo = (wr - wr_hi.astype(F32)).astype(BF16)
    lam_p = jnp.stack([lam_q1[l], lam_k1[l], lam_q2[l], lam_k2[l]], axis=0)
    g_head = g_diff_head[l][None, :]
    sink_l = sink[l]
    cos_t, sin_t = _rope_tables(ss)

    xp2 = x_prompt.reshape(bp * sp, d)
    xs2 = x_sample.reshape(bs * ss, d)
    gpre = g_pre_mix[l][None, :]

    (qa_c, ka_c, va_c, qb_c, kb2_c, vb_c, sga_c, sgb_c, kaf, vaf, kbf, vbf) = _inproj(
        xp2, mod3, 0, gpre, cos_t, sin_t, w_in_b, seq=sp, tm=sp, is_ctx=True)
    r3 = lambda a, b_: a.reshape(b_, -1, a.shape[-1])
    oa_c = _diff_attention(lam_p, g_head, r3(qa_c, bp), r3(ka_c, bp), r3(va_c, bp), lambda_init=lambda_init)
    ob_c = _swa_attention(sink_l, r3(qb_c, bp), r3(kb2_c, bp), r3(vb_c, bp))

    (qa_s, ka_s, va_s, qb_s, kb2_s, vb_s, sga_s, sgb_s) = _inproj(
        xs2, mod3, 1, gpre, cos_t, sin_t, w_in_b, seq=ss, tm=512, is_ctx=False)
    past = cache_diff_k.shape[2]
    ck = cache_diff_k[:, l].reshape(bs, past, -1)
    cv = cache_diff_v[:, l].reshape(bs, past, -1)
    oa_s = _diff_attention_lat(lam_p, g_head, r3(qa_s, bs), ck, ka_s, cv, r3(va_s, bs),
                               tq=512, lambda_init=lambda_init)
    sk = cache_swa_k[:, l].reshape(bs, past, -1)
    sv = cache_swa_v[:, l].reshape(bs, past, -1)
    nqb = ss // SWA_Q
    kb2_3, vb_3 = r3(kb2_s, bs), r3(vb_s, bs)
    ob_s = _swa_attention_lat(sink_l, r3(qb_s, bs), sk, kb2_3, sv, vb_3)

    gpm = g_post_mix[l][None, :]
    gpf = g_pre_ffn[l][None, :]
    t_ctx, t_lat = bp * sp, bs * ss
    x1, h2t, route = _postmix(
        (xp2, oa_c.reshape(t_ctx, -1), ob_c.reshape(t_ctx, -1), sga_c, sgb_c),
        (xs2, oa_s.reshape(t_lat, -1), ob_s.reshape(t_lat, -1), sga_s, sgb_s),
        mod3, gpm, gpf, wpa, wpb, wo, wr_hi, wr_lo, br, lat_seq=ss, tm=512)

    tables, blk_tables = _routing_tables(route, MOE_TILE, MOE_ROWS)
    xs = _dispatch(h2t, route, tables, tm=MOE_TILE, rows=MOE_ROWS)
    ys = _moe(blk_tables, xs, w_e1[l], w_e3[l], w_e2[l])

    gpost = g_post_ffn[l][None, :]
    y_p = _combine(x1, ys, route, tables, mod3, 0, gpost, t=t_ctx, seq=sp, tm=MOE_TILE, tok_off=0, is_ctx=True)
    y_s = _combine(x1, ys, route, tables, mod3, 1, gpost, t=t_lat, seq=ss, tm=MOE_TILE, tok_off=t_ctx,
                   is_ctx=False)

    ha = A_HEADS
    return (y_p.reshape(bp, sp, d), y_s.reshape(bs, ss, d),
            kaf.reshape(bp, 1, sp, ha, 2, A_HEAD_DIM), vaf.reshape(bp, 1, sp, ha, A_V_DIM),
            kbf.reshape(bp, 1, sp, B_KV_HEADS, B_HEAD_DIM), vbf.reshape(bp, 1, sp, B_KV_HEADS, B_HEAD_DIM))
```

```python
import functools
import math

import jax
import jax.numpy as jnp
from jax import lax
from jax.experimental import pallas as pl
from jax.experimental.pallas import tpu as pltpu

F32 = jnp.float32
BF16 = jnp.bfloat16
HIGHEST = lax.Precision.HIGHEST

GRID_W = 64
ROPE_BASE = 10000.0
EPS = 1e-6
NEG_INF = -1e30
A_HEADS = 4
A_HEAD_DIM = 64
A_V_DIM = 2 * A_HEAD_DIM
B_HEADS = 8
B_KV_HEADS = 2
B_GROUP = B_HEADS // B_KV_HEADS
B_HEAD_DIM = 64
WINDOW = 128
N_GROUPS = 4
EXPERTS_PER_GROUP = 8
N_EXPERTS = N_GROUPS * EXPERTS_PER_GROUP
TOP_K = 2

LANES = 128
MOD_ROWS = 16
MOE_ROWS = 512
MOE_TILE = 512
SWA_Q = 128

_QA = 0
_KA = _QA + A_HEADS * 2 * A_HEAD_DIM
_VA = _KA + A_HEADS * 2 * A_HEAD_DIM
_QB = _VA + A_HEADS * A_V_DIM
_KB = _QB + B_HEADS * B_HEAD_DIM
_VB = _KB + B_KV_HEADS * B_HEAD_DIM
_GA = _VB + B_KV_HEADS * B_HEAD_DIM


def _rms(x, g):
    return x * lax.rsqrt(jnp.mean(x * x, axis=-1, keepdims=True) + EPS) * g


def _store_row_tiles(ref, base, val):
    sub = val.shape[1] // LANES
    for s in range(sub):
        ref[pl.ds(base + s, val.shape[0], stride=sub), :] = val[:, s * LANES:(s + 1) * LANES]


def _load_row_tiles(ref, base, n_rows, sub):
    return jnp.concatenate([ref[pl.ds(base + s, n_rows, stride=sub), :] for s in range(sub)], axis=1)


def _mod_kernel(c_ref, w_ref, b_ref, o_ref):
    c = c_ref[...]
    a = c * jax.nn.sigmoid(c)
    o_ref[...] = jnp.dot(a, w_ref[...], precision=HIGHEST, preferred_element_type=F32) + b_ref[...]


def _modulation(c_all, w_ada, b_ada):
    d, n = w_ada.shape
    tn = 512
    return pl.pallas_call(
        _mod_kernel,
        grid=(n // tn,),
        in_specs=[
            pl.BlockSpec((MOD_ROWS, d), lambda j: (0, 0)),
            pl.BlockSpec((d, tn), lambda j: (0, j)),
            pl.BlockSpec((1, tn), lambda j: (0, j)),
        ],
        out_specs=pl.BlockSpec((MOD_ROWS, tn), lambda j: (0, j)),
        out_shape=jax.ShapeDtypeStruct((MOD_ROWS, n), F32),
        name="modulation",
    )(c_all, w_ada, b_ada)


def _rope128(z, cos, sin_signed, first_half):
    rot = jnp.where(first_half, pltpu.roll(z, 96, 1), pltpu.roll(z, 32, 1))
    return z * cos + rot * sin_signed


def _inproj_kernel(x_ref, sh_ref, sc_ref, g_ref, cos_ref, sin_ref, w_ref, *outs, is_ctx):
    x = x_ref[...]
    h = _rms(x, g_ref[...]) * (1.0 + sc_ref[0]) + sh_ref[0]
    hb = h.astype(BF16)
    lane = lax.broadcasted_iota(jnp.int32, (1, LANES), 1)
    first_half = (lane % 64) < 32
    low = lane < 64

    def seg(lo, hi):
        return jnp.dot(hb, w_ref[:, lo:hi], preferred_element_type=F32)

    def rope(z):
        if is_ctx:
            return z
        cos = cos_ref[...]
        sin = sin_ref[...]
        parts = [_rope128(z[:, j:j + LANES], cos, sin, first_half) for j in range(0, z.shape[1], LANES)]
        return parts[0] if len(parts) == 1 else jnp.concatenate(parts, axis=1)

    if is_ctx:
        qa_o, ka_o, va_o, qb_o, kb2_o, vb_o, sga_o, sgb_o, kaf_o, vaf_o, kbf_o, vbf_o = outs
    else:
        qa_o, ka_o, va_o, qb_o, kb2_o, vb_o, sga_o, sgb_o = outs

    scale = A_HEAD_DIM ** -0.5
    qa_o[...] = (rope(seg(_QA, _KA)) * scale).astype(BF16)
    ka = rope(seg(_KA, _VA))
    if is_ctx:
        ka_o[...] = ka.astype(BF16)
    else:
        ka_o[0] = ka.T.astype(BF16)
    va = seg(_VA, _QB)
    va_o[...] = va.astype(BF16)
    qb_o[...] = (rope(seg(_QB, _KB)) * (B_HEAD_DIM ** -0.5)).astype(BF16)
    kb = rope(seg(_KB, _VB))
    kb_sw = pltpu.roll(kb, 64, 1)
    kb2_o[:, 0:LANES] = jnp.where(low, kb, kb_sw).astype(BF16)
    kb2_o[:, LANES:2 * LANES] = jnp.where(low, kb_sw, kb).astype(BF16)
    vb = seg(_VB, _GA)
    if is_ctx:
        vb_o[...] = vb.astype(BF16)
    else:
        vb_sw = pltpu.roll(vb, 64, 1)
        vb_o[:, 0:LANES] = jnp.where(low, vb, vb_sw).astype(BF16)
        vb_o[:, LANES:2 * LANES] = jnp.where(low, vb_sw, vb).astype(BF16)
    d = x.shape[1]
    sga_o[...] = jax.nn.sigmoid(seg(_GA, _GA + d)).astype(BF16)
    sgb_o[...] = jax.nn.sigmoid(seg(_GA + d, _GA + 2 * d)).astype(BF16)
    if is_ctx:
        kaf_o[...] = ka
        vaf_o[...] = va
        kbf_o[...] = kb
        vbf_o[...] = vb


def _inproj(x2, mod3, mod_row0, g_pre, cos_t, sin_t, w_in_b, *, seq, tm, is_ctx):
    t, d = x2.shape
    per = seq // tm
    n_in = w_in_b.shape[1]
    wa = A_HEADS * 2 * A_HEAD_DIM
    wkb = B_KV_HEADS * B_HEAD_DIM

    def row(i):
        return (i // per) if not is_ctx else 0

    tok = lambda w: pl.BlockSpec((tm, w), lambda i: (i, 0))
    out_shape = [
        jax.ShapeDtypeStruct((t, wa), BF16), jax.ShapeDtypeStruct((t, wa), BF16),
        jax.ShapeDtypeStruct((t, wa), BF16), jax.ShapeDtypeStruct((t, wa), BF16),
        jax.ShapeDtypeStruct((t, 2 * wkb), BF16), jax.ShapeDtypeStruct((t, wkb), BF16),
        jax.ShapeDtypeStruct((t, d), BF16), jax.ShapeDtypeStruct((t, d), BF16),
    ]
    out_specs = [tok(wa), tok(wa), tok(wa), tok(wa), tok(2 * wkb), tok(wkb), tok(d), tok(d)]
    if not is_ctx:
        out_shape[5] = jax.ShapeDtypeStruct((t, 2 * wkb), BF16)
        out_specs[5] = tok(2 * wkb)
        out_shape[1] = jax.ShapeDtypeStruct((t // seq, wa, seq), BF16)
        out_specs[1] = pl.BlockSpec((1, wa, tm), lambda i: (i // per, 0, i % per))
    if is_ctx:
        out_shape += [jax.ShapeDtypeStruct((t, wa), F32), jax.ShapeDtypeStruct((t, wa), F32),
                      jax.ShapeDtypeStruct((t, wkb), F32), jax.ShapeDtypeStruct((t, wkb), F32)]
        out_specs += [tok(wa), tok(wa), tok(wkb), tok(wkb)]
    return pl.pallas_call(
        functools.partial(_inproj_kernel, is_ctx=is_ctx),
        grid=(t // tm,),
        in_specs=[
            pl.BlockSpec((tm, d), lambda i: (i, 0)),
            pl.BlockSpec((1, 1, d), lambda i: (mod_row0 + row(i), 0, 0)),
            pl.BlockSpec((1, 1, d), lambda i: (mod_row0 + row(i), 0, 1)),
            pl.BlockSpec((1, d), lambda i: (0, 0)),
            pl.BlockSpec((tm, LANES), lambda i: (i % per, 0)),
            pl.BlockSpec((tm, LANES), lambda i: (i % per, 0)),
            pl.BlockSpec((d, n_in), lambda i: (0, 0)),
        ],
        out_specs=out_specs,
        out_shape=out_shape,
        compiler_params=pltpu.CompilerParams(dimension_semantics=("arbitrary",)),
        name="inproj_ctx" if is_ctx else "inproj_lat",
    )(x2, mod3, mod3, g_pre, cos_t, sin_t, w_in_b)


def _nt(a, b):
    return lax.dot_general(a, b, (((1,), (1,)), ((), ())), preferred_element_type=F32)


def _diff_kernel(lam_ref, g_ref, q_ref, k_ref, v_ref, o_ref, *, lambda_init):
    lp = lam_ref[...]
    lam = (jnp.exp(jnp.sum(lp[0:1] * lp[1:2], axis=-1, keepdims=True))
           - jnp.exp(jnp.sum(lp[2:3] * lp[3:4], axis=-1, keepdims=True)) + lambda_init)
    tq = q_ref.shape[1]
    lane = lax.broadcasted_iota(jnp.int32, (1, LANES), 1)
    for h in range(A_HEADS):
        cols = slice(h * LANES, (h + 1) * LANES)
        q = q_ref[0, :, cols]
        q2 = jnp.concatenate([q * (lane < 64).astype(BF16), q * (lane >= 64).astype(BF16)], axis=0)
        s = _nt(q2, k_ref[0, :, cols])
        mx = jnp.max(s, axis=-1, keepdims=True)
        v = v_ref[0, :, cols]
        v_ext = jnp.concatenate([v, jnp.ones_like(v)], axis=1)
        acc = jnp.dot(jnp.exp(s - mx).astype(BF16), v_ext, preferred_element_type=F32)
        on = acc[:, 0:LANES] / acc[:, LANES:2 * LANES]
        o = on[0:tq] - lam * on[tq:2 * tq]
        o_ref[0, :, cols] = (_rms(o, g_ref[...]) * (1.0 - lambda_init)).astype(BF16)


def _diff_attention(lam_p, g_head, q, k, v, *, lambda_init):
    b, s, w = q.shape
    seq = pl.BlockSpec((1, s, w), lambda bi: (bi, 0, 0))
    return pl.pallas_call(
        functools.partial(_diff_kernel, lambda_init=lambda_init),
        grid=(b,),
        in_specs=[pl.BlockSpec((4, A_HEAD_DIM), lambda bi: (0, 0)),
                  pl.BlockSpec((1, A_V_DIM), lambda bi: (0, 0)), seq, seq, seq],
        out_specs=seq,
        out_shape=jax.ShapeDtypeStruct((b, s, w), BF16),
        compiler_params=pltpu.CompilerParams(dimension_semantics=("arbitrary",)),
        name="diff_attn_ctx",
    )(lam_p, g_head, q, k, v)


def _diff_lat_kernel(lam_ref, g_ref, q_ref, kc_ref, kt_ref, vc_ref, v_ref, o_ref, s_a, m_a, s_b, m_b, *,
                     lambda_init):
    t = pl.program_id(0)
    tq = q_ref.shape[1]
    nkc = kc_ref.shape[1]
    nkn = kt_ref.shape[2]
    nk = nkc + nkn

    @pl.when(t == 0)
    def _():
        s_b[...] = jnp.zeros_like(s_b)
        m_b[...] = jnp.zeros_like(m_b)

    def body(s_w, m_w, s_r, m_r):
        lp = lam_ref[...]
        lam = (jnp.exp(jnp.sum(lp[0:1] * lp[1:2], axis=-1, keepdims=True))
               - jnp.exp(jnp.sum(lp[2:3] * lp[3:4], axis=-1, keepdims=True)) + lambda_init)
        lane = lax.broadcasted_iota(jnp.int32, (1, LANES), 1)

        q = q_ref[0]
        q2 = jnp.concatenate([q * (lane < 64).astype(BF16), q * (lane >= 64).astype(BF16)], axis=0)
        sc = _nt(q2, kc_ref[0].astype(BF16))
        sn = jnp.dot(q2, kt_ref[0], preferred_element_type=F32)
        mx = jnp.maximum(jnp.max(sc, axis=-1, keepdims=True), jnp.max(sn, axis=-1, keepdims=True))
        s_w[:, 0:nkc] = sc
        s_w[:, nkc:nk] = sn
        m_w[...] = jnp.broadcast_to(mx, (2 * tq, LANES))

        mp = m_r[...]
        v_all = jnp.concatenate([vc_ref[0].astype(BF16), v_ref[0]], axis=0)
        v_ext = jnp.concatenate([v_all, jnp.ones_like(v_all)], axis=1)
        p = jnp.concatenate(
            [jnp.exp(s_r[:, c:c + LANES] - mp).astype(BF16) for c in range(0, nk, LANES)], axis=1)
        acc = jnp.dot(p, v_ext, preferred_element_type=F32)
        on = acc[:, 0:LANES] / acc[:, LANES:2 * LANES]
        o = on[0:tq] - lam * on[tq:2 * tq]
        o_ref[0] = (_rms(o, g_ref[...]) * (1.0 - lambda_init)).astype(BF16)

    @pl.when(t % 2 == 0)
    def _():
        body(s_a, m_a, s_b, m_b)

    @pl.when(t % 2 == 1)
    def _():
        body(s_b, m_b, s_a, m_a)


def _diff_attention_lat(lam_p, g_head, q, kc, kt, vc, v, *, tq, lambda_init):
    b, s, w = q.shape
    past = kc.shape[1]
    nq = s // tq
    n_units = b * A_HEADS * nq
    last = n_units - 1

    def unit(u):
        return u // (A_HEADS * nq), (u // nq) % A_HEADS, u % nq

    def cur(t):
        return unit(jnp.minimum(t, last))

    def prev(t):
        return unit(jnp.maximum(t - 1, 0))

    return pl.pallas_call(
        functools.partial(_diff_lat_kernel, lambda_init=lambda_init),
        grid=(n_units + 1,),
        in_specs=[
            pl.BlockSpec((4, A_HEAD_DIM), lambda t: (0, 0)),
            pl.BlockSpec((1, A_V_DIM), lambda t: (0, 0)),
            pl.BlockSpec((1, tq, LANES), lambda t: (cur(t)[0], cur(t)[2], cur(t)[1])),
            pl.BlockSpec((1, past, LANES), lambda t: (cur(t)[0], 0, cur(t)[1])),
            pl.BlockSpec((1, LANES, s), lambda t: (cur(t)[0], cur(t)[1], 0)),
            pl.BlockSpec((1, past, LANES), lambda t: (prev(t)[0], 0, prev(t)[1])),
            pl.BlockSpec((1, s, LANES), lambda t: (prev(t)[0], 0, prev(t)[1])),
        ],
        out_specs=pl.BlockSpec((1, tq, LANES), lambda t: (prev(t)[0], prev(t)[2], prev(t)[1])),
        out_shape=jax.ShapeDtypeStruct((b, s, w), BF16),
        scratch_shapes=[pltpu.VMEM((2 * tq, past + s), F32), pltpu.VMEM((2 * tq, LANES), F32),
                        pltpu.VMEM((2 * tq, past + s), F32), pltpu.VMEM((2 * tq, LANES), F32)],
        compiler_params=pltpu.CompilerParams(dimension_semantics=("arbitrary",),
                                             vmem_limit_bytes=56 * 1024 * 1024),
        name="diff_attn_lat",
    )(lam_p, g_head, q, kc, kt, vc, v)


def _swa_kernel(sink_ref, q_ref, k_ref, v_ref, o_ref):
    tq = q_ref.shape[1]
    rows = B_GROUP * tq
    lane = lax.broadcasted_iota(jnp.int32, (1, LANES), 1)
    low = lane < 64
    lane2 = lax.broadcasted_iota(jnp.int32, (1, 2 * LANES), 1)
    head_masks = [((lane2 // 64) == g).astype(BF16) for g in range(B_GROUP)]
    gw = B_GROUP * B_HEAD_DIM
    v = v_ref[0]
    v_ext = jnp.concatenate([v, jnp.ones_like(v)], axis=1)
    finishers = []
    for n in range(B_KV_HEADS):
        q = q_ref[0, :, n * gw:(n + 1) * gw]
        qs = jnp.concatenate([q * hm for hm in head_masks], axis=0)
        k2 = k_ref[0, :, n * LANES:(n + 1) * LANES]
        s = _nt(qs, jnp.concatenate([k2, k2], axis=1))
        sink = jnp.concatenate(
            [jnp.full((tq, 1), sink_ref[n * B_GROUP + g], F32) for g in range(B_GROUP)], axis=0)
        top = jnp.max(s, keepdims=True)
        for g in range(B_GROUP):
            top = jnp.maximum(top, sink_ref[n * B_GROUP + g])

        def finish(mp, n=n, s=s, sink=sink):
            acc = jnp.dot(jnp.exp(s - mp).astype(BF16), v_ext, preferred_element_type=F32)
            den = acc[:, LANES:2 * LANES] + jnp.exp(sink - mp)
            o = acc[:, 0:LANES] / den
            osw = pltpu.roll(o, 64, 1)
            for j in range(B_GROUP // 2):
                ra = slice((2 * j) * tq, (2 * j + 1) * tq)
                rb = slice((2 * j + 1) * tq, (2 * j + 2) * tq)
                pair = jnp.where(low, o[ra], osw[rb]) if n == 0 else jnp.where(low, osw[ra], o[rb])
                o_ref[0, :, n * gw + j * LANES:n * gw + (j + 1) * LANES] = pair.astype(BF16)
            return jnp.min(den)

        finishers.append((finish, s, sink, finish(top)))

    smallest = functools.reduce(jnp.minimum, [f[3] for f in finishers])

    @pl.when(jnp.logical_not(smallest >= 1e-30))
    def _():
        for finish, s, sink, _ in finishers:
            finish(jnp.maximum(jnp.max(s, axis=-1, keepdims=True), sink))


def _swa_attention(sink, q, k2, v):
    b, s, w = q.shape
    seq = lambda a: pl.BlockSpec((1, s, a.shape[2]), lambda bi: (bi, 0, 0))
    return pl.pallas_call(
        _swa_kernel,
        grid=(b,),
        in_specs=[pl.BlockSpec(memory_space=pltpu.SMEM), seq(q), seq(k2), seq(v)],
        out_specs=seq(q),
        out_shape=jax.ShapeDtypeStruct((b, s, w), BF16),
        compiler_params=pltpu.CompilerParams(dimension_semantics=("arbitrary",)),
        name="swa_attn_ctx",
    )(sink, q, k2, v)


def _swa_lat_kernel(sink_ref, q_ref, kc_ref, kl_ref, km_ref, kr_ref, vc_ref, vl_ref, vm_ref, vr_ref, o_ref,
                    s_a, m_a, s_b, m_b, *, nqb):
    t = pl.program_id(0)
    n_units = pl.num_programs(0) - 1
    tq = q_ref.shape[1]
    gw = B_GROUP * B_HEAD_DIM
    i_cur = jnp.minimum(t, n_units - 1) % nqb
    nkc = kc_ref.shape[1]
    rows = B_GROUP * tq

    @pl.when(t == 0)
    def _():
        s_b[...] = jnp.zeros_like(s_b)
        m_b[...] = jnp.zeros_like(m_b)

    def sink_col(n):
        return jnp.concatenate(
            [jnp.full((tq, 1), sink_ref[n * B_GROUP + g], F32) for g in range(B_GROUP)], axis=0)

    def body(s_w, m_w, s_r, m_r):
        lane = lax.broadcasted_iota(jnp.int32, (1, LANES), 1)
        low = lane < 64
        lane2 = lax.broadcasted_iota(jnp.int32, (1, 2 * LANES), 1)
        head_masks = [((lane2 // 64) == g).astype(BF16) for g in range(B_GROUP)]
        qi = lax.broadcasted_iota(jnp.int32, (rows, SWA_Q), 0) & (tq - 1)
        kj = lax.broadcasted_iota(jnp.int32, (rows, SWA_Q), 1)
        far = 2 * SWA_Q
        left_ok = kj >= qi + jnp.where(i_cur > 0, 0, far)
        right_ok = kj <= qi - jnp.where(i_cur < nqb - 1, 0, far)

        kc = kc_ref[0]
        kc_sw = pltpu.roll(kc, 64, 1)
        for n in range(B_KV_HEADS):
            q = q_ref[0, :, n * gw:(n + 1) * gw]
            qs = jnp.concatenate([q * hm for hm in head_masks], axis=0)
            kc2 = (jnp.where(low, kc, kc_sw) if n == 0 else jnp.where(low, kc_sw, kc)).astype(BF16)
            ks = [kc2] + [r[0, :, n * LANES:(n + 1) * LANES] for r in (kl_ref, km_ref, kr_ref)]
            k_all = jnp.concatenate([jnp.concatenate([k, k], axis=1) for k in ks], axis=0)
            s = _nt(qs, k_all)
            chunks = [s[:, c:c + LANES] for c in range(0, s.shape[1], LANES)]
            il = nkc // LANES
            chunks[il] = jnp.where(left_ok, chunks[il], NEG_INF)
            chunks[il + 2] = jnp.where(right_ok, chunks[il + 2], NEG_INF)
            top = jnp.max(functools.reduce(jnp.maximum, chunks), keepdims=True)
            for g in range(B_GROUP):
                top = jnp.maximum(top, sink_ref[n * B_GROUP + g])
            for c, ch in enumerate(chunks):
                s_w[n * rows:(n + 1) * rows, c * LANES:(c + 1) * LANES] = ch
            m_w[n * rows:(n + 1) * rows, :] = jnp.broadcast_to(top, (rows, LANES))

        vc = vc_ref[0]
        vc_sw = pltpu.roll(vc, 64, 1)
        nk = nkc + 3 * tq
        finishers = []
        for n in range(B_KV_HEADS):
            vc2 =(jnp.where(low, vc, vc_sw) if n == 0 else jnp.where(low, vc_sw, vc)).astype(BF16)
            v_all = jnp.concatenate(
                [vc2] + [r[0, :, n * LANES:(n + 1) * LANES] for r in (vl_ref, vm_ref, vr_ref)], axis=0)
            v_ext = jnp.concatenate([v_all, jnp.ones_like(v_all)], axis=1)
            sink = sink_col(n)

            def finish(mp, n=n, v_ext=v_ext, sink=sink):
                p = jnp.concatenate([jnp.exp(s_r[n * rows:(n + 1) * rows, c:c + LANES] - mp).astype(BF16)
                                     for c in range(0, nk, LANES)], axis=1)
                acc = jnp.dot(p, v_ext, preferred_element_type=F32)
                den = acc[:, LANES:2 * LANES] + jnp.exp(sink - mp)
                o = acc[:, 0:LANES] / den
                for j in range(B_GROUP // 2):
                    pair = jnp.where(low, o[(2 * j) * tq:(2 * j + 1) * tq], o[(2 * j + 1) * tq:(2 * j + 2) * tq])
                    o_ref[0, :, n * gw + j * LANES:n * gw + (j + 1) * LANES] = pair.astype(BF16)
                return jnp.min(den)

            finishers.append((n, finish, sink, finish(m_r[n * rows:(n + 1) * rows, :])))

        smallest = functools.reduce(jnp.minimum, [f[3] for f in finishers])

        @pl.when(jnp.logical_not(smallest >= 1e-30))
        def _():
            for n, finish, sink, _ in finishers:
                row_max = functools.reduce(
                    jnp.maximum, [s_r[n * rows:(n + 1) * rows, c:c + LANES] for c in range(0, nk, LANES)])
                mx = jnp.maximum(jnp.max(row_max, axis=-1, keepdims=True), sink)
                finish(jnp.broadcast_to(mx, (rows, LANES)))

    @pl.when(t % 2 == 0)
    def _():
        body(s_a, m_a, s_b, m_b)

    @pl.when(t % 2 == 1)
    def _():
        body(s_b, m_b, s_a, m_a)


def _swa_attention_lat(sink, q, kc, k2, vc, v):
    b, s, w = q.shape
    past = kc.shape[1]
    tq = SWA_Q
    nqb = s // tq
    n_units = b * nqb
    last = n_units - 1
    nk = past + 3 * tq

    def cur(t):
        u = jnp.minimum(t, last)
        return u // nqb, u % nqb

    def prev(t):
        u = jnp.maximum(t - 1, 0)
        return u // nqb, u % nqb

    lo = lambda i: jnp.maximum(i - 1, 0)
    hi = lambda i: jnp.minimum(i + 1, nqb - 1)
    kspec = lambda f: pl.BlockSpec((1, tq, 2 * LANES), lambda t: (cur(t)[0], f(cur(t)[1]), 0))
    vspec = lambda f: pl.BlockSpec((1, tq, 2 * LANES), lambda t: (prev(t)[0], f(prev(t)[1]), 0))
    same = lambda i: i
    rows = B_KV_HEADS * B_GROUP * tq
    return pl.pallas_call(
        functools.partial(_swa_lat_kernel, nqb=nqb),
        grid=(n_units + 1,),
        in_specs=[
            pl.BlockSpec(memory_space=pltpu.SMEM),
            pl.BlockSpec((1, tq, w), lambda t: (cur(t)[0], cur(t)[1], 0)),
            pl.BlockSpec((1, past, LANES), lambda t: (cur(t)[0], 0, 0)),
            kspec(lo), kspec(same), kspec(hi),
            pl.BlockSpec((1, past, LANES), lambda t: (prev(t)[0], 0, 0)),
            vspec(lo), vspec(same), vspec(hi),
        ],
        out_specs=pl.BlockSpec((1, tq, w), lambda t: (prev(t)[0], prev(t)[1], 0)),
        out_shape=jax.ShapeDtypeStruct((b, s, w), BF16),
        scratch_shapes=[pltpu.VMEM((rows, nk), F32), pltpu.VMEM((rows, LANES), F32),
                        pltpu.VMEM((rows, nk), F32), pltpu.VMEM((rows, LANES), F32)],
        compiler_params=pltpu.CompilerParams(dimension_semantics=("arbitrary",)),
        name="swa_attn_lat",
    )(sink, q, kc, k2, k2, k2, vc, v, v, v)


def _postmix_kernel(*refs, n_ctx_tiles):
    (xc, xl, oac, oal, obc, obl, sgac, sgal, sgbc, sgbl, g1_ref, sh2_ref, sc2_ref, gpm_ref, gpf_ref,
     wpa_ref, wpb_ref, wo_ref, wr_ref, br_ref, x1_o, h2_o, route_o, x1_a, x1_b) = refs
    t = pl.program_id(0)
    is_ctx = t < n_ctx_tiles
    pick = lambda a, b: jnp.where(is_ctx, a[...], b[...])

    @pl.when(t == 0)
    def _():
        x1_b[...] = jnp.zeros_like(x1_b)

    def body(x1_w, x1_r):
        pa = jnp.dot(pick(oac, oal), wpa_ref[...], preferred_element_type=F32)
        pb = jnp.dot(pick(obc, obl), wpb_ref[...], preferred_element_type=F32)
        mix = pick(sgac, sgal).astype(F32) * pa + pick(sgbc, sgbl).astype(F32) * pb
        m2 = jnp.dot(mix.astype(BF16), wo_ref[...], preferred_element_type=F32)
        x1 = pick(xc, xl) + g1_ref[0] * _rms(m2, gpm_ref[...])
        x1_o[...] = x1
        x1_w[...] = x1
        _postmix_route(x1_r[...], sh2_ref, sc2_ref, gpf_ref, wr_ref, br_ref, h2_o, route_o)

    @pl.when(t % 2 == 0)
    def _():
        body(x1_a, x1_b)

    @pl.when(t % 2 == 1)
    def _():
        body(x1_b, x1_a)


def _postmix_route(x1, sh2_ref, sc2_ref, gpf_ref, wr_ref, br_ref, h2_o, route_o):
    h2 = _rms(x1, gpf_ref[...]) * (1.0 + sc2_ref[0]) + sh2_ref[0]
    h2_o[...] = h2.astype(BF16)

    h_hi = h2.astype(BF16)
    h_lo = (h2 - h_hi.astype(F32)).astype(BF16)
    both = jnp.dot(h_hi, wr_ref[...], preferred_element_type=F32)
    logits = (both[:, 0:LANES] + both[:, LANES:2 * LANES]
              + jnp.dot(h_lo, wr_ref[:, 0:LANES], preferred_element_type=F32) + br_ref[...])
    tm = logits.shape[0]
    lt = logits.T
    row = lax.broadcasted_iota(jnp.int32, (EXPERTS_PER_GROUP, tm), 0).astype(F32)
    none = float(EXPERTS_PER_GROUP)
    lg = jnp.where(row < N_GROUPS, lt[N_EXPERTS:N_EXPERTS + EXPERTS_PER_GROUP], -jnp.inf)
    mg = jnp.max(lg, axis=0, keepdims=True)
    g_sel = jnp.min(jnp.where(lg == mg, row, none), axis=0, keepdims=True)
    g_w = 1.0 / jnp.sum(jnp.exp(lg - mg), axis=0, keepdims=True)
    le = lt[0:EXPERTS_PER_GROUP]
    for g in range(1, N_GROUPS):
        le = jnp.where(g_sel == g, lt[g * EXPERTS_PER_GROUP:(g + 1) * EXPERTS_PER_GROUP], le)
    v0 = jnp.max(le, axis=0, keepdims=True)
    i0 = jnp.min(jnp.where(le == v0, row, none), axis=0, keepdims=True)
    le1 = jnp.where(row == i0, -jnp.inf, le)
    v1 = jnp.max(le1, axis=0, keepdims=True)
    i1 = jnp.min(jnp.where(le1 == v1, row, none), axis=0, keepdims=True)
    e = jnp.exp(v1 - v0)
    w0 = g_w / (1.0 + e)
    w1 = g_w * e / (1.0 + e)
    e0 = g_sel * EXPERTS_PER_GROUP + i0
    e1 = g_sel * EXPERTS_PER_GROUP + i1
    rt = jnp.where(row == 0, e0, jnp.where(row == 1, e1, jnp.where(row == 2, w0, jnp.where(row == 3, w1, 0.0))))
    rt = jnp.concatenate([rt, jnp.zeros((LANES - EXPERTS_PER_GROUP, tm), F32)], axis=0)
    route_o[...] = rt.T


def _postmix(ctx_in, lat_in, mod3, gpm, gpf, wpa, wpb, wo, wr, br, *, lat_seq, tm):
    t_ctx, d = ctx_in[0].shape
    t_lat = lat_in[0].shape[0]
    assert t_ctx % tm == 0 and lat_seq % tm == 0
    nc = t_ctx // tm
    nl = t_lat // tm
    per = lat_seq // tm
    sub = d // LANES
    t_all = t_ctx + t_lat

    last = nc + nl - 1
    cur = lambda i: jnp.minimum(i, last)
    prev = lambda i: jnp.maximum(i - 1, 0)
    mod_row = lambda j: jnp.where(j < nc, 0, 1 + jnp.maximum(j - nc, 0) // per)
    full = lambda a: pl.BlockSpec(a.shape, lambda i: (0,) * a.ndim)
    in_specs, args = [], []
    for a_c, a_l in zip(ctx_in, lat_in):
        w = a_c.shape[1]
        in_specs += [pl.BlockSpec((tm, w), lambda i: (jnp.minimum(cur(i), nc - 1), 0)),
                     pl.BlockSpec((tm, w), lambda i: (jnp.maximum(cur(i) - nc, 0), 0))]
        args += [a_c, a_l]
    in_specs += [pl.BlockSpec((1, 1, d), lambda i: (mod_row(cur(i)), 0, 2)),
                 pl.BlockSpec((1, 1, d), lambda i: (mod_row(prev(i)), 0, 3)),
                 pl.BlockSpec((1, 1, d), lambda i: (mod_row(prev(i)), 0, 4)),
                 full(gpm), full(gpf), full(wpa), full(wpb), full(wo), full(wr), full(br)]
    args += [mod3, mod3, mod3, gpm, gpf, wpa, wpb, wo, wr, br]
    return pl.pallas_call(
        functools.partial(_postmix_kernel, n_ctx_tiles=nc),
        grid=(nc + nl + 1,),
        in_specs=in_specs,
        out_specs=[pl.BlockSpec((tm, d), lambda i: (cur(i), 0)),
                   pl.BlockSpec((tm, d), lambda i: (prev(i), 0)),
                   pl.BlockSpec((tm, LANES), lambda i: (prev(i), 0))],
        out_shape=[jax.ShapeDtypeStruct((t_all, d), F32),
                   jax.ShapeDtypeStruct((t_all, d), BF16),
                   jax.ShapeDtypeStruct((t_all, LANES), F32)],
        scratch_shapes=[pltpu.VMEM((tm, d), F32), pltpu.VMEM((tm, d), F32)],
        compiler_params=pltpu.CompilerParams(dimension_semantics=("arbitrary",)),
        name="postmix",
    )(*args)


def _segment_copies(src, src_row, dst, dst_row, n, sub, sem):
    @pl.when(n > 0)
    def _():
        pltpu.make_async_copy(src.at[pl.ds(pl.multiple_of(src_row * sub, sub), n * sub)],
                              dst.at[pl.ds(pl.multiple_of(dst_row * sub, sub), n * sub)], sem).start()


def _local_positions(route, tile_base):
    tm = route.shape[0]
    lane = lax.broadcasted_iota(jnp.int32, (tm, LANES), 1).astype(F32)
    is0 = lane == route[:, 0:1]
    is1 = lane == route[:, 1:2]
    earlier = (lax.broadcasted_iota(jnp.int32, (tm, tm), 1)
               < lax.broadcasted_iota(jnp.int32, (tm, tm), 0)).astype(BF16)
    pre0 = jnp.dot(earlier, is0.astype(BF16), preferred_element_type=F32)
    pre1 = jnp.dot(earlier, is1.astype(BF16), preferred_element_type=F32)
    cnt0 = jnp.sum(is0.astype(F32), axis=0, keepdims=True)
    lpos0 = jnp.sum(jnp.where(is0, tile_base + pre0, 0.0), axis=-1, keepdims=True)
    lpos1 = jnp.sum(jnp.where(is1, tile_base + cnt0 + pre1, 0.0), axis=-1, keepdims=True)
    return lpos0.astype(jnp.int32), lpos1.astype(jnp.int32)


def _dispatch_kernel(ss_ref, sl_ref, tb_ref, h_ref, r_ref, tbv_ref, xs_hbm, pbuf, zbuf, sem, zsem, *, n_asg, rows):
    i = pl.program_id(0)
    nt = pl.num_programs(0)
    tm, d = h_ref.shape
    sub = d // LANES
    nrow = TOP_K * tm
    slot = i % 2

    def wait_slot(s):
        pltpu.make_async_copy(pbuf.at[pl.ds(pl.multiple_of(s * nrow * sub, nrow * sub), nrow * sub)],
                              xs_hbm.at[pl.ds(0, nrow * sub)], sem.at[s]).wait()

    def slack_copy():
        return pltpu.make_async_copy(zbuf, xs_hbm.at[pl.ds(n_asg * sub, rows * sub)], zsem.at[0])

    @pl.when(i == 0)
    def _():
        zbuf[...] = jnp.zeros_like(zbuf)
        slack_copy().start()

    lpos0, lpos1 = _local_positions(r_ref[...], tbv_ref[0])
    p = lax.broadcasted_iota(jnp.int32, (tm, nrow), 1)
    sel = ((p == lpos0) | (p == lpos1)).astype(BF16)
    xp = lax.dot_general(sel, h_ref[...].astype(BF16), (((0,), (0,)), ((), ())),
                         preferred_element_type=F32)

    @pl.when(i >= 2)
    def _():
        wait_slot(slot)

    _store_row_tiles(pbuf, slot * nrow * sub, xp)

    def seg(e, c):
        k = i * N_EXPERTS + e
        _segment_copies(pbuf, slot * nrow + tb_ref[k], xs_hbm, ss_ref[k], sl_ref[k], sub, sem.at[slot])
        return c
    lax.fori_loop(0, N_EXPERTS, seg, 0)

    @pl.when(i == nt - 1)
    def _():
        wait_slot(slot)

        @pl.when(nt >= 2)
        def _():
            wait_slot(1 - slot)
        slack_copy().wait()


def _dispatch(h2, route, tables, *, tm, rows):
    seg_start, seg_len, tile_base, tile_base_v = tables
    t, d = h2.shape
    sub = d // LANES
    n_asg = t * TOP_K
    grid_spec = pltpu.PrefetchScalarGridSpec(
        num_scalar_prefetch=3,
        grid=(t // tm,),
        in_specs=[
            pl.BlockSpec((tm, d), lambda i, *_: (i, 0)),
            pl.BlockSpec((tm, LANES), lambda i, *_: (i, 0)),
            pl.BlockSpec((1, 1, LANES), lambda i, *_: (i, 0, 0)),
        ],
        out_specs=pl.BlockSpec(memory_space=pl.ANY),
        scratch_shapes=[pltpu.VMEM((2 * TOP_K * tm * sub, LANES), F32), pltpu.VMEM((rows * sub, LANES), F32),
                        pltpu.SemaphoreType.DMA((2,)), pltpu.SemaphoreType.DMA((1,))],
    )
    return pl.pallas_call(
        functools.partial(_dispatch_kernel, n_asg=n_asg, rows=rows),
        grid_spec=grid_spec,
        out_shape=jax.ShapeDtypeStruct(((n_asg + rows) * sub, LANES), F32),
        compiler_params=pltpu.CompilerParams(dimension_semantics=("arbitrary",)),
        name="dispatch",
    )(seg_start, seg_len, tile_base, h2, route, tile_base_v)


def _moe_kernel(be_ref, row0_ref, nact_ref, par_ref, nxt_ref, xs_hbm, w1_hbm, w3_hbm, w2_hbm, ys_hbm,
                xbuf, obuf, wf1, wf3, wf2, w1b, w3b, w2b, rsem, wsem, gsem, *, rows, sub):
    i = pl.program_id(0)
    nact = nact_ref[0]
    slot = i % 2
    nslot = 1 - slot
    blk = rows * sub

    def weight_copies(e, s):
        return [pltpu.make_async_copy(w_hbm.at[e], wf.at[s], gsem.at[s])
                for w_hbm, wf in ((w1_hbm, wf1), (w3_hbm, wf3), (w2_hbm, wf2))]

    def read(j, s):
        return pltpu.make_async_copy(xs_hbm.at[pl.ds(pl.multiple_of(row0_ref[j] * sub, sub), blk)],
                                     xbuf.at[pl.ds(pl.multiple_of(s * blk, blk), blk)], rsem.at[s])

    def write(j, s):
        return pltpu.make_async_copy(obuf.at[pl.ds(pl.multiple_of(s * blk, blk), blk)],
                                     ys_hbm.at[pl.ds(pl.multiple_of(row0_ref[j] * sub, sub), blk)], wsem.at[s])

    @pl.when(i == 0)
    def _():
        read(0, 0).start()
        for c in weight_copies(be_ref[0], 0):
            c.start()

    @pl.when(i < nact)
    def _():
        @pl.when(i + 1 < nact)
        def _():
            read(i + 1, nslot).start()

        changed = jnp.logical_or(i == 0, be_ref[i] != be_ref[jnp.maximum(i - 1, 0)])

        @pl.when(changed)
        def _():
            s = par_ref[i]
            for c in weight_copies(be_ref[i], s):
                c.wait()
            w1b[...] = wf1[s].astype(BF16)
            w3b[...] = wf3[s].astype(BF16)
            w2b[...] = wf2[s].astype(BF16)

            @pl.when(nxt_ref[i] >= 0)
            def _():
                for c in weight_copies(nxt_ref[i], 1 - s):
                    c.start()

        read(i, slot).wait()
        x = _load_row_tiles(xbuf, slot * blk, rows, sub).astype(BF16)
        a = jnp.dot(x, w1b[...], preferred_element_type=F32)
        b = jnp.dot(x, w3b[...], preferred_element_type=F32)
        hmid = (a * jax.nn.sigmoid(a) * b).astype(BF16)
        y = jnp.dot(hmid, w2b[...], preferred_element_type=F32)
        _store_row_tiles(obuf, slot * blk, y)

        @pl.when(i >= 1)
        def _():
            write(i - 1, nslot).wait()
        write(i, slot).start()

    @pl.when(i == nact)
    def _():
        write(i - 1, nslot).wait()
        obuf[pl.ds(pl.multiple_of(slot * blk, blk), blk), :] = jnp.zeros((blk, LANES), F32)
        tail = pltpu.make_async_copy(obuf.at[pl.ds(pl.multiple_of(slot * blk, blk), blk)],
                                     ys_hbm.at[pl.ds(ys_hbm.shape[0] - blk, blk)], wsem.at[slot])
        tail.start()
        tail.wait()


def _moe(blk_tables, xs, w1, w3, w2):
    blk_expert, row0, nact, parity, nxt = blk_tables
    nblk = blk_expert.shape[0] - 1
    d, de = w1.shape[1], w1.shape[2]
    sub = d // LANES
    rows = MOE_ROWS
    anyspec = pl.BlockSpec(memory_space=pl.ANY)
    grid_spec = pltpu.PrefetchScalarGridSpec(
        num_scalar_prefetch=5,
        grid=(nblk + 1,),
        in_specs=[anyspec, anyspec, anyspec, anyspec],
        out_specs=anyspec,
        scratch_shapes=[
            pltpu.VMEM((2 * rows * sub, LANES), F32),
            pltpu.VMEM((2 * rows * sub, LANES), F32),
            pltpu.VMEM((2, d, de), F32),
            pltpu.VMEM((2, d, de), F32),
            pltpu.VMEM((2, de, d), F32),
            pltpu.VMEM((d, de), BF16),
            pltpu.VMEM((d, de), BF16),
            pltpu.VMEM((de, d), BF16),
            pltpu.SemaphoreType.DMA((2,)),
            pltpu.SemaphoreType.DMA((2,)),
            pltpu.SemaphoreType.DMA((2,)),
        ],
    )
    return pl.pallas_call(
        functools.partial(_moe_kernel, rows=rows, sub=sub),
        grid_spec=grid_spec,
        out_shape=jax.ShapeDtypeStruct(xs.shape, F32),
        compiler_params=pltpu.CompilerParams(dimension_semantics=("arbitrary",)),
        name="expert_mlp",
    )(blk_expert, row0, nact, parity, nxt, xs, w1, w3, w2)


def _combine_kernel(ss_ref, sl_ref, tb_ref, x1_ref, r_ref, tbv_ref, g2_ref, gpost_ref, ys_hbm, o_ref,
                    ybuf, sem, *, tile0):
    i = pl.program_id(0)
    nt = pl.num_programs(0)
    tm, d = x1_ref.shape
    sub = d // LANES
    nrow = TOP_K * tm
    slot = i % 2

    def fetch(tile, s):
        def seg(e, c):
            k = tile * N_EXPERTS + e
            _segment_copies(ys_hbm, ss_ref[k], ybuf, s * nrow + tb_ref[k], sl_ref[k], sub, sem.at[s])
            return c
        lax.fori_loop(0, N_EXPERTS, seg, 0)

    @pl.when(i == 0)
    def _():
        fetch(tile0, 0)

    @pl.when(i + 1 < nt)
    def _():
        fetch(tile0 + i + 1, 1 - slot)

    pltpu.make_async_copy(ys_hbm.at[pl.ds(0, nrow * sub)],
                          ybuf.at[pl.ds(pl.multiple_of(slot * nrow * sub, nrow * sub), nrow * sub)],
                          sem.at[slot]).wait()
    r = r_ref[...]
    lpos0, lpos1 = _local_positions(r, tbv_ref[0])
    p = lax.broadcasted_iota(jnp.int32, (tm, nrow), 1)
    q = (jnp.where(p == lpos0, r[:, 2:3], 0.0) + jnp.where(p == lpos1, r[:, 3:4], 0.0)).astype(BF16)
    ysort = _load_row_tiles(ybuf, slot * nrow * sub, nrow, sub).astype(BF16)
    y = jnp.dot(q, ysort, preferred_element_type=F32)
    o_ref[...] = x1_ref[...] + g2_ref[0] * _rms(y, gpost_ref[...])


def _combine(x1, ys, route, tables, mod3, mod_row0, gpost, *, t, seq, tm, tok_off, is_ctx):
    seg_start, seg_len, tile_base, tile_base_v = tables
    t_all, d = x1.shape
    per = seq // tm
    sub = d // LANES
    boff = tok_off // tm

    def row(i):
        return (i // per) if not is_ctx else 0

    grid_spec = pltpu.PrefetchScalarGridSpec(
        num_scalar_prefetch=3,
        grid=(t // tm,),
        in_specs=[
            pl.BlockSpec((tm, d), lambda i, *_: (boff + i, 0)),
            pl.BlockSpec((tm, LANES), lambda i, *_: (boff + i, 0)),
            pl.BlockSpec((1, 1, LANES), lambda i, *_: (boff + i, 0, 0)),
            pl.BlockSpec((1, 1, d), lambda i, *_: (mod_row0 + row(i), 0, 5)),
            pl.BlockSpec((1, d), lambda i, *_: (0, 0)),
            pl.BlockSpec(memory_space=pl.ANY),
        ],
        out_specs=pl.BlockSpec((tm, d), lambda i, *_: (i, 0)),
        scratch_shapes=[pltpu.VMEM((2 * TOP_K * tm * sub, LANES), F32), pltpu.SemaphoreType.DMA((2,))],
    )
    return pl.pallas_call(
        functools.partial(_combine_kernel, tile0=boff),
        grid_spec=grid_spec,
        out_shape=jax.ShapeDtypeStruct((t, d), F32),
        compiler_params=pltpu.CompilerParams(dimension_semantics=("arbitrary",)),
        name="combine_ctx" if is_ctx else "combine_lat",
    )(seg_start, seg_len, tile_base, x1, route, tile_base_v, mod3, gpost, ys)


def _routing_tables(route, tm, rows):
    t = route.shape[0]
    nt = t // tm
    n_asg = t * TOP_K
    ex = jnp.arange(N_EXPERTS, dtype=jnp.int32)
    e01 = route[:, 0:TOP_K].astype(jnp.int32)
    cnt_te = jnp.sum((e01.reshape(nt, tm * TOP_K, 1) == ex).astype(jnp.int32), axis=1)
    cnt_e = jnp.sum(cnt_te, axis=0)
    start_e = jnp.cumsum(cnt_e) - cnt_e
    seg_start = start_e[None, :] + jnp.cumsum(cnt_te, axis=0) - cnt_te
    tile_base = jnp.cumsum(cnt_te, axis=1) - cnt_te
    tile_base_v = jnp.pad(tile_base.astype(F32), ((0, 0), (0, LANES - N_EXPERTS))).reshape(nt, 1, LANES)

    nblk_e = (cnt_e + rows - 1) // rows
    blk_end = jnp.cumsum(nblk_e)
    blk_start = blk_end - nblk_e
    n_blocks = n_asg // rows + N_EXPERTS
    b = jnp.arange(n_blocks + 1, dtype=jnp.int32)
    be = jnp.minimum(jnp.sum((blk_end[None, :] <= b[:, None]).astype(jnp.int32), axis=1), N_EXPERTS - 1)
    first = jnp.sum(jnp.where(be[:, None] == ex[None, :], (start_e - blk_start * rows)[None, :], 0), axis=1)
    row0 = jnp.clip(first + b * rows, 0, n_asg)
    nact = blk_end[-1:].astype(jnp.int32)
    used = cnt_e > 0
    parity_e = (jnp.cumsum(used.astype(jnp.int32)) - 1) % 2
    later = (ex[None, :] > ex[:, None]) & used[None, :]
    nxt_e = jnp.min(jnp.where(later, ex[None, :], N_EXPERTS), axis=1)
    nxt_e = jnp.where(nxt_e == N_EXPERTS, -1, nxt_e)
    pick = lambda tab: jnp.sum(jnp.where(be[:, None] == ex[None, :], tab[None, :], 0), axis=1).astype(jnp.int32)
    seg = (seg_start.reshape(-1).astype(jnp.int32), cnt_te.reshape(-1).astype(jnp.int32),
           tile_base.reshape(-1).astype(jnp.int32), tile_base_v)
    blk = (be.astype(jnp.int32), row0.astype(jnp.int32), nact, pick(parity_e), pick(nxt_e))
    return seg, blk


def _rope_tables(n_tok):
    n_rows = n_tok // GRID_W
    rows = jnp.repeat(jnp.arange(n_rows), GRID_W).astype(F32)
    cols = jnp.tile(jnp.arange(GRID_W), n_rows).astype(F32)
    quarter = A_HEAD_DIM // 4
    inv = ROPE_BASE ** (-jnp.arange(quarter, dtype=F32) / quarter)
    ang = jnp.concatenate([rows[:, None] * inv, cols[:, None] * inv], axis=-1)
    cos, sin = jnp.cos(ang), jnp.sin(ang)
    cos_t = jnp.tile(jnp.concatenate([cos, cos], axis=-1), (1, LANES // A_HEAD_DIM))
    sin_t = jnp.tile(jnp.concatenate([-sin, sin], axis=-1), (1, LANES // A_HEAD_DIM))
    return cos_t, sin_t


def kernel(x_prompt, x_sample, c, cache_diff_k, cache_diff_v, cache_swa_k, cache_swa_v, c_ctx, w_ada, b_ada, g_pre_mix, g_post_mix, g_pre_ffn, g_post_ffn, w_in, lam_q1, lam_k1, lam_q2, lam_k2, g_diff_head, sink, w_proj_a, w_proj_b, w_out, w_router_group, b_router_group, w_router_expert, b_router_expert, w_e1, w_e3, w_e2):
    depth = w_in.shape[0]
    assert depth == 1
    l = 0
    bp, sp, d = x_prompt.shape
    bs, ss, _ = x_sample.shape
    lambda_init = 0.8 - 0.6 * math.exp(-0.3 * l)
    assert A_HEAD_DIM == B_HEAD_DIM and ss % GRID_W == 0 and bs + 1 <= MOD_ROWS

    c_all = jnp.concatenate([c_ctx[None, :], c, jnp.zeros((MOD_ROWS - 1 - bs, d), F32)], axis=0)
    mod = _modulation(c_all, w_ada[l], b_ada[l][None, :])
    mod3 = mod.reshape(MOD_ROWS, 1, 6 * d)

    w_in_b = w_in[l].astype(BF16)
    wpa = w_proj_a[l].astype(BF16)
    wpb = w_proj_b[l].astype(BF16)
    wo = w_out[l].astype(BF16)
    n_r = N_GROUPS + N_EXPERTS
    wr = jnp.concatenate([w_router_expert[l], w_router_group[l], jnp.zeros((d, LANES - n_r), F32)], axis=1)
    br = jnp.concatenate([b_router_expert[l], b_router_group[l], jnp.zeros((LANES - n_r,), F32)])[None, :]
    wr_hi = wr.astype(BF16)
    wr2 = jnp.concatenate([wr_hi, (wr - wr_hi.astype(F32)).astype(BF16)], axis=1)
    lam_p = jnp.stack([lam_q1[l], lam_k1[l], lam_q2[l], lam_k2[l]], axis=0)
    g_head = g_diff_head[l][None, :]
    sink_l = sink[l]
    cos_t, sin_t = _rope_tables(ss)

    xp2 = x_prompt.reshape(bp * sp, d)
    xs2 = x_sample.reshape(bs * ss, d)
    gpre = g_pre_mix[l][None, :]

    (qa_c, ka_c, va_c, qb_c, kb2_c, vb_c, sga_c, sgb_c, kaf, vaf, kbf, vbf) = _inproj(
        xp2, mod3, 0, gpre, cos_t, sin_t, w_in_b, seq=sp, tm=sp, is_ctx=True)
    r3 = lambda a, b_: a.reshape(b_, -1, a.shape[-1])
    oa_c = _diff_attention(lam_p, g_head, r3(qa_c, bp), r3(ka_c, bp), r3(va_c, bp), lambda_init=lambda_init)
    ob_c = _swa_attention(sink_l, r3(qb_c, bp), r3(kb2_c, bp), r3(vb_c, bp))

    (qa_s, ka_s, va_s, qb_s, kb2_s, vb_s, sga_s, sgb_s) = _inproj(
        xs2, mod3, 1, gpre, cos_t, sin_t, w_in_b, seq=ss, tm=512, is_ctx=False)
    past = cache_diff_k.shape[2]
    ck = cache_diff_k[:, l].reshape(bs, past, -1)
    cv = cache_diff_v[:, l].reshape(bs, past, -1)
    oa_s = _diff_attention_lat(lam_p, g_head, r3(qa_s, bs), ck, ka_s, cv, r3(va_s, bs),
                               tq=512, lambda_init=lambda_init)
    sk = cache_swa_k[:, l].reshape(bs, past, -1)
    sv = cache_swa_v[:, l].reshape(bs, past, -1)
    nqb = ss // SWA_Q
    kb2_3, vb_3 = r3(kb2_s, bs), r3(vb_s, bs)
    ob_s = _swa_attention_lat(sink_l, r3(qb_s, bs), sk, kb2_3, sv, vb_3)

    gpm = g_post_mix[l][None, :]
    gpf = g_pre_ffn[l][None, :]
    t_ctx, t_lat = bp * sp, bs * ss
    x1, h2t, route = _postmix(
        (xp2, oa_c.reshape(t_ctx, -1), ob_c.reshape(t_ctx, -1), sga_c, sgb_c),
        (xs2, oa_s.reshape(t_lat, -1), ob_s.reshape(t_lat, -1), sga_s, sgb_s),
        mod3, gpm, gpf, wpa, wpb, wo, wr2, br, lat_seq=ss, tm=512)

    tables, blk_tables = _routing_tables(route, MOE_TILE, MOE_ROWS)
    xs = _dispatch(h2t, route, tables, tm=MOE_TILE, rows=MOE_ROWS)
    ys = _moe(blk_tables, xs, w_e1[l], w_e3[l], w_e2[l])

    gpost = g_post_ffn[l][None, :]
    y_p = _combine(x1, ys, route, tables, mod3, 0, gpost, t=t_ctx, seq=sp, tm=MOE_TILE, tok_off=0, is_ctx=True)
    y_s = _combine(x1, ys, route, tables, mod3, 1, gpost, t=t_lat, seq=ss, tm=MOE_TILE, tok_off=t_ctx,
                   is_ctx=False)

    ha = A_HEADS
    return (y_p.reshape(bp, sp, d), y_s.reshape(bs, ss, d),
            kaf.reshape(bp, 1, sp, ha, 2, A_HEAD_DIM), vaf.reshape(bp, 1, sp, ha, A_V_DIM),
            kbf.reshape(bp, 1, sp, B_KV_HEADS, B_HEAD_DIM), vbf.reshape(bp, 1, sp, B_KV_HEADS, B_HEAD_DIM))
```

```python
import functools
import math

import jax
import jax.numpy as jnp
from jax import lax
from jax.experimental import pallas as pl
from jax.experimental.pallas import tpu as pltpu

F32 = jnp.float32
BF16 = jnp.bfloat16
HIGHEST = lax.Precision.HIGHEST

GRID_W = 64
ROPE_BASE = 10000.0
EPS = 1e-6
NEG_INF = -1e30
A_HEADS = 4
A_HEAD_DIM = 64
A_V_DIM = 2 * A_HEAD_DIM
B_HEADS = 8
B_KV_HEADS = 2
B_GROUP = B_HEADS // B_KV_HEADS
B_HEAD_DIM = 64
WINDOW = 128
N_GROUPS = 4
EXPERTS_PER_GROUP = 8
N_EXPERTS = N_GROUPS * EXPERTS_PER_GROUP
TOP_K = 2

LANES = 128
MOD_ROWS = 16
MOE_ROWS = 512
MOE_TILE = 512
SWA_Q = 128

_QA = 0
_KA = _QA + A_HEADS * 2 * A_HEAD_DIM
_VA = _KA + A_HEADS * 2 * A_HEAD_DIM
_QB = _VA + A_HEADS * A_V_DIM
_KB = _QB + B_HEADS * B_HEAD_DIM
_VB = _KB + B_KV_HEADS * B_HEAD_DIM
_GA = _VB + B_KV_HEADS * B_HEAD_DIM


def _rms(x, g):
    return x * lax.rsqrt(jnp.mean(x * x, axis=-1, keepdims=True) + EPS) * g


def _store_row_tiles(ref, base, val):
    sub = val.shape[1] // LANES
    for s in range(sub):
        ref[pl.ds(base + s, val.shape[0], stride=sub), :] = val[:, s * LANES:(s + 1) * LANES]


def _load_row_tiles(ref, base, n_rows, sub):
    return jnp.concatenate([ref[pl.ds(base + s, n_rows, stride=sub), :] for s in range(sub)], axis=1)


def _mod_kernel(c_ref, w_ref, b_ref, o_ref):
    c = c_ref[...]
    a = c * jax.nn.sigmoid(c)
    o_ref[...] = jnp.dot(a, w_ref[...], precision=HIGHEST, preferred_element_type=F32) + b_ref[...]


def _modulation(c_all, w_ada, b_ada):
    d, n = w_ada.shape
    tn = 512
    return pl.pallas_call(
        _mod_kernel,
        grid=(n // tn,),
        in_specs=[
            pl.BlockSpec((MOD_ROWS, d), lambda j: (0, 0)),
            pl.BlockSpec((d, tn), lambda j: (0, j)),
            pl.BlockSpec((1, tn), lambda j: (0, j)),
        ],
        out_specs=pl.BlockSpec((MOD_ROWS, tn), lambda j: (0, j)),
        out_shape=jax.ShapeDtypeStruct((MOD_ROWS, n), F32),
        name="modulation",
    )(c_all, w_ada, b_ada)


def _rope128(z, cos, sin_signed, first_half):
    rot = jnp.where(first_half, pltpu.roll(z, 96, 1), pltpu.roll(z, 32, 1))
    return z * cos + rot * sin_signed


def _inproj_kernel(x_ref, sh_ref, sc_ref, g_ref, cos_ref, sin_ref, w_ref, *outs, is_ctx):
    x = x_ref[...]
    h = _rms(x, g_ref[...]) * (1.0 + sc_ref[0]) + sh_ref[0]
    hb = h.astype(BF16)
    lane = lax.broadcasted_iota(jnp.int32, (1, LANES), 1)
    first_half = (lane % 64) < 32
    low = lane < 64

    def seg(lo, hi):
        return jnp.dot(hb, w_ref[:, lo:hi], preferred_element_type=F32)

    def rope(z):
        if is_ctx:
            return z
        cos = cos_ref[...]
        sin = sin_ref[...]
        parts = [_rope128(z[:, j:j + LANES], cos, sin, first_half) for j in range(0, z.shape[1], LANES)]
        return parts[0] if len(parts) == 1 else jnp.concatenate(parts, axis=1)

    if is_ctx:
        qa_o, ka_o, va_o, qb_o, kb2_o, vb_o, sga_o, sgb_o, kaf_o, vaf_o, kbf_o, vbf_o = outs
    else:
        qa_o, ka_o, va_o, qb_o, kb2_o, vb_o, sga_o, sgb_o = outs

    scale = A_HEAD_DIM ** -0.5
    qa_o[...] = (rope(seg(_QA, _KA)) * scale).astype(BF16)
    ka = rope(seg(_KA, _VA))
    if is_ctx:
        ka_o[...] = ka.astype(BF16)
    else:
        ka_o[0] = ka.T.astype(BF16)
    va = seg(_VA, _QB)
    va_o[...] = va.astype(BF16)
    qb_o[...] = (rope(seg(_QB, _KB)) * (B_HEAD_DIM ** -0.5)).astype(BF16)
    kb = rope(seg(_KB, _VB))
    kb_sw = pltpu.roll(kb, 64, 1)
    kb2_o[:, 0:LANES] = jnp.where(low, kb, kb_sw).astype(BF16)
    kb2_o[:, LANES:2 * LANES] = jnp.where(low, kb_sw, kb).astype(BF16)
    vb = seg(_VB, _GA)
    if is_ctx:
        vb_o[...] = vb.astype(BF16)
    else:
        vb_sw = pltpu.roll(vb, 64, 1)
        vb_o[:, 0:LANES] = jnp.where(low, vb, vb_sw).astype(BF16)
        vb_o[:, LANES:2 * LANES] = jnp.where(low, vb_sw, vb).astype(BF16)
    d = x.shape[1]
    sga_o[...] = jax.nn.sigmoid(seg(_GA, _GA + d)).astype(BF16)
    sgb_o[...] = jax.nn.sigmoid(seg(_GA + d, _GA + 2 * d)).astype(BF16)
    if is_ctx:
        kaf_o[...] = ka
        vaf_o[...] = va
        kbf_o[...] = kb
        vbf_o[...] = vb


def _inproj(x2, mod3, mod_row0, g_pre, cos_t, sin_t, w_in_b, *, seq, tm, is_ctx):
    t, d = x2.shape
    per = seq // tm
    n_in = w_in_b.shape[1]
    wa = A_HEADS * 2 * A_HEAD_DIM
    wkb = B_KV_HEADS * B_HEAD_DIM

    def row(i):
        return (i // per) if not is_ctx else 0

    tok = lambda w: pl.BlockSpec((tm, w), lambda i: (i, 0))
    out_shape = [
        jax.ShapeDtypeStruct((t, wa), BF16), jax.ShapeDtypeStruct((t, wa), BF16),
        jax.ShapeDtypeStruct((t, wa), BF16), jax.ShapeDtypeStruct((t, wa), BF16),
        jax.ShapeDtypeStruct((t, 2 * wkb), BF16), jax.ShapeDtypeStruct((t, wkb), BF16),
        jax.ShapeDtypeStruct((t, d), BF16), jax.ShapeDtypeStruct((t, d), BF16),
    ]
    out_specs = [tok(wa), tok(wa), tok(wa), tok(wa), tok(2 * wkb), tok(wkb), tok(d), tok(d)]
    if not is_ctx:
        out_shape[5] = jax.ShapeDtypeStruct((t, 2 * wkb), BF16)
        out_specs[5] = tok(2 * wkb)
        out_shape[1] = jax.ShapeDtypeStruct((t // seq, wa, seq), BF16)
        out_specs[1] = pl.BlockSpec((1, wa, tm), lambda i: (i // per, 0, i % per))
    if is_ctx:
        out_shape += [jax.ShapeDtypeStruct((t, wa), F32), jax.ShapeDtypeStruct((t, wa), F32),
                      jax.ShapeDtypeStruct((t, wkb), F32), jax.ShapeDtypeStruct((t, wkb), F32)]
        out_specs += [tok(wa), tok(wa), tok(wkb), tok(wkb)]
    return pl.pallas_call(
        functools.partial(_inproj_kernel, is_ctx=is_ctx),
        grid=(t // tm,),
        in_specs=[
            pl.BlockSpec((tm, d), lambda i: (i, 0)),
            pl.BlockSpec((1, 1, d), lambda i: (mod_row0 + row(i), 0, 0)),
            pl.BlockSpec((1, 1, d), lambda i: (mod_row0 + row(i), 0, 1)),
            pl.BlockSpec((1, d), lambda i: (0, 0)),
            pl.BlockSpec((tm, LANES), lambda i: (i % per, 0)),
            pl.BlockSpec((tm, LANES), lambda i: (i % per, 0)),
            pl.BlockSpec((d, n_in), lambda i: (0, 0)),
        ],
        out_specs=out_specs,
        out_shape=out_shape,
        compiler_params=pltpu.CompilerParams(dimension_semantics=("arbitrary",)),
        name="inproj_ctx" if is_ctx else "inproj_lat",
    )(x2, mod3, mod3, g_pre, cos_t, sin_t, w_in_b)


def _nt(a, b):
    return lax.dot_general(a, b, (((1,), (1,)), ((), ())), preferred_element_type=F32)


def _diff_kernel(lam_ref, g_ref, q_ref, k_ref, v_ref, o_ref, *, lambda_init):
    lp = lam_ref[...]
    lam = (jnp.exp(jnp.sum(lp[0:1] * lp[1:2], axis=-1, keepdims=True))
           - jnp.exp(jnp.sum(lp[2:3] * lp[3:4], axis=-1, keepdims=True)) + lambda_init)
    tq = q_ref.shape[1]
    lane = lax.broadcasted_iota(jnp.int32, (1, LANES), 1)
    for h in range(A_HEADS):
        cols = slice(h * LANES, (h + 1) * LANES)
        q = q_ref[0, :, cols]
        q2 = jnp.concatenate([q * (lane < 64).astype(BF16), q * (lane >= 64).astype(BF16)], axis=0)
        s = _nt(q2, k_ref[0, :, cols])
        mx = jnp.max(s, axis=-1, keepdims=True)
        v = v_ref[0, :, cols]
        v_ext = jnp.concatenate([v, jnp.ones_like(v)], axis=1)
        acc = jnp.dot(jnp.exp(s - mx).astype(BF16), v_ext, preferred_element_type=F32)
        on = acc[:, 0:LANES] / acc[:, LANES:2 * LANES]
        o = on[0:tq] - lam * on[tq:2 * tq]
        o_ref[0, :, cols] = (_rms(o, g_ref[...]) * (1.0 - lambda_init)).astype(BF16)


def _diff_attention(lam_p, g_head, q, k, v, *, lambda_init):
    b, s, w = q.shape
    seq = pl.BlockSpec((1, s, w), lambda bi: (bi, 0, 0))
    return pl.pallas_call(
        functools.partial(_diff_kernel, lambda_init=lambda_init),
        grid=(b,),
        in_specs=[pl.BlockSpec((4, A_HEAD_DIM), lambda bi: (0, 0)),
                  pl.BlockSpec((1, A_V_DIM), lambda bi: (0, 0)), seq, seq, seq],
        out_specs=seq,
        out_shape=jax.ShapeDtypeStruct((b, s, w), BF16),
        compiler_params=pltpu.CompilerParams(dimension_semantics=("arbitrary",)),
        name="diff_attn_ctx",
    )(lam_p, g_head, q, k, v)


def _diff_lat_kernel(lam_ref, g_ref, q_ref, kc_ref, kt_ref, vc_ref, v_ref, o_ref, s_a, m_a, s_b, m_b, *,
                     lambda_init):
    t = pl.program_id(0)
    tq = q_ref.shape[1]
    nkc = kc_ref.shape[1]
    nkn = kt_ref.shape[2]
    nk = nkc + nkn

    @pl.when(t == 0)
    def _():
        s_b[...] = jnp.zeros_like(s_b)
        m_b[...] = jnp.zeros_like(m_b)

    def body(s_w, m_w, s_r, m_r):
        lp = lam_ref[...]
        lam = (jnp.exp(jnp.sum(lp[0:1] * lp[1:2], axis=-1, keepdims=True))
               - jnp.exp(jnp.sum(lp[2:3] * lp[3:4], axis=-1, keepdims=True)) + lambda_init)
        lane = lax.broadcasted_iota(jnp.int32, (1, LANES), 1)

        q = q_ref[0]
        q2 = jnp.concatenate([q * (lane < 64).astype(BF16), q * (lane >= 64).astype(BF16)], axis=0)
        sc = _nt(q2, kc_ref[0].astype(BF16))
        sn = jnp.dot(q2, kt_ref[0], preferred_element_type=F32)
        mx = jnp.maximum(jnp.max(sc, axis=-1, keepdims=True), jnp.max(sn, axis=-1, keepdims=True))
        s_w[:, 0:nkc] = sc
        s_w[:, nkc:nk] = sn
        m_w[...] = jnp.broadcast_to(mx, (2 * tq, LANES))

        mp = m_r[...]
        v_all = jnp.concatenate([vc_ref[0].astype(BF16), v_ref[0]], axis=0)
        v_ext = jnp.concatenate([v_all, jnp.ones_like(v_all)], axis=1)
        p = jnp.concatenate(
            [jnp.exp(s_r[:, c:c + LANES] - mp).astype(BF16) for c in range(0, nk, LANES)], axis=1)
        acc = jnp.dot(p, v_ext, preferred_element_type=F32)
        on = acc[:, 0:LANES] / acc[:, LANES:2 * LANES]
        o = on[0:tq] - lam * on[tq:2 * tq]
        o_ref[0] = (_rms(o, g_ref[...]) * (1.0 - lambda_init)).astype(BF16)

    @pl.when(t % 2 == 0)
    def _():
        body(s_a, m_a, s_b, m_b)

    @pl.when(t % 2 == 1)
    def _():
        body(s_b, m_b, s_a, m_a)


def _diff_attention_lat(lam_p, g_head, q, kc, kt, vc, v, *, tq, lambda_init):
    b, s, w = q.shape
    past = kc.shape[1]
    nq = s // tq
    n_units = b * A_HEADS * nq
    last = n_units - 1

    def unit(u):
        return u // (A_HEADS * nq), (u // nq) % A_HEADS, u % nq

    def cur(t):
        return unit(jnp.minimum(t, last))

    def prev(t):
        return unit(jnp.maximum(t - 1, 0))

    return pl.pallas_call(
        functools.partial(_diff_lat_kernel, lambda_init=lambda_init),
        grid=(n_units + 1,),
        in_specs=[
            pl.BlockSpec((4, A_HEAD_DIM), lambda t: (0, 0)),
            pl.BlockSpec((1, A_V_DIM), lambda t: (0, 0)),
            pl.BlockSpec((1, tq, LANES), lambda t: (cur(t)[0], cur(t)[2], cur(t)[1])),
            pl.BlockSpec((1, past, LANES), lambda t: (cur(t)[0], 0, cur(t)[1])),
            pl.BlockSpec((1, LANES, s), lambda t: (cur(t)[0], cur(t)[1], 0)),
            pl.BlockSpec((1, past, LANES), lambda t: (prev(t)[0], 0, prev(t)[1])),
            pl.BlockSpec((1, s, LANES), lambda t: (prev(t)[0], 0, prev(t)[1])),
        ],
        out_specs=pl.BlockSpec((1, tq, LANES), lambda t: (prev(t)[0], prev(t)[2], prev(t)[1])),
        out_shape=jax.ShapeDtypeStruct((b, s, w), BF16),
        scratch_shapes=[pltpu.VMEM((2 * tq, past + s), F32), pltpu.VMEM((2 * tq, LANES), F32),
                        pltpu.VMEM((2 * tq, past + s), F32), pltpu.VMEM((2 * tq, LANES), F32)],
        compiler_params=pltpu.CompilerParams(dimension_semantics=("arbitrary",),
                                             vmem_limit_bytes=56 * 1024 * 1024),
        name="diff_attn_lat",
    )(lam_p, g_head, q, kc, kt, vc, v)


def _swa_kernel(sink_ref, q_ref, k_ref, v_ref, o_ref):
    tq = q_ref.shape[1]
    rows = B_GROUP * tq
    lane = lax.broadcasted_iota(jnp.int32, (1, LANES), 1)
    low = lane < 64
    lane2 = lax.broadcasted_iota(jnp.int32, (1, 2 * LANES), 1)
    head_masks = [((lane2 // 64) == g).astype(BF16) for g in range(B_GROUP)]
    gw = B_GROUP * B_HEAD_DIM
    v = v_ref[0]
    v_ext = jnp.concatenate([v, jnp.ones_like(v)], axis=1)
    finishers = []
    for n in range(B_KV_HEADS):
        q = q_ref[0, :, n * gw:(n + 1) * gw]
        qs = jnp.concatenate([q * hm for hm in head_masks], axis=0)
        k2 = k_ref[0, :, n * LANES:(n + 1) * LANES]
        s = _nt(qs, jnp.concatenate([k2, k2], axis=1))
        sink = jnp.concatenate(
            [jnp.full((tq, 1), sink_ref[n * B_GROUP + g], F32) for g in range(B_GROUP)], axis=0)
        top = jnp.max(s, keepdims=True)
        for g in range(B_GROUP):
            top = jnp.maximum(top, sink_ref[n * B_GROUP + g])

        def finish(mp, n=n, s=s, sink=sink):
            acc = jnp.dot(jnp.exp(s - mp).astype(BF16), v_ext, preferred_element_type=F32)
            den = acc[:, LANES:2 * LANES] + jnp.exp(sink - mp)
            o = acc[:, 0:LANES] / den
            osw = pltpu.roll(o, 64, 1)
            for j in range(B_GROUP // 2):
                ra = slice((2 * j) * tq, (2 * j + 1) * tq)
                rb = slice((2 * j + 1) * tq, (2 * j + 2) * tq)
                pair = jnp.where(low, o[ra], osw[rb]) if n == 0 else jnp.where(low, osw[ra], o[rb])
                o_ref[0, :, n * gw + j * LANES:n * gw + (j + 1) * LANES] = pair.astype(BF16)
            return jnp.min(den)

        finishers.append((finish, s, sink, finish(top)))

    smallest = functools.reduce(jnp.minimum, [f[3] for f in finishers])

    @pl.when(jnp.logical_not(smallest >= 1e-30))
    def _():
        for finish, s, sink, _ in finishers:
            finish(jnp.maximum(jnp.max(s, axis=-1, keepdims=True), sink))


def _swa_attention(sink, q, k2, v):
    b, s, w = q.shape
    seq = lambda a: pl.BlockSpec((1, s, a.shape[2]), lambda bi: (bi, 0, 0))
    return pl.pallas_call(
        _swa_kernel,
        grid=(b,),
        in_specs=[pl.BlockSpec(memory_space=pltpu.SMEM), seq(q), seq(k2), seq(v)],
        out_specs=seq(q),
        out_shape=jax.ShapeDtypeStruct((b, s, w), BF16),
        compiler_params=pltpu.CompilerParams(dimension_semantics=("arbitrary",)),
        name="swa_attn_ctx",
    )(sink, q, k2, v)


def _swa_lat_kernel(sink_ref, q_ref, kc_ref, kl_ref, km_ref, kr_ref, vc_ref, vl_ref, vm_ref, vr_ref, o_ref,
                    s_a, m_a, s_b, m_b, *, nqb):
    t = pl.program_id(0)
    n_units = pl.num_programs(0) - 1
    tq = q_ref.shape[1]
    gw = B_GROUP * B_HEAD_DIM
    i_cur = jnp.minimum(t, n_units - 1) % nqb
    nkc = kc_ref.shape[1]
    rows = B_GROUP * tq

    @pl.when(t == 0)
    def _():
        s_b[...] = jnp.zeros_like(s_b)
        m_b[...] = jnp.zeros_like(m_b)

    def sink_col(n):
        return jnp.concatenate(
            [jnp.full((tq, 1), sink_ref[n * B_GROUP + g], F32) for g in range(B_GROUP)], axis=0)

    def body(s_w, m_w, s_r, m_r):
        lane = lax.broadcasted_iota(jnp.int32, (1, LANES), 1)
        low = lane < 64
        lane2 = lax.broadcasted_iota(jnp.int32, (1, 2 * LANES), 1)
        head_masks = [((lane2 // 64) == g).astype(BF16) for g in range(B_GROUP)]
        qi = lax.broadcasted_iota(jnp.int32, (rows, SWA_Q), 0) & (tq - 1)
        kj = lax.broadcasted_iota(jnp.int32, (rows, SWA_Q), 1)
        far = 2 * SWA_Q
        left_ok = kj >= qi + jnp.where(i_cur > 0, 0, far)
        right_ok = kj <= qi - jnp.where(i_cur < nqb - 1, 0, far)

        kc = kc_ref[0]
        kc_sw = pltpu.roll(kc, 64, 1)
        for n in range(B_KV_HEADS):
            q = q_ref[0, :, n * gw:(n + 1) * gw]
            qs = jnp.concatenate([q * hm for hm in head_masks], axis=0)
            kc2 = (jnp.where(low, kc, kc_sw) if n == 0 else jnp.where(low, kc_sw, kc)).astype(BF16)
            ks = [kc2] + [r[0, :, n * LANES:(n + 1) * LANES] for r in (kl_ref, km_ref, kr_ref)]
            k_all = jnp.concatenate([jnp.concatenate([k, k], axis=1) for k in ks], axis=0)
            s = _nt(qs, k_all)
            chunks = [s[:, c:c + LANES] for c in range(0, s.shape[1], LANES)]
            il = nkc // LANES
            chunks[il] = jnp.where(left_ok, chunks[il], NEG_INF)
            chunks[il + 2] = jnp.where(right_ok, chunks[il + 2], NEG_INF)
            top = jnp.max(functools.reduce(jnp.maximum, chunks), keepdims=True)
            for g in range(B_GROUP):
                top = jnp.maximum(top, sink_ref[n * B_GROUP + g])
            for c, ch in enumerate(chunks):
                s_w[n * rows:(n + 1) * rows, c * LANES:(c + 1) * LANES] = ch
            m_w[n * rows:(n + 1) * rows, :] = jnp.broadcast_to(top, (rows, LANES))

        vc = vc_ref[0]
        vc_sw = pltpu.roll(vc, 64, 1)
        nk = nkc + 3 * tq
        finishers = []
        for n in range(B_KV_HEADS):
            vc2 =(jnp.where(low, vc, vc_sw) if n == 0 else jnp.where(low, vc_sw, vc)).astype(BF16)
            v_all = jnp.concatenate(
                [vc2] + [r[0, :, n * LANES:(n + 1) * LANES] for r in (vl_ref, vm_ref, vr_ref)], axis=0)
            v_ext = jnp.concatenate([v_all, jnp.ones_like(v_all)], axis=1)
            sink = sink_col(n)

            def finish(mp, n=n, v_ext=v_ext, sink=sink):
                p = jnp.concatenate([jnp.exp(s_r[n * rows:(n + 1) * rows, c:c + LANES] - mp).astype(BF16)
                                     for c in range(0, nk, LANES)], axis=1)
                acc = jnp.dot(p, v_ext, preferred_element_type=F32)
                den = acc[:, LANES:2 * LANES] + jnp.exp(sink - mp)
                o = acc[:, 0:LANES] / den
                for j in range(B_GROUP // 2):
                    pair = jnp.where(low, o[(2 * j) * tq:(2 * j + 1) * tq], o[(2 * j + 1) * tq:(2 * j + 2) * tq])
                    o_ref[0, :, n * gw + j * LANES:n * gw + (j + 1) * LANES] = pair.astype(BF16)
                return jnp.min(den)

            finishers.append((n, finish, sink, finish(m_r[n * rows:(n + 1) * rows, :])))

        smallest = functools.reduce(jnp.minimum, [f[3] for f in finishers])

        @pl.when(jnp.logical_not(smallest >= 1e-30))
        def _():
            for n, finish, sink, _ in finishers:
                row_max = functools.reduce(
                    jnp.maximum, [s_r[n * rows:(n + 1) * rows, c:c + LANES] for c in range(0, nk, LANES)])
                mx = jnp.maximum(jnp.max(row_max, axis=-1, keepdims=True), sink)
                finish(jnp.broadcast_to(mx, (rows, LANES)))

    @pl.when(t % 2 == 0)
    def _():
        body(s_a, m_a, s_b, m_b)

    @pl.when(t % 2 == 1)
    def _():
        body(s_b, m_b, s_a, m_a)


def _swa_attention_lat(sink, q, kc, k2, vc, v):
    b, s, w = q.shape
    past = kc.shape[1]
    tq = SWA_Q
    nqb = s // tq
    n_units = b * nqb
    last = n_units - 1
    nk = past + 3 * tq

    def cur(t):
        u = jnp.minimum(t, last)
        return u // nqb, u % nqb

    def prev(t):
        u = jnp.maximum(t - 1, 0)
        return u // nqb, u % nqb

    lo = lambda i: jnp.maximum(i - 1, 0)
    hi = lambda i: jnp.minimum(i + 1, nqb - 1)
    kspec = lambda f: pl.BlockSpec((1, tq, 2 * LANES), lambda t: (cur(t)[0], f(cur(t)[1]), 0))
    vspec = lambda f: pl.BlockSpec((1, tq, 2 * LANES), lambda t: (prev(t)[0], f(prev(t)[1]), 0))
    same = lambda i: i
    rows = B_KV_HEADS * B_GROUP * tq
    return pl.pallas_call(
        functools.partial(_swa_lat_kernel, nqb=nqb),
        grid=(n_units + 1,),
        in_specs=[
            pl.BlockSpec(memory_space=pltpu.SMEM),
            pl.BlockSpec((1, tq, w), lambda t: (cur(t)[0], cur(t)[1], 0)),
            pl.BlockSpec((1, past, LANES), lambda t: (cur(t)[0], 0, 0)),
            kspec(lo), kspec(same), kspec(hi),
            pl.BlockSpec((1, past, LANES), lambda t: (prev(t)[0], 0, 0)),
            vspec(lo), vspec(same), vspec(hi),
        ],
        out_specs=pl.BlockSpec((1, tq, w), lambda t: (prev(t)[0], prev(t)[1], 0)),
        out_shape=jax.ShapeDtypeStruct((b, s, w), BF16),
        scratch_shapes=[pltpu.VMEM((rows, nk), F32), pltpu.VMEM((rows, LANES), F32),
                        pltpu.VMEM((rows, nk), F32), pltpu.VMEM((rows, LANES), F32)],
        compiler_params=pltpu.CompilerParams(dimension_semantics=("arbitrary",)),
        name="swa_attn_lat",
    )(sink, q, kc, k2, k2, k2, vc, v, v, v)


def _postmix_kernel(*refs, n_ctx_tiles):
    (xc, xl, oac, oal, obc, obl, sgac, sgal, sgbc, sgbl, g1_ref, sh2_ref, sc2_ref, gpm_ref, gpf_ref,
     wpa_ref, wpb_ref, wo_ref, wr_ref, br_ref, x1_o, h2_o, route_o, cnt_o, x1_a, x1_b) = refs
    t = pl.program_id(0)
    is_ctx = t < n_ctx_tiles
    pick = lambda a, b: jnp.where(is_ctx, a[...], b[...])

    @pl.when(t == 0)
    def _():
        x1_b[...] = jnp.zeros_like(x1_b)

    def body(x1_w, x1_r):
        pa = jnp.dot(pick(oac, oal), wpa_ref[...], preferred_element_type=F32)
        pb = jnp.dot(pick(obc, obl), wpb_ref[...], preferred_element_type=F32)
        mix = pick(sgac, sgal).astype(F32) * pa + pick(sgbc, sgbl).astype(F32) * pb
        m2 = jnp.dot(mix.astype(BF16), wo_ref[...], preferred_element_type=F32)
        x1 = pick(xc, xl) + g1_ref[0] * _rms(m2, gpm_ref[...])
        x1_o[...] = x1
        x1_w[...] = x1
        _postmix_route(x1_r[...], sh2_ref, sc2_ref, gpf_ref, wr_ref, br_ref, h2_o, route_o, cnt_o)

    @pl.when(t % 2 == 0)
    def _():
        body(x1_a, x1_b)

    @pl.when(t % 2 == 1)
    def _():
        body(x1_b, x1_a)


def _postmix_route(x1, sh2_ref, sc2_ref, gpf_ref, wr_ref, br_ref, h2_o, route_o, cnt_o):
    h2 = _rms(x1, gpf_ref[...]) * (1.0 + sc2_ref[0]) + sh2_ref[0]
    h2_o[...] = h2.astype(BF16)

    h_hi = h2.astype(BF16)
    h_lo = (h2 - h_hi.astype(F32)).astype(BF16)
    both = jnp.dot(h_hi, wr_ref[...], preferred_element_type=F32)
    logits = (both[:, 0:LANES] + both[:, LANES:2 * LANES]
              + jnp.dot(h_lo, wr_ref[:, 0:LANES], preferred_element_type=F32) + br_ref[...])
    tm = logits.shape[0]
    lt = logits.T
    row = lax.broadcasted_iota(jnp.int32, (EXPERTS_PER_GROUP, tm), 0).astype(F32)
    none = float(EXPERTS_PER_GROUP)
    lg = jnp.where(row < N_GROUPS, lt[N_EXPERTS:N_EXPERTS + EXPERTS_PER_GROUP], -jnp.inf)
    mg = jnp.max(lg, axis=0, keepdims=True)
    g_sel = jnp.min(jnp.where(lg == mg, row, none), axis=0, keepdims=True)
    g_w = 1.0 / jnp.sum(jnp.exp(lg - mg), axis=0, keepdims=True)
    le = lt[0:EXPERTS_PER_GROUP]
    for g in range(1, N_GROUPS):
        le = jnp.where(g_sel == g, lt[g * EXPERTS_PER_GROUP:(g + 1) * EXPERTS_PER_GROUP], le)
    v0 = jnp.max(le, axis=0, keepdims=True)
    i0 = jnp.min(jnp.where(le == v0, row, none), axis=0, keepdims=True)
    le1 = jnp.where(row == i0, -jnp.inf, le)
    v1 = jnp.max(le1, axis=0, keepdims=True)
    i1 = jnp.min(jnp.where(le1 == v1, row, none), axis=0, keepdims=True)
    e = jnp.exp(v1 - v0)
    w0 = g_w / (1.0 + e)
    w1 = g_w * e / (1.0 + e)
    e0 = g_sel * EXPERTS_PER_GROUP + i0
    e1 = g_sel * EXPERTS_PER_GROUP + i1
    rt = jnp.where(row == 0, e0, jnp.where(row == 1, e1, jnp.where(row == 2, w0, jnp.where(row == 3, w1, 0.0))))
    rt = jnp.concatenate([rt, jnp.zeros((LANES - EXPERTS_PER_GROUP, tm), F32)], axis=0)
    route = rt.T
    route_o[...] = route
    lane = lax.broadcasted_iota(jnp.int32, (tm, LANES), 1).astype(F32)
    cnt_o[0] = jnp.sum((lane == route[:, 0:1]).astype(F32) + (lane == route[:, 1:2]).astype(F32),
                       axis=0, keepdims=True)


def _postmix(ctx_in, lat_in, mod3, gpm, gpf, wpa, wpb, wo, wr, br, *, lat_seq, tm):
    t_ctx, d = ctx_in[0].shape
    t_lat = lat_in[0].shape[0]
    assert t_ctx % tm == 0 and lat_seq % tm == 0
    nc = t_ctx // tm
    nl = t_lat // tm
    per = lat_seq // tm
    sub = d // LANES
    t_all = t_ctx + t_lat

    last = nc + nl - 1
    cur = lambda i: jnp.minimum(i, last)
    prev = lambda i: jnp.maximum(i - 1, 0)
    mod_row = lambda j: jnp.where(j < nc, 0, 1 + jnp.maximum(j - nc, 0) // per)
    full = lambda a: pl.BlockSpec(a.shape, lambda i: (0,) * a.ndim)
    in_specs, args = [], []
    for a_c, a_l in zip(ctx_in, lat_in):
        w = a_c.shape[1]
        in_specs += [pl.BlockSpec((tm, w), lambda i: (jnp.minimum(cur(i), nc - 1), 0)),
                     pl.BlockSpec((tm, w), lambda i: (jnp.maximum(cur(i) - nc, 0), 0))]
        args += [a_c, a_l]
    in_specs += [pl.BlockSpec((1, 1, d), lambda i: (mod_row(cur(i)), 0, 2)),
                 pl.BlockSpec((1, 1, d), lambda i: (mod_row(prev(i)), 0, 3)),
                 pl.BlockSpec((1, 1, d), lambda i: (mod_row(prev(i)), 0, 4)),
                 full(gpm), full(gpf), full(wpa), full(wpb), full(wo), full(wr), full(br)]
    args += [mod3, mod3, mod3, gpm, gpf, wpa, wpb, wo, wr, br]
    return pl.pallas_call(
        functools.partial(_postmix_kernel, n_ctx_tiles=nc),
        grid=(nc + nl + 1,),
        in_specs=in_specs,
        out_specs=[pl.BlockSpec((tm, d), lambda i: (cur(i), 0)),
                   pl.BlockSpec((tm, d), lambda i: (prev(i), 0)),
                   pl.BlockSpec((tm, LANES), lambda i: (prev(i), 0)),
                   pl.BlockSpec((1, 1, LANES), lambda i: (prev(i), 0, 0))],
        out_shape=[jax.ShapeDtypeStruct((t_all, d), F32),
                   jax.ShapeDtypeStruct((t_all, d), BF16),
                   jax.ShapeDtypeStruct((t_all, LANES), F32),
                   jax.ShapeDtypeStruct((nc + nl, 1, LANES), F32)],
        scratch_shapes=[pltpu.VMEM((tm, d), F32), pltpu.VMEM((tm, d), F32)],
        compiler_params=pltpu.CompilerParams(dimension_semantics=("arbitrary",)),
        name="postmix",
    )(*args)


def _segment_copies(src, src_row, dst, dst_row, n, sub, sem):
    @pl.when(n > 0)
    def _():
        pltpu.make_async_copy(src.at[pl.ds(pl.multiple_of(src_row * sub, sub), n * sub)],
                              dst.at[pl.ds(pl.multiple_of(dst_row * sub, sub), n * sub)], sem).start()


def _local_positions(route, tile_base):
    tm = route.shape[0]
    lane = lax.broadcasted_iota(jnp.int32, (tm, LANES), 1).astype(F32)
    is0 = lane == route[:, 0:1]
    is1 = lane == route[:, 1:2]
    earlier = (lax.broadcasted_iota(jnp.int32, (tm, tm), 1)
               < lax.broadcasted_iota(jnp.int32, (tm, tm), 0)).astype(BF16)
    pre0 = jnp.dot(earlier, is0.astype(BF16), preferred_element_type=F32)
    pre1 = jnp.dot(earlier, is1.astype(BF16), preferred_element_type=F32)
    cnt0 = jnp.sum(is0.astype(F32), axis=0, keepdims=True)
    lpos0 = jnp.sum(jnp.where(is0, tile_base + pre0, 0.0), axis=-1, keepdims=True)
    lpos1 = jnp.sum(jnp.where(is1, tile_base + cnt0 + pre1, 0.0), axis=-1, keepdims=True)
    return lpos0.astype(jnp.int32), lpos1.astype(jnp.int32)


def _dispatch_kernel(ss_ref, sl_ref, tb_ref, h_ref, r_ref, tbv_ref, xs_hbm, pbuf, zbuf, sem, zsem, *, n_asg, rows):
    i = pl.program_id(0)
    nt = pl.num_programs(0)
    tm, d = h_ref.shape
    sub = d // LANES
    nrow = TOP_K * tm
    slot = i % 2

    def wait_slot(s):
        pltpu.make_async_copy(pbuf.at[pl.ds(pl.multiple_of(s * nrow * sub, nrow * sub), nrow * sub)],
                              xs_hbm.at[pl.ds(0, nrow * sub)], sem.at[s]).wait()

    def slack_copy():
        return pltpu.make_async_copy(zbuf, xs_hbm.at[pl.ds(n_asg * sub, rows * sub)], zsem.at[0])

    @pl.when(i == 0)
    def _():
        zbuf[...] = jnp.zeros_like(zbuf)
        slack_copy().start()

    lpos0, lpos1 = _local_positions(r_ref[...], tbv_ref[0])
    p = lax.broadcasted_iota(jnp.int32, (tm, nrow), 1)
    sel = ((p == lpos0) | (p == lpos1)).astype(BF16)
    xp = lax.dot_general(sel, h_ref[...].astype(BF16), (((0,), (0,)), ((), ())),
                         preferred_element_type=F32)

    @pl.when(i >= 2)
    def _():
        wait_slot(slot)

    _store_row_tiles(pbuf, slot * nrow * sub, xp)

    def seg(e, c):
        k = i * N_EXPERTS + e
        _segment_copies(pbuf, slot * nrow + tb_ref[k], xs_hbm, ss_ref[k], sl_ref[k], sub, sem.at[slot])
        return c
    lax.fori_loop(0, N_EXPERTS, seg, 0)

    @pl.when(i == nt - 1)
    def _():
        wait_slot(slot)

        @pl.when(nt >= 2)
        def _():
            wait_slot(1 - slot)
        slack_copy().wait()


def _dispatch(h2, route, tables, *, tm, rows):
    seg_start, seg_len, tile_base, tile_base_v = tables
    t, d = h2.shape
    sub = d // LANES
    n_asg = t * TOP_K
    grid_spec = pltpu.PrefetchScalarGridSpec(
        num_scalar_prefetch=3,
        grid=(t // tm,),
        in_specs=[
            pl.BlockSpec((tm, d), lambda i, *_: (i, 0)),
            pl.BlockSpec((tm, LANES), lambda i, *_: (i, 0)),
            pl.BlockSpec((1, 1, LANES), lambda i, *_: (i, 0, 0)),
        ],
        out_specs=pl.BlockSpec(memory_space=pl.ANY),
        scratch_shapes=[pltpu.VMEM((2 * TOP_K * tm * sub, LANES), F32), pltpu.VMEM((rows * sub, LANES), F32),
                        pltpu.SemaphoreType.DMA((2,)), pltpu.SemaphoreType.DMA((1,))],
    )
    return pl.pallas_call(
        functools.partial(_dispatch_kernel, n_asg=n_asg, rows=rows),
        grid_spec=grid_spec,
        out_shape=jax.ShapeDtypeStruct(((n_asg + rows) * sub, LANES), F32),
        compiler_params=pltpu.CompilerParams(dimension_semantics=("arbitrary",)),
        name="dispatch",
    )(seg_start, seg_len, tile_base, h2, route, tile_base_v)


def _moe_kernel(be_ref, row0_ref, nact_ref, par_ref, nxt_ref, xs_hbm, w1_hbm, w3_hbm, w2_hbm, ys_hbm,
                xbuf, obuf, wf1, wf3, wf2, w1b, w3b, w2b, rsem, wsem, gsem, *, rows, sub):
    i = pl.program_id(0)
    nact = nact_ref[0]
    slot = i % 2
    nslot = 1 - slot
    blk = rows * sub

    def weight_copies(e, s):
        return [pltpu.make_async_copy(w_hbm.at[e], wf.at[s], gsem.at[s])
                for w_hbm, wf in ((w1_hbm, wf1), (w3_hbm, wf3), (w2_hbm, wf2))]

    def read(j, s):
        return pltpu.make_async_copy(xs_hbm.at[pl.ds(pl.multiple_of(row0_ref[j] * sub, sub), blk)],
                                     xbuf.at[pl.ds(pl.multiple_of(s * blk, blk), blk)], rsem.at[s])

    def write(j, s):
        return pltpu.make_async_copy(obuf.at[pl.ds(pl.multiple_of(s * blk, blk), blk)],
                                     ys_hbm.at[pl.ds(pl.multiple_of(row0_ref[j] * sub, sub), blk)], wsem.at[s])

    @pl.when(i == 0)
    def _():
        read(0, 0).start()
        for c in weight_copies(be_ref[0], 0):
            c.start()

    @pl.when(i < nact)
    def _():
        @pl.when(i + 1 < nact)
        def _():
            read(i + 1, nslot).start()

        changed = jnp.logical_or(i == 0, be_ref[i] != be_ref[jnp.maximum(i - 1, 0)])

        @pl.when(changed)
        def _():
            s = par_ref[i]
            for c in weight_copies(be_ref[i], s):
                c.wait()
            w1b[...] = wf1[s].astype(BF16)
            w3b[...] = wf3[s].astype(BF16)
            w2b[...] = wf2[s].astype(BF16)

            @pl.when(nxt_ref[i] >= 0)
            def _():
                for c in weight_copies(nxt_ref[i], 1 - s):
                    c.start()

        read(i, slot).wait()
        x = _load_row_tiles(xbuf, slot * blk, rows, sub).astype(BF16)
        a = jnp.dot(x, w1b[...], preferred_element_type=F32)
        b = jnp.dot(x, w3b[...], preferred_element_type=F32)
        hmid = (a * jax.nn.sigmoid(a) * b).astype(BF16)
        y = jnp.dot(hmid, w2b[...], preferred_element_type=F32)
        _store_row_tiles(obuf, slot * blk, y)

        @pl.when(i >= 1)
        def _():
            write(i - 1, nslot).wait()
        write(i, slot).start()

    @pl.when(i == nact)
    def _():
        write(i - 1, nslot).wait()
        obuf[pl.ds(pl.multiple_of(slot * blk, blk), blk), :] = jnp.zeros((blk, LANES), F32)
        tail = pltpu.make_async_copy(obuf.at[pl.ds(pl.multiple_of(slot * blk, blk), blk)],
                                     ys_hbm.at[pl.ds(ys_hbm.shape[0] - blk, blk)], wsem.at[slot])
        tail.start()
        tail.wait()


def _moe(blk_tables, xs, w1, w3, w2):
    blk_expert, row0, nact, parity, nxt = blk_tables
    nblk = blk_expert.shape[0] - 1
    d, de = w1.shape[1], w1.shape[2]
    sub = d // LANES
    rows = MOE_ROWS
    anyspec = pl.BlockSpec(memory_space=pl.ANY)
    grid_spec = pltpu.PrefetchScalarGridSpec(
        num_scalar_prefetch=5,
        grid=(nblk + 1,),
        in_specs=[anyspec, anyspec, anyspec, anyspec],
        out_specs=anyspec,
        scratch_shapes=[
            pltpu.VMEM((2 * rows * sub, LANES), F32),
            pltpu.VMEM((2 * rows * sub, LANES), F32),
            pltpu.VMEM((2, d, de), F32),
            pltpu.VMEM((2, d, de), F32),
            pltpu.VMEM((2, de, d), F32),
            pltpu.VMEM((d, de), BF16),
            pltpu.VMEM((d, de), BF16),
            pltpu.VMEM((de, d), BF16),
            pltpu.SemaphoreType.DMA((2,)),
            pltpu.SemaphoreType.DMA((2,)),
            pltpu.SemaphoreType.DMA((2,)),
        ],
    )
    return pl.pallas_call(
        functools.partial(_moe_kernel, rows=rows, sub=sub),
        grid_spec=grid_spec,
        out_shape=jax.ShapeDtypeStruct(xs.shape, F32),
        compiler_params=pltpu.CompilerParams(dimension_semantics=("arbitrary",)),
        name="expert_mlp",
    )(blk_expert, row0, nact, parity, nxt, xs, w1, w3, w2)


def _combine_kernel(ss_ref, sl_ref, tb_ref, x1_ref, r_ref, tbv_ref, g2_ref, gpost_ref, ys_hbm, o_ref,
                    ybuf, sem, *, tile0):
    i = pl.program_id(0)
    nt = pl.num_programs(0)
    tm, d = x1_ref.shape
    sub = d // LANES
    nrow = TOP_K * tm
    slot = i % 2

    def fetch(tile, s):
        def seg(e, c):
            k = tile * N_EXPERTS + e
            _segment_copies(ys_hbm, ss_ref[k], ybuf, s * nrow + tb_ref[k], sl_ref[k], sub, sem.at[s])
            return c
        lax.fori_loop(0, N_EXPERTS, seg, 0)

    @pl.when(i == 0)
    def _():
        fetch(tile0, 0)

    @pl.when(i + 1 < nt)
    def _():
        fetch(tile0 + i + 1, 1 - slot)

    pltpu.make_async_copy(ys_hbm.at[pl.ds(0, nrow * sub)],
                          ybuf.at[pl.ds(pl.multiple_of(slot * nrow * sub, nrow * sub), nrow * sub)],
                          sem.at[slot]).wait()
    r = r_ref[...]
    lpos0, lpos1 = _local_positions(r, tbv_ref[0])
    p = lax.broadcasted_iota(jnp.int32, (tm, nrow), 1)
    q = (jnp.where(p == lpos0, r[:, 2:3], 0.0) + jnp.where(p == lpos1, r[:, 3:4], 0.0)).astype(BF16)
    ysort = _load_row_tiles(ybuf, slot * nrow * sub, nrow, sub).astype(BF16)
    y = jnp.dot(q, ysort, preferred_element_type=F32)
    o_ref[...] = x1_ref[...] + g2_ref[0] * _rms(y, gpost_ref[...])


def _combine(x1, ys, route, tables, mod3, mod_row0, gpost, *, t, seq, tm, tok_off, is_ctx):
    seg_start, seg_len, tile_base, tile_base_v = tables
    t_all, d = x1.shape
    per = seq // tm
    sub = d // LANES
    boff = tok_off // tm

    def row(i):
        return (i // per) if not is_ctx else 0

    grid_spec = pltpu.PrefetchScalarGridSpec(
        num_scalar_prefetch=3,
        grid=(t // tm,),
        in_specs=[
            pl.BlockSpec((tm, d), lambda i, *_: (boff + i, 0)),
            pl.BlockSpec((tm, LANES), lambda i, *_: (boff + i, 0)),
            pl.BlockSpec((1, 1, LANES), lambda i, *_: (boff + i, 0, 0)),
            pl.BlockSpec((1, 1, d), lambda i, *_: (mod_row0 + row(i), 0, 5)),
            pl.BlockSpec((1, d), lambda i, *_: (0, 0)),
            pl.BlockSpec(memory_space=pl.ANY),
        ],
        out_specs=pl.BlockSpec((tm, d), lambda i, *_: (i, 0)),
        scratch_shapes=[pltpu.VMEM((2 * TOP_K * tm * sub, LANES), F32), pltpu.SemaphoreType.DMA((2,))],
    )
    return pl.pallas_call(
        functools.partial(_combine_kernel, tile0=boff),
        grid_spec=grid_spec,
        out_shape=jax.ShapeDtypeStruct((t, d), F32),
        compiler_params=pltpu.CompilerParams(dimension_semantics=("arbitrary",)),
        name="combine_ctx" if is_ctx else "combine_lat",
    )(seg_start, seg_len, tile_base, x1, route, tile_base_v, mod3, gpost, ys)


def _routing_tables(counts, tm, rows):
    nt = counts.shape[0]
    n_asg = nt * tm * TOP_K
    ex = jnp.arange(N_EXPERTS, dtype=jnp.int32)
    cnt_te = counts[:, 0, 0:N_EXPERTS].astype(jnp.int32)
    cnt_e = jnp.sum(cnt_te, axis=0)
    start_e = jnp.cumsum(cnt_e) - cnt_e
    seg_start = start_e[None, :] + jnp.cumsum(cnt_te, axis=0) - cnt_te
    tile_base = jnp.cumsum(cnt_te, axis=1) - cnt_te
    tile_base_v = jnp.pad(tile_base.astype(F32), ((0, 0), (0, LANES - N_EXPERTS))).reshape(nt, 1, LANES)

    nblk_e = (cnt_e + rows - 1) // rows
    blk_end = jnp.cumsum(nblk_e)
    blk_start = blk_end - nblk_e
    n_blocks = n_asg // rows + N_EXPERTS
    b = jnp.arange(n_blocks + 1, dtype=jnp.int32)
    be = jnp.minimum(jnp.sum((blk_end[None, :] <= b[:, None]).astype(jnp.int32), axis=1), N_EXPERTS - 1)
    first = jnp.sum(jnp.where(be[:, None] == ex[None, :], (start_e - blk_start * rows)[None, :], 0), axis=1)
    row0 = jnp.clip(first + b * rows, 0, n_asg)
    nact = blk_end[-1:].astype(jnp.int32)
    used = cnt_e > 0
    parity_e = (jnp.cumsum(used.astype(jnp.int32)) - 1) % 2
    later = (ex[None, :] > ex[:, None]) & used[None, :]
    nxt_e = jnp.min(jnp.where(later, ex[None, :], N_EXPERTS), axis=1)
    nxt_e = jnp.where(nxt_e == N_EXPERTS, -1, nxt_e)
    pick = lambda tab: jnp.sum(jnp.where(be[:, None] == ex[None, :], tab[None, :], 0), axis=1).astype(jnp.int32)
    seg = (seg_start.reshape(-1).astype(jnp.int32), cnt_te.reshape(-1).astype(jnp.int32),
           tile_base.reshape(-1).astype(jnp.int32), tile_base_v)
    blk = (be.astype(jnp.int32), row0.astype(jnp.int32), nact, pick(parity_e), pick(nxt_e))
    return seg, blk


def _rope_tables(n_tok):
    n_rows = n_tok // GRID_W
    rows = jnp.repeat(jnp.arange(n_rows), GRID_W).astype(F32)
    cols = jnp.tile(jnp.arange(GRID_W), n_rows).astype(F32)
    quarter = A_HEAD_DIM // 4
    inv = ROPE_BASE ** (-jnp.arange(quarter, dtype=F32) / quarter)
    ang = jnp.concatenate([rows[:, None] * inv, cols[:, None] * inv], axis=-1)
    cos, sin = jnp.cos(ang), jnp.sin(ang)
    cos_t = jnp.tile(jnp.concatenate([cos, cos], axis=-1), (1, LANES // A_HEAD_DIM))
    sin_t = jnp.tile(jnp.concatenate([-sin, sin], axis=-1), (1, LANES // A_HEAD_DIM))
    return cos_t, sin_t


def kernel(x_prompt, x_sample, c, cache_diff_k, cache_diff_v, cache_swa_k, cache_swa_v, c_ctx, w_ada, b_ada, g_pre_mix, g_post_mix, g_pre_ffn, g_post_ffn, w_in, lam_q1, lam_k1, lam_q2, lam_k2, g_diff_head, sink, w_proj_a, w_proj_b, w_out, w_router_group, b_router_group, w_router_expert, b_router_expert, w_e1, w_e3, w_e2):
    depth = w_in.shape[0]
    assert depth == 1
    l = 0
    bp, sp, d = x_prompt.shape
    bs, ss, _ = x_sample.shape
    lambda_init = 0.8 - 0.6 * math.exp(-0.3 * l)
    assert A_HEAD_DIM == B_HEAD_DIM and ss % GRID_W == 0 and bs + 1 <= MOD_ROWS

    c_all = jnp.concatenate([c_ctx[None, :], c, jnp.zeros((MOD_ROWS - 1 - bs, d), F32)], axis=0)
    mod = _modulation(c_all, w_ada[l], b_ada[l][None, :])
    mod3 = mod.reshape(MOD_ROWS, 1, 6 * d)

    w_in_b = w_in[l].astype(BF16)
    wpa = w_proj_a[l].astype(BF16)
    wpb = w_proj_b[l].astype(BF16)
    wo = w_out[l].astype(BF16)
    n_r = N_GROUPS + N_EXPERTS
    wr = jnp.concatenate([w_router_expert[l], w_router_group[l], jnp.zeros((d, LANES - n_r), F32)], axis=1)
    br = jnp.concatenate([b_router_expert[l], b_router_group[l], jnp.zeros((LANES - n_r,), F32)])[None, :]
    wr_hi = wr.astype(BF16)
    wr2 = jnp.concatenate([wr_hi, (wr - wr_hi.astype(F32)).astype(BF16)], axis=1)
    lam_p = jnp.stack([lam_q1[l], lam_k1[l], lam_q2[l], lam_k2[l]], axis=0)
    g_head = g_diff_head[l][None, :]
    sink_l = sink[l]
    cos_t, sin_t = _rope_tables(ss)

    xp2 = x_prompt.reshape(bp * sp, d)
    xs2 = x_sample.reshape(bs * ss, d)
    gpre = g_pre_mix[l][None, :]

    (qa_c, ka_c, va_c, qb_c, kb2_c, vb_c, sga_c, sgb_c, kaf, vaf, kbf, vbf) = _inproj(
        xp2, mod3, 0, gpre, cos_t, sin_t, w_in_b, seq=sp, tm=sp, is_ctx=True)
    r3 = lambda a, b_: a.reshape(b_, -1, a.shape[-1])
    oa_c = _diff_attention(lam_p, g_head, r3(qa_c, bp), r3(ka_c, bp), r3(va_c, bp), lambda_init=lambda_init)
    ob_c = _swa_attention(sink_l, r3(qb_c, bp), r3(kb2_c, bp), r3(vb_c, bp))

    (qa_s, ka_s, va_s, qb_s, kb2_s, vb_s, sga_s, sgb_s) = _inproj(
        xs2, mod3, 1, gpre, cos_t, sin_t, w_in_b, seq=ss, tm=512, is_ctx=False)
    past = cache_diff_k.shape[2]
    ck = cache_diff_k[:, l].reshape(bs, past, -1)
    cv = cache_diff_v[:, l].reshape(bs, past, -1)
    oa_s = _diff_attention_lat(lam_p, g_head, r3(qa_s, bs), ck, ka_s, cv, r3(va_s, bs),
                               tq=512, lambda_init=lambda_init)
    sk = cache_swa_k[:, l].reshape(bs, past, -1)
    sv = cache_swa_v[:, l].reshape(bs, past, -1)
    nqb = ss // SWA_Q
    kb2_3, vb_3 = r3(kb2_s, bs), r3(vb_s, bs)
    ob_s = _swa_attention_lat(sink_l, r3(qb_s, bs), sk, kb2_3, sv, vb_3)

    gpm = g_post_mix[l][None, :]
    gpf = g_pre_ffn[l][None, :]
    t_ctx, t_lat = bp * sp, bs * ss
    x1, h2t, route, counts = _postmix(
        (xp2, oa_c.reshape(t_ctx, -1), ob_c.reshape(t_ctx, -1), sga_c, sgb_c),
        (xs2, oa_s.reshape(t_lat, -1), ob_s.reshape(t_lat, -1), sga_s, sgb_s),
        mod3, gpm, gpf, wpa, wpb, wo, wr2, br, lat_seq=ss, tm=512)

    tables, blk_tables = _routing_tables(counts, MOE_TILE, MOE_ROWS)
    xs = _dispatch(h2t, route, tables, tm=MOE_TILE, rows=MOE_ROWS)
    ys = _moe(blk_tables, xs, w_e1[l], w_e3[l], w_e2[l])

    gpost = g_post_ffn[l][None, :]
    y_p = _combine(x1, ys, route, tables, mod3, 0, gpost, t=t_ctx, seq=sp, tm=MOE_TILE, tok_off=0, is_ctx=True)
    y_s = _combine(x1, ys, route, tables, mod3, 1, gpost, t=t_lat, seq=ss, tm=MOE_TILE, tok_off=t_ctx,
                   is_ctx=False)

    ha = A_HEADS
    return (y_p.reshape(bp, sp, d), y_s.reshape(bs, ss, d),
            kaf.reshape(bp, 1, sp, ha, 2, A_HEAD_DIM), vaf.reshape(bp, 1, sp, ha, A_V_DIM),
            kbf.reshape(bp, 1, sp, B_KV_HEADS, B_HEAD_DIM), vbf.reshape(bp, 1, sp, B_KV_HEADS, B_HEAD_DIM))
```

```python
import functools
import math

import jax
import jax.numpy as jnp
from jax import lax
from jax.experimental import pallas as pl
from jax.experimental.pallas import tpu as pltpu

F32 = jnp.float32
BF16 = jnp.bfloat16
HIGHEST = lax.Precision.HIGHEST

GRID_W = 64
ROPE_BASE = 10000.0
EPS = 1e-6
NEG_INF = -1e30
A_HEADS = 4
A_HEAD_DIM = 64
A_V_DIM = 2 * A_HEAD_DIM
B_HEADS = 8
B_KV_HEADS = 2
B_GROUP = B_HEADS // B_KV_HEADS
B_HEAD_DIM = 64
WINDOW = 128
N_GROUPS = 4
EXPERTS_PER_GROUP = 8
N_EXPERTS = N_GROUPS * EXPERTS_PER_GROUP
TOP_K = 2

LANES = 128
MOD_ROWS = 16
MOE_ROWS = 512
MOE_TILE = 512
SWA_Q = 128

_QA = 0
_KA = _QA + A_HEADS * 2 * A_HEAD_DIM
_VA = _KA + A_HEADS * 2 * A_HEAD_DIM
_QB = _VA + A_HEADS * A_V_DIM
_KB = _QB + B_HEADS * B_HEAD_DIM
_VB = _KB + B_KV_HEADS * B_HEAD_DIM
_GA = _VB + B_KV_HEADS * B_HEAD_DIM


def _rms(x, g):
    return x * lax.rsqrt(jnp.mean(x * x, axis=-1, keepdims=True) + EPS) * g


def _store_row_tiles(ref, base, val):
    sub = val.shape[1] // LANES
    for s in range(sub):
        ref[pl.ds(base + s, val.shape[0], stride=sub), :] = val[:, s * LANES:(s + 1) * LANES]


def _load_row_tiles(ref, base, n_rows, sub):
    return jnp.concatenate([ref[pl.ds(base + s, n_rows, stride=sub), :] for s in range(sub)], axis=1)


def _mod_kernel(c_ref, w_ref, b_ref, o_ref):
    c = c_ref[...]
    a = c * jax.nn.sigmoid(c)
    o_ref[...] = jnp.dot(a, w_ref[...], precision=HIGHEST, preferred_element_type=F32) + b_ref[...]


def _modulation(c_all, w_ada, b_ada):
    d, n = w_ada.shape
    tn = 512
    return pl.pallas_call(
        _mod_kernel,
        grid=(n // tn,),
        in_specs=[
            pl.BlockSpec((MOD_ROWS, d), lambda j: (0, 0)),
            pl.BlockSpec((d, tn), lambda j: (0, j)),
            pl.BlockSpec((1, tn), lambda j: (0, j)),
        ],
        out_specs=pl.BlockSpec((MOD_ROWS, tn), lambda j: (0, j)),
        out_shape=jax.ShapeDtypeStruct((MOD_ROWS, n), F32),
        name="modulation",
    )(c_all, w_ada, b_ada)


def _rope128(z, cos, sin_signed, first_half):
    rot = jnp.where(first_half, pltpu.roll(z, 96, 1), pltpu.roll(z, 32, 1))
    return z * cos + rot * sin_signed


def _inproj_kernel(x_ref, sh_ref, sc_ref, g_ref, cos_ref, sin_ref, w_ref, *outs, is_ctx):
    x = x_ref[...]
    h = _rms(x, g_ref[...]) * (1.0 + sc_ref[0]) + sh_ref[0]
    hb = h.astype(BF16)
    lane = lax.broadcasted_iota(jnp.int32, (1, LANES), 1)
    first_half = (lane % 64) < 32
    low = lane < 64

    def seg(lo, hi):
        return jnp.dot(hb, w_ref[:, lo:hi], preferred_element_type=F32)

    def rope(z):
        if is_ctx:
            return z
        cos = cos_ref[...]
        sin = sin_ref[...]
        parts = [_rope128(z[:, j:j + LANES], cos, sin, first_half) for j in range(0, z.shape[1], LANES)]
        return parts[0] if len(parts) == 1 else jnp.concatenate(parts, axis=1)

    if is_ctx:
        qa_o, ka_o, va_o, qb_o, kb2_o, vb_o, sga_o, sgb_o, kaf_o, vaf_o, kbf_o, vbf_o = outs
    else:
        qa_o, ka_o, va_o, qb_o, kb2_o, vb_o, sga_o, sgb_o = outs

    scale = A_HEAD_DIM ** -0.5
    qa_o[...] = (rope(seg(_QA, _KA)) * scale).astype(BF16)
    ka = rope(seg(_KA, _VA))
    if is_ctx:
        ka_o[...] = ka.astype(BF16)
    else:
        ka_o[0] = ka.T.astype(BF16)
    va = seg(_VA, _QB)
    va_o[...] = va.astype(BF16)
    qb_o[...] = (rope(seg(_QB, _KB)) * (B_HEAD_DIM ** -0.5)).astype(BF16)
    kb = rope(seg(_KB, _VB))
    kb_sw = pltpu.roll(kb, 64, 1)
    kb2_o[:, 0:LANES] = jnp.where(low, kb, kb_sw).astype(BF16)
    kb2_o[:, LANES:2 * LANES] = jnp.where(low, kb_sw, kb).astype(BF16)
    vb = seg(_VB, _GA)
    if is_ctx:
        vb_o[...] = vb.astype(BF16)
    else:
        vb_sw = pltpu.roll(vb, 64, 1)
        vb_o[:, 0:LANES] = jnp.where(low, vb, vb_sw).astype(BF16)
        vb_o[:, LANES:2 * LANES] = jnp.where(low, vb_sw, vb).astype(BF16)
    d = x.shape[1]
    sga_o[...] = jax.nn.sigmoid(seg(_GA, _GA + d)).astype(BF16)
    sgb_o[...] = jax.nn.sigmoid(seg(_GA + d, _GA + 2 * d)).astype(BF16)
    if is_ctx:
        kaf_o[...] = ka
        vaf_o[...] = va
        kbf_o[...] = kb
        vbf_o[...] = vb


def _inproj(x2, mod3, mod_row0, g_pre, cos_t, sin_t, w_in_b, *, seq, tm, is_ctx):
    t, d = x2.shape
    per = seq // tm
    n_in = w_in_b.shape[1]
    wa = A_HEADS * 2 * A_HEAD_DIM
    wkb = B_KV_HEADS * B_HEAD_DIM

    def row(i):
        return (i // per) if not is_ctx else 0

    tok = lambda w: pl.BlockSpec((tm, w), lambda i: (i, 0))
    out_shape = [
        jax.ShapeDtypeStruct((t, wa), BF16), jax.ShapeDtypeStruct((t, wa), BF16),
        jax.ShapeDtypeStruct((t, wa), BF16), jax.ShapeDtypeStruct((t, wa), BF16),
        jax.ShapeDtypeStruct((t, 2 * wkb), BF16), jax.ShapeDtypeStruct((t, wkb), BF16),
        jax.ShapeDtypeStruct((t, d), BF16), jax.ShapeDtypeStruct((t, d), BF16),
    ]
    out_specs = [tok(wa), tok(wa), tok(wa), tok(wa), tok(2 * wkb), tok(wkb), tok(d), tok(d)]
    if not is_ctx:
        out_shape[5] = jax.ShapeDtypeStruct((t, 2 * wkb), BF16)
        out_specs[5] = tok(2 * wkb)
        out_shape[1] = jax.ShapeDtypeStruct((t // seq, wa, seq), BF16)
        out_specs[1] = pl.BlockSpec((1, wa, tm), lambda i: (i // per, 0, i % per))
    if is_ctx:
        out_shape += [jax.ShapeDtypeStruct((t, wa), F32), jax.ShapeDtypeStruct((t, wa), F32),
                      jax.ShapeDtypeStruct((t, wkb), F32), jax.ShapeDtypeStruct((t, wkb), F32)]
        out_specs += [tok(wa), tok(wa), tok(wkb), tok(wkb)]
    return pl.pallas_call(
        functools.partial(_inproj_kernel, is_ctx=is_ctx),
        grid=(t // tm,),
        in_specs=[
            pl.BlockSpec((tm, d), lambda i: (i, 0)),
            pl.BlockSpec((1, 1, d), lambda i: (mod_row0 + row(i), 0, 0)),
            pl.BlockSpec((1, 1, d), lambda i: (mod_row0 + row(i), 0, 1)),
            pl.BlockSpec((1, d), lambda i: (0, 0)),
            pl.BlockSpec((tm, LANES), lambda i: (i % per, 0)),
            pl.BlockSpec((tm, LANES), lambda i: (i % per, 0)),
            pl.BlockSpec((d, n_in), lambda i: (0, 0)),
        ],
        out_specs=out_specs,
        out_shape=out_shape,
        compiler_params=pltpu.CompilerParams(dimension_semantics=("arbitrary",)),
        name="inproj_ctx" if is_ctx else "inproj_lat",
    )(x2, mod3, mod3, g_pre, cos_t, sin_t, w_in_b)


def _nt(a, b):
    return lax.dot_general(a, b, (((1,), (1,)), ((), ())), preferred_element_type=F32)


def _diff_kernel(lam_ref, g_ref, q_ref, k_ref, v_ref, o_ref, *, lambda_init):
    lp = lam_ref[...]
    lam = (jnp.exp(jnp.sum(lp[0:1] * lp[1:2], axis=-1, keepdims=True))
           - jnp.exp(jnp.sum(lp[2:3] * lp[3:4], axis=-1, keepdims=True)) + lambda_init)
    tq = q_ref.shape[1]
    lane = lax.broadcasted_iota(jnp.int32, (1, LANES), 1)
    for h in range(A_HEADS):
        cols = slice(h * LANES, (h + 1) * LANES)
        q = q_ref[0, :, cols]
        q2 = jnp.concatenate([q * (lane < 64).astype(BF16), q * (lane >= 64).astype(BF16)], axis=0)
        s = _nt(q2, k_ref[0, :, cols])
        mx = jnp.max(s, axis=-1, keepdims=True)
        v = v_ref[0, :, cols]
        v_ext = jnp.concatenate([v, jnp.ones_like(v)], axis=1)
        acc = jnp.dot(jnp.exp(s - mx).astype(BF16), v_ext, preferred_element_type=F32)
        on = acc[:, 0:LANES] / acc[:, LANES:2 * LANES]
        o = on[0:tq] - lam * on[tq:2 * tq]
        o_ref[0, :, cols] = (_rms(o, g_ref[...]) * (1.0 - lambda_init)).astype(BF16)


def _diff_attention(lam_p, g_head, q, k, v, *, lambda_init):
    b, s, w = q.shape
    seq = pl.BlockSpec((1, s, w), lambda bi: (bi, 0, 0))
    return pl.pallas_call(
        functools.partial(_diff_kernel, lambda_init=lambda_init),
        grid=(b,),
        in_specs=[pl.BlockSpec((4, A_HEAD_DIM), lambda bi: (0, 0)),
                  pl.BlockSpec((1, A_V_DIM), lambda bi: (0, 0)), seq, seq, seq],
        out_specs=seq,
        out_shape=jax.ShapeDtypeStruct((b, s, w), BF16),
        compiler_params=pltpu.CompilerParams(dimension_semantics=("arbitrary",)),
        name="diff_attn_ctx",
    )(lam_p, g_head, q, k, v)


def _diff_lat_kernel(lam_ref, g_ref, q_ref, kc_ref, kt_ref, vc_ref, v_ref, o_ref, s_a, m_a, s_b, m_b, *,
                     lambda_init):
    t = pl.program_id(0)
    tq = q_ref.shape[1]
    nkc = kc_ref.shape[1]
    nkn = kt_ref.shape[2]
    nk = nkc + nkn

    @pl.when(t == 0)
    def _():
        s_b[...] = jnp.zeros_like(s_b)
        m_b[...] = jnp.zeros_like(m_b)

    def body(s_w, m_w, s_r, m_r):
        lp = lam_ref[...]
        lam = (jnp.exp(jnp.sum(lp[0:1] * lp[1:2], axis=-1, keepdims=True))
               - jnp.exp(jnp.sum(lp[2:3] * lp[3:4], axis=-1, keepdims=True)) + lambda_init)
        lane = lax.broadcasted_iota(jnp.int32, (1, LANES), 1)

        q = q_ref[0]
        q2 = jnp.concatenate([q * (lane < 64).astype(BF16), q * (lane >= 64).astype(BF16)], axis=0)
        sc = _nt(q2, kc_ref[0].astype(BF16))
        sn = jnp.dot(q2, kt_ref[0], preferred_element_type=F32)
        mx = jnp.maximum(jnp.max(sc, axis=-1, keepdims=True), jnp.max(sn, axis=-1, keepdims=True))
        s_w[:, 0:nkc] = sc
        s_w[:, nkc:nk] = sn
        m_w[...] = jnp.broadcast_to(mx, (2 * tq, LANES))

        mp = m_r[...]
        v_all = jnp.concatenate([vc_ref[0].astype(BF16), v_ref[0]], axis=0)
        v_ext = jnp.concatenate([v_all, jnp.ones_like(v_all)], axis=1)
        p = jnp.concatenate(
            [jnp.exp(s_r[:, c:c + LANES] - mp).astype(BF16) for c in range(0, nk, LANES)], axis=1)
        acc = jnp.dot(p, v_ext, preferred_element_type=F32)
        on = acc[:, 0:LANES] / acc[:, LANES:2 * LANES]
        o = on[0:tq] - lam * on[tq:2 * tq]
        o_ref[0] = (_rms(o, g_ref[...]) * (1.0 - lambda_init)).astype(BF16)

    @pl.when(t % 2 == 0)
    def _():
        body(s_a, m_a, s_b, m_b)

    @pl.when(t % 2 == 1)
    def _():
        body(s_b, m_b, s_a, m_a)


def _diff_attention_lat(lam_p, g_head, q, kc, kt, vc, v, *, tq, lambda_init):
    b, s, w = q.shape
    past = kc.shape[1]
    nq = s // tq
    n_units = b * A_HEADS * nq
    last = n_units - 1

    def unit(u):
        return u // (A_HEADS * nq), (u // nq) % A_HEADS, u % nq

    def cur(t):
        return unit(jnp.minimum(t, last))

    def prev(t):
        return unit(jnp.maximum(t - 1, 0))

    return pl.pallas_call(
        functools.partial(_diff_lat_kernel, lambda_init=lambda_init),
        grid=(n_units + 1,),
        in_specs=[
            pl.BlockSpec((4, A_HEAD_DIM), lambda t: (0, 0)),
            pl.BlockSpec((1, A_V_DIM), lambda t: (0, 0)),
            pl.BlockSpec((1, tq, LANES), lambda t: (cur(t)[0], cur(t)[2], cur(t)[1])),
            pl.BlockSpec((1, past, LANES), lambda t: (cur(t)[0], 0, cur(t)[1])),
            pl.BlockSpec((1, LANES, s), lambda t: (cur(t)[0], cur(t)[1], 0)),
            pl.BlockSpec((1, past, LANES), lambda t: (prev(t)[0], 0, prev(t)[1])),
            pl.BlockSpec((1, s, LANES), lambda t: (prev(t)[0], 0, prev(t)[1])),
        ],
        out_specs=pl.BlockSpec((1, tq, LANES), lambda t: (prev(t)[0], prev(t)[2], prev(t)[1])),
        out_shape=jax.ShapeDtypeStruct((b, s, w), BF16),
        scratch_shapes=[pltpu.VMEM((2 * tq, past + s), F32), pltpu.VMEM((2 * tq, LANES), F32),
                        pltpu.VMEM((2 * tq, past + s), F32), pltpu.VMEM((2 * tq, LANES), F32)],
        compiler_params=pltpu.CompilerParams(dimension_semantics=("arbitrary",),
                                             vmem_limit_bytes=56 * 1024 * 1024),
        name="diff_attn_lat",
    )(lam_p, g_head, q, kc, kt, vc, v)


def _swa_kernel(sink_ref, q_ref, k_ref, v_ref, o_ref):
    tq = q_ref.shape[1]
    rows = B_GROUP * tq
    lane = lax.broadcasted_iota(jnp.int32, (1, LANES), 1)
    low = lane < 64
    lane2 = lax.broadcasted_iota(jnp.int32, (1, 2 * LANES), 1)
    head_masks = [((lane2 // 64) == g).astype(BF16) for g in range(B_GROUP)]
    gw = B_GROUP * B_HEAD_DIM
    v = v_ref[0]
    v_ext = jnp.concatenate([v, jnp.ones_like(v)], axis=1)
    finishers = []
    for n in range(B_KV_HEADS):
        q = q_ref[0, :, n * gw:(n + 1) * gw]
        qs = jnp.concatenate([q * hm for hm in head_masks], axis=0)
        k2 = k_ref[0, :, n * LANES:(n + 1) * LANES]
        s = _nt(qs, jnp.concatenate([k2, k2], axis=1))
        sink = jnp.concatenate(
            [jnp.full((tq, 1), sink_ref[n * B_GROUP + g], F32) for g in range(B_GROUP)], axis=0)
        top = jnp.max(s, keepdims=True)
        for g in range(B_GROUP):
            top = jnp.maximum(top, sink_ref[n * B_GROUP + g])

        def finish(mp, n=n, s=s, sink=sink):
            acc = jnp.dot(jnp.exp(s - mp).astype(BF16), v_ext, preferred_element_type=F32)
            den = acc[:, LANES:2 * LANES] + jnp.exp(sink - mp)
            o = acc[:, 0:LANES] / den
            osw = pltpu.roll(o, 64, 1)
            for j in range(B_GROUP // 2):
                ra = slice((2 * j) * tq, (2 * j + 1) * tq)
                rb = slice((2 * j + 1) * tq, (2 * j + 2) * tq)
                pair = jnp.where(low, o[ra], osw[rb]) if n == 0 else jnp.where(low, osw[ra], o[rb])
                o_ref[0, :, n * gw + j * LANES:n * gw + (j + 1) * LANES] = pair.astype(BF16)
            return jnp.min(den)

        finishers.append((finish, s, sink, finish(top)))

    smallest = functools.reduce(jnp.minimum, [f[3] for f in finishers])

    @pl.when(jnp.logical_not(smallest >= 1e-30))
    def _():
        for finish, s, sink, _ in finishers:
            finish(jnp.maximum(jnp.max(s, axis=-1, keepdims=True), sink))


def _swa_attention(sink, q, k2, v):
    b, s, w = q.shape
    seq = lambda a: pl.BlockSpec((1, s, a.shape[2]), lambda bi: (bi, 0, 0))
    return pl.pallas_call(
        _swa_kernel,
        grid=(b,),
        in_specs=[pl.BlockSpec(memory_space=pltpu.SMEM), seq(q), seq(k2), seq(v)],
        out_specs=seq(q),
        out_shape=jax.ShapeDtypeStruct((b, s, w), BF16),
        compiler_params=pltpu.CompilerParams(dimension_semantics=("arbitrary",)),
        name="swa_attn_ctx",
    )(sink, q, k2, v)


def _swa_lat_kernel(sink_ref, q_ref, kc_ref, kl_ref, km_ref, kr_ref, vc_ref, vl_ref, vm_ref, vr_ref, o_ref,
                    s_a, m_a, s_b, m_b, *, nqb):
    t = pl.program_id(0)
    n_units = pl.num_programs(0) - 1
    tq = q_ref.shape[1]
    gw = B_GROUP * B_HEAD_DIM
    i_cur = jnp.minimum(t, n_units - 1) % nqb
    nkc = kc_ref.shape[1]
    rows = B_GROUP * tq

    @pl.when(t == 0)
    def _():
        s_b[...] = jnp.zeros_like(s_b)
        m_b[...] = jnp.zeros_like(m_b)

    def sink_col(n):
        return jnp.concatenate(
            [jnp.full((tq, 1), sink_ref[n * B_GROUP + g], F32) for g in range(B_GROUP)], axis=0)

    def body(s_w, m_w, s_r, m_r):
        lane = lax.broadcasted_iota(jnp.int32, (1, LANES), 1)
        low = lane < 64
        lane2 = lax.broadcasted_iota(jnp.int32, (1, 2 * LANES), 1)
        head_masks = [((lane2 // 64) == g).astype(BF16) for g in range(B_GROUP)]
        qi = lax.broadcasted_iota(jnp.int32, (rows, SWA_Q), 0) & (tq - 1)
        kj = lax.broadcasted_iota(jnp.int32, (rows, SWA_Q), 1)
        far = 2 * SWA_Q
        left_ok = kj >= qi + jnp.where(i_cur > 0, 0, far)
        right_ok = kj <= qi - jnp.where(i_cur < nqb - 1, 0, far)

        kc = kc_ref[0]
        kc_sw = pltpu.roll(kc, 64, 1)
        for n in range(B_KV_HEADS):
            q = q_ref[0, :, n * gw:(n + 1) * gw]
            qs = jnp.concatenate([q * hm for hm in head_masks], axis=0)
            kc2 = (jnp.where(low, kc, kc_sw) if n == 0 else jnp.where(low, kc_sw, kc)).astype(BF16)
            ks = [kc2] + [r[0, :, n * LANES:(n + 1) * LANES] for r in (kl_ref, km_ref, kr_ref)]
            k_all = jnp.concatenate([jnp.concatenate([k, k], axis=1) for k in ks], axis=0)
            s = _nt(qs, k_all)
            chunks = [s[:, c:c + LANES] for c in range(0, s.shape[1], LANES)]
            il = nkc // LANES
            chunks[il] = jnp.where(left_ok, chunks[il], NEG_INF)
            chunks[il + 2] = jnp.where(right_ok, chunks[il + 2], NEG_INF)
            top = jnp.max(functools.reduce(jnp.maximum, chunks), keepdims=True)
            for g in range(B_GROUP):
                top = jnp.maximum(top, sink_ref[n * B_GROUP + g])
            for c, ch in enumerate(chunks):
                s_w[n * rows:(n + 1) * rows, c * LANES:(c + 1) * LANES] = ch
            m_w[n * rows:(n + 1) * rows, :] = jnp.broadcast_to(top, (rows, LANES))

        vc = vc_ref[0]
        vc_sw = pltpu.roll(vc, 64, 1)
        nk = nkc + 3 * tq
        finishers = []
        for n in range(B_KV_HEADS):
            vc2 =(jnp.where(low, vc, vc_sw) if n == 0 else jnp.where(low, vc_sw, vc)).astype(BF16)
            v_all = jnp.concatenate(
                [vc2] + [r[0, :, n * LANES:(n + 1) * LANES] for r in (vl_ref, vm_ref, vr_ref)], axis=0)
            v_ext = jnp.concatenate([v_all, jnp.ones_like(v_all)], axis=1)
            sink = sink_col(n)

            def finish(mp, n=n, v_ext=v_ext, sink=sink):
                p = jnp.concatenate([jnp.exp(s_r[n * rows:(n + 1) * rows, c:c + LANES] - mp).astype(BF16)
                                     for c in range(0, nk, LANES)], axis=1)
                acc = jnp.dot(p, v_ext, preferred_element_type=F32)
                den = acc[:, LANES:2 * LANES] + jnp.exp(sink - mp)
                o = acc[:, 0:LANES] / den
                for j in range(B_GROUP // 2):
                    pair = jnp.where(low, o[(2 * j) * tq:(2 * j + 1) * tq], o[(2 * j + 1) * tq:(2 * j + 2) * tq])
                    o_ref[0, :, n * gw + j * LANES:n * gw + (j + 1) * LANES] = pair.astype(BF16)
                return jnp.min(den)

            finishers.append((n, finish, sink, finish(m_r[n * rows:(n + 1) * rows, :])))

        smallest = functools.reduce(jnp.minimum, [f[3] for f in finishers])

        @pl.when(jnp.logical_not(smallest >= 1e-30))
        def _():
            for n, finish, sink, _ in finishers:
                row_max = functools.reduce(
                    jnp.maximum, [s_r[n * rows:(n + 1) * rows, c:c + LANES] for c in range(0, nk, LANES)])
                mx = jnp.maximum(jnp.max(row_max, axis=-1, keepdims=True), sink)
                finish(jnp.broadcast_to(mx, (rows, LANES)))

    @pl.when(t % 2 == 0)
    def _():
        body(s_a, m_a, s_b, m_b)

    @pl.when(t % 2 == 1)
    def _():
        body(s_b, m_b, s_a, m_a)


def _swa_attention_lat(sink, q, kc, k2, vc, v):
    b, s, w = q.shape
    past = kc.shape[1]
    tq = SWA_Q
    nqb = s // tq
    n_units = b * nqb
    last = n_units - 1
    nk = past + 3 * tq

    def cur(t):
        u = jnp.minimum(t, last)
        return u // nqb, u % nqb

    def prev(t):
        u = jnp.maximum(t - 1, 0)
        return u // nqb, u % nqb

    lo = lambda i: jnp.maximum(i - 1, 0)
    hi = lambda i: jnp.minimum(i + 1, nqb - 1)
    kspec = lambda f: pl.BlockSpec((1, tq, 2 * LANES), lambda t: (cur(t)[0], f(cur(t)[1]), 0))
    vspec = lambda f: pl.BlockSpec((1, tq, 2 * LANES), lambda t: (prev(t)[0], f(prev(t)[1]), 0))
    same = lambda i: i
    rows = B_KV_HEADS * B_GROUP * tq
    return pl.pallas_call(
        functools.partial(_swa_lat_kernel, nqb=nqb),
        grid=(n_units + 1,),
        in_specs=[
            pl.BlockSpec(memory_space=pltpu.SMEM),
            pl.BlockSpec((1, tq, w), lambda t: (cur(t)[0], cur(t)[1], 0)),
            pl.BlockSpec((1, past, LANES), lambda t: (cur(t)[0], 0, 0)),
            kspec(lo), kspec(same), kspec(hi),
            pl.BlockSpec((1, past, LANES), lambda t: (prev(t)[0], 0, 0)),
            vspec(lo), vspec(same), vspec(hi),
        ],
        out_specs=pl.BlockSpec((1, tq, w), lambda t: (prev(t)[0], prev(t)[1], 0)),
        out_shape=jax.ShapeDtypeStruct((b, s, w), BF16),
        scratch_shapes=[pltpu.VMEM((rows, nk), F32), pltpu.VMEM((rows, LANES), F32),
                        pltpu.VMEM((rows, nk), F32), pltpu.VMEM((rows, LANES), F32)],
        compiler_params=pltpu.CompilerParams(dimension_semantics=("arbitrary",)),
        name="swa_attn_lat",
    )(sink, q, kc, k2, k2, k2, vc, v, v, v)


def _postmix_kernel(*refs, n_ctx_tiles):
    (xc, xl, oac, oal, obc, obl, sgac, sgal, sgbc, sgbl, g1_ref, sh2_ref, sc2_ref, gpm_ref, gpf_ref,
     wpa_ref, wpb_ref, wo_ref, wr_ref, br_ref, x1_o, h2_o, route_o, cnt_o, x1_a, x1_b) = refs
    t = pl.program_id(0)
    is_ctx = t < n_ctx_tiles
    pick = lambda a, b: jnp.where(is_ctx, a[...], b[...])

    @pl.when(t == 0)
    def _():
        x1_b[...] = jnp.zeros_like(x1_b)

    def body(x1_w, x1_r):
        pa = jnp.dot(pick(oac, oal), wpa_ref[...], preferred_element_type=F32)
        pb = jnp.dot(pick(obc, obl), wpb_ref[...], preferred_element_type=F32)
        mix = pick(sgac, sgal).astype(F32) * pa + pick(sgbc, sgbl).astype(F32) * pb
        m2 = jnp.dot(mix.astype(BF16), wo_ref[...], preferred_element_type=F32)
        x1 = pick(xc, xl) + g1_ref[0] * _rms(m2, gpm_ref[...])
        x1_o[...] = x1
        x1_w[...] = x1
        _postmix_route(x1_r[...], sh2_ref, sc2_ref, gpf_ref, wr_ref, br_ref, h2_o, route_o, cnt_o)

    @pl.when(t % 2 == 0)
    def _():
        body(x1_a, x1_b)

    @pl.when(t % 2 == 1)
    def _():
        body(x1_b, x1_a)


def _postmix_route(x1, sh2_ref, sc2_ref, gpf_ref, wr_ref, br_ref, h2_o, route_o, cnt_o):
    h2 = _rms(x1, gpf_ref[...]) * (1.0 + sc2_ref[0]) + sh2_ref[0]
    h2_o[...] = h2.astype(BF16)

    h_hi = h2.astype(BF16)
    h_lo = (h2 - h_hi.astype(F32)).astype(BF16)
    both = jnp.dot(h_hi, wr_ref[...], preferred_element_type=F32)
    logits = (both[:, 0:LANES] + both[:, LANES:2 * LANES]
              + jnp.dot(h_lo, wr_ref[:, 0:LANES], preferred_element_type=F32) + br_ref[...])
    tm = logits.shape[0]
    lt = logits.T
    row = lax.broadcasted_iota(jnp.int32, (EXPERTS_PER_GROUP, tm), 0).astype(F32)
    none = float(EXPERTS_PER_GROUP)
    lg = jnp.where(row < N_GROUPS, lt[N_EXPERTS:N_EXPERTS + EXPERTS_PER_GROUP], -jnp.inf)
    mg = jnp.max(lg, axis=0, keepdims=True)
    g_sel = jnp.min(jnp.where(lg == mg, row, none), axis=0, keepdims=True)
    g_w = 1.0 / jnp.sum(jnp.exp(lg - mg), axis=0, keepdims=True)
    le = lt[0:EXPERTS_PER_GROUP]
    for g in range(1, N_GROUPS):
        le = jnp.where(g_sel == g, lt[g * EXPERTS_PER_GROUP:(g + 1) * EXPERTS_PER_GROUP], le)
    v0 = jnp.max(le, axis=0, keepdims=True)
    i0 = jnp.min(jnp.where(le == v0, row, none), axis=0, keepdims=True)
    le1 = jnp.where(row == i0, -jnp.inf, le)
    v1 = jnp.max(le1, axis=0, keepdims=True)
    i1 = jnp.min(jnp.where(le1 == v1, row, none), axis=0, keepdims=True)
    e = jnp.exp(v1 - v0)
    w0 = g_w / (1.0 + e)
    w1 = g_w * e / (1.0 + e)
    e0 = g_sel * EXPERTS_PER_GROUP + i0
    e1 = g_sel * EXPERTS_PER_GROUP + i1
    route_o[...] = jnp.where(row == 0, e0, jnp.where(row == 1, e1, jnp.where(row == 2, w0,
                                                                             jnp.where(row == 3, w1, 0.0))))
    erow = lax.broadcasted_iota(jnp.int32, (N_EXPERTS, tm), 0).astype(F32)
    cnt = jnp.sum((erow == e0).astype(F32) + (erow == e1).astype(F32), axis=1, keepdims=True)
    cnt_o[0] = jnp.broadcast_to(cnt, (N_EXPERTS, LANES))


def _postmix(ctx_in, lat_in, mod3, gpm, gpf, wpa, wpb, wo, wr, br, *, lat_seq, tm):
    t_ctx, d = ctx_in[0].shape
    t_lat = lat_in[0].shape[0]
    assert t_ctx % tm == 0 and lat_seq % tm == 0
    nc = t_ctx // tm
    nl = t_lat // tm
    per = lat_seq // tm
    sub = d // LANES
    t_all = t_ctx + t_lat

    last = nc + nl - 1
    cur = lambda i: jnp.minimum(i, last)
    prev = lambda i: jnp.maximum(i - 1, 0)
    mod_row = lambda j: jnp.where(j < nc, 0, 1 + jnp.maximum(j - nc, 0) // per)
    full = lambda a: pl.BlockSpec(a.shape, lambda i: (0,) * a.ndim)
    in_specs, args = [], []
    for a_c, a_l in zip(ctx_in, lat_in):
        w = a_c.shape[1]
        in_specs += [pl.BlockSpec((tm, w), lambda i: (jnp.minimum(cur(i), nc - 1), 0)),
                     pl.BlockSpec((tm, w), lambda i: (jnp.maximum(cur(i) - nc, 0), 0))]
        args += [a_c, a_l]
    in_specs += [pl.BlockSpec((1, 1, d), lambda i: (mod_row(cur(i)), 0, 2)),
                 pl.BlockSpec((1, 1, d), lambda i: (mod_row(prev(i)), 0, 3)),
                 pl.BlockSpec((1, 1, d), lambda i: (mod_row(prev(i)), 0, 4)),
                 full(gpm), full(gpf), full(wpa), full(wpb), full(wo), full(wr), full(br)]
    args += [mod3, mod3, mod3, gpm, gpf, wpa, wpb, wo, wr, br]
    return pl.pallas_call(
        functools.partial(_postmix_kernel, n_ctx_tiles=nc),
        grid=(nc + nl + 1,),
        in_specs=in_specs,
        out_specs=[pl.BlockSpec((tm, d), lambda i: (cur(i), 0)),
                   pl.BlockSpec((tm, d), lambda i: (prev(i), 0)),
                   pl.BlockSpec((EXPERTS_PER_GROUP, tm), lambda i: (0, prev(i))),
                   pl.BlockSpec((1, N_EXPERTS, LANES), lambda i: (prev(i), 0, 0))],
        out_shape=[jax.ShapeDtypeStruct((t_all, d), F32),
                   jax.ShapeDtypeStruct((t_all, d), BF16),
                   jax.ShapeDtypeStruct((EXPERTS_PER_GROUP, t_all), F32),
                   jax.ShapeDtypeStruct((nc + nl, N_EXPERTS, LANES), F32)],
        scratch_shapes=[pltpu.VMEM((tm, d), F32), pltpu.VMEM((tm, d), F32)],
        compiler_params=pltpu.CompilerParams(dimension_semantics=("arbitrary",)),
        name="postmix",
    )(*args)


def _segment_copies(src, src_row, dst, dst_row, n, sub, sem):
    @pl.when(n > 0)
    def _():
        pltpu.make_async_copy(src.at[pl.ds(pl.multiple_of(src_row * sub, sub), n * sub)],
                              dst.at[pl.ds(pl.multiple_of(dst_row * sub, sub), n * sub)], sem).start()


def _local_positions(route_t, tile_base):
    tm = route_t.shape[1]
    erow = lax.broadcasted_iota(jnp.int32, (N_EXPERTS, tm), 0).astype(F32)
    is0 = erow == route_t[0:1]
    is1 = erow == route_t[1:2]
    earlier = (lax.broadcasted_iota(jnp.int32, (tm, tm), 0)
               < lax.broadcasted_iota(jnp.int32, (tm, tm), 1)).astype(BF16)
    pre0 = jnp.dot(is0.astype(BF16), earlier, preferred_element_type=F32)
    pre1 = jnp.dot(is1.astype(BF16), earlier, preferred_element_type=F32)
    cnt0 = jnp.sum(is0.astype(F32), axis=1, keepdims=True)
    base = tile_base[:, 0:1]
    lpos0 = jnp.sum(jnp.where(is0, base + pre0, 0.0), axis=0, keepdims=True)
    lpos1 = jnp.sum(jnp.where(is1, base + cnt0 + pre1, 0.0), axis=0, keepdims=True)
    return lpos0, lpos1


def _dispatch_kernel(ss_ref, sl_ref, tb_ref, h_ref, r_ref, tbv_ref, xs_hbm, pbuf, zbuf, sem, zsem, *, n_asg, rows):
    i = pl.program_id(0)
    nt = pl.num_programs(0)
    tm, d = h_ref.shape
    sub = d // LANES
    nrow = TOP_K * tm
    slot = i % 2

    def wait_slot(s):
        pltpu.make_async_copy(pbuf.at[pl.ds(pl.multiple_of(s * nrow * sub, nrow * sub), nrow * sub)],
                              xs_hbm.at[pl.ds(0, nrow * sub)], sem.at[s]).wait()

    def slack_copy():
        return pltpu.make_async_copy(zbuf, xs_hbm.at[pl.ds(n_asg * sub, rows * sub)], zsem.at[0])

    @pl.when(i == 0)
    def _():
        zbuf[...] = jnp.zeros_like(zbuf)
        slack_copy().start()

    lpos0, lpos1 = _local_positions(r_ref[...], tbv_ref[0])
    p = lax.broadcasted_iota(jnp.int32, (nrow, tm), 0).astype(F32)
    sel = ((p == lpos0) | (p == lpos1)).astype(BF16)
    xp = jnp.dot(sel, h_ref[...], preferred_element_type=F32)

    @pl.when(i >= 2)
    def _():
        wait_slot(slot)

    _store_row_tiles(pbuf, slot * nrow * sub, xp)

    def seg(e, c):
        k = i * N_EXPERTS + e
        _segment_copies(pbuf, slot * nrow + tb_ref[k], xs_hbm, ss_ref[k], sl_ref[k], sub, sem.at[slot])
        return c
    lax.fori_loop(0, N_EXPERTS, seg, 0)

    @pl.when(i == nt - 1)
    def _():
        wait_slot(slot)

        @pl.when(nt >= 2)
        def _():
            wait_slot(1 - slot)
        slack_copy().wait()


def _dispatch(h2, route, tables, *, tm, rows):
    seg_start, seg_len, tile_base, tile_base_v = tables
    t, d = h2.shape
    sub = d // LANES
    n_asg = t * TOP_K
    grid_spec = pltpu.PrefetchScalarGridSpec(
        num_scalar_prefetch=3,
        grid=(t // tm,),
        in_specs=[
            pl.BlockSpec((tm, d), lambda i, *_: (i, 0)),
            pl.BlockSpec((EXPERTS_PER_GROUP, tm), lambda i, *_: (0, i)),
            pl.BlockSpec((1, N_EXPERTS, LANES), lambda i, *_: (i, 0, 0)),
        ],
        out_specs=pl.BlockSpec(memory_space=pl.ANY),
        scratch_shapes=[pltpu.VMEM((2 * TOP_K * tm * sub, LANES), F32), pltpu.VMEM((rows * sub, LANES), F32),
                        pltpu.SemaphoreType.DMA((2,)), pltpu.SemaphoreType.DMA((1,))],
    )
    return pl.pallas_call(
        functools.partial(_dispatch_kernel, n_asg=n_asg, rows=rows),
        grid_spec=grid_spec,
        out_shape=jax.ShapeDtypeStruct(((n_asg + rows) * sub, LANES), F32),
        compiler_params=pltpu.CompilerParams(dimension_semantics=("arbitrary",)),
        name="dispatch",
    )(seg_start, seg_len, tile_base, h2, route, tile_base_v)


def _moe_kernel(be_ref, row0_ref, nact_ref, par_ref, nxt_ref, xs_hbm, w1_hbm, w3_hbm, w2_hbm, ys_hbm,
                xbuf, obuf, wf1, wf3, wf2, w1b, w3b, w2b, rsem, wsem, gsem, *, rows, sub):
    i = pl.program_id(0)
    nact = nact_ref[0]
    slot = i % 2
    nslot = 1 - slot
    blk = rows * sub

    def weight_copies(e, s):
        return [pltpu.make_async_copy(w_hbm.at[e], wf.at[s], gsem.at[s])
                for w_hbm, wf in ((w1_hbm, wf1), (w3_hbm, wf3), (w2_hbm, wf2))]

    def read(j, s):
        return pltpu.make_async_copy(xs_hbm.at[pl.ds(pl.multiple_of(row0_ref[j] * sub, sub), blk)],
                                     xbuf.at[pl.ds(pl.multiple_of(s * blk, blk), blk)], rsem.at[s])

    def write(j, s):
        return pltpu.make_async_copy(obuf.at[pl.ds(pl.multiple_of(s * blk, blk), blk)],
                                     ys_hbm.at[pl.ds(pl.multiple_of(row0_ref[j] * sub, sub), blk)], wsem.at[s])

    @pl.when(i == 0)
    def _():
        read(0, 0).start()
        for c in weight_copies(be_ref[0], 0):
            c.start()

    @pl.when(i < nact)
    def _():
        @pl.when(i + 1 < nact)
        def _():
            read(i + 1, nslot).start()

        changed = jnp.logical_or(i == 0, be_ref[i] != be_ref[jnp.maximum(i - 1, 0)])

        @pl.when(changed)
        def _():
            s = par_ref[i]
            for c in weight_copies(be_ref[i], s):
                c.wait()
            w1b[...] = wf1[s].astype(BF16)
            w3b[...] = wf3[s].astype(BF16)
            w2b[...] = wf2[s].astype(BF16)

            @pl.when(nxt_ref[i] >= 0)
            def _():
                for c in weight_copies(nxt_ref[i], 1 - s):
                    c.start()

        read(i, slot).wait()
        x = _load_row_tiles(xbuf, slot * blk, rows, sub).astype(BF16)
        a = jnp.dot(x, w1b[...], preferred_element_type=F32)
        b = jnp.dot(x, w3b[...], preferred_element_type=F32)
        hmid = (a * jax.nn.sigmoid(a) * b).astype(BF16)
        y = jnp.dot(hmid, w2b[...], preferred_element_type=F32)
        _store_row_tiles(obuf, slot * blk, y)

        @pl.when(i >= 1)
        def _():
            write(i - 1, nslot).wait()
        write(i, slot).start()

    @pl.when(i == nact)
    def _():
        write(i - 1, nslot).wait()
        obuf[pl.ds(pl.multiple_of(slot * blk, blk), blk), :] = jnp.zeros((blk, LANES), F32)
        tail = pltpu.make_async_copy(obuf.at[pl.ds(pl.multiple_of(slot * blk, blk), blk)],
                                     ys_hbm.at[pl.ds(ys_hbm.shape[0] - blk, blk)], wsem.at[slot])
        tail.start()
        tail.wait()


def _moe(blk_tables, xs, w1, w3, w2):
    blk_expert, row0, nact, parity, nxt = blk_tables
    nblk = blk_expert.shape[0] - 1
    d, de = w1.shape[1], w1.shape[2]
    sub = d // LANES
    rows = MOE_ROWS
    anyspec = pl.BlockSpec(memory_space=pl.ANY)
    grid_spec = pltpu.PrefetchScalarGridSpec(
        num_scalar_prefetch=5,
        grid=(nblk + 1,),
        in_specs=[anyspec, anyspec, anyspec, anyspec],
        out_specs=anyspec,
        scratch_shapes=[
            pltpu.VMEM((2 * rows * sub, LANES), F32),
            pltpu.VMEM((2 * rows * sub, LANES), F32),
            pltpu.VMEM((2, d, de), F32),
            pltpu.VMEM((2, d, de), F32),
            pltpu.VMEM((2, de, d), F32),
            pltpu.VMEM((d, de), BF16),
            pltpu.VMEM((d, de), BF16),
            pltpu.VMEM((de, d), BF16),
            pltpu.SemaphoreType.DMA((2,)),
            pltpu.SemaphoreType.DMA((2,)),
            pltpu.SemaphoreType.DMA((2,)),
        ],
    )
    return pl.pallas_call(
        functools.partial(_moe_kernel, rows=rows, sub=sub),
        grid_spec=grid_spec,
        out_shape=jax.ShapeDtypeStruct(xs.shape, F32),
        compiler_params=pltpu.CompilerParams(dimension_semantics=("arbitrary",)),
        name="expert_mlp",
    )(blk_expert, row0, nact, parity, nxt, xs, w1, w3, w2)


def _combine_kernel(ss_ref, sl_ref, tb_ref, x1_ref, r_ref, tbv_ref, g2_ref, gpost_ref, ys_hbm, o_ref,
                    ybuf, sem, *, tile0):
    i = pl.program_id(0)
    nt = pl.num_programs(0)
    tm, d = x1_ref.shape
    sub = d // LANES
    nrow = TOP_K * tm
    slot = i % 2

    def fetch(tile, s):
        def seg(e, c):
            k = tile * N_EXPERTS + e
            _segment_copies(ys_hbm, ss_ref[k], ybuf, s * nrow + tb_ref[k], sl_ref[k], sub, sem.at[s])
            return c
        lax.fori_loop(0, N_EXPERTS, seg, 0)

    @pl.when(i == 0)
    def _():
        fetch(tile0, 0)

    @pl.when(i + 1 < nt)
    def _():
        fetch(tile0 + i + 1, 1 - slot)

    pltpu.make_async_copy(ys_hbm.at[pl.ds(0, nrow * sub)],
                          ybuf.at[pl.ds(pl.multiple_of(slot * nrow * sub, nrow * sub), nrow * sub)],
                          sem.at[slot]).wait()
    rt = r_ref[...]
    lpos0, lpos1 = _local_positions(rt, tbv_ref[0])
    rows8 = lax.broadcasted_iota(jnp.int32, (EXPERTS_PER_GROUP, tm), 0)
    cols = jnp.where(rows8 == 0, lpos0, jnp.where(rows8 == 1, lpos1, rt))
    cols = jnp.concatenate([cols, jnp.zeros((LANES - EXPERTS_PER_GROUP, tm), F32)], axis=0).T
    p = lax.broadcasted_iota(jnp.int32, (tm, nrow), 1).astype(F32)
    q = (jnp.where(p == cols[:, 0:1], cols[:, 2:3], 0.0)
         + jnp.where(p == cols[:, 1:2], cols[:, 3:4], 0.0)).astype(BF16)
    ysort = _load_row_tiles(ybuf, slot * nrow * sub, nrow, sub).astype(BF16)
    y = jnp.dot(q, ysort, preferred_element_type=F32)
    o_ref[...] = x1_ref[...] + g2_ref[0] * _rms(y, gpost_ref[...])


def _combine(x1, ys, route, tables, mod3, mod_row0, gpost, *, t, seq, tm, tok_off, is_ctx):
    seg_start, seg_len, tile_base, tile_base_v = tables
    t_all, d = x1.shape
    per = seq // tm
    sub = d // LANES
    boff = tok_off // tm

    def row(i):
        return (i // per) if not is_ctx else 0

    grid_spec = pltpu.PrefetchScalarGridSpec(
        num_scalar_prefetch=3,
        grid=(t // tm,),
        in_specs=[
            pl.BlockSpec((tm, d), lambda i, *_: (boff + i, 0)),
            pl.BlockSpec((EXPERTS_PER_GROUP, tm), lambda i, *_: (0, boff + i)),
            pl.BlockSpec((1, N_EXPERTS, LANES), lambda i, *_: (boff + i, 0, 0)),
            pl.BlockSpec((1, 1, d), lambda i, *_: (mod_row0 + row(i), 0, 5)),
            pl.BlockSpec((1, d), lambda i, *_: (0, 0)),
            pl.BlockSpec(memory_space=pl.ANY),
        ],
        out_specs=pl.BlockSpec((tm, d), lambda i, *_: (i, 0)),
        scratch_shapes=[pltpu.VMEM((2 * TOP_K * tm * sub, LANES), F32), pltpu.SemaphoreType.DMA((2,))],
    )
    return pl.pallas_call(
        functools.partial(_combine_kernel, tile0=boff),
        grid_spec=grid_spec,
        out_shape=jax.ShapeDtypeStruct((t, d), F32),
        compiler_params=pltpu.CompilerParams(dimension_semantics=("arbitrary",)),
        name="combine_ctx" if is_ctx else "combine_lat",
    )(seg_start, seg_len, tile_base, x1, route, tile_base_v, mod3, gpost, ys)


def _routing_tables(counts, tm, rows):
    nt = counts.shape[0]
    n_asg = nt * tm * TOP_K
    ex = jnp.arange(N_EXPERTS, dtype=jnp.int32)
    cnt_te = counts[:, :, 0].astype(jnp.int32)
    cnt_e = jnp.sum(cnt_te, axis=0)
    start_e = jnp.cumsum(cnt_e) - cnt_e
    seg_start = start_e[None, :] + jnp.cumsum(cnt_te, axis=0) - cnt_te
    tile_base = jnp.cumsum(cnt_te, axis=1) - cnt_te
    tile_base_v = jnp.broadcast_to(tile_base.astype(F32)[:, :, None], (nt, N_EXPERTS, LANES))

    nblk_e = (cnt_e + rows - 1) // rows
    blk_end = jnp.cumsum(nblk_e)
    blk_start = blk_end - nblk_e
    n_blocks = n_asg // rows + N_EXPERTS
    b = jnp.arange(n_blocks + 1, dtype=jnp.int32)
    be = jnp.minimum(jnp.sum((blk_end[None, :] <= b[:, None]).astype(jnp.int32), axis=1), N_EXPERTS - 1)
    first = jnp.sum(jnp.where(be[:, None] == ex[None, :], (start_e - blk_start * rows)[None, :], 0), axis=1)
    row0 = jnp.clip(first + b * rows, 0, n_asg)
    nact = blk_end[-1:].astype(jnp.int32)
    used = cnt_e > 0
    parity_e = (jnp.cumsum(used.astype(jnp.int32)) - 1) % 2
    later = (ex[None, :] > ex[:, None]) & used[None, :]
    nxt_e = jnp.min(jnp.where(later, ex[None, :], N_EXPERTS), axis=1)
    nxt_e = jnp.where(nxt_e == N_EXPERTS, -1, nxt_e)
    pick = lambda tab: jnp.sum(jnp.where(be[:, None] == ex[None, :], tab[None, :], 0), axis=1).astype(jnp.int32)
    seg = (seg_start.reshape(-1).astype(jnp.int32), cnt_te.reshape(-1).astype(jnp.int32),
           tile_base.reshape(-1).astype(jnp.int32), tile_base_v)
    blk = (be.astype(jnp.int32), row0.astype(jnp.int32), nact, pick(parity_e), pick(nxt_e))
    return seg, blk


def _rope_tables(n_tok):
    n_rows = n_tok // GRID_W
    rows = jnp.repeat(jnp.arange(n_rows), GRID_W).astype(F32)
    cols = jnp.tile(jnp.arange(GRID_W), n_rows).astype(F32)
    quarter = A_HEAD_DIM // 4
    inv = ROPE_BASE ** (-jnp.arange(quarter, dtype=F32) / quarter)
    ang = jnp.concatenate([rows[:, None] * inv, cols[:, None] * inv], axis=-1)
    cos, sin = jnp.cos(ang), jnp.sin(ang)
    cos_t = jnp.tile(jnp.concatenate([cos, cos], axis=-1), (1, LANES // A_HEAD_DIM))
    sin_t = jnp.tile(jnp.concatenate([-sin, sin], axis=-1), (1, LANES // A_HEAD_DIM))
    return cos_t, sin_t


def kernel(x_prompt, x_sample, c, cache_diff_k, cache_diff_v, cache_swa_k, cache_swa_v, c_ctx, w_ada, b_ada, g_pre_mix, g_post_mix, g_pre_ffn, g_post_ffn, w_in, lam_q1, lam_k1, lam_q2, lam_k2, g_diff_head, sink, w_proj_a, w_proj_b, w_out, w_router_group, b_router_group, w_router_expert, b_router_expert, w_e1, w_e3, w_e2):
    depth = w_in.shape[0]
    assert depth == 1
    l = 0
    bp, sp, d = x_prompt.shape
    bs, ss, _ = x_sample.shape
    lambda_init = 0.8 - 0.6 * math.exp(-0.3 * l)
    assert A_HEAD_DIM == B_HEAD_DIM and ss % GRID_W == 0 and bs + 1 <= MOD_ROWS

    c_all = jnp.concatenate([c_ctx[None, :], c, jnp.zeros((MOD_ROWS - 1 - bs, d), F32)], axis=0)
    mod = _modulation(c_all, w_ada[l], b_ada[l][None, :])
    mod3 = mod.reshape(MOD_ROWS, 1, 6 * d)

    w_in_b = w_in[l].astype(BF16)
    wpa = w_proj_a[l].astype(BF16)
    wpb = w_proj_b[l].astype(BF16)
    wo = w_out[l].astype(BF16)
    n_r = N_GROUPS + N_EXPERTS
    wr = jnp.concatenate([w_router_expert[l], w_router_group[l], jnp.zeros((d, LANES - n_r), F32)], axis=1)
    br = jnp.concatenate([b_router_expert[l], b_router_group[l], jnp.zeros((LANES - n_r,), F32)])[None, :]
    wr_hi = wr.astype(BF16)
    wr2 = jnp.concatenate([wr_hi, (wr - wr_hi.astype(F32)).astype(BF16)], axis=1)
    lam_p = jnp.stack([lam_q1[l], lam_k1[l], lam_q2[l], lam_k2[l]], axis=0)
    g_head = g_diff_head[l][None, :]
    sink_l = sink[l]
    cos_t, sin_t = _rope_tables(ss)

    xp2 = x_prompt.reshape(bp * sp, d)
    xs2 = x_sample.reshape(bs * ss, d)
    gpre = g_pre_mix[l][None, :]

    (qa_c, ka_c, va_c, qb_c, kb2_c, vb_c, sga_c, sgb_c, kaf, vaf, kbf, vbf) = _inproj(
        xp2, mod3, 0, gpre, cos_t, sin_t, w_in_b, seq=sp, tm=sp, is_ctx=True)
    r3 = lambda a, b_: a.reshape(b_, -1, a.shape[-1])
    oa_c = _diff_attention(lam_p, g_head, r3(qa_c, bp), r3(ka_c, bp), r3(va_c, bp), lambda_init=lambda_init)
    ob_c = _swa_attention(sink_l, r3(qb_c, bp), r3(kb2_c, bp), r3(vb_c, bp))

    (qa_s, ka_s, va_s, qb_s, kb2_s, vb_s, sga_s, sgb_s) = _inproj(
        xs2, mod3, 1, gpre, cos_t, sin_t, w_in_b, seq=ss, tm=512, is_ctx=False)
    past = cache_diff_k.shape[2]
    ck = cache_diff_k[:, l].reshape(bs, past, -1)
    cv = cache_diff_v[:, l].reshape(bs, past, -1)
    oa_s = _diff_attention_lat(lam_p, g_head, r3(qa_s, bs), ck, ka_s, cv, r3(va_s, bs),
                               tq=512, lambda_init=lambda_init)
    sk = cache_swa_k[:, l].reshape(bs, past, -1)
    sv = cache_swa_v[:, l].reshape(bs, past, -1)
    nqb = ss // SWA_Q
    kb2_3, vb_3 = r3(kb2_s, bs), r3(vb_s, bs)
    ob_s = _swa_attention_lat(sink_l, r3(qb_s, bs), sk, kb2_3, sv, vb_3)

    gpm = g_post_mix[l][None, :]
    gpf = g_pre_ffn[l][None, :]
    t_ctx, t_lat = bp * sp, bs * ss
    x1, h2t, route, counts = _postmix(
        (xp2, oa_c.reshape(t_ctx, -1), ob_c.reshape(t_ctx, -1), sga_c, sgb_c),
        (xs2, oa_s.reshape(t_lat, -1), ob_s.reshape(t_lat, -1), sga_s, sgb_s),
        mod3, gpm, gpf, wpa, wpb, wo, wr2, br, lat_seq=ss, tm=512)

    tables, blk_tables = _routing_tables(counts, MOE_TILE, MOE_ROWS)
    xs = _dispatch(h2t, route, tables, tm=MOE_TILE, rows=MOE_ROWS)
    ys = _moe(blk_tables, xs, w_e1[l], w_e3[l], w_e2[l])

    gpost = g_post_ffn[l][None, :]
    y_p = _combine(x1, ys, route, tables, mod3, 0, gpost, t=t_ctx, seq=sp, tm=MOE_TILE, tok_off=0, is_ctx=True)
    y_s = _combine(x1, ys, route, tables, mod3, 1, gpost, t=t_lat, seq=ss, tm=MOE_TILE, tok_off=t_ctx,
                   is_ctx=False)

    ha = A_HEADS
    return (y_p.reshape(bp, sp, d), y_s.reshape(bs, ss, d),
            kaf.reshape(bp, 1, sp, ha, 2, A_HEAD_DIM), vaf.reshape(bp, 1, sp, ha, A_V_DIM),
            kbf.reshape(bp, 1, sp, B_KV_HEADS, B_HEAD_DIM), vbf.reshape(bp, 1, sp, B_KV_HEADS, B_HEAD_DIM))
```

```python
import functools
import math

import jax
import jax.numpy as jnp
from jax import lax
from jax.experimental import pallas as pl
from jax.experimental.pallas import tpu as pltpu

F32 = jnp.float32
BF16 = jnp.bfloat16
HIGHEST = lax.Precision.HIGHEST

GRID_W = 64
ROPE_BASE = 10000.0
EPS = 1e-6
NEG_INF = -1e30
A_HEADS = 4
A_HEAD_DIM = 64
A_V_DIM = 2 * A_HEAD_DIM
B_HEADS = 8
B_KV_HEADS = 2
B_GROUP = B_HEADS // B_KV_HEADS
B_HEAD_DIM = 64
WINDOW = 128
N_GROUPS = 4
EXPERTS_PER_GROUP = 8
N_EXPERTS = N_GROUPS * EXPERTS_PER_GROUP
TOP_K = 2

LANES = 128
MOD_ROWS = 16
MOE_ROWS = 512
MOE_TILE = 512
SWA_Q = 128

_QA = 0
_KA = _QA + A_HEADS * 2 * A_HEAD_DIM
_VA = _KA + A_HEADS * 2 * A_HEAD_DIM
_QB = _VA + A_HEADS * A_V_DIM
_KB = _QB + B_HEADS * B_HEAD_DIM
_VB = _KB + B_KV_HEADS * B_HEAD_DIM
_GA = _VB + B_KV_HEADS * B_HEAD_DIM


def _rms(x, g):
    return x * lax.rsqrt(jnp.mean(x * x, axis=-1, keepdims=True) + EPS) * g


def _store_row_tiles(ref, base, val):
    sub = val.shape[1] // LANES
    for s in range(sub):
        ref[pl.ds(base + s, val.shape[0], stride=sub), :] = val[:, s * LANES:(s + 1) * LANES]


def _load_row_tiles(ref, base, n_rows, sub):
    return jnp.concatenate([ref[pl.ds(base + s, n_rows, stride=sub), :] for s in range(sub)], axis=1)


def _mod_kernel(c_ref, w_ref, b_ref, o_ref):
    c = c_ref[...]
    a = c * jax.nn.sigmoid(c)
    w = w_ref[...]
    a_hi = a.astype(BF16)
    a_lo = (a - a_hi.astype(F32)).astype(BF16)
    w_hi = w.astype(BF16)
    w_lo = (w - w_hi.astype(F32)).astype(BF16)
    o_ref[...] = (jnp.dot(a_hi, w_hi, preferred_element_type=F32) + jnp.dot(a_lo, w_hi, preferred_element_type=F32)
                  + jnp.dot(a_hi, w_lo, preferred_element_type=F32) + b_ref[...])


def _modulation(c_all, w_ada, b_ada):
    d, n = w_ada.shape
    tn = 512
    return pl.pallas_call(
        _mod_kernel,
        grid=(n // tn,),
        in_specs=[
            pl.BlockSpec((MOD_ROWS, d), lambda j: (0, 0)),
            pl.BlockSpec((d, tn), lambda j: (0, j)),
            pl.BlockSpec((1, tn), lambda j: (0, j)),
        ],
        out_specs=pl.BlockSpec((MOD_ROWS, tn), lambda j: (0, j)),
        out_shape=jax.ShapeDtypeStruct((MOD_ROWS, n), F32),
        name="modulation",
    )(c_all, w_ada, b_ada)


def _rope128(z, cos, sin_signed, first_half):
    rot = jnp.where(first_half, pltpu.roll(z, 96, 1), pltpu.roll(z, 32, 1))
    return z * cos + rot * sin_signed


def _inproj_kernel(x_ref, sh_ref, sc_ref, g_ref, cos_ref, sin_ref, w_ref, *outs, is_ctx):
    x = x_ref[...]
    h = _rms(x, g_ref[...]) * (1.0 + sc_ref[0]) + sh_ref[0]
    hb = h.astype(BF16)
    lane = lax.broadcasted_iota(jnp.int32, (1, LANES), 1)
    first_half = (lane % 64) < 32
    low = lane < 64

    def seg(lo, hi):
        return jnp.dot(hb, w_ref[:, lo:hi], preferred_element_type=F32)

    def rope(z):
        if is_ctx:
            return z
        cos = cos_ref[...]
        sin = sin_ref[...]
        parts = [_rope128(z[:, j:j + LANES], cos, sin, first_half) for j in range(0, z.shape[1], LANES)]
        return parts[0] if len(parts) == 1 else jnp.concatenate(parts, axis=1)

    if is_ctx:
        qa_o, ka_o, va_o, qb_o, kb2_o, vb_o, sga_o, sgb_o, kaf_o, vaf_o, kbf_o, vbf_o = outs
    else:
        qa_o, ka_o, va_o, qb_o, kb2_o, vb_o, sga_o, sgb_o = outs

    scale = A_HEAD_DIM ** -0.5
    qa_o[...] = (rope(seg(_QA, _KA)) * scale).astype(BF16)
    ka = rope(seg(_KA, _VA))
    if is_ctx:
        ka_o[...] = ka.astype(BF16)
    else:
        ka_o[0] = ka.T.astype(BF16)
    va = seg(_VA, _QB)
    va_o[...] = va.astype(BF16)
    qb_o[...] = (rope(seg(_QB, _KB)) * (B_HEAD_DIM ** -0.5)).astype(BF16)
    kb = rope(seg(_KB, _VB))
    kb_sw = pltpu.roll(kb, 64, 1)
    kb2_o[:, 0:LANES] = jnp.where(low, kb, kb_sw).astype(BF16)
    kb2_o[:, LANES:2 * LANES] = jnp.where(low, kb_sw, kb).astype(BF16)
    vb = seg(_VB, _GA)
    if is_ctx:
        vb_o[...] = vb.astype(BF16)
    else:
        vb_sw = pltpu.roll(vb, 64, 1)
        vb_o[:, 0:LANES] = jnp.where(low, vb, vb_sw).astype(BF16)
        vb_o[:, LANES:2 * LANES] = jnp.where(low, vb_sw, vb).astype(BF16)
    d = x.shape[1]
    sga_o[...] = jax.nn.sigmoid(seg(_GA, _GA + d)).astype(BF16)
    sgb_o[...] = jax.nn.sigmoid(seg(_GA + d, _GA + 2 * d)).astype(BF16)
    if is_ctx:
        kaf_o[...] = ka
        vaf_o[...] = va
        kbf_o[...] = kb
        vbf_o[...] = vb


def _inproj(x2, mod3, mod_row0, g_pre, cos_t, sin_t, w_in_b, *, seq, tm, is_ctx):
    t, d = x2.shape
    per = seq // tm
    n_in = w_in_b.shape[1]
    wa = A_HEADS * 2 * A_HEAD_DIM
    wkb = B_KV_HEADS * B_HEAD_DIM

    def row(i):
        return (i // per) if not is_ctx else 0

    tok = lambda w: pl.BlockSpec((tm, w), lambda i: (i, 0))
    out_shape = [
        jax.ShapeDtypeStruct((t, wa), BF16), jax.ShapeDtypeStruct((t, wa), BF16),
        jax.ShapeDtypeStruct((t, wa), BF16), jax.ShapeDtypeStruct((t, wa), BF16),
        jax.ShapeDtypeStruct((t, 2 * wkb), BF16), jax.ShapeDtypeStruct((t, wkb), BF16),
        jax.ShapeDtypeStruct((t, d), BF16), jax.ShapeDtypeStruct((t, d), BF16),
    ]
    out_specs = [tok(wa), tok(wa), tok(wa), tok(wa), tok(2 * wkb), tok(wkb), tok(d), tok(d)]
    if not is_ctx:
        out_shape[5] = jax.ShapeDtypeStruct((t, 2 * wkb), BF16)
        out_specs[5] = tok(2 * wkb)
        out_shape[1] = jax.ShapeDtypeStruct((t // seq, wa, seq), BF16)
        out_specs[1] = pl.BlockSpec((1, wa, tm), lambda i: (i // per, 0, i % per))
    if is_ctx:
        out_shape += [jax.ShapeDtypeStruct((t, wa), F32), jax.ShapeDtypeStruct((t, wa), F32),
                      jax.ShapeDtypeStruct((t, wkb), F32), jax.ShapeDtypeStruct((t, wkb), F32)]
        out_specs += [tok(wa), tok(wa), tok(wkb), tok(wkb)]
    return pl.pallas_call(
        functools.partial(_inproj_kernel, is_ctx=is_ctx),
        grid=(t // tm,),
        in_specs=[
            pl.BlockSpec((tm, d), lambda i: (i, 0)),
            pl.BlockSpec((1, 1, d), lambda i: (mod_row0 + row(i), 0, 0)),
            pl.BlockSpec((1, 1, d), lambda i: (mod_row0 + row(i), 0, 1)),
            pl.BlockSpec((1, d), lambda i: (0, 0)),
            pl.BlockSpec((tm, LANES), lambda i: (i % per, 0)),
            pl.BlockSpec((tm, LANES), lambda i: (i % per, 0)),
            pl.BlockSpec((d, n_in), lambda i: (0, 0)),
        ],
        out_specs=out_specs,
        out_shape=out_shape,
        compiler_params=pltpu.CompilerParams(dimension_semantics=("arbitrary",)),
        name="inproj_ctx" if is_ctx else "inproj_lat",
    )(x2, mod3, mod3, g_pre, cos_t, sin_t, w_in_b)


def _nt(a, b):
    return lax.dot_general(a, b, (((1,), (1,)), ((), ())), preferred_element_type=F32)


def _diff_kernel(lam_ref, g_ref, q_ref, k_ref, v_ref, o_ref, *, lambda_init):
    lp = lam_ref[...]
    lam = (jnp.exp(jnp.sum(lp[0:1] * lp[1:2], axis=-1, keepdims=True))
           - jnp.exp(jnp.sum(lp[2:3] * lp[3:4], axis=-1, keepdims=True)) + lambda_init)
    tq = q_ref.shape[1]
    lane = lax.broadcasted_iota(jnp.int32, (1, LANES), 1)
    for h in range(A_HEADS):
        cols = slice(h * LANES, (h + 1) * LANES)
        q = q_ref[0, :, cols]
        q2 = jnp.concatenate([q * (lane < 64).astype(BF16), q * (lane >= 64).astype(BF16)], axis=0)
        s = _nt(q2, k_ref[0, :, cols])
        mx = jnp.max(s, axis=-1, keepdims=True)
        v = v_ref[0, :, cols]
        v_ext = jnp.concatenate([v, jnp.ones_like(v)], axis=1)
        acc = jnp.dot(jnp.exp(s - mx).astype(BF16), v_ext, preferred_element_type=F32)
        on = acc[:, 0:LANES] / acc[:, LANES:2 * LANES]
        o = on[0:tq] - lam * on[tq:2 * tq]
        o_ref[0, :, cols] = (_rms(o, g_ref[...]) * (1.0 - lambda_init)).astype(BF16)


def _diff_attention(lam_p, g_head, q, k, v, *, lambda_init):
    b, s, w = q.shape
    seq = pl.BlockSpec((1, s, w), lambda bi: (bi, 0, 0))
    return pl.pallas_call(
        functools.partial(_diff_kernel, lambda_init=lambda_init),
        grid=(b,),
        in_specs=[pl.BlockSpec((4, A_HEAD_DIM), lambda bi: (0, 0)),
                  pl.BlockSpec((1, A_V_DIM), lambda bi: (0, 0)), seq, seq, seq],
        out_specs=seq,
        out_shape=jax.ShapeDtypeStruct((b, s, w), BF16),
        compiler_params=pltpu.CompilerParams(dimension_semantics=("arbitrary",)),
        name="diff_attn_ctx",
    )(lam_p, g_head, q, k, v)


def _diff_lat_kernel(lam_ref, g_ref, q_ref, kc_ref, kt_ref, vc_ref, v_ref, o_ref, s_a, m_a, s_b, m_b, *,
                     lambda_init):
    t = pl.program_id(0)
    tq = q_ref.shape[1]
    nkc = kc_ref.shape[1]
    nkn = kt_ref.shape[2]
    nk = nkc + nkn

    @pl.when(t == 0)
    def _():
        s_b[...] = jnp.zeros_like(s_b)
        m_b[...] = jnp.zeros_like(m_b)

    def body(s_w, m_w, s_r, m_r):
        lp = lam_ref[...]
        lam = (jnp.exp(jnp.sum(lp[0:1] * lp[1:2], axis=-1, keepdims=True))
               - jnp.exp(jnp.sum(lp[2:3] * lp[3:4], axis=-1, keepdims=True)) + lambda_init)
        lane = lax.broadcasted_iota(jnp.int32, (1, LANES), 1)

        q = q_ref[0]
        q2 = jnp.concatenate([q * (lane < 64).astype(BF16), q * (lane >= 64).astype(BF16)], axis=0)
        sc = _nt(q2, kc_ref[0].astype(BF16))
        sn = jnp.dot(q2, kt_ref[0], preferred_element_type=F32)
        mx = jnp.maximum(jnp.max(sc, axis=-1, keepdims=True), jnp.max(sn, axis=-1, keepdims=True))
        s_w[:, 0:nkc] = sc
        s_w[:, nkc:nk] = sn
        m_w[...] = jnp.broadcast_to(mx, (2 * tq, LANES))

        mp = m_r[...]
        v_all = jnp.concatenate([vc_ref[0].astype(BF16), v_ref[0]], axis=0)
        v_ext = jnp.concatenate([v_all, jnp.ones_like(v_all)], axis=1)
        p = jnp.concatenate(
            [jnp.exp(s_r[:, c:c + LANES] - mp).astype(BF16) for c in range(0, nk, LANES)], axis=1)
        acc = jnp.dot(p, v_ext, preferred_element_type=F32)
        on = acc[:, 0:LANES] / acc[:, LANES:2 * LANES]
        o = on[0:tq] - lam * on[tq:2 * tq]
        o_ref[0] = (_rms(o, g_ref[...]) * (1.0 - lambda_init)).astype(BF16)

    @pl.when(t % 2 == 0)
    def _():
        body(s_a, m_a, s_b, m_b)

    @pl.when(t % 2 == 1)
    def _():
        body(s_b, m_b, s_a, m_a)


def _diff_attention_lat(lam_p, g_head, q, kc, kt, vc, v, *, tq, lambda_init):
    b, s, w = q.shape
    past = kc.shape[1]
    nq = s // tq
    n_units = b * A_HEADS * nq
    last = n_units - 1

    def unit(u):
        return u // (A_HEADS * nq), (u // nq) % A_HEADS, u % nq

    def cur(t):
        return unit(jnp.minimum(t, last))

    def prev(t):
        return unit(jnp.maximum(t - 1, 0))

    return pl.pallas_call(
        functools.partial(_diff_lat_kernel, lambda_init=lambda_init),
        grid=(n_units + 1,),
        in_specs=[
            pl.BlockSpec((4, A_HEAD_DIM), lambda t: (0, 0)),
            pl.BlockSpec((1, A_V_DIM), lambda t: (0, 0)),
            pl.BlockSpec((1, tq, LANES), lambda t: (cur(t)[0], cur(t)[2], cur(t)[1])),
            pl.BlockSpec((1, past, LANES), lambda t: (cur(t)[0], 0, cur(t)[1])),
            pl.BlockSpec((1, LANES, s), lambda t: (cur(t)[0], cur(t)[1], 0)),
            pl.BlockSpec((1, past, LANES), lambda t: (prev(t)[0], 0, prev(t)[1])),
            pl.BlockSpec((1, s, LANES), lambda t: (prev(t)[0], 0, prev(t)[1])),
        ],
        out_specs=pl.BlockSpec((1, tq, LANES), lambda t: (prev(t)[0], prev(t)[2], prev(t)[1])),
        out_shape=jax.ShapeDtypeStruct((b, s, w), BF16),
        scratch_shapes=[pltpu.VMEM((2 * tq, past + s), F32), pltpu.VMEM((2 * tq, LANES), F32),
                        pltpu.VMEM((2 * tq, past + s), F32), pltpu.VMEM((2 * tq, LANES), F32)],
        compiler_params=pltpu.CompilerParams(dimension_semantics=("arbitrary",),
                                             vmem_limit_bytes=56 * 1024 * 1024),
        name="diff_attn_lat",
    )(lam_p, g_head, q, kc, kt, vc, v)


def _swa_kernel(sink_ref, q_ref, k_ref, v_ref, o_ref):
    tq = q_ref.shape[1]
    rows = B_GROUP * tq
    lane = lax.broadcasted_iota(jnp.int32, (1, LANES), 1)
    low = lane < 64
    lane2 = lax.broadcasted_iota(jnp.int32, (1, 2 * LANES), 1)
    head_masks = [((lane2 // 64) == g).astype(BF16) for g in range(B_GROUP)]
    gw = B_GROUP * B_HEAD_DIM
    v = v_ref[0]
    v_ext = jnp.concatenate([v, jnp.ones_like(v)], axis=1)
    finishers = []
    for n in range(B_KV_HEADS):
        q = q_ref[0, :, n * gw:(n + 1) * gw]
        qs = jnp.concatenate([q * hm for hm in head_masks], axis=0)
        k2 = k_ref[0, :, n * LANES:(n + 1) * LANES]
        s = _nt(qs, jnp.concatenate([k2, k2], axis=1))
        sink = jnp.concatenate(
            [jnp.full((tq, 1), sink_ref[n * B_GROUP + g], F32) for g in range(B_GROUP)], axis=0)
        top = jnp.max(s, keepdims=True)
        for g in range(B_GROUP):
            top = jnp.maximum(top, sink_ref[n * B_GROUP + g])

        def finish(mp, n=n, s=s, sink=sink):
            acc = jnp.dot(jnp.exp(s - mp).astype(BF16), v_ext, preferred_element_type=F32)
            den = acc[:, LANES:2 * LANES] + jnp.exp(sink - mp)
            o = acc[:, 0:LANES] / den
            osw = pltpu.roll(o, 64, 1)
            for j in range(B_GROUP // 2):
                ra = slice((2 * j) * tq, (2 * j + 1) * tq)
                rb = slice((2 * j + 1) * tq, (2 * j + 2) * tq)
                pair = jnp.where(low, o[ra], osw[rb]) if n == 0 else jnp.where(low, osw[ra], o[rb])
                o_ref[0, :, n * gw + j * LANES:n * gw + (j + 1) * LANES] = pair.astype(BF16)
            return jnp.min(den)

        finishers.append((finish, s, sink, finish(top)))

    smallest = functools.reduce(jnp.minimum, [f[3] for f in finishers])

    @pl.when(jnp.logical_not(smallest >= 1e-30))
    def _():
        for finish, s, sink, _ in finishers:
            finish(jnp.maximum(jnp.max(s, axis=-1, keepdims=True), sink))


def _swa_attention(sink, q, k2, v):
    b, s, w = q.shape
    seq = lambda a: pl.BlockSpec((1, s, a.shape[2]), lambda bi: (bi, 0, 0))
    return pl.pallas_call(
        _swa_kernel,
        grid=(b,),
        in_specs=[pl.BlockSpec(memory_space=pltpu.SMEM), seq(q), seq(k2), seq(v)],
        out_specs=seq(q),
        out_shape=jax.ShapeDtypeStruct((b, s, w), BF16),
        compiler_params=pltpu.CompilerParams(dimension_semantics=("arbitrary",)),
        name="swa_attn_ctx",
    )(sink, q, k2, v)


def _swa_lat_kernel(sink_ref, q_ref, kc_ref, kl_ref, km_ref, kr_ref, vc_ref, vl_ref, vm_ref, vr_ref, o_ref,
                    s_a, m_a, s_b, m_b, *, nqb):
    t = pl.program_id(0)
    n_units = pl.num_programs(0) - 1
    tq = q_ref.shape[1]
    gw = B_GROUP * B_HEAD_DIM
    i_cur = jnp.minimum(t, n_units - 1) % nqb
    nkc = kc_ref.shape[1]
    rows = B_GROUP * tq

    @pl.when(t == 0)
    def _():
        s_b[...] = jnp.zeros_like(s_b)
        m_b[...] = jnp.zeros_like(m_b)

    def sink_col(n):
        return jnp.concatenate(
            [jnp.full((tq, 1), sink_ref[n * B_GROUP + g], F32) for g in range(B_GROUP)], axis=0)

    def body(s_w, m_w, s_r, m_r):
        lane = lax.broadcasted_iota(jnp.int32, (1, LANES), 1)
        low = lane < 64
        lane2 = lax.broadcasted_iota(jnp.int32, (1, 2 * LANES), 1)
        head_masks = [((lane2 // 64) == g).astype(BF16) for g in range(B_GROUP)]
        qi = lax.broadcasted_iota(jnp.int32, (rows, SWA_Q), 0) & (tq - 1)
        kj = lax.broadcasted_iota(jnp.int32, (rows, SWA_Q), 1)
        far = 2 * SWA_Q
        left_ok = kj >= qi + jnp.where(i_cur > 0, 0, far)
        right_ok = kj <= qi - jnp.where(i_cur < nqb - 1, 0, far)

        kc = kc_ref[0]
        kc_sw = pltpu.roll(kc, 64, 1)
        for n in range(B_KV_HEADS):
            q = q_ref[0, :, n * gw:(n + 1) * gw]
            qs = jnp.concatenate([q * hm for hm in head_masks], axis=0)
            kc2 = (jnp.where(low, kc, kc_sw) if n == 0 else jnp.where(low, kc_sw, kc)).astype(BF16)
            ks = [kc2] + [r[0, :, n * LANES:(n + 1) * LANES] for r in (kl_ref, km_ref, kr_ref)]
            k_all = jnp.concatenate([jnp.concatenate([k, k], axis=1) for k in ks], axis=0)
            s = _nt(qs, k_all)
            chunks = [s[:, c:c + LANES] for c in range(0, s.shape[1], LANES)]
            il = nkc // LANES
            chunks[il] = jnp.where(left_ok, chunks[il], NEG_INF)
            chunks[il + 2] = jnp.where(right_ok, chunks[il + 2], NEG_INF)
            top = jnp.max(functools.reduce(jnp.maximum, chunks), keepdims=True)
            for g in range(B_GROUP):
                top = jnp.maximum(top, sink_ref[n * B_GROUP + g])
            for c, ch in enumerate(chunks):
                s_w[n * rows:(n + 1) * rows, c * LANES:(c + 1) * LANES] = ch
            m_w[n * rows:(n + 1) * rows, :] = jnp.broadcast_to(top, (rows, LANES))

        vc = vc_ref[0]
        vc_sw = pltpu.roll(vc, 64, 1)
        nk = nkc + 3 * tq
        finishers = []
        for n in range(B_KV_HEADS):
            vc2 =(jnp.where(low, vc, vc_sw) if n == 0 else jnp.where(low, vc_sw, vc)).astype(BF16)
            v_all = jnp.concatenate(
                [vc2] + [r[0, :, n * LANES:(n + 1) * LANES] for r in (vl_ref, vm_ref, vr_ref)], axis=0)
            v_ext = jnp.concatenate([v_all, jnp.ones_like(v_all)], axis=1)
            sink = sink_col(n)

            def finish(mp, n=n, v_ext=v_ext, sink=sink):
                p = jnp.concatenate([jnp.exp(s_r[n * rows:(n + 1) * rows, c:c + LANES] - mp).astype(BF16)
                                     for c in range(0, nk, LANES)], axis=1)
                acc = jnp.dot(p, v_ext, preferred_element_type=F32)
                den = acc[:, LANES:2 * LANES] + jnp.exp(sink - mp)
                o = acc[:, 0:LANES] / den
                for j in range(B_GROUP // 2):
                    pair = jnp.where(low, o[(2 * j) * tq:(2 * j + 1) * tq], o[(2 * j + 1) * tq:(2 * j + 2) * tq])
                    o_ref[0, :, n * gw + j * LANES:n * gw + (j + 1) * LANES] = pair.astype(BF16)
                return jnp.min(den)

            finishers.append((n, finish, sink, finish(m_r[n * rows:(n + 1) * rows, :])))

        smallest = functools.reduce(jnp.minimum, [f[3] for f in finishers])

        @pl.when(jnp.logical_not(smallest >= 1e-30))
        def _():
            for n, finish, sink, _ in finishers:
                row_max = functools.reduce(
                    jnp.maximum, [s_r[n * rows:(n + 1) * rows, c:c + LANES] for c in range(0, nk, LANES)])
                mx = jnp.maximum(jnp.max(row_max, axis=-1, keepdims=True), sink)
                finish(jnp.broadcast_to(mx, (rows, LANES)))

    @pl.when(t % 2 == 0)
    def _():
        body(s_a, m_a, s_b, m_b)

    @pl.when(t % 2 == 1)
    def _():
        body(s_b, m_b, s_a, m_a)


def _swa_attention_lat(sink, q, kc, k2, vc, v):
    b, s, w = q.shape
    past = kc.shape[1]
    tq = SWA_Q
    nqb = s // tq
    n_units = b * nqb
    last = n_units - 1
    nk = past + 3 * tq

    def cur(t):
        u = jnp.minimum(t, last)
        return u // nqb, u % nqb

    def prev(t):
        u = jnp.maximum(t - 1, 0)
        return u // nqb, u % nqb

    lo = lambda i: jnp.maximum(i - 1, 0)
    hi = lambda i: jnp.minimum(i + 1, nqb - 1)
    kspec = lambda f: pl.BlockSpec((1, tq, 2 * LANES), lambda t: (cur(t)[0], f(cur(t)[1]), 0))
    vspec = lambda f: pl.BlockSpec((1, tq, 2 * LANES), lambda t: (prev(t)[0], f(prev(t)[1]), 0))
    same = lambda i: i
    rows = B_KV_HEADS * B_GROUP * tq
    return pl.pallas_call(
        functools.partial(_swa_lat_kernel, nqb=nqb),
        grid=(n_units + 1,),
        in_specs=[
            pl.BlockSpec(memory_space=pltpu.SMEM),
            pl.BlockSpec((1, tq, w), lambda t: (cur(t)[0], cur(t)[1], 0)),
            pl.BlockSpec((1, past, LANES), lambda t: (cur(t)[0], 0, 0)),
            kspec(lo), kspec(same), kspec(hi),
            pl.BlockSpec((1, past, LANES), lambda t: (prev(t)[0], 0, 0)),
            vspec(lo), vspec(same), vspec(hi),
        ],
        out_specs=pl.BlockSpec((1, tq, w), lambda t: (prev(t)[0], prev(t)[1], 0)),
        out_shape=jax.ShapeDtypeStruct((b, s, w), BF16),
        scratch_shapes=[pltpu.VMEM((rows, nk), F32), pltpu.VMEM((rows, LANES), F32),
                        pltpu.VMEM((rows, nk), F32), pltpu.VMEM((rows, LANES), F32)],
        compiler_params=pltpu.CompilerParams(dimension_semantics=("arbitrary",)),
        name="swa_attn_lat",
    )(sink, q, kc, k2, k2, k2, vc, v, v, v)


def _postmix_kernel(*refs, n_ctx_tiles):
    (xc, xl, oac, oal, obc, obl, sgac, sgal, sgbc, sgbl, g1_ref, sh2_ref, sc2_ref, gpm_ref, gpf_ref,
     wpa_ref, wpb_ref, wo_ref, wr_ref, br_ref, x1_o, h2_o, route_o, cnt_o, x1_a, x1_b) = refs
    t = pl.program_id(0)
    is_ctx = t < n_ctx_tiles
    pick = lambda a, b: jnp.where(is_ctx, a[...], b[...])

    @pl.when(t == 0)
    def _():
        x1_b[...] = jnp.zeros_like(x1_b)

    def body(x1_w, x1_r):
        pa = jnp.dot(pick(oac, oal), wpa_ref[...], preferred_element_type=F32)
        pb = jnp.dot(pick(obc, obl), wpb_ref[...], preferred_element_type=F32)
        mix = pick(sgac, sgal).astype(F32) * pa + pick(sgbc, sgbl).astype(F32) * pb
        m2 = jnp.dot(mix.astype(BF16), wo_ref[...], preferred_element_type=F32)
        x1 = pick(xc, xl) + g1_ref[0] * _rms(m2, gpm_ref[...])
        x1_o[...] = x1
        x1_w[...] = x1
        _postmix_route(x1_r[...], sh2_ref, sc2_ref, gpf_ref, wr_ref, br_ref, h2_o, route_o, cnt_o)

    @pl.when(t % 2 == 0)
    def _():
        body(x1_a, x1_b)

    @pl.when(t % 2 == 1)
    def _():
        body(x1_b, x1_a)


def _postmix_route(x1, sh2_ref, sc2_ref, gpf_ref, wr_ref, br_ref, h2_o, route_o, cnt_o):
    h2 = _rms(x1, gpf_ref[...]) * (1.0 + sc2_ref[0]) + sh2_ref[0]
    h2_o[...] = h2.astype(BF16)

    h_hi = h2.astype(BF16)
    h_lo = (h2 - h_hi.astype(F32)).astype(BF16)
    both = jnp.dot(h_hi, wr_ref[...], preferred_element_type=F32)
    logits = (both[:, 0:LANES] + both[:, LANES:2 * LANES]
              + jnp.dot(h_lo, wr_ref[:, 0:LANES], preferred_element_type=F32) + br_ref[...])
    tm = logits.shape[0]
    lt = logits.T
    row = lax.broadcasted_iota(jnp.int32, (EXPERTS_PER_GROUP, tm), 0).astype(F32)
    none = float(EXPERTS_PER_GROUP)
    lg = jnp.where(row < N_GROUPS, lt[N_EXPERTS:N_EXPERTS + EXPERTS_PER_GROUP], -jnp.inf)
    mg = jnp.max(lg, axis=0, keepdims=True)
    g_sel = jnp.min(jnp.where(lg == mg, row, none), axis=0, keepdims=True)
    g_w = 1.0 / jnp.sum(jnp.exp(lg - mg), axis=0, keepdims=True)
    le = lt[0:EXPERTS_PER_GROUP]
    for g in range(1, N_GROUPS):
        le = jnp.where(g_sel == g, lt[g * EXPERTS_PER_GROUP:(g + 1) * EXPERTS_PER_GROUP], le)
    v0 = jnp.max(le, axis=0, keepdims=True)
    i0 = jnp.min(jnp.where(le == v0, row, none), axis=0, keepdims=True)
    le1 = jnp.where(row == i0, -jnp.inf, le)
    v1 = jnp.max(le1, axis=0, keepdims=True)
    i1 = jnp.min(jnp.where(le1 == v1, row, none), axis=0, keepdims=True)
    e = jnp.exp(v1 - v0)
    w0 = g_w / (1.0 + e)
    w1 = g_w * e / (1.0 + e)
    e0 = g_sel * EXPERTS_PER_GROUP + i0
    e1 = g_sel * EXPERTS_PER_GROUP + i1
    route_o[...] = jnp.where(row == 0, e0, jnp.where(row == 1, e1, jnp.where(row == 2, w0,
                                                                             jnp.where(row == 3, w1, 0.0))))
    erow = lax.broadcasted_iota(jnp.int32, (N_EXPERTS, tm), 0).astype(F32)
    cnt = jnp.sum((erow == e0).astype(F32) + (erow == e1).astype(F32), axis=1, keepdims=True)
    cnt_o[0] = jnp.broadcast_to(cnt, (N_EXPERTS, LANES))


def _postmix(ctx_in, lat_in, mod3, gpm, gpf, wpa, wpb, wo, wr, br, *, lat_seq, tm):
    t_ctx, d = ctx_in[0].shape
    t_lat = lat_in[0].shape[0]
    assert t_ctx % tm == 0 and lat_seq % tm == 0
    nc = t_ctx // tm
    nl = t_lat // tm
    per = lat_seq // tm
    sub = d // LANES
    t_all = t_ctx + t_lat

    last = nc + nl - 1
    cur = lambda i: jnp.minimum(i, last)
    prev = lambda i: jnp.maximum(i - 1, 0)
    mod_row = lambda j: jnp.where(j < nc, 0, 1 + jnp.maximum(j - nc, 0) // per)
    full = lambda a: pl.BlockSpec(a.shape, lambda i: (0,) * a.ndim)
    in_specs, args = [], []
    for a_c, a_l in zip(ctx_in, lat_in):
        w = a_c.shape[1]
        in_specs += [pl.BlockSpec((tm, w), lambda i: (jnp.minimum(cur(i), nc - 1), 0)),
                     pl.BlockSpec((tm, w), lambda i: (jnp.maximum(cur(i) - nc, 0), 0))]
        args += [a_c, a_l]
    in_specs += [pl.BlockSpec((1, 1, d), lambda i: (mod_row(cur(i)), 0, 2)),
                 pl.BlockSpec((1, 1, d), lambda i: (mod_row(prev(i)), 0, 3)),
                 pl.BlockSpec((1, 1, d), lambda i: (mod_row(prev(i)), 0, 4)),
                 full(gpm), full(gpf), full(wpa), full(wpb), full(wo), full(wr), full(br)]
    args += [mod3, mod3, mod3, gpm, gpf, wpa, wpb, wo, wr, br]
    return pl.pallas_call(
        functools.partial(_postmix_kernel, n_ctx_tiles=nc),
        grid=(nc + nl + 1,),
        in_specs=in_specs,
        out_specs=[pl.BlockSpec((tm, d), lambda i: (cur(i), 0)),
                   pl.BlockSpec((tm, d), lambda i: (prev(i), 0)),
                   pl.BlockSpec((EXPERTS_PER_GROUP, tm), lambda i: (0, prev(i))),
                   pl.BlockSpec((1, N_EXPERTS, LANES), lambda i: (prev(i), 0, 0))],
        out_shape=[jax.ShapeDtypeStruct((t_all, d), F32),
                   jax.ShapeDtypeStruct((t_all, d), BF16),
                   jax.ShapeDtypeStruct((EXPERTS_PER_GROUP, t_all), F32),
                   jax.ShapeDtypeStruct((nc + nl, N_EXPERTS, LANES), F32)],
        scratch_shapes=[pltpu.VMEM((tm, d), F32), pltpu.VMEM((tm, d), F32)],
        compiler_params=pltpu.CompilerParams(dimension_semantics=("arbitrary",)),
        name="postmix",
    )(*args)


def _segment_copies(src, src_row, dst, dst_row, n, sub, sem):
    @pl.when(n > 0)
    def _():
        pltpu.make_async_copy(src.at[pl.ds(pl.multiple_of(src_row * sub, sub), n * sub)],
                              dst.at[pl.ds(pl.multiple_of(dst_row * sub, sub), n * sub)], sem).start()


def _local_positions(route_t, tile_base):
    tm = route_t.shape[1]
    erow = lax.broadcasted_iota(jnp.int32, (N_EXPERTS, tm), 0).astype(F32)
    is0 = erow == route_t[0:1]
    is1 = erow == route_t[1:2]
    earlier = (lax.broadcasted_iota(jnp.int32, (tm, tm), 0)
               < lax.broadcasted_iota(jnp.int32, (tm, tm), 1)).astype(BF16)
    pre0 = jnp.dot(is0.astype(BF16), earlier, preferred_element_type=F32)
    pre1 = jnp.dot(is1.astype(BF16), earlier, preferred_element_type=F32)
    cnt0 = jnp.sum(is0.astype(F32), axis=1, keepdims=True)
    base = tile_base[:, 0:1]
    lpos0 = jnp.sum(jnp.where(is0, base + pre0, 0.0), axis=0, keepdims=True)
    lpos1 = jnp.sum(jnp.where(is1, base + cnt0 + pre1, 0.0), axis=0, keepdims=True)
    return lpos0, lpos1


def _dispatch_kernel(ss_ref, sl_ref, tb_ref, h_ref, r_ref, tbv_ref, xs_hbm, pbuf, zbuf, sem, zsem, *, n_asg, rows):
    i = pl.program_id(0)
    nt = pl.num_programs(0)
    tm, d = h_ref.shape
    sub = d // LANES
    nrow = TOP_K * tm
    slot = i % 2

    def wait_slot(s):
        pltpu.make_async_copy(pbuf.at[pl.ds(pl.multiple_of(s * nrow * sub, nrow * sub), nrow * sub)],
                              xs_hbm.at[pl.ds(0, nrow * sub)], sem.at[s]).wait()

    def slack_copy():
        return pltpu.make_async_copy(zbuf, xs_hbm.at[pl.ds(n_asg * sub, rows * sub)], zsem.at[0])

    @pl.when(i == 0)
    def _():
        zbuf[...] = jnp.zeros_like(zbuf)
        slack_copy().start()

    lpos0, lpos1 = _local_positions(r_ref[...], tbv_ref[0])
    p = lax.broadcasted_iota(jnp.int32, (nrow, tm), 0).astype(F32)
    sel = ((p == lpos0) | (p == lpos1)).astype(BF16)
    xp = jnp.dot(sel, h_ref[...], preferred_element_type=F32)

    @pl.when(i >= 2)
    def _():
        wait_slot(slot)

    _store_row_tiles(pbuf, slot * nrow * sub, xp)

    def seg(e, c):
        k = i * N_EXPERTS + e
        _segment_copies(pbuf, slot * nrow + tb_ref[k], xs_hbm, ss_ref[k], sl_ref[k], sub, sem.at[slot])
        return c
    lax.fori_loop(0, N_EXPERTS, seg, 0)

    @pl.when(i == nt - 1)
    def _():
        wait_slot(slot)

        @pl.when(nt >= 2)
        def _():
            wait_slot(1 - slot)
        slack_copy().wait()


def _dispatch(h2, route, tables, *, tm, rows):
    seg_start, seg_len, tile_base, tile_base_v = tables
    t, d = h2.shape
    sub = d // LANES
    n_asg = t * TOP_K
    grid_spec = pltpu.PrefetchScalarGridSpec(
        num_scalar_prefetch=3,
        grid=(t // tm,),
        in_specs=[
            pl.BlockSpec((tm, d), lambda i, *_: (i, 0)),
            pl.BlockSpec((EXPERTS_PER_GROUP, tm), lambda i, *_: (0, i)),
            pl.BlockSpec((1, N_EXPERTS, LANES), lambda i, *_: (i, 0, 0)),
        ],
        out_specs=pl.BlockSpec(memory_space=pl.ANY),
        scratch_shapes=[pltpu.VMEM((2 * TOP_K * tm * sub, LANES), F32), pltpu.VMEM((rows * sub, LANES), F32),
                        pltpu.SemaphoreType.DMA((2,)), pltpu.SemaphoreType.DMA((1,))],
    )
    return pl.pallas_call(
        functools.partial(_dispatch_kernel, n_asg=n_asg, rows=rows),
        grid_spec=grid_spec,
        out_shape=jax.ShapeDtypeStruct(((n_asg + rows) * sub, LANES), F32),
        compiler_params=pltpu.CompilerParams(dimension_semantics=("arbitrary",)),
        name="dispatch",
    )(seg_start, seg_len, tile_base, h2, route, tile_base_v)


def _moe_kernel(be_ref, row0_ref, nact_ref, par_ref, nxt_ref, xs_hbm, w1_hbm, w3_hbm, w2_hbm, ys_hbm,
                xbuf, obuf, wf1, wf3, wf2, w1b, w3b, w2b, rsem, wsem, gsem, *, rows, sub):
    i = pl.program_id(0)
    nact = nact_ref[0]
    slot = i % 2
    nslot = 1 - slot
    blk = rows * sub

    def weight_copies(e, s):
        return [pltpu.make_async_copy(w_hbm.at[e], wf.at[s], gsem.at[s])
                for w_hbm, wf in ((w1_hbm, wf1), (w3_hbm, wf3), (w2_hbm, wf2))]

    def read(j, s):
        return pltpu.make_async_copy(xs_hbm.at[pl.ds(pl.multiple_of(row0_ref[j] * sub, sub), blk)],
                                     xbuf.at[pl.ds(pl.multiple_of(s * blk, blk), blk)], rsem.at[s])

    def write(j, s):
        return pltpu.make_async_copy(obuf.at[pl.ds(pl.multiple_of(s * blk, blk), blk)],
                                     ys_hbm.at[pl.ds(pl.multiple_of(row0_ref[j] * sub, sub), blk)], wsem.at[s])

    @pl.when(i == 0)
    def _():
        read(0, 0).start()
        for c in weight_copies(be_ref[0], 0):
            c.start()

    @pl.when(i < nact)
    def _():
        @pl.when(i + 1 < nact)
        def _():
            read(i + 1, nslot).start()

        changed = jnp.logical_or(i == 0, be_ref[i] != be_ref[jnp.maximum(i - 1, 0)])

        @pl.when(changed)
        def _():
            s = par_ref[i]
            for c in weight_copies(be_ref[i], s):
                c.wait()
            w1b[...] = wf1[s].astype(BF16)
            w3b[...] = wf3[s].astype(BF16)
            w2b[...] = wf2[s].astype(BF16)

            @pl.when(nxt_ref[i] >= 0)
            def _():
                for c in weight_copies(nxt_ref[i], 1 - s):
                    c.start()

        read(i, slot).wait()
        x = _load_row_tiles(xbuf, slot * blk, rows, sub).astype(BF16)
        a = jnp.dot(x, w1b[...], preferred_element_type=F32)
        b = jnp.dot(x, w3b[...], preferred_element_type=F32)
        hmid = (a * jax.nn.sigmoid(a) * b).astype(BF16)
        y = jnp.dot(hmid, w2b[...], preferred_element_type=F32)
        _store_row_tiles(obuf, slot * blk, y)

        @pl.when(i >= 1)
        def _():
            write(i - 1, nslot).wait()
        write(i, slot).start()

    @pl.when(i == nact)
    def _():
        write(i - 1, nslot).wait()
        obuf[pl.ds(pl.multiple_of(slot * blk, blk), blk), :] = jnp.zeros((blk, LANES), F32)
        tail = pltpu.make_async_copy(obuf.at[pl.ds(pl.multiple_of(slot * blk, blk), blk)],
                                     ys_hbm.at[pl.ds(ys_hbm.shape[0] - blk, blk)], wsem.at[slot])
        tail.start()
        tail.wait()


def _moe(blk_tables, xs, w1, w3, w2):
    blk_expert, row0, nact, parity, nxt = blk_tables
    nblk = blk_expert.shape[0] - 1
    d, de = w1.shape[1], w1.shape[2]
    sub = d // LANES
    rows = MOE_ROWS
    anyspec = pl.BlockSpec(memory_space=pl.ANY)
    grid_spec = pltpu.PrefetchScalarGridSpec(
        num_scalar_prefetch=5,
        grid=(nblk + 1,),
        in_specs=[anyspec, anyspec, anyspec, anyspec],
        out_specs=anyspec,
        scratch_shapes=[
            pltpu.VMEM((2 * rows * sub, LANES), F32),
            pltpu.VMEM((2 * rows * sub, LANES), F32),
            pltpu.VMEM((2, d, de), F32),
            pltpu.VMEM((2, d, de), F32),
            pltpu.VMEM((2, de, d), F32),
            pltpu.VMEM((d, de), BF16),
            pltpu.VMEM((d, de), BF16),
            pltpu.VMEM((de, d), BF16),
            pltpu.SemaphoreType.DMA((2,)),
            pltpu.SemaphoreType.DMA((2,)),
            pltpu.SemaphoreType.DMA((2,)),
        ],
    )
    return pl.pallas_call(
        functools.partial(_moe_kernel, rows=rows, sub=sub),
        grid_spec=grid_spec,
        out_shape=jax.ShapeDtypeStruct(xs.shape, F32),
        compiler_params=pltpu.CompilerParams(dimension_semantics=("arbitrary",)),
        name="expert_mlp",
    )(blk_expert, row0, nact, parity, nxt, xs, w1, w3, w2)


def _combine_kernel(ss_ref, sl_ref, tb_ref, x1_ref, r_ref, tbv_ref, g2_ref, gpost_ref, ys_hbm, o_ref,
                    ybuf, sem, *, tile0):
    i = pl.program_id(0)
    nt = pl.num_programs(0)
    tm, d = x1_ref.shape
    sub = d // LANES
    nrow = TOP_K * tm
    slot = i % 2

    def fetch(tile, s):
        def seg(e, c):
            k = tile * N_EXPERTS + e
            _segment_copies(ys_hbm, ss_ref[k], ybuf, s * nrow + tb_ref[k], sl_ref[k], sub, sem.at[s])
            return c
        lax.fori_loop(0, N_EXPERTS, seg, 0)

    @pl.when(i == 0)
    def _():
        fetch(tile0, 0)

    @pl.when(i + 1 < nt)
    def _():
        fetch(tile0 + i + 1, 1 - slot)

    pltpu.make_async_copy(ys_hbm.at[pl.ds(0, nrow * sub)],
                          ybuf.at[pl.ds(pl.multiple_of(slot * nrow * sub, nrow * sub), nrow * sub)],
                          sem.at[slot]).wait()
    rt = r_ref[...]
    lpos0, lpos1 = _local_positions(rt, tbv_ref[0])
    rows8 = lax.broadcasted_iota(jnp.int32, (EXPERTS_PER_GROUP, tm), 0)
    cols = jnp.where(rows8 == 0, lpos0, jnp.where(rows8 == 1, lpos1, rt))
    cols = jnp.concatenate([cols, jnp.zeros((LANES - EXPERTS_PER_GROUP, tm), F32)], axis=0).T
    p = lax.broadcasted_iota(jnp.int32, (tm, nrow), 1).astype(F32)
    q = (jnp.where(p == cols[:, 0:1], cols[:, 2:3], 0.0)
         + jnp.where(p == cols[:, 1:2], cols[:, 3:4], 0.0)).astype(BF16)
    ysort = _load_row_tiles(ybuf, slot * nrow * sub, nrow, sub).astype(BF16)
    y = jnp.dot(q, ysort, preferred_element_type=F32)
    o_ref[...] = x1_ref[...] + g2_ref[0] * _rms(y, gpost_ref[...])


def _combine(x1, ys, route, tables, mod3, mod_row0, gpost, *, t, seq, tm, tok_off, is_ctx):
    seg_start, seg_len, tile_base, tile_base_v = tables
    t_all, d = x1.shape
    per = seq // tm
    sub = d // LANES
    boff = tok_off // tm

    def row(i):
        return (i // per) if not is_ctx else 0

    grid_spec = pltpu.PrefetchScalarGridSpec(
        num_scalar_prefetch=3,
        grid=(t // tm,),
        in_specs=[
            pl.BlockSpec((tm, d), lambda i, *_: (boff + i, 0)),
            pl.BlockSpec((EXPERTS_PER_GROUP, tm), lambda i, *_: (0, boff + i)),
            pl.BlockSpec((1, N_EXPERTS, LANES), lambda i, *_: (boff + i, 0, 0)),
            pl.BlockSpec((1, 1, d), lambda i, *_: (mod_row0 + row(i), 0, 5)),
            pl.BlockSpec((1, d), lambda i, *_: (0, 0)),
            pl.BlockSpec(memory_space=pl.ANY),
        ],
        out_specs=pl.BlockSpec((tm, d), lambda i, *_: (i, 0)),
        scratch_shapes=[pltpu.VMEM((2 * TOP_K * tm * sub, LANES), F32), pltpu.SemaphoreType.DMA((2,))],
    )
    return pl.pallas_call(
        functools.partial(_combine_kernel, tile0=boff),
        grid_spec=grid_spec,
        out_shape=jax.ShapeDtypeStruct((t, d), F32),
        compiler_params=pltpu.CompilerParams(dimension_semantics=("arbitrary",)),
        name="combine_ctx" if is_ctx else "combine_lat",
    )(seg_start, seg_len, tile_base, x1, route, tile_base_v, mod3, gpost, ys)


def _routing_tables(counts, tm, rows):
    nt = counts.shape[0]
    n_asg = nt * tm * TOP_K
    ex = jnp.arange(N_EXPERTS, dtype=jnp.int32)
    cnt_te = counts[:, :, 0].astype(jnp.int32)
    cnt_e = jnp.sum(cnt_te, axis=0)
    start_e = jnp.cumsum(cnt_e) - cnt_e
    seg_start = start_e[None, :] + jnp.cumsum(cnt_te, axis=0) - cnt_te
    tile_base = jnp.cumsum(cnt_te, axis=1) - cnt_te
    tile_base_v = jnp.broadcast_to(tile_base.astype(F32)[:, :, None], (nt, N_EXPERTS, LANES))

    nblk_e = (cnt_e + rows - 1) // rows
    blk_end = jnp.cumsum(nblk_e)
    blk_start = blk_end - nblk_e
    n_blocks = n_asg // rows + N_EXPERTS
    b = jnp.arange(n_blocks + 1, dtype=jnp.int32)
    be = jnp.minimum(jnp.sum((blk_end[None, :] <= b[:, None]).astype(jnp.int32), axis=1), N_EXPERTS - 1)
    first = jnp.sum(jnp.where(be[:, None] == ex[None, :], (start_e - blk_start * rows)[None, :], 0), axis=1)
    row0 = jnp.clip(first + b * rows, 0, n_asg)
    nact = blk_end[-1:].astype(jnp.int32)
    used = cnt_e > 0
    parity_e = (jnp.cumsum(used.astype(jnp.int32)) - 1) % 2
    later = (ex[None, :] > ex[:, None]) & used[None, :]
    nxt_e = jnp.min(jnp.where(later, ex[None, :], N_EXPERTS), axis=1)
    nxt_e = jnp.where(nxt_e == N_EXPERTS, -1, nxt_e)
    pick = lambda tab: jnp.sum(jnp.where(be[:, None] == ex[None, :], tab[None, :], 0), axis=1).astype(jnp.int32)
    seg = (seg_start.reshape(-1).astype(jnp.int32), cnt_te.reshape(-1).astype(jnp.int32),
           tile_base.reshape(-1).astype(jnp.int32), tile_base_v)
    blk = (be.astype(jnp.int32), row0.astype(jnp.int32), nact, pick(parity_e), pick(nxt_e))
    return seg, blk


def _rope_tables(n_tok):
    n_rows = n_tok // GRID_W
    rows = jnp.repeat(jnp.arange(n_rows), GRID_W).astype(F32)
    cols = jnp.tile(jnp.arange(GRID_W), n_rows).astype(F32)
    quarter = A_HEAD_DIM // 4
    inv = ROPE_BASE ** (-jnp.arange(quarter, dtype=F32) / quarter)
    ang = jnp.concatenate([rows[:, None] * inv, cols[:, None] * inv], axis=-1)
    cos, sin = jnp.cos(ang), jnp.sin(ang)
    cos_t = jnp.tile(jnp.concatenate([cos, cos], axis=-1), (1, LANES // A_HEAD_DIM))
    sin_t = jnp.tile(jnp.concatenate([-sin, sin], axis=-1), (1, LANES // A_HEAD_DIM))
    return cos_t, sin_t


def kernel(x_prompt, x_sample, c, cache_diff_k, cache_diff_v, cache_swa_k, cache_swa_v, c_ctx, w_ada, b_ada, g_pre_mix, g_post_mix, g_pre_ffn, g_post_ffn, w_in, lam_q1, lam_k1, lam_q2, lam_k2, g_diff_head, sink, w_proj_a, w_proj_b, w_out, w_router_group, b_router_group, w_router_expert, b_router_expert, w_e1, w_e3, w_e2):
    depth = w_in.shape[0]
    assert depth == 1
    l = 0
    bp, sp, d = x_prompt.shape
    bs, ss, _ = x_sample.shape
    lambda_init = 0.8 - 0.6 * math.exp(-0.3 * l)
    assert A_HEAD_DIM == B_HEAD_DIM and ss % GRID_W == 0 and bs + 1 <= MOD_ROWS

    c_all = jnp.concatenate([c_ctx[None, :], c, jnp.zeros((MOD_ROWS - 1 - bs, d), F32)], axis=0)
    mod = _modulation(c_all, w_ada[l], b_ada[l][None, :])
    mod3 = mod.reshape(MOD_ROWS, 1, 6 * d)

    w_in_b = w_in[l].astype(BF16)
    wpa = w_proj_a[l].astype(BF16)
    wpb = w_proj_b[l].astype(BF16)
    wo = w_out[l].astype(BF16)
    n_r = N_GROUPS + N_EXPERTS
    wr = jnp.concatenate([w_router_expert[l], w_router_group[l], jnp.zeros((d, LANES - n_r), F32)], axis=1)
    br = jnp.concatenate([b_router_expert[l], b_router_group[l], jnp.zeros((LANES - n_r,), F32)])[None, :]
    wr_hi = wr.astype(BF16)
    wr2 = jnp.concatenate([wr_hi, (wr - wr_hi.astype(F32)).astype(BF16)], axis=1)
    lam_p = jnp.stack([lam_q1[l], lam_k1[l], lam_q2[l], lam_k2[l]], axis=0)
    g_head = g_diff_head[l][None, :]
    sink_l = sink[l]
    cos_t, sin_t = _rope_tables(ss)

    xp2 = x_prompt.reshape(bp * sp, d)
    xs2 = x_sample.reshape(bs * ss, d)
    gpre = g_pre_mix[l][None, :]

    (qa_c, ka_c, va_c, qb_c, kb2_c, vb_c, sga_c, sgb_c, kaf, vaf, kbf, vbf) = _inproj(
        xp2, mod3, 0, gpre, cos_t, sin_t, w_in_b, seq=sp, tm=sp, is_ctx=True)
    r3 = lambda a, b_: a.reshape(b_, -1, a.shape[-1])
    oa_c = _diff_attention(lam_p, g_head, r3(qa_c, bp), r3(ka_c, bp), r3(va_c, bp), lambda_init=lambda_init)
    ob_c = _swa_attention(sink_l, r3(qb_c, bp), r3(kb2_c, bp), r3(vb_c, bp))

    (qa_s, ka_s, va_s, qb_s, kb2_s, vb_s, sga_s, sgb_s) = _inproj(
        xs2, mod3, 1, gpre, cos_t, sin_t, w_in_b, seq=ss, tm=512, is_ctx=False)
    past = cache_diff_k.shape[2]
    ck = cache_diff_k[:, l].reshape(bs, past, -1)
    cv = cache_diff_v[:, l].reshape(bs, past, -1)
    oa_s = _diff_attention_lat(lam_p, g_head, r3(qa_s, bs), ck, ka_s, cv, r3(va_s, bs),
                               tq=512, lambda_init=lambda_init)
    sk = cache_swa_k[:, l].reshape(bs, past, -1)
    sv = cache_swa_v[:, l].reshape(bs, past, -1)
    nqb = ss // SWA_Q
    kb2_3, vb_3 = r3(kb2_s, bs), r3(vb_s, bs)
    ob_s = _swa_attention_lat(sink_l, r3(qb_s, bs), sk, kb2_3, sv, vb_3)

    gpm = g_post_mix[l][None, :]
    gpf = g_pre_ffn[l][None, :]
    t_ctx, t_lat = bp * sp, bs * ss
    x1, h2t, route, counts = _postmix(
        (xp2, oa_c.reshape(t_ctx, -1), ob_c.reshape(t_ctx, -1), sga_c, sgb_c),
        (xs2, oa_s.reshape(t_lat, -1), ob_s.reshape(t_lat, -1), sga_s, sgb_s),
        mod3, gpm, gpf, wpa, wpb, wo, wr2, br, lat_seq=ss, tm=512)

    tables, blk_tables = _routing_tables(counts, MOE_TILE, MOE_ROWS)
    xs = _dispatch(h2t, route, tables, tm=MOE_TILE, rows=MOE_ROWS)
    ys = _moe(blk_tables, xs, w_e1[l], w_e3[l], w_e2[l])

    gpost = g_post_ffn[l][None, :]
    y_p = _combine(x1, ys, route, tables, mod3, 0, gpost, t=t_ctx, seq=sp, tm=MOE_TILE, tok_off=0, is_ctx=True)
    y_s = _combine(x1, ys, route, tables, mod3, 1, gpost, t=t_lat, seq=ss, tm=MOE_TILE, tok_off=t_ctx,
                   is_ctx=False)

    ha = A_HEADS
    return (y_p.reshape(bp, sp, d), y_s.reshape(bs, ss, d),
            kaf.reshape(bp, 1, sp, ha, 2, A_HEAD_DIM), vaf.reshape(bp, 1, sp, ha, A_V_DIM),
            kbf.reshape(bp, 1, sp, B_KV_HEADS, B_HEAD_DIM), vbf.reshape(bp, 1, sp, B_KV_HEADS, B_HEAD_DIM))
```

```python
import functools
import math

import jax
import jax.numpy as jnp
from jax import lax
from jax.experimental import pallas as pl
from jax.experimental.pallas import tpu as pltpu

F32 = jnp.float32
BF16 = jnp.bfloat16

GRID_W = 64
ROPE_BASE = 10000.0
EPS = 1e-6
NEG_INF = -1e30
A_HEADS = 4
A_HEAD_DIM = 64
A_V_DIM = 2 * A_HEAD_DIM
B_HEADS = 8
B_KV_HEADS = 2
B_GROUP = B_HEADS // B_KV_HEADS
B_HEAD_DIM = 64
WINDOW = 128
N_GROUPS = 4
EXPERTS_PER_GROUP = 8
N_EXPERTS = N_GROUPS * EXPERTS_PER_GROUP
TOP_K = 2

LANES = 128
MOD_ROWS = 16
MOE_ROWS = 512
MOE_TILE = 512
SWA_Q = 128

_QA = 0
_KA = _QA + A_HEADS * 2 * A_HEAD_DIM
_VA = _KA + A_HEADS * 2 * A_HEAD_DIM
_QB = _VA + A_HEADS * A_V_DIM
_KB = _QB + B_HEADS * B_HEAD_DIM
_VB = _KB + B_KV_HEADS * B_HEAD_DIM
_GA = _VB + B_KV_HEADS * B_HEAD_DIM


def _rms(x, g):
    return x * lax.rsqrt(jnp.mean(x * x, axis=-1, keepdims=True) + EPS) * g


def _store_row_tiles(ref, base, val):
    sub = val.shape[1] // LANES
    for s in range(sub):
        ref[pl.ds(base + s, val.shape[0], stride=sub), :] = val[:, s * LANES:(s + 1) * LANES]


def _load_row_tiles(ref, base, n_rows, sub):
    return jnp.concatenate([ref[pl.ds(base + s, n_rows, stride=sub), :] for s in range(sub)], axis=1)


def _mod_kernel(c_ref, w_ref, b_ref, o_ref):
    c = c_ref[...]
    a = c * jax.nn.sigmoid(c)
    w = w_ref[...]
    a_hi = a.astype(BF16)
    a_lo = (a - a_hi.astype(F32)).astype(BF16)
    w_hi = w.astype(BF16)
    w_lo = (w - w_hi.astype(F32)).astype(BF16)
    o_ref[...] = (jnp.dot(a_hi, w_hi, preferred_element_type=F32) + jnp.dot(a_lo, w_hi, preferred_element_type=F32)
                  + jnp.dot(a_hi, w_lo, preferred_element_type=F32) + b_ref[...])


def _modulation(c_all, w_ada, b_ada):
    d, n = w_ada.shape
    tn = 512
    return pl.pallas_call(
        _mod_kernel,
        grid=(n // tn,),
        in_specs=[
            pl.BlockSpec((MOD_ROWS, d), lambda j: (0, 0)),
            pl.BlockSpec((d, tn), lambda j: (0, j)),
            pl.BlockSpec((1, tn), lambda j: (0, j)),
        ],
        out_specs=pl.BlockSpec((MOD_ROWS, tn), lambda j: (0, j)),
        out_shape=jax.ShapeDtypeStruct((MOD_ROWS, n), F32),
        name="modulation",
    )(c_all, w_ada, b_ada)


def _rope128(z, cos, sin_signed, first_half):
    rot = jnp.where(first_half, pltpu.roll(z, 96, 1), pltpu.roll(z, 32, 1))
    return z * cos + rot * sin_signed


def _inproj_kernel(x_ref, sh_ref, sc_ref, g_ref, cos_ref, sin_ref, w_ref, *outs, is_ctx):
    x = x_ref[...]
    h = _rms(x, g_ref[...]) * (1.0 + sc_ref[0]) + sh_ref[0]
    hb = h.astype(BF16)
    lane = lax.broadcasted_iota(jnp.int32, (1, LANES), 1)
    first_half = (lane % 64) < 32
    low = lane < 64

    def seg(lo, hi):
        return jnp.dot(hb, w_ref[:, lo:hi], preferred_element_type=F32)

    def rope(z):
        if is_ctx:
            return z
        cos = cos_ref[...]
        sin = sin_ref[...]
        parts = [_rope128(z[:, j:j + LANES], cos, sin, first_half) for j in range(0, z.shape[1], LANES)]
        return parts[0] if len(parts) == 1 else jnp.concatenate(parts, axis=1)

    if is_ctx:
        qa_o, ka_o, va_o, qb_o, kb2_o, vb_o, sga_o, sgb_o, kaf_o, vaf_o, kbf_o, vbf_o = outs
    else:
        qa_o, ka_o, va_o, qb_o, kb2_o, vb_o, sga_o, sgb_o = outs

    scale = A_HEAD_DIM ** -0.5
    qa_o[...] = (rope(seg(_QA, _KA)) * scale).astype(BF16)
    ka = rope(seg(_KA, _VA))
    if is_ctx:
        ka_o[...] = ka.astype(BF16)
    else:
        ka_o[0] = ka.T.astype(BF16)
    va = seg(_VA, _QB)
    va_o[...] = va.astype(BF16)
    qb_o[...] = (rope(seg(_QB, _KB)) * (B_HEAD_DIM ** -0.5)).astype(BF16)
    kb = rope(seg(_KB, _VB))
    kb_sw = pltpu.roll(kb, 64, 1)
    kb2_o[:, 0:LANES] = jnp.where(low, kb, kb_sw).astype(BF16)
    kb2_o[:, LANES:2 * LANES] = jnp.where(low, kb_sw, kb).astype(BF16)
    vb = seg(_VB, _GA)
    if is_ctx:
        vb_o[...] = vb.astype(BF16)
    else:
        vb_sw = pltpu.roll(vb, 64, 1)
        vb_o[:, 0:LANES] = jnp.where(low, vb, vb_sw).astype(BF16)
        vb_o[:, LANES:2 * LANES] = jnp.where(low, vb_sw, vb).astype(BF16)
    d = x.shape[1]
    sga_o[...] = jax.nn.sigmoid(seg(_GA, _GA + d)).astype(BF16)
    sgb_o[...] = jax.nn.sigmoid(seg(_GA + d, _GA + 2 * d)).astype(BF16)
    if is_ctx:
        kaf_o[...] = ka
        vaf_o[...] = va
        kbf_o[...] = kb
        vbf_o[...] = vb


def _inproj(x2, mod3, mod_row0, g_pre, cos_t, sin_t, w_in_b, *, seq, tm, is_ctx):
    t, d = x2.shape
    per = seq // tm
    assert t % tm == 0 and seq % tm == 0
    n_in = w_in_b.shape[1]
    wa = A_HEADS * 2 * A_HEAD_DIM
    wkb = B_KV_HEADS * B_HEAD_DIM

    def row(i):
        return (i // per) if not is_ctx else 0

    tok = lambda w: pl.BlockSpec((tm, w), lambda i: (i, 0))
    out_shape = [
        jax.ShapeDtypeStruct((t, wa), BF16), jax.ShapeDtypeStruct((t, wa), BF16),
        jax.ShapeDtypeStruct((t, wa), BF16), jax.ShapeDtypeStruct((t, wa), BF16),
        jax.ShapeDtypeStruct((t, 2 * wkb), BF16), jax.ShapeDtypeStruct((t, wkb), BF16),
        jax.ShapeDtypeStruct((t, d), BF16), jax.ShapeDtypeStruct((t, d), BF16),
    ]
    out_specs = [tok(wa), tok(wa), tok(wa), tok(wa), tok(2 * wkb), tok(wkb), tok(d), tok(d)]
    if not is_ctx:
        out_shape[5] = jax.ShapeDtypeStruct((t, 2 * wkb), BF16)
        out_specs[5] = tok(2 * wkb)
        out_shape[1] = jax.ShapeDtypeStruct((t // seq, wa, seq), BF16)
        out_specs[1] = pl.BlockSpec((1, wa, tm), lambda i: (i // per, 0, i % per))
    if is_ctx:
        out_shape += [jax.ShapeDtypeStruct((t, wa), F32), jax.ShapeDtypeStruct((t, wa), F32),
                      jax.ShapeDtypeStruct((t, wkb), F32), jax.ShapeDtypeStruct((t, wkb), F32)]
        out_specs += [tok(wa), tok(wa), tok(wkb), tok(wkb)]
    return pl.pallas_call(
        functools.partial(_inproj_kernel, is_ctx=is_ctx),
        grid=(t // tm,),
        in_specs=[
            pl.BlockSpec((tm, d), lambda i: (i, 0)),
            pl.BlockSpec((1, 1, d), lambda i: (mod_row0 + row(i), 0, 0)),
            pl.BlockSpec((1, 1, d), lambda i: (mod_row0 + row(i), 0, 1)),
            pl.BlockSpec((1, d), lambda i: (0, 0)),
            pl.BlockSpec((tm, LANES), lambda i: (i % per, 0)),
            pl.BlockSpec((tm, LANES), lambda i: (i % per, 0)),
            pl.BlockSpec((d, n_in), lambda i: (0, 0)),
        ],
        out_specs=out_specs,
        out_shape=out_shape,
        compiler_params=pltpu.CompilerParams(dimension_semantics=("arbitrary",)),
        name="inproj_ctx" if is_ctx else "inproj_lat",
    )(x2, mod3, mod3, g_pre, cos_t, sin_t, w_in_b)


def _nt(a, b):
    return lax.dot_general(a, b, (((1,), (1,)), ((), ())), preferred_element_type=F32)


def _diff_kernel(lam_ref, g_ref, q_ref, k_ref, v_ref, o_ref, *, lambda_init):
    lp = lam_ref[...]
    lam = (jnp.exp(jnp.sum(lp[0:1] * lp[1:2], axis=-1, keepdims=True))
           - jnp.exp(jnp.sum(lp[2:3] * lp[3:4], axis=-1, keepdims=True)) + lambda_init)
    tq = q_ref.shape[1]
    lane = lax.broadcasted_iota(jnp.int32, (1, LANES), 1)
    for h in range(A_HEADS):
        cols = slice(h * LANES, (h + 1) * LANES)
        q = q_ref[0, :, cols]
        q2 = jnp.concatenate([q * (lane < 64).astype(BF16), q * (lane >= 64).astype(BF16)], axis=0)
        s = _nt(q2, k_ref[0, :, cols])
        mx = jnp.max(s, axis=-1, keepdims=True)
        v = v_ref[0, :, cols]
        v_ext = jnp.concatenate([v, jnp.ones_like(v)], axis=1)
        acc = jnp.dot(jnp.exp(s - mx).astype(BF16), v_ext, preferred_element_type=F32)
        on = acc[:, 0:LANES] / acc[:, LANES:2 * LANES]
        o = on[0:tq] - lam * on[tq:2 * tq]
        o_ref[0, :, cols] = (_rms(o, g_ref[...]) * (1.0 - lambda_init)).astype(BF16)


def _diff_attention(lam_p, g_head, q, k, v, *, lambda_init):
    b, s, w = q.shape
    seq = pl.BlockSpec((1, s, w), lambda bi: (bi, 0, 0))
    return pl.pallas_call(
        functools.partial(_diff_kernel, lambda_init=lambda_init),
        grid=(b,),
        in_specs=[pl.BlockSpec((4, A_HEAD_DIM), lambda bi: (0, 0)),
                  pl.BlockSpec((1, A_V_DIM), lambda bi: (0, 0)), seq, seq, seq],
        out_specs=seq,
        out_shape=jax.ShapeDtypeStruct((b, s, w), BF16),
        compiler_params=pltpu.CompilerParams(dimension_semantics=("arbitrary",)),
        name="diff_attn_ctx",
    )(lam_p, g_head, q, k, v)


def _diff_lat_kernel(lam_ref, g_ref, q_ref, kc_ref, kt_ref, vc_ref, v_ref, o_ref, s_a, m_a, s_b, m_b, *,
                     lambda_init):
    t = pl.program_id(0)
    tq = q_ref.shape[1]
    nkc = kc_ref.shape[1]
    nkn = kt_ref.shape[2]
    nk = nkc + nkn

    @pl.when(t == 0)
    def _():
        s_b[...] = jnp.zeros_like(s_b)
        m_b[...] = jnp.zeros_like(m_b)

    def body(s_w, m_w, s_r, m_r):
        lp = lam_ref[...]
        lam = (jnp.exp(jnp.sum(lp[0:1] * lp[1:2], axis=-1, keepdims=True))
               - jnp.exp(jnp.sum(lp[2:3] * lp[3:4], axis=-1, keepdims=True)) + lambda_init)
        lane = lax.broadcasted_iota(jnp.int32, (1, LANES), 1)

        q = q_ref[0]
        q2 = jnp.concatenate([q * (lane < 64).astype(BF16), q * (lane >= 64).astype(BF16)], axis=0)
        sc = _nt(q2, kc_ref[0].astype(BF16))
        sn = jnp.dot(q2, kt_ref[0], preferred_element_type=F32)
        mx = jnp.maximum(jnp.max(sc, axis=-1, keepdims=True), jnp.max(sn, axis=-1, keepdims=True))
        s_w[:, 0:nkc] = sc
        s_w[:, nkc:nk] = sn
        m_w[...] = jnp.broadcast_to(mx, (2 * tq, LANES))

        mp = m_r[...]
        v_all = jnp.concatenate([vc_ref[0].astype(BF16), v_ref[0]], axis=0)
        v_ext = jnp.concatenate([v_all, jnp.ones_like(v_all)], axis=1)
        p = jnp.concatenate(
            [jnp.exp(s_r[:, c:c + LANES] - mp).astype(BF16) for c in range(0, nk, LANES)], axis=1)
        acc = jnp.dot(p, v_ext, preferred_element_type=F32)
        on = acc[:, 0:LANES] / acc[:, LANES:2 * LANES]
        o = on[0:tq] - lam * on[tq:2 * tq]
        o_ref[0] = (_rms(o, g_ref[...]) * (1.0 - lambda_init)).astype(BF16)

    @pl.when(t % 2 == 0)
    def _():
        body(s_a, m_a, s_b, m_b)

    @pl.when(t % 2 == 1)
    def _():
        body(s_b, m_b, s_a, m_a)


def _diff_attention_lat(lam_p, g_head, q, kc, kt, vc, v, *, tq, lambda_init):
    b, s, w = q.shape
    past = kc.shape[1]
    nq = s // tq
    n_units = b * A_HEADS * nq
    last = n_units - 1

    def unit(u):
        return u // (A_HEADS * nq), (u // nq) % A_HEADS, u % nq

    def cur(t):
        return unit(jnp.minimum(t, last))

    def prev(t):
        return unit(jnp.maximum(t - 1, 0))

    return pl.pallas_call(
        functools.partial(_diff_lat_kernel, lambda_init=lambda_init),
        grid=(n_units + 1,),
        in_specs=[
            pl.BlockSpec((4, A_HEAD_DIM), lambda t: (0, 0)),
            pl.BlockSpec((1, A_V_DIM), lambda t: (0, 0)),
            pl.BlockSpec((1, tq, LANES), lambda t: (cur(t)[0], cur(t)[2], cur(t)[1])),
            pl.BlockSpec((1, past, LANES), lambda t: (cur(t)[0], 0, cur(t)[1])),
            pl.BlockSpec((1, LANES, s), lambda t: (cur(t)[0], cur(t)[1], 0)),
            pl.BlockSpec((1, past, LANES), lambda t: (prev(t)[0], 0, prev(t)[1])),
            pl.BlockSpec((1, s, LANES), lambda t: (prev(t)[0], 0, prev(t)[1])),
        ],
        out_specs=pl.BlockSpec((1, tq, LANES), lambda t: (prev(t)[0], prev(t)[2], prev(t)[1])),
        out_shape=jax.ShapeDtypeStruct((b, s, w), BF16),
        scratch_shapes=[pltpu.VMEM((2 * tq, past + s), F32), pltpu.VMEM((2 * tq, LANES), F32),
                        pltpu.VMEM((2 * tq, past + s), F32), pltpu.VMEM((2 * tq, LANES), F32)],
        compiler_params=pltpu.CompilerParams(dimension_semantics=("arbitrary",),
                                             vmem_limit_bytes=56 * 1024 * 1024),
        name="diff_attn_lat",
    )(lam_p, g_head, q, kc, kt, vc, v)


def _swa_kernel(sink_ref, q_ref, k_ref, v_ref, o_ref):
    tq = q_ref.shape[1]
    rows = B_GROUP * tq
    lane = lax.broadcasted_iota(jnp.int32, (1, LANES), 1)
    low = lane < 64
    lane2 = lax.broadcasted_iota(jnp.int32, (1, 2 * LANES), 1)
    head_masks = [((lane2 // 64) == g).astype(BF16) for g in range(B_GROUP)]
    gw = B_GROUP * B_HEAD_DIM
    v = v_ref[0]
    v_ext = jnp.concatenate([v, jnp.ones_like(v)], axis=1)
    finishers = []
    for n in range(B_KV_HEADS):
        q = q_ref[0, :, n * gw:(n + 1) * gw]
        qs = jnp.concatenate([q * hm for hm in head_masks], axis=0)
        k2 = k_ref[0, :, n * LANES:(n + 1) * LANES]
        s = _nt(qs, jnp.concatenate([k2, k2], axis=1))
        sink = jnp.concatenate(
            [jnp.full((tq, 1), sink_ref[n * B_GROUP + g], F32) for g in range(B_GROUP)], axis=0)
        top = jnp.max(s, keepdims=True)
        for g in range(B_GROUP):
            top = jnp.maximum(top, sink_ref[n * B_GROUP + g])

        def finish(mp, n=n, s=s, sink=sink):
            acc = jnp.dot(jnp.exp(s - mp).astype(BF16), v_ext, preferred_element_type=F32)
            den = acc[:, LANES:2 * LANES] + jnp.exp(sink - mp)
            o = acc[:, 0:LANES] / den
            osw = pltpu.roll(o, 64, 1)
            for j in range(B_GROUP // 2):
                ra = slice((2 * j) * tq, (2 * j + 1) * tq)
                rb = slice((2 * j + 1) * tq, (2 * j + 2) * tq)
                pair = jnp.where(low, o[ra], osw[rb]) if n == 0 else jnp.where(low, osw[ra], o[rb])
                o_ref[0, :, n * gw + j * LANES:n * gw + (j + 1) * LANES] = pair.astype(BF16)
            return jnp.min(den)

        finishers.append((finish, s, sink, finish(top)))

    smallest = functools.reduce(jnp.minimum, [f[3] for f in finishers])

    @pl.when(jnp.logical_not(smallest >= 1e-30))
    def _():
        for finish, s, sink, _ in finishers:
            finish(jnp.maximum(jnp.max(s, axis=-1, keepdims=True), sink))


def _swa_attention(sink, q, k2, v):
    b, s, w = q.shape
    seq = lambda a: pl.BlockSpec((1, s, a.shape[2]), lambda bi: (bi, 0, 0))
    return pl.pallas_call(
        _swa_kernel,
        grid=(b,),
        in_specs=[pl.BlockSpec(memory_space=pltpu.SMEM), seq(q), seq(k2), seq(v)],
        out_specs=seq(q),
        out_shape=jax.ShapeDtypeStruct((b, s, w), BF16),
        compiler_params=pltpu.CompilerParams(dimension_semantics=("arbitrary",)),
        name="swa_attn_ctx",
    )(sink, q, k2, v)


def _swa_lat_kernel(sink_ref, q_ref, kc_ref, kl_ref, km_ref, kr_ref, vc_ref, vl_ref, vm_ref, vr_ref, o_ref,
                    s_a, m_a, s_b, m_b, *, nqb):
    t = pl.program_id(0)
    n_units = pl.num_programs(0) - 1
    tq = q_ref.shape[1]
    gw = B_GROUP * B_HEAD_DIM
    i_cur = jnp.minimum(t, n_units - 1) % nqb
    nkc = kc_ref.shape[1]
    rows = B_GROUP * tq

    @pl.when(t == 0)
    def _():
        s_b[...] = jnp.zeros_like(s_b)
        m_b[...] = jnp.zeros_like(m_b)

    def sink_col(n):
        return jnp.concatenate(
            [jnp.full((tq, 1), sink_ref[n * B_GROUP + g], F32) for g in range(B_GROUP)], axis=0)

    def body(s_w, m_w, s_r, m_r):
        lane = lax.broadcasted_iota(jnp.int32, (1, LANES), 1)
        low = lane < 64
        lane2 = lax.broadcasted_iota(jnp.int32, (1, 2 * LANES), 1)
        head_masks = [((lane2 // 64) == g).astype(BF16) for g in range(B_GROUP)]
        qi = lax.broadcasted_iota(jnp.int32, (rows, SWA_Q), 0) & (tq - 1)
        kj = lax.broadcasted_iota(jnp.int32, (rows, SWA_Q), 1)
        far = 2 * SWA_Q
        left_ok = kj >= qi + jnp.where(i_cur > 0, 0, far)
        right_ok = kj <= qi - jnp.where(i_cur < nqb - 1, 0, far)

        kc = kc_ref[0]
        kc_sw = pltpu.roll(kc, 64, 1)
        for n in range(B_KV_HEADS):
            q = q_ref[0, :, n * gw:(n + 1) * gw]
            qs = jnp.concatenate([q * hm for hm in head_masks], axis=0)
            kc2 = (jnp.where(low, kc, kc_sw) if n == 0 else jnp.where(low, kc_sw, kc)).astype(BF16)
            ks = [kc2] + [r[0, :, n * LANES:(n + 1) * LANES] for r in (kl_ref, km_ref, kr_ref)]
            k_all = jnp.concatenate([jnp.concatenate([k, k], axis=1) for k in ks], axis=0)
            s = _nt(qs, k_all)
            chunks = [s[:, c:c + LANES] for c in range(0, s.shape[1], LANES)]
            il = nkc // LANES
            chunks[il] = jnp.where(left_ok, chunks[il], NEG_INF)
            chunks[il + 2] = jnp.where(right_ok, chunks[il + 2], NEG_INF)
            top = jnp.max(functools.reduce(jnp.maximum, chunks), keepdims=True)
            for g in range(B_GROUP):
                top = jnp.maximum(top, sink_ref[n * B_GROUP + g])
            for c, ch in enumerate(chunks):
                s_w[n * rows:(n + 1) * rows, c * LANES:(c + 1) * LANES] = ch
            m_w[n * rows:(n + 1) * rows, :] = jnp.broadcast_to(top, (rows, LANES))

        vc = vc_ref[0]
        vc_sw = pltpu.roll(vc, 64, 1)
        nk = nkc + 3 * tq
        finishers = []
        for n in range(B_KV_HEADS):
            vc2 =(jnp.where(low, vc, vc_sw) if n == 0 else jnp.where(low, vc_sw, vc)).astype(BF16)
            v_all = jnp.concatenate(
                [vc2] + [r[0, :, n * LANES:(n + 1) * LANES] for r in (vl_ref, vm_ref, vr_ref)], axis=0)
            v_ext = jnp.concatenate([v_all, jnp.ones_like(v_all)], axis=1)
            sink = sink_col(n)

            def finish(mp, n=n, v_ext=v_ext, sink=sink):
                p = jnp.concatenate([jnp.exp(s_r[n * rows:(n + 1) * rows, c:c + LANES] - mp).astype(BF16)
                                     for c in range(0, nk, LANES)], axis=1)
                acc = jnp.dot(p, v_ext, preferred_element_type=F32)
                den = acc[:, LANES:2 * LANES] + jnp.exp(sink - mp)
                o = acc[:, 0:LANES] / den
                for j in range(B_GROUP // 2):
                    pair = jnp.where(low, o[(2 * j) * tq:(2 * j + 1) * tq], o[(2 * j + 1) * tq:(2 * j + 2) * tq])
                    o_ref[0, :, n * gw + j * LANES:n * gw + (j + 1) * LANES] = pair.astype(BF16)
                return jnp.min(den)

            finishers.append((n, finish, sink, finish(m_r[n * rows:(n + 1) * rows, :])))

        smallest = functools.reduce(jnp.minimum, [f[3] for f in finishers])

        @pl.when(jnp.logical_not(smallest >= 1e-30))
        def _():
            for n, finish, sink, _ in finishers:
                row_max = functools.reduce(
                    jnp.maximum, [s_r[n * rows:(n + 1) * rows, c:c + LANES] for c in range(0, nk, LANES)])
                mx = jnp.maximum(jnp.max(row_max, axis=-1, keepdims=True), sink)
                finish(jnp.broadcast_to(mx, (rows, LANES)))

    @pl.when(t % 2 == 0)
    def _():
        body(s_a, m_a, s_b, m_b)

    @pl.when(t % 2 == 1)
    def _():
        body(s_b, m_b, s_a, m_a)


def _swa_attention_lat(sink, q, kc, k2, vc, v):
    b, s, w = q.shape
    past = kc.shape[1]
    tq = SWA_Q
    nqb = s // tq
    n_units = b * nqb
    last = n_units - 1
    nk = past + 3 * tq

    def cur(t):
        u = jnp.minimum(t, last)
        return u // nqb, u % nqb

    def prev(t):
        u = jnp.maximum(t - 1, 0)
        return u // nqb, u % nqb

    lo = lambda i: jnp.maximum(i - 1, 0)
    hi = lambda i: jnp.minimum(i + 1, nqb - 1)
    kspec = lambda f: pl.BlockSpec((1, tq, 2 * LANES), lambda t: (cur(t)[0], f(cur(t)[1]), 0))
    vspec = lambda f: pl.BlockSpec((1, tq, 2 * LANES), lambda t: (prev(t)[0], f(prev(t)[1]), 0))
    same = lambda i: i
    rows = B_KV_HEADS * B_GROUP * tq
    return pl.pallas_call(
        functools.partial(_swa_lat_kernel, nqb=nqb),
        grid=(n_units + 1,),
        in_specs=[
            pl.BlockSpec(memory_space=pltpu.SMEM),
            pl.BlockSpec((1, tq, w), lambda t: (cur(t)[0], cur(t)[1], 0)),
            pl.BlockSpec((1, past, LANES), lambda t: (cur(t)[0], 0, 0)),
            kspec(lo), kspec(same), kspec(hi),
            pl.BlockSpec((1, past, LANES), lambda t: (prev(t)[0], 0, 0)),
            vspec(lo), vspec(same), vspec(hi),
        ],
        out_specs=pl.BlockSpec((1, tq, w), lambda t: (prev(t)[0], prev(t)[1], 0)),
        out_shape=jax.ShapeDtypeStruct((b, s, w), BF16),
        scratch_shapes=[pltpu.VMEM((rows, nk), F32), pltpu.VMEM((rows, LANES), F32),
                        pltpu.VMEM((rows, nk), F32), pltpu.VMEM((rows, LANES), F32)],
        compiler_params=pltpu.CompilerParams(dimension_semantics=("arbitrary",)),
        name="swa_attn_lat",
    )(sink, q, kc, k2, k2, k2, vc, v, v, v)


def _postmix_kernel(*refs, n_ctx_tiles):
    (xc, xl, oac, oal, obc, obl, sgac, sgal, sgbc, sgbl, g1_ref, sh2_ref, sc2_ref, gpm_ref, gpf_ref,
     wpa_ref, wpb_ref, wo_ref, wr_ref, br_ref, x1_o, h2_o, route_o, cnt_o, x1_a, x1_b) = refs
    t = pl.program_id(0)
    is_ctx = t < n_ctx_tiles
    pick = lambda a, b: jnp.where(is_ctx, a[...], b[...])

    @pl.when(t == 0)
    def _():
        x1_b[...] = jnp.zeros_like(x1_b)

    def body(x1_w, x1_r):
        pa = jnp.dot(pick(oac, oal), wpa_ref[...], preferred_element_type=F32)
        pb = jnp.dot(pick(obc, obl), wpb_ref[...], preferred_element_type=F32)
        mix = pick(sgac, sgal).astype(F32) * pa + pick(sgbc, sgbl).astype(F32) * pb
        m2 = jnp.dot(mix.astype(BF16), wo_ref[...], preferred_element_type=F32)
        x1 = pick(xc, xl) + g1_ref[0] * _rms(m2, gpm_ref[...])
        x1_o[...] = x1
        x1_w[...] = x1
        _postmix_route(x1_r[...], sh2_ref, sc2_ref, gpf_ref, wr_ref, br_ref, h2_o, route_o, cnt_o)

    @pl.when(t % 2 == 0)
    def _():
        body(x1_a, x1_b)

    @pl.when(t % 2 == 1)
    def _():
        body(x1_b, x1_a)


def _postmix_route(x1, sh2_ref, sc2_ref, gpf_ref, wr_ref, br_ref, h2_o, route_o, cnt_o):
    h2 = _rms(x1, gpf_ref[...]) * (1.0 + sc2_ref[0]) + sh2_ref[0]
    h2_o[...] = h2.astype(BF16)

    h_hi = h2.astype(BF16)
    h_lo = (h2 - h_hi.astype(F32)).astype(BF16)
    both = jnp.dot(h_hi, wr_ref[...], preferred_element_type=F32)
    logits = (both[:, 0:LANES] + both[:, LANES:2 * LANES]
              + jnp.dot(h_lo, wr_ref[:, 0:LANES], preferred_element_type=F32) + br_ref[...])
    tm = logits.shape[0]
    lt = logits.T
    row = lax.broadcasted_iota(jnp.int32, (EXPERTS_PER_GROUP, tm), 0).astype(F32)
    none = float(EXPERTS_PER_GROUP)
    lg = jnp.where(row < N_GROUPS, lt[N_EXPERTS:N_EXPERTS + EXPERTS_PER_GROUP], -jnp.inf)
    mg = jnp.max(lg, axis=0, keepdims=True)
    g_sel = jnp.min(jnp.where(lg == mg, row, none), axis=0, keepdims=True)
    g_w = 1.0 / jnp.sum(jnp.exp(lg - mg), axis=0, keepdims=True)
    le = lt[0:EXPERTS_PER_GROUP]
    for g in range(1, N_GROUPS):
        le = jnp.where(g_sel == g, lt[g * EXPERTS_PER_GROUP:(g + 1) * EXPERTS_PER_GROUP], le)
    v0 = jnp.max(le, axis=0, keepdims=True)
    i0 = jnp.min(jnp.where(le == v0, row, none), axis=0, keepdims=True)
    le1 = jnp.where(row == i0, -jnp.inf, le)
    v1 = jnp.max(le1, axis=0, keepdims=True)
    i1 = jnp.min(jnp.where(le1 == v1, row, none), axis=0, keepdims=True)
    e = jnp.exp(v1 - v0)
    w0 = g_w / (1.0 + e)
    w1 = g_w * e / (1.0 + e)
    e0 = g_sel * EXPERTS_PER_GROUP + i0
    e1 = g_sel * EXPERTS_PER_GROUP + i1
    route_o[...] = jnp.where(row == 0, e0, jnp.where(row == 1, e1, jnp.where(row == 2, w0,
                                                                             jnp.where(row == 3, w1, 0.0))))
    erow = lax.broadcasted_iota(jnp.int32, (N_EXPERTS, tm), 0).astype(F32)
    cnt = jnp.sum((erow == e0).astype(F32) + (erow == e1).astype(F32), axis=1, keepdims=True)
    cnt_o[0] = jnp.broadcast_to(cnt, (N_EXPERTS, LANES))


def _postmix(ctx_in, lat_in, mod3, gpm, gpf, wpa, wpb, wo, wr, br, *, lat_seq, tm):
    t_ctx, d = ctx_in[0].shape
    t_lat = lat_in[0].shape[0]
    assert t_ctx % tm == 0 and lat_seq % tm == 0
    nc = t_ctx // tm
    nl = t_lat // tm
    per = lat_seq // tm
    t_all = t_ctx + t_lat

    last = nc + nl - 1
    cur = lambda i: jnp.minimum(i, last)
    prev = lambda i: jnp.maximum(i - 1, 0)
    mod_row = lambda j: jnp.where(j < nc, 0, 1 + jnp.maximum(j - nc, 0) // per)
    full = lambda a: pl.BlockSpec(a.shape, lambda i: (0,) * a.ndim)
    in_specs, args = [], []
    for a_c, a_l in zip(ctx_in, lat_in):
        w = a_c.shape[1]
        in_specs += [pl.BlockSpec((tm, w), lambda i: (jnp.minimum(cur(i), nc - 1), 0)),
                     pl.BlockSpec((tm, w), lambda i: (jnp.maximum(cur(i) - nc, 0), 0))]
        args += [a_c, a_l]
    in_specs += [pl.BlockSpec((1, 1, d), lambda i: (mod_row(cur(i)), 0, 2)),
                 pl.BlockSpec((1, 1, d), lambda i: (mod_row(prev(i)), 0, 3)),
                 pl.BlockSpec((1, 1, d), lambda i: (mod_row(prev(i)), 0, 4)),
                 full(gpm), full(gpf), full(wpa), full(wpb), full(wo), full(wr), full(br)]
    args += [mod3, mod3, mod3, gpm, gpf, wpa, wpb, wo, wr, br]
    return pl.pallas_call(
        functools.partial(_postmix_kernel, n_ctx_tiles=nc),
        grid=(nc + nl + 1,),
        in_specs=in_specs,
        out_specs=[pl.BlockSpec((tm, d), lambda i: (cur(i), 0)),
                   pl.BlockSpec((tm, d), lambda i: (prev(i), 0)),
                   pl.BlockSpec((EXPERTS_PER_GROUP, tm), lambda i: (0, prev(i))),
                   pl.BlockSpec((1, N_EXPERTS, LANES), lambda i: (prev(i), 0, 0))],
        out_shape=[jax.ShapeDtypeStruct((t_all, d), F32),
                   jax.ShapeDtypeStruct((t_all, d), BF16),
                   jax.ShapeDtypeStruct((EXPERTS_PER_GROUP, t_all), F32),
                   jax.ShapeDtypeStruct((nc + nl, N_EXPERTS, LANES), F32)],
        scratch_shapes=[pltpu.VMEM((tm, d), F32), pltpu.VMEM((tm, d), F32)],
        compiler_params=pltpu.CompilerParams(dimension_semantics=("arbitrary",)),
        name="postmix",
    )(*args)


def _segment_copies(src, src_row, dst, dst_row, n, sub, sem):
    @pl.when(n > 0)
    def _():
        pltpu.make_async_copy(src.at[pl.ds(pl.multiple_of(src_row * sub, sub), n * sub)],
                              dst.at[pl.ds(pl.multiple_of(dst_row * sub, sub), n * sub)], sem).start()


def _local_positions(route_t, tile_base):
    tm = route_t.shape[1]
    erow = lax.broadcasted_iota(jnp.int32, (N_EXPERTS, tm), 0).astype(F32)
    is0 = erow == route_t[0:1]
    is1 = erow == route_t[1:2]
    earlier = (lax.broadcasted_iota(jnp.int32, (tm, tm), 0)
               < lax.broadcasted_iota(jnp.int32, (tm, tm), 1)).astype(BF16)
    pre0 = jnp.dot(is0.astype(BF16), earlier, preferred_element_type=F32)
    pre1 = jnp.dot(is1.astype(BF16), earlier, preferred_element_type=F32)
    cnt0 = jnp.sum(is0.astype(F32), axis=1, keepdims=True)
    base = tile_base[:, 0:1]
    lpos0 = jnp.sum(jnp.where(is0, base + pre0, 0.0), axis=0, keepdims=True)
    lpos1 = jnp.sum(jnp.where(is1, base + cnt0 + pre1, 0.0), axis=0, keepdims=True)
    return lpos0, lpos1


def _dispatch_kernel(ss_ref, sl_ref, tb_ref, h_ref, r_ref, tbv_ref, xs_hbm, pbuf, zbuf, sem, zsem, *, n_asg, rows):
    i = pl.program_id(0)
    nt = pl.num_programs(0)
    tm, d = h_ref.shape
    sub = d // LANES
    nrow = TOP_K * tm
    slot = i % 2

    def wait_slot(s):
        pltpu.make_async_copy(pbuf.at[pl.ds(pl.multiple_of(s * nrow * sub, nrow * sub), nrow * sub)],
                              xs_hbm.at[pl.ds(0, nrow * sub)], sem.at[s]).wait()

    def slack_copy():
        return pltpu.make_async_copy(zbuf, xs_hbm.at[pl.ds(n_asg * sub, rows * sub)], zsem.at[0])

    @pl.when(i == 0)
    def _():
        zbuf[...] = jnp.zeros_like(zbuf)
        slack_copy().start()

    lpos0, lpos1 = _local_positions(r_ref[...], tbv_ref[0])
    p = lax.broadcasted_iota(jnp.int32, (nrow, tm), 0).astype(F32)
    sel = ((p == lpos0) | (p == lpos1)).astype(BF16)
    xp = jnp.dot(sel, h_ref[...], preferred_element_type=F32)

    @pl.when(i >= 2)
    def _():
        wait_slot(slot)

    _store_row_tiles(pbuf, slot * nrow * sub, xp)

    def seg(e, c):
        k = i * N_EXPERTS + e
        _segment_copies(pbuf, slot * nrow + tb_ref[k], xs_hbm, ss_ref[k], sl_ref[k], sub, sem.at[slot])
        return c
    lax.fori_loop(0, N_EXPERTS, seg, 0)

    @pl.when(i == nt - 1)
    def _():
        wait_slot(slot)

        @pl.when(nt >= 2)
        def _():
            wait_slot(1 - slot)
        slack_copy().wait()


def _dispatch(h2, route, tables, *, tm, rows):
    seg_start, seg_len, tile_base, tile_base_v = tables
    t, d = h2.shape
    sub = d // LANES
    n_asg = t * TOP_K
    grid_spec = pltpu.PrefetchScalarGridSpec(
        num_scalar_prefetch=3,
        grid=(t // tm,),
        in_specs=[
            pl.BlockSpec((tm, d), lambda i, *_: (i, 0)),
            pl.BlockSpec((EXPERTS_PER_GROUP, tm), lambda i, *_: (0, i)),
            pl.BlockSpec((1, N_EXPERTS, LANES), lambda i, *_: (i, 0, 0)),
        ],
        out_specs=pl.BlockSpec(memory_space=pl.ANY),
        scratch_shapes=[pltpu.VMEM((2 * TOP_K * tm * sub, LANES), F32), pltpu.VMEM((rows * sub, LANES), F32),
                        pltpu.SemaphoreType.DMA((2,)), pltpu.SemaphoreType.DMA((1,))],
    )
    return pl.pallas_call(
        functools.partial(_dispatch_kernel, n_asg=n_asg, rows=rows),
        grid_spec=grid_spec,
        out_shape=jax.ShapeDtypeStruct(((n_asg + rows) * sub, LANES), F32),
        compiler_params=pltpu.CompilerParams(dimension_semantics=("arbitrary",)),
        name="dispatch",
    )(seg_start, seg_len, tile_base, h2, route, tile_base_v)


def _moe_kernel(be_ref, row0_ref, nact_ref, par_ref, nxt_ref, xs_hbm, w1_hbm, w3_hbm, w2_hbm, ys_hbm,
                xbuf, obuf, wf1, wf3, wf2, w1b, w3b, w2b, rsem, wsem, gsem, *, rows, sub):
    i = pl.program_id(0)
    nact = nact_ref[0]
    slot = i % 2
    nslot = 1 - slot
    blk = rows * sub

    def weight_copies(e, s):
        return [pltpu.make_async_copy(w_hbm.at[e], wf.at[s], gsem.at[s])
                for w_hbm, wf in ((w1_hbm, wf1), (w3_hbm, wf3), (w2_hbm, wf2))]

    def read(j, s):
        return pltpu.make_async_copy(xs_hbm.at[pl.ds(pl.multiple_of(row0_ref[j] * sub, sub), blk)],
                                     xbuf.at[pl.ds(pl.multiple_of(s * blk, blk), blk)], rsem.at[s])

    def write(j, s):
        return pltpu.make_async_copy(obuf.at[pl.ds(pl.multiple_of(s * blk, blk), blk)],
                                     ys_hbm.at[pl.ds(pl.multiple_of(row0_ref[j] * sub, sub), blk)], wsem.at[s])

    @pl.when(i == 0)
    def _():
        read(0, 0).start()
        for c in weight_copies(be_ref[0], 0):
            c.start()

    @pl.when(i < nact)
    def _():
        @pl.when(i + 1 < nact)
        def _():
            read(i + 1, nslot).start()

        changed = jnp.logical_or(i == 0, be_ref[i] != be_ref[jnp.maximum(i - 1, 0)])

        @pl.when(changed)
        def _():
            s = par_ref[i]
            for c in weight_copies(be_ref[i], s):
                c.wait()
            w1b[...] = wf1[s].astype(BF16)
            w3b[...] = wf3[s].astype(BF16)
            w2b[...] = wf2[s].astype(BF16)

            @pl.when(nxt_ref[i] >= 0)
            def _():
                for c in weight_copies(nxt_ref[i], 1 - s):
                    c.start()

        read(i, slot).wait()
        x = _load_row_tiles(xbuf, slot * blk, rows, sub).astype(BF16)
        a = jnp.dot(x, w1b[...], preferred_element_type=F32)
        b = jnp.dot(x, w3b[...], preferred_element_type=F32)
        hmid = (a * jax.nn.sigmoid(a) * b).astype(BF16)
        y = jnp.dot(hmid, w2b[...], preferred_element_type=F32)
        _store_row_tiles(obuf, slot * blk, y)

        @pl.when(i >= 1)
        def _():
            write(i - 1, nslot).wait()
        write(i, slot).start()

    @pl.when(i == nact)
    def _():
        write(i - 1, nslot).wait()
        obuf[pl.ds(pl.multiple_of(slot * blk, blk), blk), :] = jnp.zeros((blk, LANES), F32)
        tail = pltpu.make_async_copy(obuf.at[pl.ds(pl.multiple_of(slot * blk, blk), blk)],
                                     ys_hbm.at[pl.ds(ys_hbm.shape[0] - blk, blk)], wsem.at[slot])
        tail.start()
        tail.wait()


def _moe(blk_tables, xs, w1, w3, w2):
    blk_expert, row0, nact, parity, nxt = blk_tables
    nblk = blk_expert.shape[0] - 1
    d, de = w1.shape[1], w1.shape[2]
    sub = d // LANES
    rows = MOE_ROWS
    anyspec = pl.BlockSpec(memory_space=pl.ANY)
    grid_spec = pltpu.PrefetchScalarGridSpec(
        num_scalar_prefetch=5,
        grid=(nblk + 1,),
        in_specs=[anyspec, anyspec, anyspec, anyspec],
        out_specs=anyspec,
        scratch_shapes=[
            pltpu.VMEM((2 * rows * sub, LANES), F32),
            pltpu.VMEM((2 * rows * sub, LANES), F32),
            pltpu.VMEM((2, d, de), F32),
            pltpu.VMEM((2, d, de), F32),
            pltpu.VMEM((2, de, d), F32),
            pltpu.VMEM((d, de), BF16),
            pltpu.VMEM((d, de), BF16),
            pltpu.VMEM((de, d), BF16),
            pltpu.SemaphoreType.DMA((2,)),
            pltpu.SemaphoreType.DMA((2,)),
            pltpu.SemaphoreType.DMA((2,)),
        ],
    )
    return pl.pallas_call(
        functools.partial(_moe_kernel, rows=rows, sub=sub),
        grid_spec=grid_spec,
        out_shape=jax.ShapeDtypeStruct(xs.shape, F32),
        compiler_params=pltpu.CompilerParams(dimension_semantics=("arbitrary",)),
        name="expert_mlp",
    )(blk_expert, row0, nact, parity, nxt, xs, w1, w3, w2)


def _combine_kernel(ss_ref, sl_ref, tb_ref, x1_ref, r_ref, tbv_ref, g2_ref, gpost_ref, ys_hbm, o_ref,
                    ybuf, sem, *, tile0):
    i = pl.program_id(0)
    nt = pl.num_programs(0)
    tm, d = x1_ref.shape
    sub = d // LANES
    nrow = TOP_K * tm
    slot = i % 2

    def fetch(tile, s):
        def seg(e, c):
            k = tile * N_EXPERTS + e
            _segment_copies(ys_hbm, ss_ref[k], ybuf, s * nrow + tb_ref[k], sl_ref[k], sub, sem.at[s])
            return c
        lax.fori_loop(0, N_EXPERTS, seg, 0)

    @pl.when(i == 0)
    def _():
        fetch(tile0, 0)

    @pl.when(i + 1 < nt)
    def _():
        fetch(tile0 + i + 1, 1 - slot)

    pltpu.make_async_copy(ys_hbm.at[pl.ds(0, nrow * sub)],
                          ybuf.at[pl.ds(pl.multiple_of(slot * nrow * sub, nrow * sub), nrow * sub)],
                          sem.at[slot]).wait()
    rt = r_ref[...]
    lpos0, lpos1 = _local_positions(rt, tbv_ref[0])
    rows8 = lax.broadcasted_iota(jnp.int32, (EXPERTS_PER_GROUP, tm), 0)
    cols = jnp.where(rows8 == 0, lpos0, jnp.where(rows8 == 1, lpos1, rt))
    cols = jnp.concatenate([cols, jnp.zeros((LANES - EXPERTS_PER_GROUP, tm), F32)], axis=0).T
    p = lax.broadcasted_iota(jnp.int32, (tm, nrow), 1).astype(F32)
    q = (jnp.where(p == cols[:, 0:1], cols[:, 2:3], 0.0)
         + jnp.where(p == cols[:, 1:2], cols[:, 3:4], 0.0)).astype(BF16)
    ysort = _load_row_tiles(ybuf, slot * nrow * sub, nrow, sub).astype(BF16)
    y = jnp.dot(q, ysort, preferred_element_type=F32)
    o_ref[...] = x1_ref[...] + g2_ref[0] * _rms(y, gpost_ref[...])


def _combine(x1, ys, route, tables, mod3, mod_row0, gpost, *, t, seq, tm, tok_off, is_ctx):
    seg_start, seg_len, tile_base, tile_base_v = tables
    t_all, d = x1.shape
    per = seq // tm
    sub = d // LANES
    boff = tok_off // tm

    def row(i):
        return (i // per) if not is_ctx else 0

    grid_spec = pltpu.PrefetchScalarGridSpec(
        num_scalar_prefetch=3,
        grid=(t // tm,),
        in_specs=[
            pl.BlockSpec((tm, d), lambda i, *_: (boff + i, 0)),
            pl.BlockSpec((EXPERTS_PER_GROUP, tm), lambda i, *_: (0, boff + i)),
            pl.BlockSpec((1, N_EXPERTS, LANES), lambda i, *_: (boff + i, 0, 0)),
            pl.BlockSpec((1, 1, d), lambda i, *_: (mod_row0 + row(i), 0, 5)),
            pl.BlockSpec((1, d), lambda i, *_: (0, 0)),
            pl.BlockSpec(memory_space=pl.ANY),
        ],
        out_specs=pl.BlockSpec((tm, d), lambda i, *_: (i, 0)),
        scratch_shapes=[pltpu.VMEM((2 * TOP_K * tm * sub, LANES), F32), pltpu.SemaphoreType.DMA((2,))],
    )
    return pl.pallas_call(
        functools.partial(_combine_kernel, tile0=boff),
        grid_spec=grid_spec,
        out_shape=jax.ShapeDtypeStruct((t, d), F32),
        compiler_params=pltpu.CompilerParams(dimension_semantics=("arbitrary",)),
        name="combine_ctx" if is_ctx else "combine_lat",
    )(seg_start, seg_len, tile_base, x1, route, tile_base_v, mod3, gpost, ys)


def _routing_tables(counts, tm, rows):
    nt = counts.shape[0]
    n_asg = nt * tm * TOP_K
    ex = jnp.arange(N_EXPERTS, dtype=jnp.int32)
    cnt_te = counts[:, :, 0].astype(jnp.int32)
    cnt_e = jnp.sum(cnt_te, axis=0)
    start_e = jnp.cumsum(cnt_e) - cnt_e
    seg_start = start_e[None, :] + jnp.cumsum(cnt_te, axis=0) - cnt_te
    tile_base = jnp.cumsum(cnt_te, axis=1) - cnt_te
    tile_base_v = jnp.broadcast_to(tile_base.astype(F32)[:, :, None], (nt, N_EXPERTS, LANES))

    nblk_e = (cnt_e + rows - 1) // rows
    blk_end = jnp.cumsum(nblk_e)
    blk_start = blk_end - nblk_e
    n_blocks = n_asg // rows + N_EXPERTS
    b = jnp.arange(n_blocks + 1, dtype=jnp.int32)
    be = jnp.minimum(jnp.sum((blk_end[None, :] <= b[:, None]).astype(jnp.int32), axis=1), N_EXPERTS - 1)
    first = jnp.sum(jnp.where(be[:, None] == ex[None, :], (start_e - blk_start * rows)[None, :], 0), axis=1)
    row0 = jnp.clip(first + b * rows, 0, n_asg)
    nact = blk_end[-1:].astype(jnp.int32)
    used = cnt_e > 0
    parity_e = (jnp.cumsum(used.astype(jnp.int32)) - 1) % 2
    later = (ex[None, :] > ex[:, None]) & used[None, :]
    nxt_e = jnp.min(jnp.where(later, ex[None, :], N_EXPERTS), axis=1)
    nxt_e = jnp.where(nxt_e == N_EXPERTS, -1, nxt_e)
    pick = lambda tab: jnp.sum(jnp.where(be[:, None] == ex[None, :], tab[None, :], 0), axis=1).astype(jnp.int32)
    seg = (seg_start.reshape(-1).astype(jnp.int32), cnt_te.reshape(-1).astype(jnp.int32),
           tile_base.reshape(-1).astype(jnp.int32), tile_base_v)
    blk = (be.astype(jnp.int32), row0.astype(jnp.int32), nact, pick(parity_e), pick(nxt_e))
    return seg, blk


def _rope_tables(n_tok):
    n_rows = n_tok // GRID_W
    rows = jnp.repeat(jnp.arange(n_rows), GRID_W).astype(F32)
    cols = jnp.tile(jnp.arange(GRID_W), n_rows).astype(F32)
    quarter = A_HEAD_DIM // 4
    inv = ROPE_BASE ** (-jnp.arange(quarter, dtype=F32) / quarter)
    ang = jnp.concatenate([rows[:, None] * inv, cols[:, None] * inv], axis=-1)
    cos, sin = jnp.cos(ang), jnp.sin(ang)
    cos_t = jnp.tile(jnp.concatenate([cos, cos], axis=-1), (1, LANES // A_HEAD_DIM))
    sin_t = jnp.tile(jnp.concatenate([-sin, sin], axis=-1), (1, LANES // A_HEAD_DIM))
    return cos_t, sin_t


def kernel(x_prompt, x_sample, c, cache_diff_k, cache_diff_v, cache_swa_k, cache_swa_v, c_ctx, w_ada, b_ada, g_pre_mix, g_post_mix, g_pre_ffn, g_post_ffn, w_in, lam_q1, lam_k1, lam_q2, lam_k2, g_diff_head, sink, w_proj_a, w_proj_b, w_out, w_router_group, b_router_group, w_router_expert, b_router_expert, w_e1, w_e3, w_e2):
    depth = w_in.shape[0]
    assert depth == 1
    l = 0
    bp, sp, d = x_prompt.shape
    bs, ss, _ = x_sample.shape
    lambda_init = 0.8 - 0.6 * math.exp(-0.3 * l)
    assert A_HEAD_DIM == B_HEAD_DIM and ss % GRID_W == 0 and bs + 1 <= MOD_ROWS

    c_all = jnp.concatenate([c_ctx[None, :], c, jnp.zeros((MOD_ROWS - 1 - bs, d), F32)], axis=0)
    mod = _modulation(c_all, w_ada[l], b_ada[l][None, :])
    mod3 = mod.reshape(MOD_ROWS, 1, 6 * d)

    w_in_b = w_in[l].astype(BF16)
    wpa = w_proj_a[l].astype(BF16)
    wpb = w_proj_b[l].astype(BF16)
    wo = w_out[l].astype(BF16)
    n_r = N_GROUPS + N_EXPERTS
    wr = jnp.concatenate([w_router_expert[l], w_router_group[l], jnp.zeros((d, LANES - n_r), F32)], axis=1)
    br = jnp.concatenate([b_router_expert[l], b_router_group[l], jnp.zeros((LANES - n_r,), F32)])[None, :]
    wr_hi = wr.astype(BF16)
    wr2 = jnp.concatenate([wr_hi, (wr - wr_hi.astype(F32)).astype(BF16)], axis=1)
    lam_p = jnp.stack([lam_q1[l], lam_k1[l], lam_q2[l], lam_k2[l]], axis=0)
    g_head = g_diff_head[l][None, :]
    sink_l = sink[l]
    cos_t, sin_t = _rope_tables(ss)

    xp2 = x_prompt.reshape(bp * sp, d)
    xs2 = x_sample.reshape(bs * ss, d)
    gpre = g_pre_mix[l][None, :]

    (qa_c, ka_c, va_c, qb_c, kb2_c, vb_c, sga_c, sgb_c, kaf, vaf, kbf, vbf) = _inproj(
        xp2, mod3, 0, gpre, cos_t, sin_t, w_in_b, seq=sp, tm=sp, is_ctx=True)
    r3 = lambda a, b_: a.reshape(b_, -1, a.shape[-1])
    oa_c = _diff_attention(lam_p, g_head, r3(qa_c, bp), r3(ka_c, bp), r3(va_c, bp), lambda_init=lambda_init)
    ob_c = _swa_attention(sink_l, r3(qb_c, bp), r3(kb2_c, bp), r3(vb_c, bp))

    (qa_s, ka_s, va_s, qb_s, kb2_s, vb_s, sga_s, sgb_s) = _inproj(
        xs2, mod3, 1, gpre, cos_t, sin_t, w_in_b, seq=ss, tm=512, is_ctx=False)
    past = cache_diff_k.shape[2]
    ck = cache_diff_k[:, l].reshape(bs, past, -1)
    cv = cache_diff_v[:, l].reshape(bs, past, -1)
    oa_s = _diff_attention_lat(lam_p, g_head, r3(qa_s, bs), ck, ka_s, cv, r3(va_s, bs),
                               tq=512, lambda_init=lambda_init)
    sk = cache_swa_k[:, l].reshape(bs, past, -1)
    sv = cache_swa_v[:, l].reshape(bs, past, -1)
    kb2_3, vb_3 = r3(kb2_s, bs), r3(vb_s, bs)
    ob_s = _swa_attention_lat(sink_l, r3(qb_s, bs), sk, kb2_3, sv, vb_3)

    gpm = g_post_mix[l][None, :]
    gpf = g_pre_ffn[l][None, :]
    t_ctx, t_lat = bp * sp, bs * ss
    x1, h2t, route, counts = _postmix(
        (xp2, oa_c.reshape(t_ctx, -1), ob_c.reshape(t_ctx, -1), sga_c, sgb_c),
        (xs2, oa_s.reshape(t_lat, -1), ob_s.reshape(t_lat, -1), sga_s, sgb_s),
        mod3, gpm, gpf, wpa, wpb, wo, wr2, br, lat_seq=ss, tm=512)

    tables, blk_tables = _routing_tables(counts, MOE_TILE, MOE_ROWS)
    xs = _dispatch(h2t, route, tables, tm=MOE_TILE, rows=MOE_ROWS)
    ys = _moe(blk_tables, xs, w_e1[l], w_e3[l], w_e2[l])

    gpost = g_post_ffn[l][None, :]
    y_p = _combine(x1, ys, route, tables, mod3, 0, gpost, t=t_ctx, seq=sp, tm=MOE_TILE, tok_off=0, is_ctx=True)
    y_s = _combine(x1, ys, route, tables, mod3, 1, gpost, t=t_lat, seq=ss, tm=MOE_TILE, tok_off=t_ctx,
                   is_ctx=False)

    ha = A_HEADS
    return (y_p.reshape(bp, sp, d), y_s.reshape(bs, ss, d),
            kaf.reshape(bp, 1, sp, ha, 2, A_HEAD_DIM), vaf.reshape(bp, 1, sp, ha, A_V_DIM),
            kbf.reshape(bp, 1, sp, B_KV_HEADS, B_HEAD_DIM), vbf.reshape(bp, 1, sp, B_KV_HEADS, B_HEAD_DIM))
```

```python
import functools
import math

import jax
import jax.numpy as jnp
from jax import lax
from jax.experimental import pallas as pl
from jax.experimental.pallas import tpu as pltpu

F32 = jnp.float32
BF16 = jnp.bfloat16

GRID_W = 64
ROPE_BASE = 10000.0
EPS = 1e-6
NEG_INF = -1e30
A_HEADS = 4
A_HEAD_DIM = 64
A_V_DIM = 2 * A_HEAD_DIM
B_HEADS = 8
B_KV_HEADS = 2
B_GROUP = B_HEADS // B_KV_HEADS
B_HEAD_DIM = 64
WINDOW = 128
N_GROUPS = 4
EXPERTS_PER_GROUP = 8
N_EXPERTS = N_GROUPS * EXPERTS_PER_GROUP
TOP_K = 2

LANES = 128
MOD_ROWS = 16
MOE_ROWS = 512
MOE_TILE = 512
SWA_Q = 128

_QA = 0
_KA = _QA + A_HEADS * 2 * A_HEAD_DIM
_VA = _KA + A_HEADS * 2 * A_HEAD_DIM
_QB = _VA + A_HEADS * A_V_DIM
_KB = _QB + B_HEADS * B_HEAD_DIM
_VB = _KB + B_KV_HEADS * B_HEAD_DIM
_GA = _VB + B_KV_HEADS * B_HEAD_DIM


def _rms(x, g):
    return x * lax.rsqrt(jnp.mean(x * x, axis=-1, keepdims=True) + EPS) * g


def _store_row_tiles(ref, base, val):
    sub = val.shape[1] // LANES
    for s in range(sub):
        ref[pl.ds(base + s, val.shape[0], stride=sub), :] = val[:, s * LANES:(s + 1) * LANES]


def _load_row_tiles(ref, base, n_rows, sub):
    return jnp.concatenate([ref[pl.ds(base + s, n_rows, stride=sub), :] for s in range(sub)], axis=1)


def _mod_kernel(c_ref, w_ref, b_ref, o_ref):
    c = c_ref[...]
    a = c * jax.nn.sigmoid(c)
    w = w_ref[...]
    a_hi = a.astype(BF16)
    a_lo = (a - a_hi.astype(F32)).astype(BF16)
    w_hi = w.astype(BF16)
    w_lo = (w - w_hi.astype(F32)).astype(BF16)
    o_ref[...] = (jnp.dot(a_hi, w_hi, preferred_element_type=F32) + jnp.dot(a_lo, w_hi, preferred_element_type=F32)
                  + jnp.dot(a_hi, w_lo, preferred_element_type=F32) + b_ref[...])


def _modulation(c_all, w_ada, b_ada):
    d, n = w_ada.shape
    tn = 512
    return pl.pallas_call(
        _mod_kernel,
        grid=(n // tn,),
        in_specs=[
            pl.BlockSpec((MOD_ROWS, d), lambda j: (0, 0)),
            pl.BlockSpec((d, tn), lambda j: (0, j)),
            pl.BlockSpec((1, tn), lambda j: (0, j)),
        ],
        out_specs=pl.BlockSpec((MOD_ROWS, tn), lambda j: (0, j)),
        out_shape=jax.ShapeDtypeStruct((MOD_ROWS, n), F32),
        name="modulation",
    )(c_all, w_ada, b_ada)


def _rope128(z, cos, sin_signed, first_half):
    rot = jnp.where(first_half, pltpu.roll(z, 96, 1), pltpu.roll(z, 32, 1))
    return z * cos + rot * sin_signed


def _inproj_kernel(x_ref, sh_ref, sc_ref, g_ref, cos_ref, sin_ref, w_ref, *outs, is_ctx):
    x = x_ref[...]
    h = _rms(x, g_ref[...]) * (1.0 + sc_ref[0]) + sh_ref[0]
    hb = h.astype(BF16)
    lane = lax.broadcasted_iota(jnp.int32, (1, LANES), 1)
    first_half = (lane % 64) < 32
    low = lane < 64

    def seg(lo, hi):
        return jnp.dot(hb, w_ref[:, lo:hi], preferred_element_type=F32)

    def rope(z):
        if is_ctx:
            return z
        cos = cos_ref[...]
        sin = sin_ref[...]
        parts = [_rope128(z[:, j:j + LANES], cos, sin, first_half) for j in range(0, z.shape[1], LANES)]
        return parts[0] if len(parts) == 1 else jnp.concatenate(parts, axis=1)

    if is_ctx:
        qa_o, ka_o, va_o, qb_o, kb2_o, vb_o, sga_o, sgb_o, kaf_o, vaf_o, kbf_o, vbf_o = outs
    else:
        qa_o, ka_o, va_o, qb_o, kb2_o, vb_o, sga_o, sgb_o = outs

    scale = A_HEAD_DIM ** -0.5
    qa_o[...] = (rope(seg(_QA, _KA)) * scale).astype(BF16)
    ka = rope(seg(_KA, _VA))
    if is_ctx:
        ka_o[...] = ka.astype(BF16)
    else:
        ka_o[0] = ka.T.astype(BF16)
    va = seg(_VA, _QB)
    va_o[...] = va.astype(BF16)
    qb_o[...] = (rope(seg(_QB, _KB)) * (B_HEAD_DIM ** -0.5)).astype(BF16)
    kb = rope(seg(_KB, _VB))
    kb_sw = pltpu.roll(kb, 64, 1)
    kb2_o[:, 0:LANES] = jnp.where(low, kb, kb_sw).astype(BF16)
    kb2_o[:, LANES:2 * LANES] = jnp.where(low, kb_sw, kb).astype(BF16)
    vb = seg(_VB, _GA)
    if is_ctx:
        vb_o[...] = vb.astype(BF16)
    else:
        vb_sw = pltpu.roll(vb, 64, 1)
        vb_o[:, 0:LANES] = jnp.where(low, vb, vb_sw).astype(BF16)
        vb_o[:, LANES:2 * LANES] = jnp.where(low, vb_sw, vb).astype(BF16)
    d = x.shape[1]
    sga_o[...] = jax.nn.sigmoid(seg(_GA, _GA + d)).astype(BF16)
    sgb_o[...] = jax.nn.sigmoid(seg(_GA + d, _GA + 2 * d)).astype(BF16)
    if is_ctx:
        kaf_o[...] = ka
        vaf_o[...] = va
        kbf_o[...] = kb
        vbf_o[...] = vb


def _inproj(x2, mod3, mod_row0, g_pre, cos_t, sin_t, w_in_b, *, seq, tm, is_ctx):
    t, d = x2.shape
    per = seq // tm
    assert t % tm == 0 and seq % tm == 0
    n_in = w_in_b.shape[1]
    wa = A_HEADS * 2 * A_HEAD_DIM
    wkb = B_KV_HEADS * B_HEAD_DIM

    def row(i):
        return (i // per) if not is_ctx else 0

    tok = lambda w: pl.BlockSpec((tm, w), lambda i: (i, 0))
    out_shape = [
        jax.ShapeDtypeStruct((t, wa), BF16), jax.ShapeDtypeStruct((t, wa), BF16),
        jax.ShapeDtypeStruct((t, wa), BF16), jax.ShapeDtypeStruct((t, wa), BF16),
        jax.ShapeDtypeStruct((t, 2 * wkb), BF16), jax.ShapeDtypeStruct((t, wkb), BF16),
        jax.ShapeDtypeStruct((t, d), BF16), jax.ShapeDtypeStruct((t, d), BF16),
    ]
    out_specs = [tok(wa), tok(wa), tok(wa), tok(wa), tok(2 * wkb), tok(wkb), tok(d), tok(d)]
    if not is_ctx:
        out_shape[5] = jax.ShapeDtypeStruct((t, 2 * wkb), BF16)
        out_specs[5] = tok(2 * wkb)
        out_shape[1] = jax.ShapeDtypeStruct((t // seq, wa, seq), BF16)
        out_specs[1] = pl.BlockSpec((1, wa, tm), lambda i: (i // per, 0, i % per))
    if is_ctx:
        out_shape += [jax.ShapeDtypeStruct((t, wa), F32), jax.ShapeDtypeStruct((t, wa), F32),
                      jax.ShapeDtypeStruct((t, wkb), F32), jax.ShapeDtypeStruct((t, wkb), F32)]
        out_specs += [tok(wa), tok(wa), tok(wkb), tok(wkb)]
    return pl.pallas_call(
        functools.partial(_inproj_kernel, is_ctx=is_ctx),
        grid=(t // tm,),
        in_specs=[
            pl.BlockSpec((tm, d), lambda i: (i, 0)),
            pl.BlockSpec((1, 1, d), lambda i: (mod_row0 + row(i), 0, 0)),
            pl.BlockSpec((1, 1, d), lambda i: (mod_row0 + row(i), 0, 1)),
            pl.BlockSpec((1, d), lambda i: (0, 0)),
            pl.BlockSpec((tm, LANES), lambda i: (i % per, 0)),
            pl.BlockSpec((tm, LANES), lambda i: (i % per, 0)),
            pl.BlockSpec((d, n_in), lambda i: (0, 0)),
        ],
        out_specs=out_specs,
        out_shape=out_shape,
        compiler_params=pltpu.CompilerParams(dimension_semantics=("arbitrary",)),
        name="inproj_ctx" if is_ctx else "inproj_lat",
    )(x2, mod3, mod3, g_pre, cos_t, sin_t, w_in_b)


def _nt(a, b):
    return lax.dot_general(a, b, (((1,), (1,)), ((), ())), preferred_element_type=F32)


def _diff_kernel(lam_ref, g_ref, q_ref, k_ref, v_ref, o_ref, *, lambda_init):
    lp = lam_ref[...]
    lam = (jnp.exp(jnp.sum(lp[0:1] * lp[1:2], axis=-1, keepdims=True))
           - jnp.exp(jnp.sum(lp[2:3] * lp[3:4], axis=-1, keepdims=True)) + lambda_init)
    tq = q_ref.shape[1]
    lane = lax.broadcasted_iota(jnp.int32, (1, LANES), 1)
    for h in range(A_HEADS):
        cols = slice(h * LANES, (h + 1) * LANES)
        q = q_ref[0, :, cols]
        q2 = jnp.concatenate([q * (lane < 64).astype(BF16), q * (lane >= 64).astype(BF16)], axis=0)
        s = _nt(q2, k_ref[0, :, cols])
        mx = jnp.max(s, axis=-1, keepdims=True)
        v = v_ref[0, :, cols]
        v_ext = jnp.concatenate([v, jnp.ones_like(v)], axis=1)
        acc = jnp.dot(jnp.exp(s - mx).astype(BF16), v_ext, preferred_element_type=F32)
        on = acc[:, 0:LANES] / acc[:, LANES:2 * LANES]
        o = on[0:tq] - lam * on[tq:2 * tq]
        o_ref[0, :, cols] = (_rms(o, g_ref[...]) * (1.0 - lambda_init)).astype(BF16)


def _diff_attention(lam_p, g_head, q, k, v, *, lambda_init):
    b, s, w = q.shape
    seq = pl.BlockSpec((1, s, w), lambda bi: (bi, 0, 0))
    return pl.pallas_call(
        functools.partial(_diff_kernel, lambda_init=lambda_init),
        grid=(b,),
        in_specs=[pl.BlockSpec((4, A_HEAD_DIM), lambda bi: (0, 0)),
                  pl.BlockSpec((1, A_V_DIM), lambda bi: (0, 0)), seq, seq, seq],
        out_specs=seq,
        out_shape=jax.ShapeDtypeStruct((b, s, w), BF16),
        compiler_params=pltpu.CompilerParams(dimension_semantics=("arbitrary",)),
        name="diff_attn_ctx",
    )(lam_p, g_head, q, k, v)


def _diff_lat_kernel(lam_ref, g_ref, q_ref, kc_ref, kt_ref, vc_ref, v_ref, o_ref, s_a, m_a, s_b, m_b, *,
                     lambda_init):
    t = pl.program_id(0)
    tq = q_ref.shape[1]
    nkc = kc_ref.shape[1]
    nkn = kt_ref.shape[2]
    nk = nkc + nkn

    @pl.when(t == 0)
    def _():
        s_b[...] = jnp.zeros_like(s_b)
        m_b[...] = jnp.zeros_like(m_b)

    def body(s_w, m_w, s_r, m_r):
        lp = lam_ref[...]
        lam = (jnp.exp(jnp.sum(lp[0:1] * lp[1:2], axis=-1, keepdims=True))
               - jnp.exp(jnp.sum(lp[2:3] * lp[3:4], axis=-1, keepdims=True)) + lambda_init)
        lane = lax.broadcasted_iota(jnp.int32, (1, LANES), 1)

        q = q_ref[0]
        q2 = jnp.concatenate([q * (lane < 64).astype(BF16), q * (lane >= 64).astype(BF16)], axis=0)
        sc = _nt(q2, kc_ref[0].astype(BF16))
        sn = jnp.dot(q2, kt_ref[0], preferred_element_type=F32)
        mx = jnp.maximum(jnp.max(sc, axis=-1, keepdims=True), jnp.max(sn, axis=-1, keepdims=True))
        s_w[:, 0:nkc] = sc
        s_w[:, nkc:nk] = sn
        m_w[...] = jnp.broadcast_to(mx, (2 * tq, LANES))

        mp = m_r[...]
        v_all = jnp.concatenate([vc_ref[0].astype(BF16), v_ref[0]], axis=0)
        v_ext = jnp.concatenate([v_all, jnp.ones_like(v_all)], axis=1)
        p = jnp.concatenate(
            [jnp.exp(s_r[:, c:c + LANES] - mp).astype(BF16) for c in range(0, nk, LANES)], axis=1)
        acc = jnp.dot(p, v_ext, preferred_element_type=F32)
        on = acc[:, 0:LANES] / acc[:, LANES:2 * LANES]
        o = on[0:tq] - lam * on[tq:2 * tq]
        o_ref[0] = (_rms(o, g_ref[...]) * (1.0 - lambda_init)).astype(BF16)

    @pl.when(t % 2 == 0)
    def _():
        body(s_a, m_a, s_b, m_b)

    @pl.when(t % 2 == 1)
    def _():
        body(s_b, m_b, s_a, m_a)


def _diff_attention_lat(lam_p, g_head, q, kc, kt, vc, v, *, tq, lambda_init):
    b, s, w = q.shape
    past = kc.shape[1]
    nq = s // tq
    n_units = b * A_HEADS * nq
    last = n_units - 1

    def unit(u):
        return u // (A_HEADS * nq), (u // nq) % A_HEADS, u % nq

    def cur(t):
        return unit(jnp.minimum(t, last))

    def prev(t):
        return unit(jnp.maximum(t - 1, 0))

    return pl.pallas_call(
        functools.partial(_diff_lat_kernel, lambda_init=lambda_init),
        grid=(n_units + 1,),
        in_specs=[
            pl.BlockSpec((4, A_HEAD_DIM), lambda t: (0, 0)),
            pl.BlockSpec((1, A_V_DIM), lambda t: (0, 0)),
            pl.BlockSpec((1, tq, LANES), lambda t: (cur(t)[0], cur(t)[2], cur(t)[1])),
            pl.BlockSpec((1, past, LANES), lambda t: (cur(t)[0], 0, cur(t)[1])),
            pl.BlockSpec((1, LANES, s), lambda t: (cur(t)[0], cur(t)[1], 0)),
            pl.BlockSpec((1, past, LANES), lambda t: (prev(t)[0], 0, prev(t)[1])),
            pl.BlockSpec((1, s, LANES), lambda t: (prev(t)[0], 0, prev(t)[1])),
        ],
        out_specs=pl.BlockSpec((1, tq, LANES), lambda t: (prev(t)[0], prev(t)[2], prev(t)[1])),
        out_shape=jax.ShapeDtypeStruct((b, s, w), BF16),
        scratch_shapes=[pltpu.VMEM((2 * tq, past + s), F32), pltpu.VMEM((2 * tq, LANES), F32),
                        pltpu.VMEM((2 * tq, past + s), F32), pltpu.VMEM((2 * tq, LANES), F32)],
        compiler_params=pltpu.CompilerParams(dimension_semantics=("arbitrary",),
                                             vmem_limit_bytes=56 * 1024 * 1024),
        name="diff_attn_lat",
    )(lam_p, g_head, q, kc, kt, vc, v)


def _swa_kernel(sink_ref, q_ref, k_ref, v_ref, o_ref):
    tq = q_ref.shape[1]
    rows = B_GROUP * tq
    lane = lax.broadcasted_iota(jnp.int32, (1, LANES), 1)
    low = lane < 64
    lane2 = lax.broadcasted_iota(jnp.int32, (1, 2 * LANES), 1)
    head_masks = [((lane2 // 64) == g).astype(BF16) for g in range(B_GROUP)]
    gw = B_GROUP * B_HEAD_DIM
    v = v_ref[0]
    v_ext = jnp.concatenate([v, jnp.ones_like(v)], axis=1)
    finishers = []
    for n in range(B_KV_HEADS):
        q = q_ref[0, :, n * gw:(n + 1) * gw]
        qs = jnp.concatenate([q * hm for hm in head_masks], axis=0)
        k2 = k_ref[0, :, n * LANES:(n + 1) * LANES]
        s = _nt(qs, jnp.concatenate([k2, k2], axis=1))
        sink = jnp.concatenate(
            [jnp.full((tq, 1), sink_ref[n * B_GROUP + g], F32) for g in range(B_GROUP)], axis=0)
        top = jnp.max(s, keepdims=True)
        for g in range(B_GROUP):
            top = jnp.maximum(top, sink_ref[n * B_GROUP + g])

        def finish(mp, n=n, s=s, sink=sink):
            acc = jnp.dot(jnp.exp(s - mp).astype(BF16), v_ext, preferred_element_type=F32)
            den = acc[:, LANES:2 * LANES] + jnp.exp(sink - mp)
            o = acc[:, 0:LANES] / den
            osw = pltpu.roll(o, 64, 1)
            for j in range(B_GROUP // 2):
                ra = slice((2 * j) * tq, (2 * j + 1) * tq)
                rb = slice((2 * j + 1) * tq, (2 * j + 2) * tq)
                pair = jnp.where(low, o[ra], osw[rb]) if n == 0 else jnp.where(low, osw[ra], o[rb])
                o_ref[0, :, n * gw + j * LANES:n * gw + (j + 1) * LANES] = pair.astype(BF16)
            return jnp.min(den)

        finishers.append((finish, s, sink, finish(top)))

    smallest = functools.reduce(jnp.minimum, [f[3] for f in finishers])

    @pl.when(jnp.logical_not(smallest >= 1e-30))
    def _():
        for finish, s, sink, _ in finishers:
            finish(jnp.maximum(jnp.max(s, axis=-1, keepdims=True), sink))


def _swa_attention(sink, q, k2, v):
    b, s, w = q.shape
    seq = lambda a: pl.BlockSpec((1, s, a.shape[2]), lambda bi: (bi, 0, 0))
    return pl.pallas_call(
        _swa_kernel,
        grid=(b,),
        in_specs=[pl.BlockSpec(memory_space=pltpu.SMEM), seq(q), seq(k2), seq(v)],
        out_specs=seq(q),
        out_shape=jax.ShapeDtypeStruct((b, s, w), BF16),
        compiler_params=pltpu.CompilerParams(dimension_semantics=("arbitrary",)),
        name="swa_attn_ctx",
    )(sink, q, k2, v)


def _swa_lat_kernel(sink_ref, q_ref, kc_ref, kl_ref, km_ref, kr_ref, vc_ref, vl_ref, vm_ref, vr_ref, o_ref,
                    s_a, m_a, s_b, m_b, *, nqb):
    t = pl.program_id(0)
    n_units = pl.num_programs(0) - 1
    tq = q_ref.shape[1]
    gw = B_GROUP * B_HEAD_DIM
    i_cur = jnp.minimum(t, n_units - 1) % nqb
    nkc = kc_ref.shape[1]
    rows = B_GROUP * tq

    @pl.when(t == 0)
    def _():
        s_b[...] = jnp.zeros_like(s_b)
        m_b[...] = jnp.zeros_like(m_b)

    def sink_col(n):
        return jnp.concatenate(
            [jnp.full((tq, 1), sink_ref[n * B_GROUP + g], F32) for g in range(B_GROUP)], axis=0)

    def body(s_w, m_w, s_r, m_r):
        lane = lax.broadcasted_iota(jnp.int32, (1, LANES), 1)
        low = lane < 64
        lane2 = lax.broadcasted_iota(jnp.int32, (1, 2 * LANES), 1)
        head_masks = [((lane2 // 64) == g).astype(BF16) for g in range(B_GROUP)]
        qi = lax.broadcasted_iota(jnp.int32, (rows, SWA_Q), 0) & (tq - 1)
        kj = lax.broadcasted_iota(jnp.int32, (rows, SWA_Q), 1)
        far = 2 * SWA_Q
        left_ok = kj >= qi + jnp.where(i_cur > 0, 0, far)
        right_ok = kj <= qi - jnp.where(i_cur < nqb - 1, 0, far)

        kc = kc_ref[0]
        kc_sw = pltpu.roll(kc, 64, 1)
        for n in range(B_KV_HEADS):
            q = q_ref[0, :, n * gw:(n + 1) * gw]
            qs = jnp.concatenate([q * hm for hm in head_masks], axis=0)
            kc2 = (jnp.where(low, kc, kc_sw) if n == 0 else jnp.where(low, kc_sw, kc)).astype(BF16)
            ks = [kc2] + [r[0, :, n * LANES:(n + 1) * LANES] for r in (kl_ref, km_ref, kr_ref)]
            k_all = jnp.concatenate([jnp.concatenate([k, k], axis=1) for k in ks], axis=0)
            s = _nt(qs, k_all)
            chunks = [s[:, c:c + LANES] for c in range(0, s.shape[1], LANES)]
            il = nkc // LANES
            chunks[il] = jnp.where(left_ok, chunks[il], NEG_INF)
            chunks[il + 2] = jnp.where(right_ok, chunks[il + 2], NEG_INF)
            top = jnp.max(functools.reduce(jnp.maximum, chunks), keepdims=True)
            for g in range(B_GROUP):
                top = jnp.maximum(top, sink_ref[n * B_GROUP + g])
            for c, ch in enumerate(chunks):
                s_w[n * rows:(n + 1) * rows, c * LANES:(c + 1) * LANES] = ch
            m_w[n * rows:(n + 1) * rows, :] = jnp.broadcast_to(top, (rows, LANES))

        vc = vc_ref[0]
        vc_sw = pltpu.roll(vc, 64, 1)
        nk = nkc + 3 * tq
        finishers = []
        for n in range(B_KV_HEADS):
            vc2 =(jnp.where(low, vc, vc_sw) if n == 0 else jnp.where(low, vc_sw, vc)).astype(BF16)
            v_all = jnp.concatenate(
                [vc2] + [r[0, :, n * LANES:(n + 1) * LANES] for r in (vl_ref, vm_ref, vr_ref)], axis=0)
            v_ext = jnp.concatenate([v_all, jnp.ones_like(v_all)], axis=1)
            sink = sink_col(n)

            def finish(mp, n=n, v_ext=v_ext, sink=sink):
                p = jnp.concatenate([jnp.exp(s_r[n * rows:(n + 1) * rows, c:c + LANES] - mp).astype(BF16)
                                     for c in range(0, nk, LANES)], axis=1)
                acc = jnp.dot(p, v_ext, preferred_element_type=F32)
                den = acc[:, LANES:2 * LANES] + jnp.exp(sink - mp)
                o = acc[:, 0:LANES] / den
                for j in range(B_GROUP // 2):
                    pair = jnp.where(low, o[(2 * j) * tq:(2 * j + 1) * tq], o[(2 * j + 1) * tq:(2 * j + 2) * tq])
                    o_ref[0, :, n * gw + j * LANES:n * gw + (j + 1) * LANES] = pair.astype(BF16)
                return jnp.min(den)

            finishers.append((n, finish, sink, finish(m_r[n * rows:(n + 1) * rows, :])))

        smallest = functools.reduce(jnp.minimum, [f[3] for f in finishers])

        @pl.when(jnp.logical_not(smallest >= 1e-30))
        def _():
            for n, finish, sink, _ in finishers:
                row_max = functools.reduce(
                    jnp.maximum, [s_r[n * rows:(n + 1) * rows, c:c + LANES] for c in range(0, nk, LANES)])
                mx = jnp.maximum(jnp.max(row_max, axis=-1, keepdims=True), sink)
                finish(jnp.broadcast_to(mx, (rows, LANES)))

    @pl.when(t % 2 == 0)
    def _():
        body(s_a, m_a, s_b, m_b)

    @pl.when(t % 2 == 1)
    def _():
        body(s_b, m_b, s_a, m_a)


def _swa_attention_lat(sink, q, kc, k2, vc, v):
    b, s, w = q.shape
    past = kc.shape[1]
    tq = SWA_Q
    nqb = s // tq
    n_units = b * nqb
    last = n_units - 1
    nk = past + 3 * tq

    def cur(t):
        u = jnp.minimum(t, last)
        return u // nqb, u % nqb

    def prev(t):
        u = jnp.maximum(t - 1, 0)
        return u // nqb, u % nqb

    lo = lambda i: jnp.maximum(i - 1, 0)
    hi = lambda i: jnp.minimum(i + 1, nqb - 1)
    kspec = lambda f: pl.BlockSpec((1, tq, 2 * LANES), lambda t: (cur(t)[0], f(cur(t)[1]), 0))
    vspec = lambda f: pl.BlockSpec((1, tq, 2 * LANES), lambda t: (prev(t)[0], f(prev(t)[1]), 0))
    same = lambda i: i
    rows = B_KV_HEADS * B_GROUP * tq
    return pl.pallas_call(
        functools.partial(_swa_lat_kernel, nqb=nqb),
        grid=(n_units + 1,),
        in_specs=[
            pl.BlockSpec(memory_space=pltpu.SMEM),
            pl.BlockSpec((1, tq, w), lambda t: (cur(t)[0], cur(t)[1], 0)),
            pl.BlockSpec((1, past, LANES), lambda t: (cur(t)[0], 0, 0)),
            kspec(lo), kspec(same), kspec(hi),
            pl.BlockSpec((1, past, LANES), lambda t: (prev(t)[0], 0, 0)),
            vspec(lo), vspec(same), vspec(hi),
        ],
        out_specs=pl.BlockSpec((1, tq, w), lambda t: (prev(t)[0], prev(t)[1], 0)),
        out_shape=jax.ShapeDtypeStruct((b, s, w), BF16),
        scratch_shapes=[pltpu.VMEM((rows, nk), F32), pltpu.VMEM((rows, LANES), F32),
                        pltpu.VMEM((rows, nk), F32), pltpu.VMEM((rows, LANES), F32)],
        compiler_params=pltpu.CompilerParams(dimension_semantics=("arbitrary",)),
        name="swa_attn_lat",
    )(sink, q, kc, k2, k2, k2, vc, v, v, v)


def _postmix_kernel(*refs, n_ctx_tiles):
    (xc, xl, oac, oal, obc, obl, sgac, sgal, sgbc, sgbl, g1_ref, sh2_ref, sc2_ref, gpm_ref, gpf_ref,
     wpa_ref, wpb_ref, wo_ref, wr_ref, br_ref, x1_o, h2_o, route_o, cnt_o, x1_a, x1_b) = refs
    t = pl.program_id(0)
    is_ctx = t < n_ctx_tiles
    pick = lambda a, b: jnp.where(is_ctx, a[...], b[...])

    @pl.when(t == 0)
    def _():
        x1_b[...] = jnp.zeros_like(x1_b)

    def body(x1_w, x1_r):
        pa = jnp.dot(pick(oac, oal), wpa_ref[...], preferred_element_type=F32)
        pb = jnp.dot(pick(obc, obl), wpb_ref[...], preferred_element_type=F32)
        mix = pick(sgac, sgal).astype(F32) * pa + pick(sgbc, sgbl).astype(F32) * pb
        m2 = jnp.dot(mix.astype(BF16), wo_ref[...], preferred_element_type=F32)
        x1 = pick(xc, xl) + g1_ref[0] * _rms(m2, gpm_ref[...])
        x1_o[...] = x1
        x1_w[...] = x1
        _postmix_route(x1_r[...], sh2_ref, sc2_ref, gpf_ref, wr_ref, br_ref, h2_o, route_o, cnt_o)

    @pl.when(t % 2 == 0)
    def _():
        body(x1_a, x1_b)

    @pl.when(t % 2 == 1)
    def _():
        body(x1_b, x1_a)


def _postmix_route(x1, sh2_ref, sc2_ref, gpf_ref, wr_ref, br_ref, h2_o, route_o, cnt_o):
    h2 = _rms(x1, gpf_ref[...]) * (1.0 + sc2_ref[0]) + sh2_ref[0]
    h2_o[...] = h2.astype(BF16)

    h_hi = h2.astype(BF16)
    h_lo = (h2 - h_hi.astype(F32)).astype(BF16)
    both = jnp.dot(h_hi, wr_ref[...], preferred_element_type=F32)
    logits = (both[:, 0:LANES] + both[:, LANES:2 * LANES]
              + jnp.dot(h_lo, wr_ref[:, 0:LANES], preferred_element_type=F32) + br_ref[...])
    tm = logits.shape[0]
    lt = logits.T
    row = lax.broadcasted_iota(jnp.int32, (EXPERTS_PER_GROUP, tm), 0).astype(F32)
    none = float(EXPERTS_PER_GROUP)
    lg = jnp.where(row < N_GROUPS, lt[N_EXPERTS:N_EXPERTS + EXPERTS_PER_GROUP], -jnp.inf)
    mg = jnp.max(lg, axis=0, keepdims=True)
    g_sel = jnp.min(jnp.where(lg == mg, row, none), axis=0, keepdims=True)
    g_w = 1.0 / jnp.sum(jnp.exp(lg - mg), axis=0, keepdims=True)
    le = lt[0:EXPERTS_PER_GROUP]
    for g in range(1, N_GROUPS):
        le = jnp.where(g_sel == g, lt[g * EXPERTS_PER_GROUP:(g + 1) * EXPERTS_PER_GROUP], le)
    v0 = jnp.max(le, axis=0, keepdims=True)
    i0 = jnp.min(jnp.where(le == v0, row, none), axis=0, keepdims=True)
    le1 = jnp.where(row == i0, -jnp.inf, le)
    v1 = jnp.max(le1, axis=0, keepdims=True)
    i1 = jnp.min(jnp.where(le1 == v1, row, none), axis=0, keepdims=True)
    e = jnp.exp(v1 - v0)
    w0 = g_w / (1.0 + e)
    w1 = g_w * e / (1.0 + e)
    e0 = g_sel * EXPERTS_PER_GROUP + i0
    e1 = g_sel * EXPERTS_PER_GROUP + i1
    route_o[...] = jnp.where(row == 0, e0, jnp.where(row == 1, e1, jnp.where(row == 2, w0,
                                                                             jnp.where(row == 3, w1, 0.0))))
    erow = lax.broadcasted_iota(jnp.int32, (N_EXPERTS, tm), 0).astype(F32)
    cnt = jnp.sum((erow == e0).astype(F32) + (erow == e1).astype(F32), axis=1, keepdims=True)
    cnt_o[0] = jnp.broadcast_to(cnt, (N_EXPERTS, LANES))


def _postmix(ctx_in, lat_in, mod3, gpm, gpf, wpa, wpb, wo, wr, br, *, lat_seq, tm):
    t_ctx, d = ctx_in[0].shape
    t_lat = lat_in[0].shape[0]
    assert t_ctx % tm == 0 and lat_seq % tm == 0
    nc = t_ctx // tm
    nl = t_lat // tm
    per = lat_seq // tm
    t_all = t_ctx + t_lat

    last = nc + nl - 1
    cur = lambda i: jnp.minimum(i, last)
    prev = lambda i: jnp.maximum(i - 1, 0)
    mod_row = lambda j: jnp.where(j < nc, 0, 1 + jnp.maximum(j - nc, 0) // per)
    full = lambda a: pl.BlockSpec(a.shape, lambda i: (0,) * a.ndim)
    in_specs, args = [], []
    for a_c, a_l in zip(ctx_in, lat_in):
        w = a_c.shape[1]
        in_specs += [pl.BlockSpec((tm, w), lambda i: (jnp.minimum(cur(i), nc - 1), 0)),
                     pl.BlockSpec((tm, w), lambda i: (jnp.maximum(cur(i) - nc, 0), 0))]
        args += [a_c, a_l]
    in_specs += [pl.BlockSpec((1, 1, d), lambda i: (mod_row(cur(i)), 0, 2)),
                 pl.BlockSpec((1, 1, d), lambda i: (mod_row(prev(i)), 0, 3)),
                 pl.BlockSpec((1, 1, d), lambda i: (mod_row(prev(i)), 0, 4)),
                 full(gpm), full(gpf), full(wpa), full(wpb), full(wo), full(wr), full(br)]
    args += [mod3, mod3, mod3, gpm, gpf, wpa, wpb, wo, wr, br]
    return pl.pallas_call(
        functools.partial(_postmix_kernel, n_ctx_tiles=nc),
        grid=(nc + nl + 1,),
        in_specs=in_specs,
        out_specs=[pl.BlockSpec((tm, d), lambda i: (cur(i), 0)),
                   pl.BlockSpec((tm, d), lambda i: (prev(i), 0)),
                   pl.BlockSpec((EXPERTS_PER_GROUP, tm), lambda i: (0, prev(i))),
                   pl.BlockSpec((1, N_EXPERTS, LANES), lambda i: (prev(i), 0, 0))],
        out_shape=[jax.ShapeDtypeStruct((t_all, d), F32),
                   jax.ShapeDtypeStruct((t_all, d), BF16),
                   jax.ShapeDtypeStruct((EXPERTS_PER_GROUP, t_all), F32),
                   jax.ShapeDtypeStruct((nc + nl, N_EXPERTS, LANES), F32)],
        scratch_shapes=[pltpu.VMEM((tm, d), F32), pltpu.VMEM((tm, d), F32)],
        compiler_params=pltpu.CompilerParams(dimension_semantics=("arbitrary",)),
        name="postmix",
    )(*args)


def _segment_copies(src, src_row, dst, dst_row, n, sub, sem):
    @pl.when(n > 0)
    def _():
        pltpu.make_async_copy(src.at[pl.ds(pl.multiple_of(src_row * sub, sub), n * sub)],
                              dst.at[pl.ds(pl.multiple_of(dst_row * sub, sub), n * sub)], sem).start()


def _local_positions(route_t, tile_base):
    tm = route_t.shape[1]
    erow = lax.broadcasted_iota(jnp.int32, (N_EXPERTS, tm), 0).astype(F32)
    is0 = erow == route_t[0:1]
    is1 = erow == route_t[1:2]
    earlier = (lax.broadcasted_iota(jnp.int32, (tm, tm), 0)
               < lax.broadcasted_iota(jnp.int32, (tm, tm), 1)).astype(BF16)
    pre0 = jnp.dot(is0.astype(BF16), earlier, preferred_element_type=F32)
    pre1 = jnp.dot(is1.astype(BF16), earlier, preferred_element_type=F32)
    cnt0 = jnp.sum(is0.astype(F32), axis=1, keepdims=True)
    base = tile_base[:, 0:1]
    lpos0 = jnp.sum(jnp.where(is0, base + pre0, 0.0), axis=0, keepdims=True)
    lpos1 = jnp.sum(jnp.where(is1, base + cnt0 + pre1, 0.0), axis=0, keepdims=True)
    return lpos0, lpos1


def _dispatch_kernel(ss_ref, sl_ref, tb_ref, h_ref, r_ref, tbv_ref, xs_hbm, pbuf, zbuf, sem, zsem, *, n_asg, rows):
    i = pl.program_id(0)
    nt = pl.num_programs(0)
    tm, d = h_ref.shape
    sub = d // LANES
    nrow = TOP_K * tm
    slot = i % 2

    def wait_slot(s):
        pltpu.make_async_copy(pbuf.at[pl.ds(pl.multiple_of(s * nrow * sub, nrow * sub), nrow * sub)],
                              xs_hbm.at[pl.ds(0, nrow * sub)], sem.at[s]).wait()

    def slack_copy():
        return pltpu.make_async_copy(zbuf, xs_hbm.at[pl.ds(n_asg * sub, rows * sub)], zsem.at[0])

    @pl.when(i == 0)
    def _():
        zbuf[...] = jnp.zeros_like(zbuf)
        slack_copy().start()

    lpos0, lpos1 = _local_positions(r_ref[...], tbv_ref[0])
    p = lax.broadcasted_iota(jnp.int32, (nrow, tm), 0).astype(F32)
    sel = ((p == lpos0) | (p == lpos1)).astype(BF16)
    xp = jnp.dot(sel, h_ref[...], preferred_element_type=F32)

    @pl.when(i >= 2)
    def _():
        wait_slot(slot)

    _store_row_tiles(pbuf, slot * nrow * sub, xp)

    def seg(e, c):
        k = i * N_EXPERTS + e
        _segment_copies(pbuf, slot * nrow + tb_ref[k], xs_hbm, ss_ref[k], sl_ref[k], sub, sem.at[slot])
        return c
    lax.fori_loop(0, N_EXPERTS, seg, 0)

    @pl.when(i == nt - 1)
    def _():
        wait_slot(slot)

        @pl.when(nt >= 2)
        def _():
            wait_slot(1 - slot)
        slack_copy().wait()


def _dispatch(h2, route, tables, *, tm, rows):
    seg_start, seg_len, tile_base, tile_base_v = tables
    t, d = h2.shape
    sub = d // LANES
    n_asg = t * TOP_K
    grid_spec = pltpu.PrefetchScalarGridSpec(
        num_scalar_prefetch=3,
        grid=(t // tm,),
        in_specs=[
            pl.BlockSpec((tm, d), lambda i, *_: (i, 0)),
            pl.BlockSpec((EXPERTS_PER_GROUP, tm), lambda i, *_: (0, i)),
            pl.BlockSpec((1, N_EXPERTS, LANES), lambda i, *_: (i, 0, 0)),
        ],
        out_specs=pl.BlockSpec(memory_space=pl.ANY),
        scratch_shapes=[pltpu.VMEM((2 * TOP_K * tm * sub, LANES), F32), pltpu.VMEM((rows * sub, LANES), F32),
                        pltpu.SemaphoreType.DMA((2,)), pltpu.SemaphoreType.DMA((1,))],
    )
    return pl.pallas_call(
        functools.partial(_dispatch_kernel, n_asg=n_asg, rows=rows),
        grid_spec=grid_spec,
        out_shape=jax.ShapeDtypeStruct(((n_asg + rows) * sub, LANES), F32),
        compiler_params=pltpu.CompilerParams(dimension_semantics=("arbitrary",)),
        name="dispatch",
    )(seg_start, seg_len, tile_base, h2, route, tile_base_v)


def _moe_kernel(be_ref, row0_ref, nact_ref, par_ref, nxt_ref, xs_hbm, w1_hbm, w3_hbm, w2_hbm, ys_hbm,
                xbuf, obuf, wf1, wf3, wf2, w1b, w3b, w2b, rsem, wsem, gsem, *, rows, sub):
    i = pl.program_id(0)
    nact = nact_ref[0]
    slot = i % 2
    nslot = 1 - slot
    blk = rows * sub

    def weight_copies(e, s):
        return [pltpu.make_async_copy(w_hbm.at[e], wf.at[s], gsem.at[s])
                for w_hbm, wf in ((w1_hbm, wf1), (w3_hbm, wf3), (w2_hbm, wf2))]

    def read(j, s):
        return pltpu.make_async_copy(xs_hbm.at[pl.ds(pl.multiple_of(row0_ref[j] * sub, sub), blk)],
                                     xbuf.at[pl.ds(pl.multiple_of(s * blk, blk), blk)], rsem.at[s])

    def write(j, s):
        return pltpu.make_async_copy(obuf.at[pl.ds(pl.multiple_of(s * blk, blk), blk)],
                                     ys_hbm.at[pl.ds(pl.multiple_of(row0_ref[j] * sub, sub), blk)], wsem.at[s])

    @pl.when(i == 0)
    def _():
        read(0, 0).start()
        for c in weight_copies(be_ref[0], 0):
            c.start()

    @pl.when(i < nact)
    def _():
        @pl.when(i + 1 < nact)
        def _():
            read(i + 1, nslot).start()

        changed = jnp.logical_or(i == 0, be_ref[i] != be_ref[jnp.maximum(i - 1, 0)])

        @pl.when(changed)
        def _():
            s = par_ref[i]
            for c in weight_copies(be_ref[i], s):
                c.wait()
            w1b[...] = wf1[s].astype(BF16)
            w3b[...] = wf3[s].astype(BF16)
            w2b[...] = wf2[s].astype(BF16)

            @pl.when(nxt_ref[i] >= 0)
            def _():
                for c in weight_copies(nxt_ref[i], 1 - s):
                    c.start()

        read(i, slot).wait()
        x = _load_row_tiles(xbuf, slot * blk, rows, sub).astype(BF16)
        a = jnp.dot(x, w1b[...], preferred_element_type=F32)
        b = jnp.dot(x, w3b[...], preferred_element_type=F32)
        hmid = (a * jax.nn.sigmoid(a) * b).astype(BF16)
        y = jnp.dot(hmid, w2b[...], preferred_element_type=F32)
        _store_row_tiles(obuf, slot * blk, y)

        @pl.when(i >= 1)
        def _():
            write(i - 1, nslot).wait()
        write(i, slot).start()

    @pl.when(i == nact)
    def _():
        write(i - 1, nslot).wait()
        obuf[pl.ds(pl.multiple_of(slot * blk, blk), blk), :] = jnp.zeros((blk, LANES), F32)
        tail = pltpu.make_async_copy(obuf.at[pl.ds(pl.multiple_of(slot * blk, blk), blk)],
                                     ys_hbm.at[pl.ds(ys_hbm.shape[0] - blk, blk)], wsem.at[slot])
        tail.start()
        tail.wait()


def _moe(blk_tables, xs, w1, w3, w2):
    blk_expert, row0, nact, parity, nxt = blk_tables
    nblk = blk_expert.shape[0] - 1
    d, de = w1.shape[1], w1.shape[2]
    sub = d // LANES
    rows = MOE_ROWS
    anyspec = pl.BlockSpec(memory_space=pl.ANY)
    grid_spec = pltpu.PrefetchScalarGridSpec(
        num_scalar_prefetch=5,
        grid=(nblk + 1,),
        in_specs=[anyspec, anyspec, anyspec, anyspec],
        out_specs=anyspec,
        scratch_shapes=[
            pltpu.VMEM((2 * rows * sub, LANES), F32),
            pltpu.VMEM((2 * rows * sub, LANES), F32),
            pltpu.VMEM((2, d, de), F32),
            pltpu.VMEM((2, d, de), F32),
            pltpu.VMEM((2, de, d), F32),
            pltpu.VMEM((d, de), BF16),
            pltpu.VMEM((d, de), BF16),
            pltpu.VMEM((de, d), BF16),
            pltpu.SemaphoreType.DMA((2,)),
            pltpu.SemaphoreType.DMA((2,)),
            pltpu.SemaphoreType.DMA((2,)),
        ],
    )
    return pl.pallas_call(
        functools.partial(_moe_kernel, rows=rows, sub=sub),
        grid_spec=grid_spec,
        out_shape=jax.ShapeDtypeStruct(xs.shape, F32),
        compiler_params=pltpu.CompilerParams(dimension_semantics=("arbitrary",)),
        name="expert_mlp",
    )(blk_expert, row0, nact, parity, nxt, xs, w1, w3, w2)


def _combine_kernel(ss_ref, sl_ref, tb_ref, x1_ref, r_ref, tbv_ref, g2_ref, gpost_ref, ys_hbm, oc_ref, ol_ref,
                    ybuf, sem, *, n_ctx_tiles):
    tile0 = 0
    i = pl.program_id(0)
    nt = pl.num_programs(0)
    tm, d = x1_ref.shape
    sub = d // LANES
    nrow = TOP_K * tm
    slot = i % 2

    def fetch(tile, s):
        def seg(e, c):
            k = tile * N_EXPERTS + e
            _segment_copies(ys_hbm, ss_ref[k], ybuf, s * nrow + tb_ref[k], sl_ref[k], sub, sem.at[s])
            return c
        lax.fori_loop(0, N_EXPERTS, seg, 0)

    @pl.when(i == 0)
    def _():
        fetch(tile0, 0)

    @pl.when(i + 1 < nt)
    def _():
        fetch(tile0 + i + 1, 1 - slot)

    pltpu.make_async_copy(ys_hbm.at[pl.ds(0, nrow * sub)],
                          ybuf.at[pl.ds(pl.multiple_of(slot * nrow * sub, nrow * sub), nrow * sub)],
                          sem.at[slot]).wait()
    rt = r_ref[...]
    lpos0, lpos1 = _local_positions(rt, tbv_ref[0])
    rows8 = lax.broadcasted_iota(jnp.int32, (EXPERTS_PER_GROUP, tm), 0)
    cols = jnp.where(rows8 == 0, lpos0, jnp.where(rows8 == 1, lpos1, rt))
    cols = jnp.concatenate([cols, jnp.zeros((LANES - EXPERTS_PER_GROUP, tm), F32)], axis=0).T
    p = lax.broadcasted_iota(jnp.int32, (tm, nrow), 1).astype(F32)
    q = (jnp.where(p == cols[:, 0:1], cols[:, 2:3], 0.0)
         + jnp.where(p == cols[:, 1:2], cols[:, 3:4], 0.0)).astype(BF16)
    ysort = _load_row_tiles(ybuf, slot * nrow * sub, nrow, sub).astype(BF16)
    y = jnp.dot(q, ysort, preferred_element_type=F32)
    out = x1_ref[...] + g2_ref[0] * _rms(y, gpost_ref[...])

    @pl.when(i < n_ctx_tiles)
    def _():
        oc_ref[...] = out

    @pl.when(i >= n_ctx_tiles)
    def _():
        ol_ref[...] = out


def _combine(x1, ys, route, tables, mod3, gpost, *, t_ctx, lat_seq, tm):
    seg_start, seg_len, tile_base, tile_base_v = tables
    t_all, d = x1.shape
    assert t_ctx % tm == 0 and lat_seq % tm == 0
    nc = t_ctx // tm
    per = lat_seq // tm
    sub = d // LANES
    mod_row = lambda i: jnp.where(i < nc, 0, 1 + jnp.maximum(i - nc, 0) // per)
    grid_spec = pltpu.PrefetchScalarGridSpec(
        num_scalar_prefetch=3,
        grid=(t_all // tm,),
        in_specs=[
            pl.BlockSpec((tm, d), lambda i, *_: (i, 0)),
            pl.BlockSpec((EXPERTS_PER_GROUP, tm), lambda i, *_: (0, i)),
            pl.BlockSpec((1, N_EXPERTS, LANES), lambda i, *_: (i, 0, 0)),
            pl.BlockSpec((1, 1, d), lambda i, *_: (mod_row(i), 0, 5)),
            pl.BlockSpec((1, d), lambda i, *_: (0, 0)),
            pl.BlockSpec(memory_space=pl.ANY),
        ],
        out_specs=[pl.BlockSpec((tm, d), lambda i, *_: (jnp.minimum(i, nc - 1), 0)),
                   pl.BlockSpec((tm, d), lambda i, *_: (jnp.maximum(i - nc, 0), 0))],
        scratch_shapes=[pltpu.VMEM((2 * TOP_K * tm * sub, LANES), F32), pltpu.SemaphoreType.DMA((2,))],
    )
    return pl.pallas_call(
        functools.partial(_combine_kernel, n_ctx_tiles=nc),
        grid_spec=grid_spec,
        out_shape=[jax.ShapeDtypeStruct((t_ctx, d), F32), jax.ShapeDtypeStruct((t_all - t_ctx, d), F32)],
        compiler_params=pltpu.CompilerParams(dimension_semantics=("arbitrary",)),
        name="combine",
    )(seg_start, seg_len, tile_base, x1, route, tile_base_v, mod3, gpost, ys)


def _routing_tables(counts, tm, rows):
    nt = counts.shape[0]
    n_asg = nt * tm * TOP_K
    ex = jnp.arange(N_EXPERTS, dtype=jnp.int32)
    cnt_te = counts[:, :, 0].astype(jnp.int32)
    cnt_e = jnp.sum(cnt_te, axis=0)
    start_e = jnp.cumsum(cnt_e) - cnt_e
    seg_start = start_e[None, :] + jnp.cumsum(cnt_te, axis=0) - cnt_te
    tile_base = jnp.cumsum(cnt_te, axis=1) - cnt_te
    tile_base_v = jnp.broadcast_to(tile_base.astype(F32)[:, :, None], (nt, N_EXPERTS, LANES))

    nblk_e = (cnt_e + rows - 1) // rows
    blk_end = jnp.cumsum(nblk_e)
    blk_start = blk_end - nblk_e
    n_blocks = n_asg // rows + N_EXPERTS
    b = jnp.arange(n_blocks + 1, dtype=jnp.int32)
    be = jnp.minimum(jnp.sum((blk_end[None, :] <= b[:, None]).astype(jnp.int32), axis=1), N_EXPERTS - 1)
    first = jnp.sum(jnp.where(be[:, None] == ex[None, :], (start_e - blk_start * rows)[None, :], 0), axis=1)
    row0 = jnp.clip(first + b * rows, 0, n_asg)
    nact = blk_end[-1:].astype(jnp.int32)
    used = cnt_e > 0
    parity_e = (jnp.cumsum(used.astype(jnp.int32)) - 1) % 2
    later = (ex[None, :] > ex[:, None]) & used[None, :]
    nxt_e = jnp.min(jnp.where(later, ex[None, :], N_EXPERTS), axis=1)
    nxt_e = jnp.where(nxt_e == N_EXPERTS, -1, nxt_e)
    pick = lambda tab: jnp.sum(jnp.where(be[:, None] == ex[None, :], tab[None, :], 0), axis=1).astype(jnp.int32)
    seg = (seg_start.reshape(-1).astype(jnp.int32), cnt_te.reshape(-1).astype(jnp.int32),
           tile_base.reshape(-1).astype(jnp.int32), tile_base_v)
    blk = (be.astype(jnp.int32), row0.astype(jnp.int32), nact, pick(parity_e), pick(nxt_e))
    return seg, blk


def _rope_tables(n_tok):
    n_rows = n_tok // GRID_W
    rows = jnp.repeat(jnp.arange(n_rows), GRID_W).astype(F32)
    cols = jnp.tile(jnp.arange(GRID_W), n_rows).astype(F32)
    quarter = A_HEAD_DIM // 4
    inv = ROPE_BASE ** (-jnp.arange(quarter, dtype=F32) / quarter)
    ang = jnp.concatenate([rows[:, None] * inv, cols[:, None] * inv], axis=-1)
    cos, sin = jnp.cos(ang), jnp.sin(ang)
    cos_t = jnp.tile(jnp.concatenate([cos, cos], axis=-1), (1, LANES // A_HEAD_DIM))
    sin_t = jnp.tile(jnp.concatenate([-sin, sin], axis=-1), (1, LANES // A_HEAD_DIM))
    return cos_t, sin_t


def kernel(x_prompt, x_sample, c, cache_diff_k, cache_diff_v, cache_swa_k, cache_swa_v, c_ctx, w_ada, b_ada, g_pre_mix, g_post_mix, g_pre_ffn, g_post_ffn, w_in, lam_q1, lam_k1, lam_q2, lam_k2, g_diff_head, sink, w_proj_a, w_proj_b, w_out, w_router_group, b_router_group, w_router_expert, b_router_expert, w_e1, w_e3, w_e2):
    depth = w_in.shape[0]
    assert depth == 1
    l = 0
    bp, sp, d = x_prompt.shape
    bs, ss, _ = x_sample.shape
    lambda_init = 0.8 - 0.6 * math.exp(-0.3 * l)
    assert A_HEAD_DIM == B_HEAD_DIM and ss % GRID_W == 0 and bs + 1 <= MOD_ROWS

    c_all = jnp.concatenate([c_ctx[None, :], c, jnp.zeros((MOD_ROWS - 1 - bs, d), F32)], axis=0)
    mod = _modulation(c_all, w_ada[l], b_ada[l][None, :])
    mod3 = mod.reshape(MOD_ROWS, 1, 6 * d)

    w_in_b = w_in[l].astype(BF16)
    wpa = w_proj_a[l].astype(BF16)
    wpb = w_proj_b[l].astype(BF16)
    wo = w_out[l].astype(BF16)
    n_r = N_GROUPS + N_EXPERTS
    wr = jnp.concatenate([w_router_expert[l], w_router_group[l], jnp.zeros((d, LANES - n_r), F32)], axis=1)
    br = jnp.concatenate([b_router_expert[l], b_router_group[l], jnp.zeros((LANES - n_r,), F32)])[None, :]
    wr_hi = wr.astype(BF16)
    wr2 = jnp.concatenate([wr_hi, (wr - wr_hi.astype(F32)).astype(BF16)], axis=1)
    lam_p = jnp.stack([lam_q1[l], lam_k1[l], lam_q2[l], lam_k2[l]], axis=0)
    g_head = g_diff_head[l][None, :]
    sink_l = sink[l]
    cos_t, sin_t = _rope_tables(ss)

    xp2 = x_prompt.reshape(bp * sp, d)
    xs2 = x_sample.reshape(bs * ss, d)
    gpre = g_pre_mix[l][None, :]

    (qa_c, ka_c, va_c, qb_c, kb2_c, vb_c, sga_c, sgb_c, kaf, vaf, kbf, vbf) = _inproj(
        xp2, mod3, 0, gpre, cos_t, sin_t, w_in_b, seq=sp, tm=sp, is_ctx=True)
    r3 = lambda a, b_: a.reshape(b_, -1, a.shape[-1])
    oa_c = _diff_attention(lam_p, g_head, r3(qa_c, bp), r3(ka_c, bp), r3(va_c, bp), lambda_init=lambda_init)
    ob_c = _swa_attention(sink_l, r3(qb_c, bp), r3(kb2_c, bp), r3(vb_c, bp))

    (qa_s, ka_s, va_s, qb_s, kb2_s, vb_s, sga_s, sgb_s) = _inproj(
        xs2, mod3, 1, gpre, cos_t, sin_t, w_in_b, seq=ss, tm=512, is_ctx=False)
    past = cache_diff_k.shape[2]
    ck = cache_diff_k[:, l].reshape(bs, past, -1)
    cv = cache_diff_v[:, l].reshape(bs, past, -1)
    oa_s = _diff_attention_lat(lam_p, g_head, r3(qa_s, bs), ck, ka_s, cv, r3(va_s, bs),
                               tq=512, lambda_init=lambda_init)
    sk = cache_swa_k[:, l].reshape(bs, past, -1)
    sv = cache_swa_v[:, l].reshape(bs, past, -1)
    kb2_3, vb_3 = r3(kb2_s, bs), r3(vb_s, bs)
    ob_s = _swa_attention_lat(sink_l, r3(qb_s, bs), sk, kb2_3, sv, vb_3)

    gpm = g_post_mix[l][None, :]
    gpf = g_pre_ffn[l][None, :]
    t_ctx, t_lat = bp * sp, bs * ss
    x1, h2t, route, counts = _postmix(
        (xp2, oa_c.reshape(t_ctx, -1), ob_c.reshape(t_ctx, -1), sga_c, sgb_c),
        (xs2, oa_s.reshape(t_lat, -1), ob_s.reshape(t_lat, -1), sga_s, sgb_s),
        mod3, gpm, gpf, wpa, wpb, wo, wr2, br, lat_seq=ss, tm=512)

    tables, blk_tables = _routing_tables(counts, MOE_TILE, MOE_ROWS)
    xs = _dispatch(h2t, route, tables, tm=MOE_TILE, rows=MOE_ROWS)
    ys = _moe(blk_tables, xs, w_e1[l], w_e3[l], w_e2[l])

    gpost = g_post_ffn[l][None, :]
    y_p, y_s = _combine(x1, ys, route, tables, mod3, gpost, t_ctx=t_ctx, lat_seq=ss, tm=MOE_TILE)

    ha = A_HEADS
    return (y_p.reshape(bp, sp, d), y_s.reshape(bs, ss, d),
            kaf.reshape(bp, 1, sp, ha, 2, A_HEAD_DIM), vaf.reshape(bp, 1, sp, ha, A_V_DIM),
            kbf.reshape(bp, 1, sp, B_KV_HEADS, B_HEAD_DIM), vbf.reshape(bp, 1, sp, B_KV_HEADS, B_HEAD_DIM))
```

```python
import functools
import math

import jax
import jax.numpy as jnp
from jax import lax
from jax.experimental import pallas as pl
from jax.experimental.pallas import tpu as pltpu

F32 = jnp.float32
BF16 = jnp.bfloat16

GRID_W = 64
ROPE_BASE = 10000.0
EPS = 1e-6
NEG_INF = -1e30
A_HEADS = 4
A_HEAD_DIM = 64
A_V_DIM = 2 * A_HEAD_DIM
B_HEADS = 8
B_KV_HEADS = 2
B_GROUP = B_HEADS // B_KV_HEADS
B_HEAD_DIM = 64
WINDOW = 128
N_GROUPS = 4
EXPERTS_PER_GROUP = 8
N_EXPERTS = N_GROUPS * EXPERTS_PER_GROUP
TOP_K = 2

LANES = 128
MOD_ROWS = 16
MOE_ROWS = 512
MOE_TILE = 512
SWA_Q = 128

_QA = 0
_KA = _QA + A_HEADS * 2 * A_HEAD_DIM
_VA = _KA + A_HEADS * 2 * A_HEAD_DIM
_QB = _VA + A_HEADS * A_V_DIM
_KB = _QB + B_HEADS * B_HEAD_DIM
_VB = _KB + B_KV_HEADS * B_HEAD_DIM
_GA = _VB + B_KV_HEADS * B_HEAD_DIM


def _rms(x, g):
    return x * lax.rsqrt(jnp.mean(x * x, axis=-1, keepdims=True) + EPS) * g


def _store_row_tiles(ref, base, val):
    sub = val.shape[1] // LANES
    for s in range(sub):
        ref[pl.ds(base + s, val.shape[0], stride=sub), :] = val[:, s * LANES:(s + 1) * LANES]


def _load_row_tiles(ref, base, n_rows, sub):
    return jnp.concatenate([ref[pl.ds(base + s, n_rows, stride=sub), :] for s in range(sub)], axis=1)


def _mod_kernel(c_ref, w_ref, b_ref, o_ref):
    c = c_ref[...]
    a = c * jax.nn.sigmoid(c)
    w = w_ref[...]
    a_hi = a.astype(BF16)
    a_lo = (a - a_hi.astype(F32)).astype(BF16)
    w_hi = w.astype(BF16)
    w_lo = (w - w_hi.astype(F32)).astype(BF16)
    o_ref[...] = (jnp.dot(a_hi, w_hi, preferred_element_type=F32) + jnp.dot(a_lo, w_hi, preferred_element_type=F32)
                  + jnp.dot(a_hi, w_lo, preferred_element_type=F32) + b_ref[...])


def _modulation(c_all, w_ada, b_ada):
    d, n = w_ada.shape
    tn = 512
    return pl.pallas_call(
        _mod_kernel,
        grid=(n // tn,),
        in_specs=[
            pl.BlockSpec((MOD_ROWS, d), lambda j: (0, 0)),
            pl.BlockSpec((d, tn), lambda j: (0, j)),
            pl.BlockSpec((1, tn), lambda j: (0, j)),
        ],
        out_specs=pl.BlockSpec((MOD_ROWS, tn), lambda j: (0, j)),
        out_shape=jax.ShapeDtypeStruct((MOD_ROWS, n), F32),
        name="modulation",
    )(c_all, w_ada, b_ada)


def _rope128(z, cos, sin_signed, first_half):
    rot = jnp.where(first_half, pltpu.roll(z, 96, 1), pltpu.roll(z, 32, 1))
    return z * cos + rot * sin_signed


def _inproj_kernel(x_ref, sh_ref, sc_ref, g_ref, cos_ref, sin_ref, w_ref, *outs, is_ctx):
    x = x_ref[...]
    h = _rms(x, g_ref[...]) * (1.0 + sc_ref[0]) + sh_ref[0]
    hb = h.astype(BF16)
    lane = lax.broadcasted_iota(jnp.int32, (1, LANES), 1)
    first_half = (lane % 64) < 32
    low = lane < 64

    def seg(lo, hi):
        return jnp.dot(hb, w_ref[:, lo:hi], preferred_element_type=F32)

    def rope(z):
        if is_ctx:
            return z
        cos = cos_ref[...]
        sin = sin_ref[...]
        parts = [_rope128(z[:, j:j + LANES], cos, sin, first_half) for j in range(0, z.shape[1], LANES)]
        return parts[0] if len(parts) == 1 else jnp.concatenate(parts, axis=1)

    if is_ctx:
        qa_o, ka_o, va_o, qb_o, kb2_o, vb_o, sga_o, sgb_o, kaf_o, vaf_o, kbf_o, vbf_o = outs
    else:
        qa_o, ka_o, va_o, qb_o, kb2_o, vb_o, sga_o, sgb_o = outs

    scale = A_HEAD_DIM ** -0.5
    qa_o[...] = (rope(seg(_QA, _KA)) * scale).astype(BF16)
    ka = rope(seg(_KA, _VA))
    if is_ctx:
        ka_o[...] = ka.astype(BF16)
    else:
        ka_o[0] = ka.T.astype(BF16)
    va = seg(_VA, _QB)
    va_o[...] = va.astype(BF16)
    qb_o[...] = (rope(seg(_QB, _KB)) * (B_HEAD_DIM ** -0.5)).astype(BF16)
    kb = rope(seg(_KB, _VB))
    kb_sw = pltpu.roll(kb, 64, 1)
    kb2_o[:, 0:LANES] = jnp.where(low, kb, kb_sw).astype(BF16)
    kb2_o[:, LANES:2 * LANES] = jnp.where(low, kb_sw, kb).astype(BF16)
    vb = seg(_VB, _GA)
    if is_ctx:
        vb_o[...] = vb.astype(BF16)
    else:
        vb_sw = pltpu.roll(vb, 64, 1)
        vb_o[:, 0:LANES] = jnp.where(low, vb, vb_sw).astype(BF16)
        vb_o[:, LANES:2 * LANES] = jnp.where(low, vb_sw, vb).astype(BF16)
    d = x.shape[1]
    sga_o[...] = jax.nn.sigmoid(seg(_GA, _GA + d)).astype(BF16)
    sgb_o[...] = jax.nn.sigmoid(seg(_GA + d, _GA + 2 * d)).astype(BF16)
    if is_ctx:
        kaf_o[...] = ka
        vaf_o[...] = va
        kbf_o[...] = kb
        vbf_o[...] = vb


def _inproj(x2, mod3, mod_row0, g_pre, cos_t, sin_t, w_in_b, *, seq, tm, is_ctx):
    t, d = x2.shape
    per = seq // tm
    assert t % tm == 0 and seq % tm == 0
    n_in = w_in_b.shape[1]
    wa = A_HEADS * 2 * A_HEAD_DIM
    wkb = B_KV_HEADS * B_HEAD_DIM

    def row(i):
        return (i // per) if not is_ctx else 0

    tok = lambda w: pl.BlockSpec((tm, w), lambda i: (i, 0))
    out_shape = [
        jax.ShapeDtypeStruct((t, wa), BF16), jax.ShapeDtypeStruct((t, wa), BF16),
        jax.ShapeDtypeStruct((t, wa), BF16), jax.ShapeDtypeStruct((t, wa), BF16),
        jax.ShapeDtypeStruct((t, 2 * wkb), BF16), jax.ShapeDtypeStruct((t, wkb), BF16),
        jax.ShapeDtypeStruct((t, d), BF16), jax.ShapeDtypeStruct((t, d), BF16),
    ]
    out_specs = [tok(wa), tok(wa), tok(wa), tok(wa), tok(2 * wkb), tok(wkb), tok(d), tok(d)]
    if not is_ctx:
        out_shape[5] = jax.ShapeDtypeStruct((t, 2 * wkb), BF16)
        out_specs[5] = tok(2 * wkb)
        out_shape[1] = jax.ShapeDtypeStruct((t // seq, wa, seq), BF16)
        out_specs[1] = pl.BlockSpec((1, wa, tm), lambda i: (i // per, 0, i % per))
    if is_ctx:
        out_shape += [jax.ShapeDtypeStruct((t, wa), F32), jax.ShapeDtypeStruct((t, wa), F32),
                      jax.ShapeDtypeStruct((t, wkb), F32), jax.ShapeDtypeStruct((t, wkb), F32)]
        out_specs += [tok(wa), tok(wa), tok(wkb), tok(wkb)]
    return pl.pallas_call(
        functools.partial(_inproj_kernel, is_ctx=is_ctx),
        grid=(t // tm,),
        in_specs=[
            pl.BlockSpec((tm, d), lambda i: (i, 0)),
            pl.BlockSpec((1, 1, d), lambda i: (mod_row0 + row(i), 0, 0)),
            pl.BlockSpec((1, 1, d), lambda i: (mod_row0 + row(i), 0, 1)),
            pl.BlockSpec((1, d), lambda i: (0, 0)),
            pl.BlockSpec((tm, LANES), lambda i: (i % per, 0)),
            pl.BlockSpec((tm, LANES), lambda i: (i % per, 0)),
            pl.BlockSpec((d, n_in), lambda i: (0, 0)),
        ],
        out_specs=out_specs,
        out_shape=out_shape,
        compiler_params=pltpu.CompilerParams(dimension_semantics=("arbitrary",)),
        name="inproj_ctx" if is_ctx else "inproj_lat",
    )(x2, mod3, mod3, g_pre, cos_t, sin_t, w_in_b)


def _nt(a, b):
    return lax.dot_general(a, b, (((1,), (1,)), ((), ())), preferred_element_type=F32)


def _diff_kernel(lam_ref, g_ref, q_ref, k_ref, v_ref, o_ref, *, lambda_init):
    lp = lam_ref[...]
    lam = (jnp.exp(jnp.sum(lp[0:1] * lp[1:2], axis=-1, keepdims=True))
           - jnp.exp(jnp.sum(lp[2:3] * lp[3:4], axis=-1, keepdims=True)) + lambda_init)
    tq = q_ref.shape[1]
    lane = lax.broadcasted_iota(jnp.int32, (1, LANES), 1)
    for h in range(A_HEADS):
        cols = slice(h * LANES, (h + 1) * LANES)
        q = q_ref[0, :, cols]
        q2 = jnp.concatenate([q * (lane < 64).astype(BF16), q * (lane >= 64).astype(BF16)], axis=0)
        s = _nt(q2, k_ref[0, :, cols])
        mx = jnp.max(s, axis=-1, keepdims=True)
        v = v_ref[0, :, cols]
        v_ext = jnp.concatenate([v, jnp.ones_like(v)], axis=1)
        acc = jnp.dot(jnp.exp(s - mx).astype(BF16), v_ext, preferred_element_type=F32)
        on = acc[:, 0:LANES] / acc[:, LANES:2 * LANES]
        o = on[0:tq] - lam * on[tq:2 * tq]
        o_ref[0, :, cols] = (_rms(o, g_ref[...]) * (1.0 - lambda_init)).astype(BF16)


def _diff_lat_kernel(lam_ref, g_ref, q_ref, kc_ref, kt_ref, vc_ref, v_ref, o_ref, s_a, m_a, s_b, m_b, *,
                     lambda_init):
    t = pl.program_id(0)
    tq = q_ref.shape[1]
    nkc = kc_ref.shape[1]
    nkn = kt_ref.shape[2]
    nk = nkc + nkn

    @pl.when(t == 0)
    def _():
        s_b[...] = jnp.zeros_like(s_b)
        m_b[...] = jnp.zeros_like(m_b)

    def body(s_w, m_w, s_r, m_r):
        lp = lam_ref[...]
        lam = (jnp.exp(jnp.sum(lp[0:1] * lp[1:2], axis=-1, keepdims=True))
               - jnp.exp(jnp.sum(lp[2:3] * lp[3:4], axis=-1, keepdims=True)) + lambda_init)
        lane = lax.broadcasted_iota(jnp.int32, (1, LANES), 1)

        q = q_ref[0]
        q2 = jnp.concatenate([q * (lane < 64).astype(BF16), q * (lane >= 64).astype(BF16)], axis=0)
        sc = _nt(q2, kc_ref[0].astype(BF16))
        sn = jnp.dot(q2, kt_ref[0], preferred_element_type=F32)
        mx = jnp.maximum(jnp.max(sc, axis=-1, keepdims=True), jnp.max(sn, axis=-1, keepdims=True))
        s_w[:, 0:nkc] = sc
        s_w[:, nkc:nk] = sn
        m_w[...] = jnp.broadcast_to(mx, (2 * tq, LANES))

        mp = m_r[...]
        v_all = jnp.concatenate([vc_ref[0].astype(BF16), v_ref[0]], axis=0)
        v_ext = jnp.concatenate([v_all, jnp.ones_like(v_all)], axis=1)
        p = jnp.concatenate(
            [jnp.exp(s_r[:, c:c + LANES] - mp).astype(BF16) for c in range(0, nk, LANES)], axis=1)
        acc = jnp.dot(p, v_ext, preferred_element_type=F32)
        on = acc[:, 0:LANES] / acc[:, LANES:2 * LANES]
        o = on[0:tq] - lam * on[tq:2 * tq]
        o_ref[0] = (_rms(o, g_ref[...]) * (1.0 - lambda_init)).astype(BF16)

    @pl.when(t % 2 == 0)
    def _():
        body(s_a, m_a, s_b, m_b)

    @pl.when(t % 2 == 1)
    def _():
        body(s_b, m_b, s_a, m_a)


def _diff_attention_lat(lam_p, g_head, q, kc, kt, vc, v, *, tq, lambda_init):
    b, s, w = q.shape
    past = kc.shape[1]
    nq = s // tq
    n_units = b * A_HEADS * nq
    last = n_units - 1

    def unit(u):
        return u // (A_HEADS * nq), (u // nq) % A_HEADS, u % nq

    def cur(t):
        return unit(jnp.minimum(t, last))

    def prev(t):
        return unit(jnp.maximum(t - 1, 0))

    return pl.pallas_call(
        functools.partial(_diff_lat_kernel, lambda_init=lambda_init),
        grid=(n_units + 1,),
        in_specs=[
            pl.BlockSpec((4, A_HEAD_DIM), lambda t: (0, 0)),
            pl.BlockSpec((1, A_V_DIM), lambda t: (0, 0)),
            pl.BlockSpec((1, tq, LANES), lambda t: (cur(t)[0], cur(t)[2], cur(t)[1])),
            pl.BlockSpec((1, past, LANES), lambda t: (cur(t)[0], 0, cur(t)[1])),
            pl.BlockSpec((1, LANES, s), lambda t: (cur(t)[0], cur(t)[1], 0)),
            pl.BlockSpec((1, past, LANES), lambda t: (prev(t)[0], 0, prev(t)[1])),
            pl.BlockSpec((1, s, LANES), lambda t: (prev(t)[0], 0, prev(t)[1])),
        ],
        out_specs=pl.BlockSpec((1, tq, LANES), lambda t: (prev(t)[0], prev(t)[2], prev(t)[1])),
        out_shape=jax.ShapeDtypeStruct((b, s, w), BF16),
        scratch_shapes=[pltpu.VMEM((2 * tq, past + s), F32), pltpu.VMEM((2 * tq, LANES), F32),
                        pltpu.VMEM((2 * tq, past + s), F32), pltpu.VMEM((2 * tq, LANES), F32)],
        compiler_params=pltpu.CompilerParams(dimension_semantics=("arbitrary",),
                                             vmem_limit_bytes=56 * 1024 * 1024),
        name="diff_attn_lat",
    )(lam_p, g_head, q, kc, kt, vc, v)


def _swa_kernel(sink_ref, q_ref, k_ref, v_ref, o_ref):
    tq = q_ref.shape[1]
    rows = B_GROUP * tq
    lane = lax.broadcasted_iota(jnp.int32, (1, LANES), 1)
    low = lane < 64
    lane2 = lax.broadcasted_iota(jnp.int32, (1, 2 * LANES), 1)
    head_masks = [((lane2 // 64) == g).astype(BF16) for g in range(B_GROUP)]
    gw = B_GROUP * B_HEAD_DIM
    v = v_ref[0]
    v_ext = jnp.concatenate([v, jnp.ones_like(v)], axis=1)
    finishers = []
    for n in range(B_KV_HEADS):
        q = q_ref[0, :, n * gw:(n + 1) * gw]
        qs = jnp.concatenate([q * hm for hm in head_masks], axis=0)
        k2 = k_ref[0, :, n * LANES:(n + 1) * LANES]
        s = _nt(qs, jnp.concatenate([k2, k2], axis=1))
        sink = jnp.concatenate(
            [jnp.full((tq, 1), sink_ref[n * B_GROUP + g], F32) for g in range(B_GROUP)], axis=0)
        top = jnp.max(s, keepdims=True)
        for g in range(B_GROUP):
            top = jnp.maximum(top, sink_ref[n * B_GROUP + g])

        def finish(mp, n=n, s=s, sink=sink):
            acc = jnp.dot(jnp.exp(s - mp).astype(BF16), v_ext, preferred_element_type=F32)
            den = acc[:, LANES:2 * LANES] + jnp.exp(sink - mp)
            o = acc[:, 0:LANES] / den
            osw = pltpu.roll(o, 64, 1)
            for j in range(B_GROUP // 2):
                ra = slice((2 * j) * tq, (2 * j + 1) * tq)
                rb = slice((2 * j + 1) * tq, (2 * j + 2) * tq)
                pair = jnp.where(low, o[ra], osw[rb]) if n == 0 else jnp.where(low, osw[ra], o[rb])
                o_ref[0, :, n * gw + j * LANES:n * gw + (j + 1) * LANES] = pair.astype(BF16)
            return jnp.min(den)

        finishers.append((finish, s, sink, finish(top)))

    smallest = functools.reduce(jnp.minimum, [f[3] for f in finishers])

    @pl.when(jnp.logical_not(smallest >= 1e-30))
    def _():
        for finish, s, sink, _ in finishers:
            finish(jnp.maximum(jnp.max(s, axis=-1, keepdims=True), sink))


def _ctx_attn_kernel(lam_ref, g_ref, sink_ref, qa_ref, ka_ref, va_ref, qb_ref, kb_ref, vb_ref, oa_ref, ob_ref, *,
                     lambda_init):
    _diff_kernel(lam_ref, g_ref, qa_ref, ka_ref, va_ref, oa_ref, lambda_init=lambda_init)
    _swa_kernel(sink_ref, qb_ref, kb_ref, vb_ref, ob_ref)


def _ctx_attention(lam_p, g_head, sink, qa, ka, va, qb, kb2, vb, *, lambda_init):
    b, s, _ = qa.shape
    seq = lambda a: pl.BlockSpec((1, s, a.shape[2]), lambda bi: (bi, 0, 0))
    return pl.pallas_call(
        functools.partial(_ctx_attn_kernel, lambda_init=lambda_init),
        grid=(b,),
        in_specs=[pl.BlockSpec((4, A_HEAD_DIM), lambda bi: (0, 0)), pl.BlockSpec((1, A_V_DIM), lambda bi: (0, 0)),
                  pl.BlockSpec(memory_space=pltpu.SMEM), seq(qa), seq(ka), seq(va), seq(qb), seq(kb2), seq(vb)],
        out_specs=[seq(qa), seq(qb)],
        out_shape=[jax.ShapeDtypeStruct(qa.shape, BF16), jax.ShapeDtypeStruct(qb.shape, BF16)],
        compiler_params=pltpu.CompilerParams(dimension_semantics=("arbitrary",)),
        name="attn_ctx",
    )(lam_p, g_head, sink, qa, ka, va, qb, kb2, vb)


def _swa_lat_kernel(sink_ref, q_ref, kc_ref, kl_ref, km_ref, kr_ref, vc_ref, vl_ref, vm_ref, vr_ref, o_ref,
                    s_a, m_a, s_b, m_b, *, nqb):
    t = pl.program_id(0)
    n_units = pl.num_programs(0) - 1
    tq = q_ref.shape[1]
    gw = B_GROUP * B_HEAD_DIM
    i_cur = jnp.minimum(t, n_units - 1) % nqb
    nkc = kc_ref.shape[1]
    rows = B_GROUP * tq

    @pl.when(t == 0)
    def _():
        s_b[...] = jnp.zeros_like(s_b)
        m_b[...] = jnp.zeros_like(m_b)

    def sink_col(n):
        return jnp.concatenate(
            [jnp.full((tq, 1), sink_ref[n * B_GROUP + g], F32) for g in range(B_GROUP)], axis=0)

    def body(s_w, m_w, s_r, m_r):
        lane = lax.broadcasted_iota(jnp.int32, (1, LANES), 1)
        low = lane < 64
        lane2 = lax.broadcasted_iota(jnp.int32, (1, 2 * LANES), 1)
        head_masks = [((lane2 // 64) == g).astype(BF16) for g in range(B_GROUP)]
        qi = lax.broadcasted_iota(jnp.int32, (rows, SWA_Q), 0) & (tq - 1)
        kj = lax.broadcasted_iota(jnp.int32, (rows, SWA_Q), 1)
        far = 2 * SWA_Q
        left_ok = kj >= qi + jnp.where(i_cur > 0, 0, far)
        right_ok = kj <= qi - jnp.where(i_cur < nqb - 1, 0, far)

        kc = kc_ref[0]
        kc_sw = pltpu.roll(kc, 64, 1)
        for n in range(B_KV_HEADS):
            q = q_ref[0, :, n * gw:(n + 1) * gw]
            qs = jnp.concatenate([q * hm for hm in head_masks], axis=0)
            kc2 = (jnp.where(low, kc, kc_sw) if n == 0 else jnp.where(low, kc_sw, kc)).astype(BF16)
            ks = [kc2] + [r[0, :, n * LANES:(n + 1) * LANES] for r in (kl_ref, km_ref, kr_ref)]
            k_all = jnp.concatenate([jnp.concatenate([k, k], axis=1) for k in ks], axis=0)
            s = _nt(qs, k_all)
            chunks = [s[:, c:c + LANES] for c in range(0, s.shape[1], LANES)]
            il = nkc // LANES
            chunks[il] = jnp.where(left_ok, chunks[il], NEG_INF)
            chunks[il + 2] = jnp.where(right_ok, chunks[il + 2], NEG_INF)
            top = jnp.max(functools.reduce(jnp.maximum, chunks), keepdims=True)
            for g in range(B_GROUP):
                top = jnp.maximum(top, sink_ref[n * B_GROUP + g])
            for c, ch in enumerate(chunks):
                s_w[n * rows:(n + 1) * rows, c * LANES:(c + 1) * LANES] = ch
            m_w[n * rows:(n + 1) * rows, :] = jnp.broadcast_to(top, (rows, LANES))

        vc = vc_ref[0]
        vc_sw = pltpu.roll(vc, 64, 1)
        nk = nkc + 3 * tq
        finishers = []
        for n in range(B_KV_HEADS):
            vc2 =(jnp.where(low, vc, vc_sw) if n == 0 else jnp.where(low, vc_sw, vc)).astype(BF16)
            v_all = jnp.concatenate(
                [vc2] + [r[0, :, n * LANES:(n + 1) * LANES] for r in (vl_ref, vm_ref, vr_ref)], axis=0)
            v_ext = jnp.concatenate([v_all, jnp.ones_like(v_all)], axis=1)
            sink = sink_col(n)

            def finish(mp, n=n, v_ext=v_ext, sink=sink):
                p = jnp.concatenate([jnp.exp(s_r[n * rows:(n + 1) * rows, c:c + LANES] - mp).astype(BF16)
                                     for c in range(0, nk, LANES)], axis=1)
                acc = jnp.dot(p, v_ext, preferred_element_type=F32)
                den = acc[:, LANES:2 * LANES] + jnp.exp(sink - mp)
                o = acc[:, 0:LANES] / den
                for j in range(B_GROUP // 2):
                    pair = jnp.where(low, o[(2 * j) * tq:(2 * j + 1) * tq], o[(2 * j + 1) * tq:(2 * j + 2) * tq])
                    o_ref[0, :, n * gw + j * LANES:n * gw + (j + 1) * LANES] = pair.astype(BF16)
                return jnp.min(den)

            finishers.append((n, finish, sink, finish(m_r[n * rows:(n + 1) * rows, :])))

        smallest = functools.reduce(jnp.minimum, [f[3] for f in finishers])

        @pl.when(jnp.logical_not(smallest >= 1e-30))
        def _():
            for n, finish, sink, _ in finishers:
                row_max = functools.reduce(
                    jnp.maximum, [s_r[n * rows:(n + 1) * rows, c:c + LANES] for c in range(0, nk, LANES)])
                mx = jnp.maximum(jnp.max(row_max, axis=-1, keepdims=True), sink)
                finish(jnp.broadcast_to(mx, (rows, LANES)))

    @pl.when(t % 2 == 0)
    def _():
        body(s_a, m_a, s_b, m_b)

    @pl.when(t % 2 == 1)
    def _():
        body(s_b, m_b, s_a, m_a)


def _swa_attention_lat(sink, q, kc, k2, vc, v):
    b, s, w = q.shape
    past = kc.shape[1]
    tq = SWA_Q
    nqb = s // tq
    n_units = b * nqb
    last = n_units - 1
    nk = past + 3 * tq

    def cur(t):
        u = jnp.minimum(t, last)
        return u // nqb, u % nqb

    def prev(t):
        u = jnp.maximum(t - 1, 0)
        return u // nqb, u % nqb

    lo = lambda i: jnp.maximum(i - 1, 0)
    hi = lambda i: jnp.minimum(i + 1, nqb - 1)
    kspec = lambda f: pl.BlockSpec((1, tq, 2 * LANES), lambda t: (cur(t)[0], f(cur(t)[1]), 0))
    vspec = lambda f: pl.BlockSpec((1, tq, 2 * LANES), lambda t: (prev(t)[0], f(prev(t)[1]), 0))
    same = lambda i: i
    rows = B_KV_HEADS * B_GROUP * tq
    return pl.pallas_call(
        functools.partial(_swa_lat_kernel, nqb=nqb),
        grid=(n_units + 1,),
        in_specs=[
            pl.BlockSpec(memory_space=pltpu.SMEM),
            pl.BlockSpec((1, tq, w), lambda t: (cur(t)[0], cur(t)[1], 0)),
            pl.BlockSpec((1, past, LANES), lambda t: (cur(t)[0], 0, 0)),
            kspec(lo), kspec(same), kspec(hi),
            pl.BlockSpec((1, past, LANES), lambda t: (prev(t)[0], 0, 0)),
            vspec(lo), vspec(same), vspec(hi),
        ],
        out_specs=pl.BlockSpec((1, tq, w), lambda t: (prev(t)[0], prev(t)[1], 0)),
        out_shape=jax.ShapeDtypeStruct((b, s, w), BF16),
        scratch_shapes=[pltpu.VMEM((rows, nk), F32), pltpu.VMEM((rows, LANES), F32),
                        pltpu.VMEM((rows, nk), F32), pltpu.VMEM((rows, LANES), F32)],
        compiler_params=pltpu.CompilerParams(dimension_semantics=("arbitrary",)),
        name="swa_attn_lat",
    )(sink, q, kc, k2, k2, k2, vc, v, v, v)


def _postmix_kernel(*refs, n_ctx_tiles):
    (xc, xl, oac, oal, obc, obl, sgac, sgal, sgbc, sgbl, g1_ref, sh2_ref, sc2_ref, gpm_ref, gpf_ref,
     wpa_ref, wpb_ref, wo_ref, wr_ref, br_ref, x1_o, h2_o, route_o, cnt_o, x1_a, x1_b) = refs
    t = pl.program_id(0)
    is_ctx = t < n_ctx_tiles
    pick = lambda a, b: jnp.where(is_ctx, a[...], b[...])

    @pl.when(t == 0)
    def _():
        x1_b[...] = jnp.zeros_like(x1_b)

    def body(x1_w, x1_r):
        pa = jnp.dot(pick(oac, oal), wpa_ref[...], preferred_element_type=F32)
        pb = jnp.dot(pick(obc, obl), wpb_ref[...], preferred_element_type=F32)
        mix = pick(sgac, sgal).astype(F32) * pa + pick(sgbc, sgbl).astype(F32) * pb
        m2 = jnp.dot(mix.astype(BF16), wo_ref[...], preferred_element_type=F32)
        x1 = pick(xc, xl) + g1_ref[0] * _rms(m2, gpm_ref[...])
        x1_o[...] = x1
        x1_w[...] = x1
        _postmix_route(x1_r[...], sh2_ref, sc2_ref, gpf_ref, wr_ref, br_ref, h2_o, route_o, cnt_o)

    @pl.when(t % 2 == 0)
    def _():
        body(x1_a, x1_b)

    @pl.when(t % 2 == 1)
    def _():
        body(x1_b, x1_a)


def _postmix_route(x1, sh2_ref, sc2_ref, gpf_ref, wr_ref, br_ref, h2_o, route_o, cnt_o):
    h2 = _rms(x1, gpf_ref[...]) * (1.0 + sc2_ref[0]) + sh2_ref[0]
    h2_o[...] = h2.astype(BF16)

    h_hi = h2.astype(BF16)
    h_lo = (h2 - h_hi.astype(F32)).astype(BF16)
    both = jnp.dot(h_hi, wr_ref[...], preferred_element_type=F32)
    logits = (both[:, 0:LANES] + both[:, LANES:2 * LANES]
              + jnp.dot(h_lo, wr_ref[:, 0:LANES], preferred_element_type=F32) + br_ref[...])
    tm = logits.shape[0]
    lt = logits.T
    row = lax.broadcasted_iota(jnp.int32, (EXPERTS_PER_GROUP, tm), 0).astype(F32)
    none = float(EXPERTS_PER_GROUP)
    lg = jnp.where(row < N_GROUPS, lt[N_EXPERTS:N_EXPERTS + EXPERTS_PER_GROUP], -jnp.inf)
    mg = jnp.max(lg, axis=0, keepdims=True)
    g_sel = jnp.min(jnp.where(lg == mg, row, none), axis=0, keepdims=True)
    g_w = 1.0 / jnp.sum(jnp.exp(lg - mg), axis=0, keepdims=True)
    le = lt[0:EXPERTS_PER_GROUP]
    for g in range(1, N_GROUPS):
        le = jnp.where(g_sel == g, lt[g * EXPERTS_PER_GROUP:(g + 1) * EXPERTS_PER_GROUP], le)
    v0 = jnp.max(le, axis=0, keepdims=True)
    i0 = jnp.min(jnp.where(le == v0, row, none), axis=0, keepdims=True)
    le1 = jnp.where(row == i0, -jnp.inf, le)
    v1 = jnp.max(le1, axis=0, keepdims=True)
    i1 = jnp.min(jnp.where(le1 == v1, row, none), axis=0, keepdims=True)
    e = jnp.exp(v1 - v0)
    w0 = g_w / (1.0 + e)
    w1 = g_w * e / (1.0 + e)
    e0 = g_sel * EXPERTS_PER_GROUP + i0
    e1 = g_sel * EXPERTS_PER_GROUP + i1
    route_o[...] = jnp.where(row == 0, e0, jnp.where(row == 1, e1, jnp.where(row == 2, w0,
                                                                             jnp.where(row == 3, w1, 0.0))))
    erow = lax.broadcasted_iota(jnp.int32, (N_EXPERTS, tm), 0).astype(F32)
    cnt = jnp.sum((erow == e0).astype(F32) + (erow == e1).astype(F32), axis=1, keepdims=True)
    cnt_o[0] = jnp.broadcast_to(cnt, (N_EXPERTS, LANES))


def _postmix(ctx_in, lat_in, mod3, gpm, gpf, wpa, wpb, wo, wr, br, *, lat_seq, tm):
    t_ctx, d = ctx_in[0].shape
    t_lat = lat_in[0].shape[0]
    assert t_ctx % tm == 0 and lat_seq % tm == 0
    nc = t_ctx // tm
    nl = t_lat // tm
    per = lat_seq // tm
    t_all = t_ctx + t_lat

    last = nc + nl - 1
    cur = lambda i: jnp.minimum(i, last)
    prev = lambda i: jnp.maximum(i - 1, 0)
    mod_row = lambda j: jnp.where(j < nc, 0, 1 + jnp.maximum(j - nc, 0) // per)
    full = lambda a: pl.BlockSpec(a.shape, lambda i: (0,) * a.ndim)
    in_specs, args = [], []
    for a_c, a_l in zip(ctx_in, lat_in):
        w = a_c.shape[1]
        in_specs += [pl.BlockSpec((tm, w), lambda i: (jnp.minimum(cur(i), nc - 1), 0)),
                     pl.BlockSpec((tm, w), lambda i: (jnp.maximum(cur(i) - nc, 0), 0))]
        args += [a_c, a_l]
    in_specs += [pl.BlockSpec((1, 1, d), lambda i: (mod_row(cur(i)), 0, 2)),
                 pl.BlockSpec((1, 1, d), lambda i: (mod_row(prev(i)), 0, 3)),
                 pl.BlockSpec((1, 1, d), lambda i: (mod_row(prev(i)), 0, 4)),
                 full(gpm), full(gpf), full(wpa), full(wpb), full(wo), full(wr), full(br)]
    args += [mod3, mod3, mod3, gpm, gpf, wpa, wpb, wo, wr, br]
    return pl.pallas_call(
        functools.partial(_postmix_kernel, n_ctx_tiles=nc),
        grid=(nc + nl + 1,),
        in_specs=in_specs,
        out_specs=[pl.BlockSpec((tm, d), lambda i: (cur(i), 0)),
                   pl.BlockSpec((tm, d), lambda i: (prev(i), 0)),
                   pl.BlockSpec((EXPERTS_PER_GROUP, tm), lambda i: (0, prev(i))),
                   pl.BlockSpec((1, N_EXPERTS, LANES), lambda i: (prev(i), 0, 0))],
        out_shape=[jax.ShapeDtypeStruct((t_all, d), F32),
                   jax.ShapeDtypeStruct((t_all, d), BF16),
                   jax.ShapeDtypeStruct((EXPERTS_PER_GROUP, t_all), F32),
                   jax.ShapeDtypeStruct((nc + nl, N_EXPERTS, LANES), F32)],
        scratch_shapes=[pltpu.VMEM((tm, d), F32), pltpu.VMEM((tm, d), F32)],
        compiler_params=pltpu.CompilerParams(dimension_semantics=("arbitrary",)),
        name="postmix",
    )(*args)


def _segment_copies(src, src_row, dst, dst_row, n, sub, sem):
    @pl.when(n > 0)
    def _():
        pltpu.make_async_copy(src.at[pl.ds(pl.multiple_of(src_row * sub, sub), n * sub)],
                              dst.at[pl.ds(pl.multiple_of(dst_row * sub, sub), n * sub)], sem).start()


def _local_positions(route_t, tile_base):
    tm = route_t.shape[1]
    erow = lax.broadcasted_iota(jnp.int32, (N_EXPERTS, tm), 0).astype(F32)
    is0 = erow == route_t[0:1]
    is1 = erow == route_t[1:2]
    earlier = (lax.broadcasted_iota(jnp.int32, (tm, tm), 0)
               < lax.broadcasted_iota(jnp.int32, (tm, tm), 1)).astype(BF16)
    pre0 = jnp.dot(is0.astype(BF16), earlier, preferred_element_type=F32)
    pre1 = jnp.dot(is1.astype(BF16), earlier, preferred_element_type=F32)
    cnt0 = jnp.sum(is0.astype(F32), axis=1, keepdims=True)
    base = tile_base[:, 0:1]
    lpos0 = jnp.sum(jnp.where(is0, base + pre0, 0.0), axis=0, keepdims=True)
    lpos1 = jnp.sum(jnp.where(is1, base + cnt0 + pre1, 0.0), axis=0, keepdims=True)
    return lpos0, lpos1


def _dispatch_kernel(ss_ref, sl_ref, tb_ref, h_ref, r_ref, tbv_ref, xs_hbm, pbuf, zbuf, sem, zsem, *, n_asg, rows):
    i = pl.program_id(0)
    nt = pl.num_programs(0)
    tm, d = h_ref.shape
    sub = d // LANES
    nrow = TOP_K * tm
    slot = i % 2

    def wait_slot(s):
        pltpu.make_async_copy(pbuf.at[pl.ds(pl.multiple_of(s * nrow * sub, nrow * sub), nrow * sub)],
                              xs_hbm.at[pl.ds(0, nrow * sub)], sem.at[s]).wait()

    def slack_copy():
        return pltpu.make_async_copy(zbuf, xs_hbm.at[pl.ds(n_asg * sub, rows * sub)], zsem.at[0])

    @pl.when(i == 0)
    def _():
        zbuf[...] = jnp.zeros_like(zbuf)
        slack_copy().start()

    lpos0, lpos1 = _local_positions(r_ref[...], tbv_ref[0])
    p = lax.broadcasted_iota(jnp.int32, (nrow, tm), 0).astype(F32)
    sel = ((p == lpos0) | (p == lpos1)).astype(BF16)
    xp = jnp.dot(sel, h_ref[...], preferred_element_type=F32)

    @pl.when(i >= 2)
    def _():
        wait_slot(slot)

    _store_row_tiles(pbuf, slot * nrow * sub, xp)

    def seg(e, c):
        k = i * N_EXPERTS + e
        _segment_copies(pbuf, slot * nrow + tb_ref[k], xs_hbm, ss_ref[k], sl_ref[k], sub, sem.at[slot])
        return c
    lax.fori_loop(0, N_EXPERTS, seg, 0)

    @pl.when(i == nt - 1)
    def _():
        wait_slot(slot)

        @pl.when(nt >= 2)
        def _():
            wait_slot(1 - slot)
        slack_copy().wait()


def _dispatch(h2, route, tables, *, tm, rows):
    seg_start, seg_len, tile_base, tile_base_v = tables
    t, d = h2.shape
    sub = d // LANES
    n_asg = t * TOP_K
    grid_spec = pltpu.PrefetchScalarGridSpec(
        num_scalar_prefetch=3,
        grid=(t // tm,),
        in_specs=[
            pl.BlockSpec((tm, d), lambda i, *_: (i, 0)),
            pl.BlockSpec((EXPERTS_PER_GROUP, tm), lambda i, *_: (0, i)),
            pl.BlockSpec((1, N_EXPERTS, LANES), lambda i, *_: (i, 0, 0)),
        ],
        out_specs=pl.BlockSpec(memory_space=pl.ANY),
        scratch_shapes=[pltpu.VMEM((2 * TOP_K * tm * sub, LANES), F32), pltpu.VMEM((rows * sub, LANES), F32),
                        pltpu.SemaphoreType.DMA((2,)), pltpu.SemaphoreType.DMA((1,))],
    )
    return pl.pallas_call(
        functools.partial(_dispatch_kernel, n_asg=n_asg, rows=rows),
        grid_spec=grid_spec,
        out_shape=jax.ShapeDtypeStruct(((n_asg + rows) * sub, LANES), F32),
        compiler_params=pltpu.CompilerParams(dimension_semantics=("arbitrary",)),
        name="dispatch",
    )(seg_start, seg_len, tile_base, h2, route, tile_base_v)


def _moe_kernel(be_ref, row0_ref, nact_ref, par_ref, nxt_ref, xs_hbm, w1_hbm, w3_hbm, w2_hbm, ys_hbm,
                xbuf, obuf, wf1, wf3, wf2, w1b, w3b, w2b, rsem, wsem, gsem, *, rows, sub):
    i = pl.program_id(0)
    nact = nact_ref[0]
    slot = i % 2
    nslot = 1 - slot
    blk = rows * sub

    def weight_copies(e, s):
        return [pltpu.make_async_copy(w_hbm.at[e], wf.at[s], gsem.at[s])
                for w_hbm, wf in ((w1_hbm, wf1), (w3_hbm, wf3), (w2_hbm, wf2))]

    def read(j, s):
        return pltpu.make_async_copy(xs_hbm.at[pl.ds(pl.multiple_of(row0_ref[j] * sub, sub), blk)],
                                     xbuf.at[pl.ds(pl.multiple_of(s * blk, blk), blk)], rsem.at[s])

    def write(j, s):
        return pltpu.make_async_copy(obuf.at[pl.ds(pl.multiple_of(s * blk, blk), blk)],
                                     ys_hbm.at[pl.ds(pl.multiple_of(row0_ref[j] * sub, sub), blk)], wsem.at[s])

    @pl.when(i == 0)
    def _():
        read(0, 0).start()
        for c in weight_copies(be_ref[0], 0):
            c.start()

    @pl.when(i < nact)
    def _():
        @pl.when(i + 1 < nact)
        def _():
            read(i + 1, nslot).start()

        changed = jnp.logical_or(i == 0, be_ref[i] != be_ref[jnp.maximum(i - 1, 0)])

        @pl.when(changed)
        def _():
            s = par_ref[i]
            for c in weight_copies(be_ref[i], s):
                c.wait()
            w1b[...] = wf1[s].astype(BF16)
            w3b[...] = wf3[s].astype(BF16)
            w2b[...] = wf2[s].astype(BF16)

            @pl.when(nxt_ref[i] >= 0)
            def _():
                for c in weight_copies(nxt_ref[i], 1 - s):
                    c.start()

        read(i, slot).wait()
        x = _load_row_tiles(xbuf, slot * blk, rows, sub).astype(BF16)
        a = jnp.dot(x, w1b[...], preferred_element_type=F32)
        b = jnp.dot(x, w3b[...], preferred_element_type=F32)
        hmid = (a * jax.nn.sigmoid(a) * b).astype(BF16)
        y = jnp.dot(hmid, w2b[...], preferred_element_type=F32)
        _store_row_tiles(obuf, slot * blk, y)

        @pl.when(i >= 1)
        def _():
            write(i - 1, nslot).wait()
        write(i, slot).start()

    @pl.when(i == nact)
    def _():
        write(i - 1, nslot).wait()
        obuf[pl.ds(pl.multiple_of(slot * blk, blk), blk), :] = jnp.zeros((blk, LANES), F32)
        tail = pltpu.make_async_copy(obuf.at[pl.ds(pl.multiple_of(slot * blk, blk), blk)],
                                     ys_hbm.at[pl.ds(ys_hbm.shape[0] - blk, blk)], wsem.at[slot])
        tail.start()
        tail.wait()


def _moe(blk_tables, xs, w1, w3, w2):
    blk_expert, row0, nact, parity, nxt = blk_tables
    nblk = blk_expert.shape[0] - 1
    d, de = w1.shape[1], w1.shape[2]
    sub = d // LANES
    rows = MOE_ROWS
    anyspec = pl.BlockSpec(memory_space=pl.ANY)
    grid_spec = pltpu.PrefetchScalarGridSpec(
        num_scalar_prefetch=5,
        grid=(nblk + 1,),
        in_specs=[anyspec, anyspec, anyspec, anyspec],
        out_specs=anyspec,
        scratch_shapes=[
            pltpu.VMEM((2 * rows * sub, LANES), F32),
            pltpu.VMEM((2 * rows * sub, LANES), F32),
            pltpu.VMEM((2, d, de), F32),
            pltpu.VMEM((2, d, de), F32),
            pltpu.VMEM((2, de, d), F32),
            pltpu.VMEM((d, de), BF16),
            pltpu.VMEM((d, de), BF16),
            pltpu.VMEM((de, d), BF16),
            pltpu.SemaphoreType.DMA((2,)),
            pltpu.SemaphoreType.DMA((2,)),
            pltpu.SemaphoreType.DMA((2,)),
        ],
    )
    return pl.pallas_call(
        functools.partial(_moe_kernel, rows=rows, sub=sub),
        grid_spec=grid_spec,
        out_shape=jax.ShapeDtypeStruct(xs.shape, F32),
        compiler_params=pltpu.CompilerParams(dimension_semantics=("arbitrary",)),
        name="expert_mlp",
    )(blk_expert, row0, nact, parity, nxt, xs, w1, w3, w2)


def _combine_kernel(ss_ref, sl_ref, tb_ref, x1_ref, r_ref, tbv_ref, g2_ref, gpost_ref, ys_hbm, oc_ref, ol_ref,
                    ybuf, sem, *, n_ctx_tiles):
    tile0 = 0
    i = pl.program_id(0)
    nt = pl.num_programs(0)
    tm, d = x1_ref.shape
    sub = d // LANES
    nrow = TOP_K * tm
    slot = i % 2

    def fetch(tile, s):
        def seg(e, c):
            k = tile * N_EXPERTS + e
            _segment_copies(ys_hbm, ss_ref[k], ybuf, s * nrow + tb_ref[k], sl_ref[k], sub, sem.at[s])
            return c
        lax.fori_loop(0, N_EXPERTS, seg, 0)

    @pl.when(i == 0)
    def _():
        fetch(tile0, 0)

    @pl.when(i + 1 < nt)
    def _():
        fetch(tile0 + i + 1, 1 - slot)

    pltpu.make_async_copy(ys_hbm.at[pl.ds(0, nrow * sub)],
                          ybuf.at[pl.ds(pl.multiple_of(slot * nrow * sub, nrow * sub), nrow * sub)],
                          sem.at[slot]).wait()
    rt = r_ref[...]
    lpos0, lpos1 = _local_positions(rt, tbv_ref[0])
    rows8 = lax.broadcasted_iota(jnp.int32, (EXPERTS_PER_GROUP, tm), 0)
    cols = jnp.where(rows8 == 0, lpos0, jnp.where(rows8 == 1, lpos1, rt))
    cols = jnp.concatenate([cols, jnp.zeros((LANES - EXPERTS_PER_GROUP, tm), F32)], axis=0).T
    p = lax.broadcasted_iota(jnp.int32, (tm, nrow), 1).astype(F32)
    q = (jnp.where(p == cols[:, 0:1], cols[:, 2:3], 0.0)
         + jnp.where(p == cols[:, 1:2], cols[:, 3:4], 0.0)).astype(BF16)
    ysort = _load_row_tiles(ybuf, slot * nrow * sub, nrow, sub).astype(BF16)
    y = jnp.dot(q, ysort, preferred_element_type=F32)
    out = x1_ref[...] + g2_ref[0] * _rms(y, gpost_ref[...])

    @pl.when(i < n_ctx_tiles)
    def _():
        oc_ref[...] = out

    @pl.when(i >= n_ctx_tiles)
    def _():
        ol_ref[...] = out


def _combine(x1, ys, route, tables, mod3, gpost, *, t_ctx, lat_seq, tm):
    seg_start, seg_len, tile_base, tile_base_v = tables
    t_all, d = x1.shape
    assert t_ctx % tm == 0 and lat_seq % tm == 0
    nc = t_ctx // tm
    per = lat_seq // tm
    sub = d // LANES
    mod_row = lambda i: jnp.where(i < nc, 0, 1 + jnp.maximum(i - nc, 0) // per)
    grid_spec = pltpu.PrefetchScalarGridSpec(
        num_scalar_prefetch=3,
        grid=(t_all // tm,),
        in_specs=[
            pl.BlockSpec((tm, d), lambda i, *_: (i, 0)),
            pl.BlockSpec((EXPERTS_PER_GROUP, tm), lambda i, *_: (0, i)),
            pl.BlockSpec((1, N_EXPERTS, LANES), lambda i, *_: (i, 0, 0)),
            pl.BlockSpec((1, 1, d), lambda i, *_: (mod_row(i), 0, 5)),
            pl.BlockSpec((1, d), lambda i, *_: (0, 0)),
            pl.BlockSpec(memory_space=pl.ANY),
        ],
        out_specs=[pl.BlockSpec((tm, d), lambda i, *_: (jnp.minimum(i, nc - 1), 0)),
                   pl.BlockSpec((tm, d), lambda i, *_: (jnp.maximum(i - nc, 0), 0))],
        scratch_shapes=[pltpu.VMEM((2 * TOP_K * tm * sub, LANES), F32), pltpu.SemaphoreType.DMA((2,))],
    )
    return pl.pallas_call(
        functools.partial(_combine_kernel, n_ctx_tiles=nc),
        grid_spec=grid_spec,
        out_shape=[jax.ShapeDtypeStruct((t_ctx, d), F32), jax.ShapeDtypeStruct((t_all - t_ctx, d), F32)],
        compiler_params=pltpu.CompilerParams(dimension_semantics=("arbitrary",)),
        name="combine",
    )(seg_start, seg_len, tile_base, x1, route, tile_base_v, mod3, gpost, ys)


def _routing_tables(counts, tm, rows):
    nt = counts.shape[0]
    n_asg = nt * tm * TOP_K
    ex = jnp.arange(N_EXPERTS, dtype=jnp.int32)
    cnt_te = counts[:, :, 0].astype(jnp.int32)
    cnt_e = jnp.sum(cnt_te, axis=0)
    start_e = jnp.cumsum(cnt_e) - cnt_e
    seg_start = start_e[None, :] + jnp.cumsum(cnt_te, axis=0) - cnt_te
    tile_base = jnp.cumsum(cnt_te, axis=1) - cnt_te
    tile_base_v = jnp.broadcast_to(tile_base.astype(F32)[:, :, None], (nt, N_EXPERTS, LANES))

    nblk_e = (cnt_e + rows - 1) // rows
    blk_end = jnp.cumsum(nblk_e)
    blk_start = blk_end - nblk_e
    n_blocks = n_asg // rows + N_EXPERTS
    b = jnp.arange(n_blocks + 1, dtype=jnp.int32)
    be = jnp.minimum(jnp.sum((blk_end[None, :] <= b[:, None]).astype(jnp.int32), axis=1), N_EXPERTS - 1)
    first = jnp.sum(jnp.where(be[:, None] == ex[None, :], (start_e - blk_start * rows)[None, :], 0), axis=1)
    row0 = jnp.clip(first + b * rows, 0, n_asg)
    nact = blk_end[-1:].astype(jnp.int32)
    used = cnt_e > 0
    parity_e = (jnp.cumsum(used.astype(jnp.int32)) - 1) % 2
    later = (ex[None, :] > ex[:, None]) & used[None, :]
    nxt_e = jnp.min(jnp.where(later, ex[None, :], N_EXPERTS), axis=1)
    nxt_e = jnp.where(nxt_e == N_EXPERTS, -1, nxt_e)
    pick = lambda tab: jnp.sum(jnp.where(be[:, None] == ex[None, :], tab[None, :], 0), axis=1).astype(jnp.int32)
    seg = (seg_start.reshape(-1).astype(jnp.int32), cnt_te.reshape(-1).astype(jnp.int32),
           tile_base.reshape(-1).astype(jnp.int32), tile_base_v)
    blk = (be.astype(jnp.int32), row0.astype(jnp.int32), nact, pick(parity_e), pick(nxt_e))
    return seg, blk


def _rope_tables(n_tok):
    n_rows = n_tok // GRID_W
    rows = jnp.repeat(jnp.arange(n_rows), GRID_W).astype(F32)
    cols = jnp.tile(jnp.arange(GRID_W), n_rows).astype(F32)
    quarter = A_HEAD_DIM // 4
    inv = ROPE_BASE ** (-jnp.arange(quarter, dtype=F32) / quarter)
    ang = jnp.concatenate([rows[:, None] * inv, cols[:, None] * inv], axis=-1)
    cos, sin = jnp.cos(ang), jnp.sin(ang)
    cos_t = jnp.tile(jnp.concatenate([cos, cos], axis=-1), (1, LANES // A_HEAD_DIM))
    sin_t = jnp.tile(jnp.concatenate([-sin, sin], axis=-1), (1, LANES // A_HEAD_DIM))
    return cos_t, sin_t


def kernel(x_prompt, x_sample, c, cache_diff_k, cache_diff_v, cache_swa_k, cache_swa_v, c_ctx, w_ada, b_ada, g_pre_mix, g_post_mix, g_pre_ffn, g_post_ffn, w_in, lam_q1, lam_k1, lam_q2, lam_k2, g_diff_head, sink, w_proj_a, w_proj_b, w_out, w_router_group, b_router_group, w_router_expert, b_router_expert, w_e1, w_e3, w_e2):
    depth = w_in.shape[0]
    assert depth == 1
    l = 0
    bp, sp, d = x_prompt.shape
    bs, ss, _ = x_sample.shape
    lambda_init = 0.8 - 0.6 * math.exp(-0.3 * l)
    assert A_HEAD_DIM == B_HEAD_DIM and ss % GRID_W == 0 and bs + 1 <= MOD_ROWS

    c_all = jnp.concatenate([c_ctx[None, :], c, jnp.zeros((MOD_ROWS - 1 - bs, d), F32)], axis=0)
    mod = _modulation(c_all, w_ada[l], b_ada[l][None, :])
    mod3 = mod.reshape(MOD_ROWS, 1, 6 * d)

    w_in_b = w_in[l].astype(BF16)
    wpa = w_proj_a[l].astype(BF16)
    wpb = w_proj_b[l].astype(BF16)
    wo = w_out[l].astype(BF16)
    n_r = N_GROUPS + N_EXPERTS
    wr = jnp.concatenate([w_router_expert[l], w_router_group[l], jnp.zeros((d, LANES - n_r), F32)], axis=1)
    br = jnp.concatenate([b_router_expert[l], b_router_group[l], jnp.zeros((LANES - n_r,), F32)])[None, :]
    wr_hi = wr.astype(BF16)
    wr2 = jnp.concatenate([wr_hi, (wr - wr_hi.astype(F32)).astype(BF16)], axis=1)
    lam_p = jnp.stack([lam_q1[l], lam_k1[l], lam_q2[l], lam_k2[l]], axis=0)
    g_head = g_diff_head[l][None, :]
    sink_l = sink[l]
    cos_t, sin_t = _rope_tables(ss)

    xp2 = x_prompt.reshape(bp * sp, d)
    xs2 = x_sample.reshape(bs * ss, d)
    gpre = g_pre_mix[l][None, :]

    (qa_c, ka_c, va_c, qb_c, kb2_c, vb_c, sga_c, sgb_c, kaf, vaf, kbf, vbf) = _inproj(
        xp2, mod3, 0, gpre, cos_t, sin_t, w_in_b, seq=sp, tm=sp, is_ctx=True)
    r3 = lambda a, b_: a.reshape(b_, -1, a.shape[-1])
    oa_c, ob_c = _ctx_attention(lam_p, g_head, sink_l, r3(qa_c, bp), r3(ka_c, bp), r3(va_c, bp),
                                r3(qb_c, bp), r3(kb2_c, bp), r3(vb_c, bp), lambda_init=lambda_init)

    (qa_s, ka_s, va_s, qb_s, kb2_s, vb_s, sga_s, sgb_s) = _inproj(
        xs2, mod3, 1, gpre, cos_t, sin_t, w_in_b, seq=ss, tm=512, is_ctx=False)
    past = cache_diff_k.shape[2]
    ck = cache_diff_k[:, l].reshape(bs, past, -1)
    cv = cache_diff_v[:, l].reshape(bs, past, -1)
    oa_s = _diff_attention_lat(lam_p, g_head, r3(qa_s, bs), ck, ka_s, cv, r3(va_s, bs),
                               tq=512, lambda_init=lambda_init)
    sk = cache_swa_k[:, l].reshape(bs, past, -1)
    sv = cache_swa_v[:, l].reshape(bs, past, -1)
    kb2_3, vb_3 = r3(kb2_s, bs), r3(vb_s, bs)
    ob_s = _swa_attention_lat(sink_l, r3(qb_s, bs), sk, kb2_3, sv, vb_3)

    gpm = g_post_mix[l][None, :]
    gpf = g_pre_ffn[l][None, :]
    t_ctx, t_lat = bp * sp, bs * ss
    x1, h2t, route, counts = _postmix(
        (xp2, oa_c.reshape(t_ctx, -1), ob_c.reshape(t_ctx, -1), sga_c, sgb_c),
        (xs2, oa_s.reshape(t_lat, -1), ob_s.reshape(t_lat, -1), sga_s, sgb_s),
        mod3, gpm, gpf, wpa, wpb, wo, wr2, br, lat_seq=ss, tm=512)

    tables, blk_tables = _routing_tables(counts, MOE_TILE, MOE_ROWS)
    xs = _dispatch(h2t, route, tables, tm=MOE_TILE, rows=MOE_ROWS)
    ys = _moe(blk_tables, xs, w_e1[l], w_e3[l], w_e2[l])

    gpost = g_post_ffn[l][None, :]
    y_p, y_s = _combine(x1, ys, route, tables, mod3, gpost, t_ctx=t_ctx, lat_seq=ss, tm=MOE_TILE)

    ha = A_HEADS
    return (y_p.reshape(bp, sp, d), y_s.reshape(bs, ss, d),
            kaf.reshape(bp, 1, sp, ha, 2, A_HEAD_DIM), vaf.reshape(bp, 1, sp, ha, A_V_DIM),
            kbf.reshape(bp, 1, sp, B_KV_HEADS, B_HEAD_DIM), vbf.reshape(bp, 1, sp, B_KV_HEADS, B_HEAD_DIM))
```

```python
import functools
import math

import jax
import jax.numpy as jnp
from jax import lax
from jax.experimental import pallas as pl
from jax.experimental.pallas import tpu as pltpu

F32 = jnp.float32
BF16 = jnp.bfloat16

GRID_W = 64
ROPE_BASE = 10000.0
EPS = 1e-6
NEG_INF = -1e30
A_HEADS = 4
A_HEAD_DIM = 64
A_V_DIM = 2 * A_HEAD_DIM
B_HEADS = 8
B_KV_HEADS = 2
B_GROUP = B_HEADS // B_KV_HEADS
B_HEAD_DIM = 64
WINDOW = 128
N_GROUPS = 4
EXPERTS_PER_GROUP = 8
N_EXPERTS = N_GROUPS * EXPERTS_PER_GROUP
TOP_K = 2

LANES = 128
MOD_ROWS = 16
MOE_ROWS = 512
MOE_TILE = 512
SWA_Q = 128

_QA = 0
_KA = _QA + A_HEADS * 2 * A_HEAD_DIM
_VA = _KA + A_HEADS * 2 * A_HEAD_DIM
_QB = _VA + A_HEADS * A_V_DIM
_KB = _QB + B_HEADS * B_HEAD_DIM
_VB = _KB + B_KV_HEADS * B_HEAD_DIM
_GA = _VB + B_KV_HEADS * B_HEAD_DIM


def _rms(x, g):
    return x * lax.rsqrt(jnp.mean(x * x, axis=-1, keepdims=True) + EPS) * g


def _store_row_tiles(ref, base, val):
    sub = val.shape[1] // LANES
    for s in range(sub):
        ref[pl.ds(base + s, val.shape[0], stride=sub), :] = val[:, s * LANES:(s + 1) * LANES]


def _load_row_tiles(ref, base, n_rows, sub):
    return jnp.concatenate([ref[pl.ds(base + s, n_rows, stride=sub), :] for s in range(sub)], axis=1)


def _mod_kernel(c_ref, w_ref, b_ref, o_ref):
    c = c_ref[...]
    a = c * jax.nn.sigmoid(c)
    w = w_ref[...]
    a_hi = a.astype(BF16)
    a_lo = (a - a_hi.astype(F32)).astype(BF16)
    w_hi = w.astype(BF16)
    w_lo = (w - w_hi.astype(F32)).astype(BF16)
    o_ref[...] = (jnp.dot(a_hi, w_hi, preferred_element_type=F32) + jnp.dot(a_lo, w_hi, preferred_element_type=F32)
                  + jnp.dot(a_hi, w_lo, preferred_element_type=F32) + b_ref[...])


def _modulation(c_all, w_ada, b_ada):
    d, n = w_ada.shape
    tn = 512
    return pl.pallas_call(
        _mod_kernel,
        grid=(n // tn,),
        in_specs=[
            pl.BlockSpec((MOD_ROWS, d), lambda j: (0, 0)),
            pl.BlockSpec((d, tn), lambda j: (0, j)),
            pl.BlockSpec((1, tn), lambda j: (0, j)),
        ],
        out_specs=pl.BlockSpec((MOD_ROWS, tn), lambda j: (0, j)),
        out_shape=jax.ShapeDtypeStruct((MOD_ROWS, n), F32),
        name="modulation",
    )(c_all, w_ada, b_ada)


def _rope128(z, cos, sin_signed, first_half):
    rot = jnp.where(first_half, pltpu.roll(z, 96, 1), pltpu.roll(z, 32, 1))
    return z * cos + rot * sin_signed


def _inproj_kernel(x_ref, sh_ref, sc_ref, g_ref, cos_ref, sin_ref, w_ref, *outs, is_ctx):
    x = x_ref[...]
    h = _rms(x, g_ref[...]) * (1.0 + sc_ref[0]) + sh_ref[0]
    hb = h.astype(BF16)
    lane = lax.broadcasted_iota(jnp.int32, (1, LANES), 1)
    first_half = (lane % 64) < 32
    low = lane < 64

    def seg(lo, hi):
        return jnp.dot(hb, w_ref[:, lo:hi], preferred_element_type=F32)

    def rope(z):
        if is_ctx:
            return z
        cos = cos_ref[...]
        sin = sin_ref[...]
        parts = [_rope128(z[:, j:j + LANES], cos, sin, first_half) for j in range(0, z.shape[1], LANES)]
        return parts[0] if len(parts) == 1 else jnp.concatenate(parts, axis=1)

    if is_ctx:
        qa_o, ka_o, va_o, qb_o, kb2_o, vb_o, sga_o, sgb_o, kaf_o, vaf_o, kbf_o, vbf_o = outs
    else:
        qa_o, ka_o, va_o, qb_o, kb2_o, vb_o, sga_o, sgb_o = outs

    scale = A_HEAD_DIM ** -0.5
    qa_o[...] = (rope(seg(_QA, _KA)) * scale).astype(BF16)
    ka = rope(seg(_KA, _VA))
    if is_ctx:
        ka_o[...] = ka.astype(BF16)
    else:
        ka_o[0] = ka.T.astype(BF16)
    va = seg(_VA, _QB)
    va_o[...] = va.astype(BF16)
    qb_o[...] = (rope(seg(_QB, _KB)) * (B_HEAD_DIM ** -0.5)).astype(BF16)
    kb = rope(seg(_KB, _VB))
    kb_sw = pltpu.roll(kb, 64, 1)
    kb2_o[:, 0:LANES] = jnp.where(low, kb, kb_sw).astype(BF16)
    kb2_o[:, LANES:2 * LANES] = jnp.where(low, kb_sw, kb).astype(BF16)
    vb = seg(_VB, _GA)
    if is_ctx:
        vb_o[...] = vb.astype(BF16)
    else:
        vb_sw = pltpu.roll(vb, 64, 1)
        vb_o[:, 0:LANES] = jnp.where(low, vb, vb_sw).astype(BF16)
        vb_o[:, LANES:2 * LANES] = jnp.where(low, vb_sw, vb).astype(BF16)
    d = x.shape[1]
    sga_o[...] = jax.nn.sigmoid(seg(_GA, _GA + d)).astype(BF16)
    sgb_o[...] = jax.nn.sigmoid(seg(_GA + d, _GA + 2 * d)).astype(BF16)
    if is_ctx:
        kaf_o[...] = ka
        vaf_o[...] = va
        kbf_o[...] = kb
        vbf_o[...] = vb


def _inproj(x2, mod3, mod_row0, g_pre, cos_t, sin_t, w_in_b, *, seq, tm, is_ctx):
    t, d = x2.shape
    per = seq // tm
    assert t % tm == 0 and seq % tm == 0
    n_in = w_in_b.shape[1]
    wa = A_HEADS * 2 * A_HEAD_DIM
    wkb = B_KV_HEADS * B_HEAD_DIM

    def row(i):
        return (i // per) if not is_ctx else 0

    tok = lambda w: pl.BlockSpec((tm, w), lambda i: (i, 0))
    out_shape = [
        jax.ShapeDtypeStruct((t, wa), BF16), jax.ShapeDtypeStruct((t, wa), BF16),
        jax.ShapeDtypeStruct((t, wa), BF16), jax.ShapeDtypeStruct((t, wa), BF16),
        jax.ShapeDtypeStruct((t, 2 * wkb), BF16), jax.ShapeDtypeStruct((t, wkb), BF16),
        jax.ShapeDtypeStruct((t, d), BF16), jax.ShapeDtypeStruct((t, d), BF16),
    ]
    out_specs = [tok(wa), tok(wa), tok(wa), tok(wa), tok(2 * wkb), tok(wkb), tok(d), tok(d)]
    if not is_ctx:
        out_shape[5] = jax.ShapeDtypeStruct((t, 2 * wkb), BF16)
        out_specs[5] = tok(2 * wkb)
        out_shape[1] = jax.ShapeDtypeStruct((t // seq, wa, seq), BF16)
        out_specs[1] = pl.BlockSpec((1, wa, tm), lambda i: (i // per, 0, i % per))
    if is_ctx:
        out_shape += [jax.ShapeDtypeStruct((t, wa), F32), jax.ShapeDtypeStruct((t, wa), F32),
                      jax.ShapeDtypeStruct((t, wkb), F32), jax.ShapeDtypeStruct((t, wkb), F32)]
        out_specs += [tok(wa), tok(wa), tok(wkb), tok(wkb)]
    return pl.pallas_call(
        functools.partial(_inproj_kernel, is_ctx=is_ctx),
        grid=(t // tm,),
        in_specs=[
            pl.BlockSpec((tm, d), lambda i: (i, 0)),
            pl.BlockSpec((1, 1, d), lambda i: (mod_row0 + row(i), 0, 0)),
            pl.BlockSpec((1, 1, d), lambda i: (mod_row0 + row(i), 0, 1)),
            pl.BlockSpec((1, d), lambda i: (0, 0)),
            pl.BlockSpec((tm, LANES), lambda i: (i % per, 0)),
            pl.BlockSpec((tm, LANES), lambda i: (i % per, 0)),
            pl.BlockSpec((d, n_in), lambda i: (0, 0)),
        ],
        out_specs=out_specs,
        out_shape=out_shape,
        compiler_params=pltpu.CompilerParams(dimension_semantics=("arbitrary",)),
        name="inproj_ctx" if is_ctx else "inproj_lat",
    )(x2, mod3, mod3, g_pre, cos_t, sin_t, w_in_b)


def _nt(a, b):
    return lax.dot_general(a, b, (((1,), (1,)), ((), ())), preferred_element_type=F32)


def _diff_kernel(lam_ref, g_ref, q_ref, k_ref, v_ref, o_ref, *, lambda_init):
    lp = lam_ref[...]
    lam = (jnp.exp(jnp.sum(lp[0:1] * lp[1:2], axis=-1, keepdims=True))
           - jnp.exp(jnp.sum(lp[2:3] * lp[3:4], axis=-1, keepdims=True)) + lambda_init)
    tq = q_ref.shape[1]
    lane = lax.broadcasted_iota(jnp.int32, (1, LANES), 1)
    for h in range(A_HEADS):
        cols = slice(h * LANES, (h + 1) * LANES)
        q = q_ref[0, :, cols]
        q2 = jnp.concatenate([q * (lane < 64).astype(BF16), q * (lane >= 64).astype(BF16)], axis=0)
        s = _nt(q2, k_ref[0, :, cols])
        mx = jnp.max(s, axis=-1, keepdims=True)
        v = v_ref[0, :, cols]
        v_ext = jnp.concatenate([v, jnp.ones_like(v)], axis=1)
        acc = jnp.dot(jnp.exp(s - mx).astype(BF16), v_ext, preferred_element_type=F32)
        on = acc[:, 0:LANES] / acc[:, LANES:2 * LANES]
        o = on[0:tq] - lam * on[tq:2 * tq]
        o_ref[0, :, cols] = (_rms(o, g_ref[...]) * (1.0 - lambda_init)).astype(BF16)


def _diff_lat_stages(lam_ref, g_ref, q_ref, kc_ref, kt_ref, vc_ref, v_ref, o_ref, *, lambda_init):
    tq = q_ref.shape[1]
    nkc = kc_ref.shape[1]
    nkn = kt_ref.shape[2]
    nk = nkc + nkn

    def body(s_w, m_w, s_r, m_r):
        lp = lam_ref[...]
        lam = (jnp.exp(jnp.sum(lp[0:1] * lp[1:2], axis=-1, keepdims=True))
               - jnp.exp(jnp.sum(lp[2:3] * lp[3:4], axis=-1, keepdims=True)) + lambda_init)
        lane = lax.broadcasted_iota(jnp.int32, (1, LANES), 1)

        q = q_ref[0]
        q2 = jnp.concatenate([q * (lane < 64).astype(BF16), q * (lane >= 64).astype(BF16)], axis=0)
        sc = _nt(q2, kc_ref[0].astype(BF16))
        sn = jnp.dot(q2, kt_ref[0], preferred_element_type=F32)
        mx = jnp.maximum(jnp.max(sc, axis=-1, keepdims=True), jnp.max(sn, axis=-1, keepdims=True))
        s_w[:, 0:nkc] = sc
        s_w[:, nkc:nk] = sn
        m_w[...] = jnp.broadcast_to(mx, (2 * tq, LANES))

        mp = m_r[...]
        v_all = jnp.concatenate([vc_ref[0].astype(BF16), v_ref[0]], axis=0)
        v_ext = jnp.concatenate([v_all, jnp.ones_like(v_all)], axis=1)
        p = jnp.concatenate(
            [jnp.exp(s_r[:, c:c + LANES] - mp).astype(BF16) for c in range(0, nk, LANES)], axis=1)
        acc = jnp.dot(p, v_ext, preferred_element_type=F32)
        on = acc[:, 0:LANES] / acc[:, LANES:2 * LANES]
        o = on[0:tq] - lam * on[tq:2 * tq]
        o_ref[0] = (_rms(o, g_ref[...]) * (1.0 - lambda_init)).astype(BF16)

    return body


def _diff_lat_pieces(q, kc, *, tq):
    b, s, w = q.shape
    past = kc.shape[1]
    nq = s // tq
    n_units = b * A_HEADS * nq
    last = n_units - 1

    def unit(u):
        return u // (A_HEADS * nq), (u // nq) % A_HEADS, u % nq

    def cur(t):
        return unit(jnp.minimum(t, last))

    def prev(t):
        return unit(jnp.maximum(t - 1, 0))

    in_specs = [
        pl.BlockSpec((1, tq, LANES), lambda t: (cur(t)[0], cur(t)[2], cur(t)[1])),
        pl.BlockSpec((1, past, LANES), lambda t: (cur(t)[0], 0, cur(t)[1])),
        pl.BlockSpec((1, LANES, s), lambda t: (cur(t)[0], cur(t)[1], 0)),
        pl.BlockSpec((1, past, LANES), lambda t: (prev(t)[0], 0, prev(t)[1])),
        pl.BlockSpec((1, s, LANES), lambda t: (prev(t)[0], 0, prev(t)[1])),
    ]
    out_spec = pl.BlockSpec((1, tq, LANES), lambda t: (prev(t)[0], prev(t)[2], prev(t)[1]))
    scratch = [pltpu.VMEM((2 * tq, past + s), F32), pltpu.VMEM((2 * tq, LANES), F32),
               pltpu.VMEM((2 * tq, past + s), F32), pltpu.VMEM((2 * tq, LANES), F32)]
    return in_specs, out_spec, scratch, n_units


def _swa_kernel(sink_ref, q_ref, k_ref, v_ref, o_ref):
    tq = q_ref.shape[1]
    rows = B_GROUP * tq
    lane = lax.broadcasted_iota(jnp.int32, (1, LANES), 1)
    low = lane < 64
    lane2 = lax.broadcasted_iota(jnp.int32, (1, 2 * LANES), 1)
    head_masks = [((lane2 // 64) == g).astype(BF16) for g in range(B_GROUP)]
    gw = B_GROUP * B_HEAD_DIM
    v = v_ref[0]
    v_ext = jnp.concatenate([v, jnp.ones_like(v)], axis=1)
    finishers = []
    for n in range(B_KV_HEADS):
        q = q_ref[0, :, n * gw:(n + 1) * gw]
        qs = jnp.concatenate([q * hm for hm in head_masks], axis=0)
        k2 = k_ref[0, :, n * LANES:(n + 1) * LANES]
        s = _nt(qs, jnp.concatenate([k2, k2], axis=1))
        sink = jnp.concatenate(
            [jnp.full((tq, 1), sink_ref[n * B_GROUP + g], F32) for g in range(B_GROUP)], axis=0)
        top = jnp.max(s, keepdims=True)
        for g in range(B_GROUP):
            top = jnp.maximum(top, sink_ref[n * B_GROUP + g])

        def finish(mp, n=n, s=s, sink=sink):
            acc = jnp.dot(jnp.exp(s - mp).astype(BF16), v_ext, preferred_element_type=F32)
            den = acc[:, LANES:2 * LANES] + jnp.exp(sink - mp)
            o = acc[:, 0:LANES] / den
            osw = pltpu.roll(o, 64, 1)
            for j in range(B_GROUP // 2):
                ra = slice((2 * j) * tq, (2 * j + 1) * tq)
                rb = slice((2 * j + 1) * tq, (2 * j + 2) * tq)
                pair = jnp.where(low, o[ra], osw[rb]) if n == 0 else jnp.where(low, osw[ra], o[rb])
                o_ref[0, :, n * gw + j * LANES:n * gw + (j + 1) * LANES] = pair.astype(BF16)
            return jnp.min(den)

        finishers.append((finish, s, sink, finish(top)))

    smallest = functools.reduce(jnp.minimum, [f[3] for f in finishers])

    @pl.when(jnp.logical_not(smallest >= 1e-30))
    def _():
        for finish, s, sink, _ in finishers:
            finish(jnp.maximum(jnp.max(s, axis=-1, keepdims=True), sink))


def _ctx_attn_kernel(lam_ref, g_ref, sink_ref, qa_ref, ka_ref, va_ref, qb_ref, kb_ref, vb_ref, oa_ref, ob_ref, *,
                     lambda_init):
    _diff_kernel(lam_ref, g_ref, qa_ref, ka_ref, va_ref, oa_ref, lambda_init=lambda_init)
    _swa_kernel(sink_ref, qb_ref, kb_ref, vb_ref, ob_ref)


def _ctx_attention(lam_p, g_head, sink, qa, ka, va, qb, kb2, vb, *, lambda_init):
    b, s, _ = qa.shape
    seq = lambda a: pl.BlockSpec((1, s, a.shape[2]), lambda bi: (bi, 0, 0))
    return pl.pallas_call(
        functools.partial(_ctx_attn_kernel, lambda_init=lambda_init),
        grid=(b,),
        in_specs=[pl.BlockSpec((4, A_HEAD_DIM), lambda bi: (0, 0)), pl.BlockSpec((1, A_V_DIM), lambda bi: (0, 0)),
                  pl.BlockSpec(memory_space=pltpu.SMEM), seq(qa), seq(ka), seq(va), seq(qb), seq(kb2), seq(vb)],
        out_specs=[seq(qa), seq(qb)],
        out_shape=[jax.ShapeDtypeStruct(qa.shape, BF16), jax.ShapeDtypeStruct(qb.shape, BF16)],
        compiler_params=pltpu.CompilerParams(dimension_semantics=("arbitrary",)),
        name="attn_ctx",
    )(lam_p, g_head, sink, qa, ka, va, qb, kb2, vb)


def _swa_lat_stages(sink_ref, q_ref, kc_ref, kl_ref, km_ref, kr_ref, vc_ref, vl_ref, vm_ref, vr_ref, o_ref, *, nqb):
    t = pl.program_id(0)
    n_units = pl.num_programs(0) - 1
    tq = q_ref.shape[1]
    gw = B_GROUP * B_HEAD_DIM
    i_cur = jnp.minimum(t, n_units - 1) % nqb
    nkc = kc_ref.shape[1]
    rows = B_GROUP * tq

    def sink_col(n):
        return jnp.concatenate(
            [jnp.full((tq, 1), sink_ref[n * B_GROUP + g], F32) for g in range(B_GROUP)], axis=0)

    def body(s_w, m_w, s_r, m_r):
        lane = lax.broadcasted_iota(jnp.int32, (1, LANES), 1)
        low = lane < 64
        lane2 = lax.broadcasted_iota(jnp.int32, (1, 2 * LANES), 1)
        head_masks = [((lane2 // 64) == g).astype(BF16) for g in range(B_GROUP)]
        qi = lax.broadcasted_iota(jnp.int32, (rows, SWA_Q), 0) & (tq - 1)
        kj = lax.broadcasted_iota(jnp.int32, (rows, SWA_Q), 1)
        far = 2 * SWA_Q
        left_ok = kj >= qi + jnp.where(i_cur > 0, 0, far)
        right_ok = kj <= qi - jnp.where(i_cur < nqb - 1, 0, far)

        kc = kc_ref[0]
        kc_sw = pltpu.roll(kc, 64, 1)
        for n in range(B_KV_HEADS):
            q = q_ref[0, :, n * gw:(n + 1) * gw]
            qs = jnp.concatenate([q * hm for hm in head_masks], axis=0)
            kc2 = (jnp.where(low, kc, kc_sw) if n == 0 else jnp.where(low, kc_sw, kc)).astype(BF16)
            ks = [kc2] + [r[0, :, n * LANES:(n + 1) * LANES] for r in (kl_ref, km_ref, kr_ref)]
            k_all = jnp.concatenate([jnp.concatenate([k, k], axis=1) for k in ks], axis=0)
            s = _nt(qs, k_all)
            chunks = [s[:, c:c + LANES] for c in range(0, s.shape[1], LANES)]
            il = nkc // LANES
            chunks[il] = jnp.where(left_ok, chunks[il], NEG_INF)
            chunks[il + 2] = jnp.where(right_ok, chunks[il + 2], NEG_INF)
            top = jnp.max(functools.reduce(jnp.maximum, chunks), keepdims=True)
            for g in range(B_GROUP):
                top = jnp.maximum(top, sink_ref[n * B_GROUP + g])
            for c, ch in enumerate(chunks):
                s_w[n * rows:(n + 1) * rows, c * LANES:(c + 1) * LANES] = ch
            m_w[n * rows:(n + 1) * rows, :] = jnp.broadcast_to(top, (rows, LANES))

        vc = vc_ref[0]
        vc_sw = pltpu.roll(vc, 64, 1)
        nk = nkc + 3 * tq
        finishers = []
        for n in range(B_KV_HEADS):
            vc2 =(jnp.where(low, vc, vc_sw) if n == 0 else jnp.where(low, vc_sw, vc)).astype(BF16)
            v_all = jnp.concatenate(
                [vc2] + [r[0, :, n * LANES:(n + 1) * LANES] for r in (vl_ref, vm_ref, vr_ref)], axis=0)
            v_ext = jnp.concatenate([v_all, jnp.ones_like(v_all)], axis=1)
            sink = sink_col(n)

            def finish(mp, n=n, v_ext=v_ext, sink=sink):
                p = jnp.concatenate([jnp.exp(s_r[n * rows:(n + 1) * rows, c:c + LANES] - mp).astype(BF16)
                                     for c in range(0, nk, LANES)], axis=1)
                acc = jnp.dot(p, v_ext, preferred_element_type=F32)
                den = acc[:, LANES:2 * LANES] + jnp.exp(sink - mp)
                o = acc[:, 0:LANES] / den
                for j in range(B_GROUP // 2):
                    pair = jnp.where(low, o[(2 * j) * tq:(2 * j + 1) * tq], o[(2 * j + 1) * tq:(2 * j + 2) * tq])
                    o_ref[0, :, n * gw + j * LANES:n * gw + (j + 1) * LANES] = pair.astype(BF16)
                return jnp.min(den)

            finishers.append((n, finish, sink, finish(m_r[n * rows:(n + 1) * rows, :])))

        smallest = functools.reduce(jnp.minimum, [f[3] for f in finishers])

        @pl.when(jnp.logical_not(smallest >= 1e-30))
        def _():
            for n, finish, sink, _ in finishers:
                row_max = functools.reduce(
                    jnp.maximum, [s_r[n * rows:(n + 1) * rows, c:c + LANES] for c in range(0, nk, LANES)])
                mx = jnp.maximum(jnp.max(row_max, axis=-1, keepdims=True), sink)
                finish(jnp.broadcast_to(mx, (rows, LANES)))

    return body


def _lat_attn_kernel(lam_ref, g_ref, sink_ref, *refs, lambda_init, nqb):
    d_in, s_in = refs[0:5], refs[5:14]
    oa_ref, ob_ref = refs[14:16]
    d_sa, d_ma, d_sb, d_mb, s_sa, s_ma, s_sb, s_mb = refs[16:24]
    t = pl.program_id(0)
    d_body = _diff_lat_stages(lam_ref, g_ref, *d_in, oa_ref, lambda_init=lambda_init)
    s_body = _swa_lat_stages(sink_ref, *s_in, ob_ref, nqb=nqb)

    @pl.when(t == 0)
    def _():
        for r in (d_sb, d_mb, s_sb, s_mb):
            r[...] = jnp.zeros_like(r)

    @pl.when(t % 2 == 0)
    def _():
        d_body(d_sa, d_ma, d_sb, d_mb)
        s_body(s_sa, s_ma, s_sb, s_mb)

    @pl.when(t % 2 == 1)
    def _():
        d_body(d_sb, d_mb, d_sa, d_ma)
        s_body(s_sb, s_mb, s_sa, s_ma)


def _attention_lat(lam_p, g_head, sink, qa, kca, kta, vca, va, q, kc, k2, vc, v, *, tq_a, lambda_init):
    b, s, w = q.shape
    past = kc.shape[1]
    tq = SWA_Q
    nqb = s // tq
    n_units = b * nqb
    last = n_units - 1
    nk = past + 3 * tq

    def cur(t):
        u = jnp.minimum(t, last)
        return u // nqb, u % nqb

    def prev(t):
        u = jnp.maximum(t - 1, 0)
        return u // nqb, u % nqb

    lo = lambda i: jnp.maximum(i - 1, 0)
    hi = lambda i: jnp.minimum(i + 1, nqb - 1)
    kspec = lambda f: pl.BlockSpec((1, tq, 2 * LANES), lambda t: (cur(t)[0], f(cur(t)[1]), 0))
    vspec = lambda f: pl.BlockSpec((1, tq, 2 * LANES), lambda t: (prev(t)[0], f(prev(t)[1]), 0))
    same = lambda i: i
    rows = B_KV_HEADS * B_GROUP * tq
    d_specs, d_out, d_scratch, d_units = _diff_lat_pieces(qa, kca, tq=tq_a)
    assert d_units == n_units
    return pl.pallas_call(
        functools.partial(_lat_attn_kernel, lambda_init=lambda_init, nqb=nqb),
        grid=(n_units + 1,),
        in_specs=[
            pl.BlockSpec((4, A_HEAD_DIM), lambda t: (0, 0)),
            pl.BlockSpec((1, A_V_DIM), lambda t: (0, 0)),
            pl.BlockSpec(memory_space=pltpu.SMEM),
            *d_specs,
            pl.BlockSpec((1, tq, w), lambda t: (cur(t)[0], cur(t)[1], 0)),
            pl.BlockSpec((1, past, LANES), lambda t: (cur(t)[0], 0, 0)),
            kspec(lo), kspec(same), kspec(hi),
            pl.BlockSpec((1, past, LANES), lambda t: (prev(t)[0], 0, 0)),
            vspec(lo), vspec(same), vspec(hi),
        ],
        out_specs=[d_out, pl.BlockSpec((1, tq, w), lambda t: (prev(t)[0], prev(t)[1], 0))],
        out_shape=[jax.ShapeDtypeStruct(qa.shape, BF16), jax.ShapeDtypeStruct((b, s, w), BF16)],
        scratch_shapes=d_scratch + [pltpu.VMEM((rows, nk), F32), pltpu.VMEM((rows, LANES), F32),
                                    pltpu.VMEM((rows, nk), F32), pltpu.VMEM((rows, LANES), F32)],
        compiler_params=pltpu.CompilerParams(dimension_semantics=("arbitrary",),
                                             vmem_limit_bytes=60 * 1024 * 1024),
        name="attn_lat",
    )(lam_p, g_head, sink, qa, kca, kta, vca, va, q, kc, k2, k2, k2, vc, v, v, v)


def _postmix_kernel(*refs, n_ctx_tiles):
    (xc, xl, oac, oal, obc, obl, sgac, sgal, sgbc, sgbl, g1_ref, sh2_ref, sc2_ref, gpm_ref, gpf_ref,
     wpa_ref, wpb_ref, wo_ref, wr_ref, br_ref, x1_o, h2_o, route_o, cnt_o, x1_a, x1_b) = refs
    t = pl.program_id(0)
    is_ctx = t < n_ctx_tiles
    pick = lambda a, b: jnp.where(is_ctx, a[...], b[...])

    @pl.when(t == 0)
    def _():
        x1_b[...] = jnp.zeros_like(x1_b)

    def body(x1_w, x1_r):
        pa = jnp.dot(pick(oac, oal), wpa_ref[...], preferred_element_type=F32)
        pb = jnp.dot(pick(obc, obl), wpb_ref[...], preferred_element_type=F32)
        mix = pick(sgac, sgal).astype(F32) * pa + pick(sgbc, sgbl).astype(F32) * pb
        m2 = jnp.dot(mix.astype(BF16), wo_ref[...], preferred_element_type=F32)
        x1 = pick(xc, xl) + g1_ref[0] * _rms(m2, gpm_ref[...])
        x1_o[...] = x1
        x1_w[...] = x1
        _postmix_route(x1_r[...], sh2_ref, sc2_ref, gpf_ref, wr_ref, br_ref, h2_o, route_o, cnt_o)

    @pl.when(t % 2 == 0)
    def _():
        body(x1_a, x1_b)

    @pl.when(t % 2 == 1)
    def _():
        body(x1_b, x1_a)


def _postmix_route(x1, sh2_ref, sc2_ref, gpf_ref, wr_ref, br_ref, h2_o, route_o, cnt_o):
    h2 = _rms(x1, gpf_ref[...]) * (1.0 + sc2_ref[0]) + sh2_ref[0]
    h2_o[...] = h2.astype(BF16)

    h_hi = h2.astype(BF16)
    h_lo = (h2 - h_hi.astype(F32)).astype(BF16)
    both = jnp.dot(h_hi, wr_ref[...], preferred_element_type=F32)
    logits = (both[:, 0:LANES] + both[:, LANES:2 * LANES]
              + jnp.dot(h_lo, wr_ref[:, 0:LANES], preferred_element_type=F32) + br_ref[...])
    tm = logits.shape[0]
    lt = logits.T
    row = lax.broadcasted_iota(jnp.int32, (EXPERTS_PER_GROUP, tm), 0).astype(F32)
    none = float(EXPERTS_PER_GROUP)
    lg = jnp.where(row < N_GROUPS, lt[N_EXPERTS:N_EXPERTS + EXPERTS_PER_GROUP], -jnp.inf)
    mg = jnp.max(lg, axis=0, keepdims=True)
    g_sel = jnp.min(jnp.where(lg == mg, row, none), axis=0, keepdims=True)
    g_w = 1.0 / jnp.sum(jnp.exp(lg - mg), axis=0, keepdims=True)
    le = lt[0:EXPERTS_PER_GROUP]
    for g in range(1, N_GROUPS):
        le = jnp.where(g_sel == g, lt[g * EXPERTS_PER_GROUP:(g + 1) * EXPERTS_PER_GROUP], le)
    v0 = jnp.max(le, axis=0, keepdims=True)
    i0 = jnp.min(jnp.where(le == v0, row, none), axis=0, keepdims=True)
    le1 = jnp.where(row == i0, -jnp.inf, le)
    v1 = jnp.max(le1, axis=0, keepdims=True)
    i1 = jnp.min(jnp.where(le1 == v1, row, none), axis=0, keepdims=True)
    e = jnp.exp(v1 - v0)
    w0 = g_w / (1.0 + e)
    w1 = g_w * e / (1.0 + e)
    e0 = g_sel * EXPERTS_PER_GROUP + i0
    e1 = g_sel * EXPERTS_PER_GROUP + i1
    route_o[...] = jnp.where(row == 0, e0, jnp.where(row == 1, e1, jnp.where(row == 2, w0,
                                                                             jnp.where(row == 3, w1, 0.0))))
    erow = lax.broadcasted_iota(jnp.int32, (N_EXPERTS, tm), 0).astype(F32)
    cnt = jnp.sum((erow == e0).astype(F32) + (erow == e1).astype(F32), axis=1, keepdims=True)
    cnt_o[0] = jnp.broadcast_to(cnt, (N_EXPERTS, LANES))


def _postmix(ctx_in, lat_in, mod3, gpm, gpf, wpa, wpb, wo, wr, br, *, lat_seq, tm):
    t_ctx, d = ctx_in[0].shape
    t_lat = lat_in[0].shape[0]
    assert t_ctx % tm == 0 and lat_seq % tm == 0
    nc = t_ctx // tm
    nl = t_lat // tm
    per = lat_seq // tm
    t_all = t_ctx + t_lat

    last = nc + nl - 1
    cur = lambda i: jnp.minimum(i, last)
    prev = lambda i: jnp.maximum(i - 1, 0)
    mod_row = lambda j: jnp.where(j < nc, 0, 1 + jnp.maximum(j - nc, 0) // per)
    full = lambda a: pl.BlockSpec(a.shape, lambda i: (0,) * a.ndim)
    in_specs, args = [], []
    for a_c, a_l in zip(ctx_in, lat_in):
        w = a_c.shape[1]
        in_specs += [pl.BlockSpec((tm, w), lambda i: (jnp.minimum(cur(i), nc - 1), 0)),
                     pl.BlockSpec((tm, w), lambda i: (jnp.maximum(cur(i) - nc, 0), 0))]
        args += [a_c, a_l]
    in_specs += [pl.BlockSpec((1, 1, d), lambda i: (mod_row(cur(i)), 0, 2)),
                 pl.BlockSpec((1, 1, d), lambda i: (mod_row(prev(i)), 0, 3)),
                 pl.BlockSpec((1, 1, d), lambda i: (mod_row(prev(i)), 0, 4)),
                 full(gpm), full(gpf), full(wpa), full(wpb), full(wo), full(wr), full(br)]
    args += [mod3, mod3, mod3, gpm, gpf, wpa, wpb, wo, wr, br]
    return pl.pallas_call(
        functools.partial(_postmix_kernel, n_ctx_tiles=nc),
        grid=(nc + nl + 1,),
        in_specs=in_specs,
        out_specs=[pl.BlockSpec((tm, d), lambda i: (cur(i), 0)),
                   pl.BlockSpec((tm, d), lambda i: (prev(i), 0)),
                   pl.BlockSpec((EXPERTS_PER_GROUP, tm), lambda i: (0, prev(i))),
                   pl.BlockSpec((1, N_EXPERTS, LANES), lambda i: (prev(i), 0, 0))],
        out_shape=[jax.ShapeDtypeStruct((t_all, d), F32),
                   jax.ShapeDtypeStruct((t_all, d), BF16),
                   jax.ShapeDtypeStruct((EXPERTS_PER_GROUP, t_all), F32),
                   jax.ShapeDtypeStruct((nc + nl, N_EXPERTS, LANES), F32)],
        scratch_shapes=[pltpu.VMEM((tm, d), F32), pltpu.VMEM((tm, d), F32)],
        compiler_params=pltpu.CompilerParams(dimension_semantics=("arbitrary",)),
        name="postmix",
    )(*args)


def _segment_copies(src, src_row, dst, dst_row, n, sub, sem):
    @pl.when(n > 0)
    def _():
        pltpu.make_async_copy(src.at[pl.ds(pl.multiple_of(src_row * sub, sub), n * sub)],
                              dst.at[pl.ds(pl.multiple_of(dst_row * sub, sub), n * sub)], sem).start()


def _local_positions(route_t, tile_base):
    tm = route_t.shape[1]
    erow = lax.broadcasted_iota(jnp.int32, (N_EXPERTS, tm), 0).astype(F32)
    is0 = erow == route_t[0:1]
    is1 = erow == route_t[1:2]
    earlier = (lax.broadcasted_iota(jnp.int32, (tm, tm), 0)
               < lax.broadcasted_iota(jnp.int32, (tm, tm), 1)).astype(BF16)
    pre0 = jnp.dot(is0.astype(BF16), earlier, preferred_element_type=F32)
    pre1 = jnp.dot(is1.astype(BF16), earlier, preferred_element_type=F32)
    cnt0 = jnp.sum(is0.astype(F32), axis=1, keepdims=True)
    base = tile_base[:, 0:1]
    lpos0 = jnp.sum(jnp.where(is0, base + pre0, 0.0), axis=0, keepdims=True)
    lpos1 = jnp.sum(jnp.where(is1, base + cnt0 + pre1, 0.0), axis=0, keepdims=True)
    return lpos0, lpos1


def _dispatch_kernel(ss_ref, sl_ref, tb_ref, h_ref, r_ref, tbv_ref, xs_hbm, pbuf, zbuf, sem, zsem, *, n_asg, rows):
    i = pl.program_id(0)
    nt = pl.num_programs(0)
    tm, d = h_ref.shape
    sub = d // LANES
    nrow = TOP_K * tm
    slot = i % 2

    def wait_slot(s):
        pltpu.make_async_copy(pbuf.at[pl.ds(pl.multiple_of(s * nrow * sub, nrow * sub), nrow * sub)],
                              xs_hbm.at[pl.ds(0, nrow * sub)], sem.at[s]).wait()

    def slack_copy():
        return pltpu.make_async_copy(zbuf, xs_hbm.at[pl.ds(n_asg * sub, rows * sub)], zsem.at[0])

    @pl.when(i == 0)
    def _():
        zbuf[...] = jnp.zeros_like(zbuf)
        slack_copy().start()

    lpos0, lpos1 = _local_positions(r_ref[...], tbv_ref[0])
    p = lax.broadcasted_iota(jnp.int32, (nrow, tm), 0).astype(F32)
    sel = ((p == lpos0) | (p == lpos1)).astype(BF16)
    xp = jnp.dot(sel, h_ref[...], preferred_element_type=F32)

    @pl.when(i >= 2)
    def _():
        wait_slot(slot)

    _store_row_tiles(pbuf, slot * nrow * sub, xp)

    def seg(e, c):
        k = i * N_EXPERTS + e
        _segment_copies(pbuf, slot * nrow + tb_ref[k], xs_hbm, ss_ref[k], sl_ref[k], sub, sem.at[slot])
        return c
    lax.fori_loop(0, N_EXPERTS, seg, 0)

    @pl.when(i == nt - 1)
    def _():
        wait_slot(slot)

        @pl.when(nt >= 2)
        def _():
            wait_slot(1 - slot)
        slack_copy().wait()


def _dispatch(h2, route, tables, *, tm, rows):
    seg_start, seg_len, tile_base, tile_base_v = tables
    t, d = h2.shape
    sub = d // LANES
    n_asg = t * TOP_K
    grid_spec = pltpu.PrefetchScalarGridSpec(
        num_scalar_prefetch=3,
        grid=(t // tm,),
        in_specs=[
            pl.BlockSpec((tm, d), lambda i, *_: (i, 0)),
            pl.BlockSpec((EXPERTS_PER_GROUP, tm), lambda i, *_: (0, i)),
            pl.BlockSpec((1, N_EXPERTS, LANES), lambda i, *_: (i, 0, 0)),
        ],
        out_specs=pl.BlockSpec(memory_space=pl.ANY),
        scratch_shapes=[pltpu.VMEM((2 * TOP_K * tm * sub, LANES), F32), pltpu.VMEM((rows * sub, LANES), F32),
                        pltpu.SemaphoreType.DMA((2,)), pltpu.SemaphoreType.DMA((1,))],
    )
    return pl.pallas_call(
        functools.partial(_dispatch_kernel, n_asg=n_asg, rows=rows),
        grid_spec=grid_spec,
        out_shape=jax.ShapeDtypeStruct(((n_asg + rows) * sub, LANES), F32),
        compiler_params=pltpu.CompilerParams(dimension_semantics=("arbitrary",)),
        name="dispatch",
    )(seg_start, seg_len, tile_base, h2, route, tile_base_v)


def _moe_kernel(be_ref, row0_ref, nact_ref, par_ref, nxt_ref, xs_hbm, w1_hbm, w3_hbm, w2_hbm, ys_hbm,
                xbuf, obuf, wf1, wf3, wf2, w1b, w3b, w2b, rsem, wsem, gsem, *, rows, sub):
    i = pl.program_id(0)
    nact = nact_ref[0]
    slot = i % 2
    nslot = 1 - slot
    blk = rows * sub

    def weight_copies(e, s):
        return [pltpu.make_async_copy(w_hbm.at[e], wf.at[s], gsem.at[s])
                for w_hbm, wf in ((w1_hbm, wf1), (w3_hbm, wf3), (w2_hbm, wf2))]

    def read(j, s):
        return pltpu.make_async_copy(xs_hbm.at[pl.ds(pl.multiple_of(row0_ref[j] * sub, sub), blk)],
                                     xbuf.at[pl.ds(pl.multiple_of(s * blk, blk), blk)], rsem.at[s])

    def write(j, s):
        return pltpu.make_async_copy(obuf.at[pl.ds(pl.multiple_of(s * blk, blk), blk)],
                                     ys_hbm.at[pl.ds(pl.multiple_of(row0_ref[j] * sub, sub), blk)], wsem.at[s])

    @pl.when(i == 0)
    def _():
        read(0, 0).start()
        for c in weight_copies(be_ref[0], 0):
            c.start()

    @pl.when(i < nact)
    def _():
        @pl.when(i + 1 < nact)
        def _():
            read(i + 1, nslot).start()

        changed = jnp.logical_or(i == 0, be_ref[i] != be_ref[jnp.maximum(i - 1, 0)])

        @pl.when(changed)
        def _():
            s = par_ref[i]
            for c in weight_copies(be_ref[i], s):
                c.wait()
            w1b[...] = wf1[s].astype(BF16)
            w3b[...] = wf3[s].astype(BF16)
            w2b[...] = wf2[s].astype(BF16)

            @pl.when(nxt_ref[i] >= 0)
            def _():
                for c in weight_copies(nxt_ref[i], 1 - s):
                    c.start()

        read(i, slot).wait()
        x = _load_row_tiles(xbuf, slot * blk, rows, sub).astype(BF16)
        a = jnp.dot(x, w1b[...], preferred_element_type=F32)
        b = jnp.dot(x, w3b[...], preferred_element_type=F32)
        hmid = (a * jax.nn.sigmoid(a) * b).astype(BF16)
        y = jnp.dot(hmid, w2b[...], preferred_element_type=F32)
        _store_row_tiles(obuf, slot * blk, y)

        @pl.when(i >= 1)
        def _():
            write(i - 1, nslot).wait()
        write(i, slot).start()

    @pl.when(i == nact)
    def _():
        write(i - 1, nslot).wait()
        obuf[pl.ds(pl.multiple_of(slot * blk, blk), blk), :] = jnp.zeros((blk, LANES), F32)
        tail = pltpu.make_async_copy(obuf.at[pl.ds(pl.multiple_of(slot * blk, blk), blk)],
                                     ys_hbm.at[pl.ds(ys_hbm.shape[0] - blk, blk)], wsem.at[slot])
        tail.start()
        tail.wait()


def _moe(blk_tables, xs, w1, w3, w2):
    blk_expert, row0, nact, parity, nxt = blk_tables
    nblk = blk_expert.shape[0] - 1
    d, de = w1.shape[1], w1.shape[2]
    sub = d // LANES
    rows = MOE_ROWS
    anyspec = pl.BlockSpec(memory_space=pl.ANY)
    grid_spec = pltpu.PrefetchScalarGridSpec(
        num_scalar_prefetch=5,
        grid=(nblk + 1,),
        in_specs=[anyspec, anyspec, anyspec, anyspec],
        out_specs=anyspec,
        scratch_shapes=[
            pltpu.VMEM((2 * rows * sub, LANES), F32),
            pltpu.VMEM((2 * rows * sub, LANES), F32),
            pltpu.VMEM((2, d, de), F32),
            pltpu.VMEM((2, d, de), F32),
            pltpu.VMEM((2, de, d), F32),
            pltpu.VMEM((d, de), BF16),
            pltpu.VMEM((d, de), BF16),
            pltpu.VMEM((de, d), BF16),
            pltpu.SemaphoreType.DMA((2,)),
            pltpu.SemaphoreType.DMA((2,)),
            pltpu.SemaphoreType.DMA((2,)),
        ],
    )
    return pl.pallas_call(
        functools.partial(_moe_kernel, rows=rows, sub=sub),
        grid_spec=grid_spec,
        out_shape=jax.ShapeDtypeStruct(xs.shape, F32),
        compiler_params=pltpu.CompilerParams(dimension_semantics=("arbitrary",)),
        name="expert_mlp",
    )(blk_expert, row0, nact, parity, nxt, xs, w1, w3, w2)


def _combine_kernel(ss_ref, sl_ref, tb_ref, x1_ref, r_ref, tbv_ref, g2_ref, gpost_ref, ys_hbm, oc_ref, ol_ref,
                    ybuf, sem, *, n_ctx_tiles):
    tile0 = 0
    i = pl.program_id(0)
    nt = pl.num_programs(0)
    tm, d = x1_ref.shape
    sub = d // LANES
    nrow = TOP_K * tm
    slot = i % 2

    def fetch(tile, s):
        def seg(e, c):
            k = tile * N_EXPERTS + e
            _segment_copies(ys_hbm, ss_ref[k], ybuf, s * nrow + tb_ref[k], sl_ref[k], sub, sem.at[s])
            return c
        lax.fori_loop(0, N_EXPERTS, seg, 0)

    @pl.when(i == 0)
    def _():
        fetch(tile0, 0)

    @pl.when(i + 1 < nt)
    def _():
        fetch(tile0 + i + 1, 1 - slot)

    pltpu.make_async_copy(ys_hbm.at[pl.ds(0, nrow * sub)],
                          ybuf.at[pl.ds(pl.multiple_of(slot * nrow * sub, nrow * sub), nrow * sub)],
                          sem.at[slot]).wait()
    rt = r_ref[...]
    lpos0, lpos1 = _local_positions(rt, tbv_ref[0])
    rows8 = lax.broadcasted_iota(jnp.int32, (EXPERTS_PER_GROUP, tm), 0)
    cols = jnp.where(rows8 == 0, lpos0, jnp.where(rows8 == 1, lpos1, rt))
    cols = jnp.concatenate([cols, jnp.zeros((LANES - EXPERTS_PER_GROUP, tm), F32)], axis=0).T
    p = lax.broadcasted_iota(jnp.int32, (tm, nrow), 1).astype(F32)
    q = (jnp.where(p == cols[:, 0:1], cols[:, 2:3], 0.0)
         + jnp.where(p == cols[:, 1:2], cols[:, 3:4], 0.0)).astype(BF16)
    ysort = _load_row_tiles(ybuf, slot * nrow * sub, nrow, sub).astype(BF16)
    y = jnp.dot(q, ysort, preferred_element_type=F32)
    out = x1_ref[...] + g2_ref[0] * _rms(y, gpost_ref[...])

    @pl.when(i < n_ctx_tiles)
    def _():
        oc_ref[...] = out

    @pl.when(i >= n_ctx_tiles)
    def _():
        ol_ref[...] = out


def _combine(x1, ys, route, tables, mod3, gpost, *, t_ctx, lat_seq, tm):
    seg_start, seg_len, tile_base, tile_base_v = tables
    t_all, d = x1.shape
    assert t_ctx % tm == 0 and lat_seq % tm == 0
    nc = t_ctx // tm
    per = lat_seq // tm
    sub = d // LANES
    mod_row = lambda i: jnp.where(i < nc, 0, 1 + jnp.maximum(i - nc, 0) // per)
    grid_spec = pltpu.PrefetchScalarGridSpec(
        num_scalar_prefetch=3,
        grid=(t_all // tm,),
        in_specs=[
            pl.BlockSpec((tm, d), lambda i, *_: (i, 0)),
            pl.BlockSpec((EXPERTS_PER_GROUP, tm), lambda i, *_: (0, i)),
            pl.BlockSpec((1, N_EXPERTS, LANES), lambda i, *_: (i, 0, 0)),
            pl.BlockSpec((1, 1, d), lambda i, *_: (mod_row(i), 0, 5)),
            pl.BlockSpec((1, d), lambda i, *_: (0, 0)),
            pl.BlockSpec(memory_space=pl.ANY),
        ],
        out_specs=[pl.BlockSpec((tm, d), lambda i, *_: (jnp.minimum(i, nc - 1), 0)),
                   pl.BlockSpec((tm, d), lambda i, *_: (jnp.maximum(i - nc, 0), 0))],
        scratch_shapes=[pltpu.VMEM((2 * TOP_K * tm * sub, LANES), F32), pltpu.SemaphoreType.DMA((2,))],
    )
    return pl.pallas_call(
        functools.partial(_combine_kernel, n_ctx_tiles=nc),
        grid_spec=grid_spec,
        out_shape=[jax.ShapeDtypeStruct((t_ctx, d), F32), jax.ShapeDtypeStruct((t_all - t_ctx, d), F32)],
        compiler_params=pltpu.CompilerParams(dimension_semantics=("arbitrary",)),
        name="combine",
    )(seg_start, seg_len, tile_base, x1, route, tile_base_v, mod3, gpost, ys)


def _routing_tables(counts, tm, rows):
    nt = counts.shape[0]
    n_asg = nt * tm * TOP_K
    ex = jnp.arange(N_EXPERTS, dtype=jnp.int32)
    cnt_te = counts[:, :, 0].astype(jnp.int32)
    cnt_e = jnp.sum(cnt_te, axis=0)
    start_e = jnp.cumsum(cnt_e) - cnt_e
    seg_start = start_e[None, :] + jnp.cumsum(cnt_te, axis=0) - cnt_te
    tile_base = jnp.cumsum(cnt_te, axis=1) - cnt_te
    tile_base_v = jnp.broadcast_to(tile_base.astype(F32)[:, :, None], (nt, N_EXPERTS, LANES))

    nblk_e = (cnt_e + rows - 1) // rows
    blk_end = jnp.cumsum(nblk_e)
    blk_start = blk_end - nblk_e
    n_blocks = n_asg // rows + N_EXPERTS
    b = jnp.arange(n_blocks + 1, dtype=jnp.int32)
    be = jnp.minimum(jnp.sum((blk_end[None, :] <= b[:, None]).astype(jnp.int32), axis=1), N_EXPERTS - 1)
    first = jnp.sum(jnp.where(be[:, None] == ex[None, :], (start_e - blk_start * rows)[None, :], 0), axis=1)
    row0 = jnp.clip(first + b * rows, 0, n_asg)
    nact = blk_end[-1:].astype(jnp.int32)
    used = cnt_e > 0
    parity_e = (jnp.cumsum(used.astype(jnp.int32)) - 1) % 2
    later = (ex[None, :] > ex[:, None]) & used[None, :]
    nxt_e = jnp.min(jnp.where(later, ex[None, :], N_EXPERTS), axis=1)
    nxt_e = jnp.where(nxt_e == N_EXPERTS, -1, nxt_e)
    pick = lambda tab: jnp.sum(jnp.where(be[:, None] == ex[None, :], tab[None, :], 0), axis=1).astype(jnp.int32)
    seg = (seg_start.reshape(-1).astype(jnp.int32), cnt_te.reshape(-1).astype(jnp.int32),
           tile_base.reshape(-1).astype(jnp.int32), tile_base_v)
    blk = (be.astype(jnp.int32), row0.astype(jnp.int32), nact, pick(parity_e), pick(nxt_e))
    return seg, blk


def _rope_tables(n_tok):
    n_rows = n_tok // GRID_W
    rows = jnp.repeat(jnp.arange(n_rows), GRID_W).astype(F32)
    cols = jnp.tile(jnp.arange(GRID_W), n_rows).astype(F32)
    quarter = A_HEAD_DIM // 4
    inv = ROPE_BASE ** (-jnp.arange(quarter, dtype=F32) / quarter)
    ang = jnp.concatenate([rows[:, None] * inv, cols[:, None] * inv], axis=-1)
    cos, sin = jnp.cos(ang), jnp.sin(ang)
    cos_t = jnp.tile(jnp.concatenate([cos, cos], axis=-1), (1, LANES // A_HEAD_DIM))
    sin_t = jnp.tile(jnp.concatenate([-sin, sin], axis=-1), (1, LANES // A_HEAD_DIM))
    return cos_t, sin_t


def kernel(x_prompt, x_sample, c, cache_diff_k, cache_diff_v, cache_swa_k, cache_swa_v, c_ctx, w_ada, b_ada, g_pre_mix, g_post_mix, g_pre_ffn, g_post_ffn, w_in, lam_q1, lam_k1, lam_q2, lam_k2, g_diff_head, sink, w_proj_a, w_proj_b, w_out, w_router_group, b_router_group, w_router_expert, b_router_expert, w_e1, w_e3, w_e2):
    depth = w_in.shape[0]
    assert depth == 1
    l = 0
    bp, sp, d = x_prompt.shape
    bs, ss, _ = x_sample.shape
    lambda_init = 0.8 - 0.6 * math.exp(-0.3 * l)
    assert A_HEAD_DIM == B_HEAD_DIM and ss % GRID_W == 0 and bs + 1 <= MOD_ROWS

    c_all = jnp.concatenate([c_ctx[None, :], c, jnp.zeros((MOD_ROWS - 1 - bs, d), F32)], axis=0)
    mod = _modulation(c_all, w_ada[l], b_ada[l][None, :])
    mod3 = mod.reshape(MOD_ROWS, 1, 6 * d)

    w_in_b = w_in[l].astype(BF16)
    wpa = w_proj_a[l].astype(BF16)
    wpb = w_proj_b[l].astype(BF16)
    wo = w_out[l].astype(BF16)
    n_r = N_GROUPS + N_EXPERTS
    wr = jnp.concatenate([w_router_expert[l], w_router_group[l], jnp.zeros((d, LANES - n_r), F32)], axis=1)
    br = jnp.concatenate([b_router_expert[l], b_router_group[l], jnp.zeros((LANES - n_r,), F32)])[None, :]
    wr_hi = wr.astype(BF16)
    wr2 = jnp.concatenate([wr_hi, (wr - wr_hi.astype(F32)).astype(BF16)], axis=1)
    lam_p = jnp.stack([lam_q1[l], lam_k1[l], lam_q2[l], lam_k2[l]], axis=0)
    g_head = g_diff_head[l][None, :]
    sink_l = sink[l]
    cos_t, sin_t = _rope_tables(ss)

    xp2 = x_prompt.reshape(bp * sp, d)
    xs2 = x_sample.reshape(bs * ss, d)
    gpre = g_pre_mix[l][None, :]

    (qa_c, ka_c, va_c, qb_c, kb2_c, vb_c, sga_c, sgb_c, kaf, vaf, kbf, vbf) = _inproj(
        xp2, mod3, 0, gpre, cos_t, sin_t, w_in_b, seq=sp, tm=sp, is_ctx=True)
    r3 = lambda a, b_: a.reshape(b_, -1, a.shape[-1])
    oa_c, ob_c = _ctx_attention(lam_p, g_head, sink_l, r3(qa_c, bp), r3(ka_c, bp), r3(va_c, bp),
                                r3(qb_c, bp), r3(kb2_c, bp), r3(vb_c, bp), lambda_init=lambda_init)

    (qa_s, ka_s, va_s, qb_s, kb2_s, vb_s, sga_s, sgb_s) = _inproj(
        xs2, mod3, 1, gpre, cos_t, sin_t, w_in_b, seq=ss, tm=512, is_ctx=False)
    past = cache_diff_k.shape[2]
    ck = cache_diff_k[:, l].reshape(bs, past, -1)
    cv = cache_diff_v[:, l].reshape(bs, past, -1)
    sk = cache_swa_k[:, l].reshape(bs, past, -1)
    sv = cache_swa_v[:, l].reshape(bs, past, -1)
    kb2_3, vb_3 = r3(kb2_s, bs), r3(vb_s, bs)
    oa_s, ob_s = _attention_lat(lam_p, g_head, sink_l, r3(qa_s, bs), ck, ka_s, cv, r3(va_s, bs),
                                r3(qb_s, bs), sk, kb2_3, sv, vb_3, tq_a=512, lambda_init=lambda_init)

    gpm = g_post_mix[l][None, :]
    gpf = g_pre_ffn[l][None, :]
    t_ctx, t_lat = bp * sp, bs * ss
    x1, h2t, route, counts = _postmix(
        (xp2, oa_c.reshape(t_ctx, -1), ob_c.reshape(t_ctx, -1), sga_c, sgb_c),
        (xs2, oa_s.reshape(t_lat, -1), ob_s.reshape(t_lat, -1), sga_s, sgb_s),
        mod3, gpm, gpf, wpa, wpb, wo, wr2, br, lat_seq=ss, tm=512)

    tables, blk_tables = _routing_tables(counts, MOE_TILE, MOE_ROWS)
    xs = _dispatch(h2t, route, tables, tm=MOE_TILE, rows=MOE_ROWS)
    ys = _moe(blk_tables, xs, w_e1[l], w_e3[l], w_e2[l])

    gpost = g_post_ffn[l][None, :]
    y_p, y_s = _combine(x1, ys, route, tables, mod3, gpost, t_ctx=t_ctx, lat_seq=ss, tm=MOE_TILE)

    ha = A_HEADS
    return (y_p.reshape(bp, sp, d), y_s.reshape(bs, ss, d),
            kaf.reshape(bp, 1, sp, ha, 2, A_HEAD_DIM), vaf.reshape(bp, 1, sp, ha, A_V_DIM),
            kbf.reshape(bp, 1, sp, B_KV_HEADS, B_HEAD_DIM), vbf.reshape(bp, 1, sp, B_KV_HEADS, B_HEAD_DIM))
```
